```python
import jax, jax.numpy as jnp
from jax import lax
import numpy as np

D_MODEL = 1024
BATCH = 8
SEQ = 8192
DEPTH = 4

N_META = 16
LRU_WIDTH = D_MODEL
LRU_HEADS = 8
LRU_HEAD_DIM = LRU_WIDTH // LRU_HEADS
LRU_CONV = 4
LRU_C = 8.0
CONV_WIDTH = D_MODEL // 2
CONV_GROUPS = 4
CONV_KERNEL = 31
MIX_WIDTH = LRU_WIDTH + CONV_WIDTH
IN_WIDTH = 2 * LRU_WIDTH + 2 * CONV_WIDTH
D_FF = 3 * D_MODEL
FFN_CONV = 3
EPS = 1e-6

kernel_name = 'hybrid_rglru_conformer_conv_trunk'


def rms_norm(x, g):
    xf = x.astype(jnp.float32)
    y = xf * lax.rsqrt(jnp.mean(xf * xf, axis=-1, keepdims=True) + EPS)
    return (y * g.astype(jnp.float32)).astype(x.dtype)


def group_layer_norm(x, g, b, groups):
    shp = x.shape
    xf = x.astype(jnp.float32).reshape(shp[:-1] + (groups, shp[-1] // groups))
    mu = jnp.mean(xf, axis=-1, keepdims=True)
    var = jnp.mean(jnp.square(xf - mu), axis=-1, keepdims=True)
    y = ((xf - mu) * lax.rsqrt(var + EPS)).reshape(shp)
    return (y * g.astype(jnp.float32) + b.astype(jnp.float32)).astype(x.dtype)


def causal_dwconv(x, w, b):
    K, C = w.shape
    y = lax.conv_general_dilated(
        x, w[:, None, :].astype(x.dtype), window_strides=(1,),
        padding=[(K - 1, 0)], dimension_numbers=('NWC', 'WIO', 'NWC'),
        feature_group_count=C)
    return y + b.astype(x.dtype)


def rg_lru(x, w_a, b_a, w_x, b_x, lam):
    B, S, W = x.shape
    xh = x.reshape(B, S, LRU_HEADS, LRU_HEAD_DIM)
    r = jax.nn.sigmoid(jnp.einsum('bshi,hij->bshj', xh, w_a).reshape(B, S, W) + b_a)
    i = jax.nn.sigmoid(jnp.einsum('bshi,hij->bshj', xh, w_x).reshape(B, S, W) + b_x)
    log_a = -LRU_C * r.astype(jnp.float32) * jax.nn.softplus(-lam.astype(jnp.float32))
    a = jnp.exp(log_a)
    mult = jnp.sqrt(-jnp.expm1(2.0 * log_a))
    u = mult * (i * x).astype(jnp.float32)

    def combine(left, right):
        a_l, b_l = left
        a_r, b_r = right
        return a_l * a_r, a_r * b_l + b_r

    _, h = lax.associative_scan(combine, (a, u), axis=1)
    return h.astype(x.dtype)


def hybrid_layer(h, g_pre_mix, w_in, lru_conv_w, lru_conv_b, lru_wa, lru_ba, lru_wx, lru_bx,
                 lru_lambda, conv_w, conv_b, conv_ln_g, conv_ln_b, g_out_lru, g_out_conv,
                 w_out, g_post_mix, g_pre_ffn, w_up, ffn_conv_w, ffn_conv_b, w_down, g_post_ffn):
    z = rms_norm(h, g_pre_mix)
    proj = jnp.einsum('bsd,de->bse', z, w_in)
    x_lru, g_lru, c_a, c_b = jnp.split(
        proj, [LRU_WIDTH, 2 * LRU_WIDTH, 2 * LRU_WIDTH + CONV_WIDTH], axis=-1)
    x_lru = causal_dwconv(x_lru, lru_conv_w, lru_conv_b)
    y_a = rg_lru(x_lru, lru_wa, lru_ba, lru_wx, lru_bx, lru_lambda) * jax.nn.gelu(g_lru)
    c = c_a * jax.nn.sigmoid(c_b)
    c = causal_dwconv(c, conv_w, conv_b)
    y_b = jax.nn.silu(group_layer_norm(c, conv_ln_g, conv_ln_b, CONV_GROUPS))
    y = jnp.concatenate([rms_norm(y_a, g_out_lru), rms_norm(y_b, g_out_conv)], axis=-1)
    h = h + rms_norm(jnp.einsum('bse,ed->bsd', y, w_out), g_post_mix)
    z = rms_norm(h, g_pre_ffn)
    u = causal_dwconv(jnp.einsum('bsd,df->bsf', z, w_up), ffn_conv_w, ffn_conv_b)
    gate, up = jnp.split(u, 2, axis=-1)
    f = jnp.einsum('bsf,fd->bsd', jax.nn.gelu(gate) * up, w_down)
    return h + rms_norm(f, g_post_ffn)


def _normal(k, shape, scale):
    return jax.random.normal(k, shape, jnp.float32) * scale


def _fwd_setup_inputs(seed: int = 0) -> dict:
    key = jax.random.key(seed)
    ks = jax.random.split(key, 32)
    L, D = DEPTH, D_MODEL

    def gain(k, n):
        return 1.0 + _normal(k, (L, n), 0.02)

    u = jax.random.uniform(ks[10], (L, LRU_WIDTH), jnp.float32, minval=0.9, maxval=0.999)
    a_base = u ** (1.0 / LRU_C)
    lam = jnp.log(a_base) - jnp.log1p(-a_base)
    return {
        'x': _normal(ks[0], (BATCH, SEQ, D), 1.0),
        'meta_tokens': _normal(ks[1], (N_META, D), 1.0),
        'g_pre_mix': gain(ks[2], D),
        'w_in': _normal(ks[3], (L, D, IN_WIDTH), D ** -0.5),
        'lru_conv_w': _normal(ks[4], (L, LRU_CONV, LRU_WIDTH), LRU_CONV ** -0.5),
        'lru_conv_b': _normal(ks[5], (L, LRU_WIDTH), 0.01),
        'lru_wa': _normal(ks[6], (L, LRU_HEADS, LRU_HEAD_DIM, LRU_HEAD_DIM), LRU_HEAD_DIM ** -0.5),
        'lru_ba': _normal(ks[7], (L, LRU_WIDTH), 0.01),
        'lru_wx': _normal(ks[8], (L, LRU_HEADS, LRU_HEAD_DIM, LRU_HEAD_DIM), LRU_HEAD_DIM ** -0.5),
        'lru_bx': _normal(ks[9], (L, LRU_WIDTH), 0.01),
        'lru_lambda': lam,
        'conv_w': _normal(ks[11], (L, CONV_KERNEL, CONV_WIDTH), CONV_KERNEL ** -0.5),
        'conv_b': _normal(ks[12], (L, CONV_WIDTH), 0.01),
        'conv_ln_g': gain(ks[13], CONV_WIDTH),
        'conv_ln_b': _normal(ks[14], (L, CONV_WIDTH), 0.01),
        'g_out_lru': gain(ks[15], LRU_WIDTH),
        'g_out_conv': gain(ks[16], CONV_WIDTH),
        'w_out': _normal(ks[17], (L, MIX_WIDTH, D), MIX_WIDTH ** -0.5),
        'g_post_mix': gain(ks[18], D),
        'g_pre_ffn': gain(ks[19], D),
        'w_up': _normal(ks[20], (L, D, 2 * D_FF), D ** -0.5),
        'ffn_conv_w': _normal(ks[21], (L, FFN_CONV, 2 * D_FF), FFN_CONV ** -0.5),
        'ffn_conv_b': _normal(ks[22], (L, 2 * D_FF), 0.01),
        'w_down': _normal(ks[23], (L, D_FF, D), D_FF ** -0.5),
        'g_post_ffn': gain(ks[24], D),
    }


def _fwd_reference(x, meta_tokens, g_pre_mix, w_in, lru_conv_w, lru_conv_b, lru_wa, lru_ba, lru_wx,
              lru_bx, lru_lambda, conv_w, conv_b, conv_ln_g, conv_ln_b, g_out_lru, g_out_conv,
              w_out, g_post_mix, g_pre_ffn, w_up, ffn_conv_w, ffn_conv_b, w_down, g_post_ffn):
    B = x.shape[0]
    meta = jnp.broadcast_to(meta_tokens[None].astype(x.dtype), (B, N_META, D_MODEL))
    h = jnp.concatenate([meta, x], axis=1)
    for l in range(DEPTH):
        h = hybrid_layer(
            h, g_pre_mix[l], w_in[l], lru_conv_w[l], lru_conv_b[l], lru_wa[l], lru_ba[l],
            lru_wx[l], lru_bx[l], lru_lambda[l], conv_w[l], conv_b[l], conv_ln_g[l],
            conv_ln_b[l], g_out_lru[l], g_out_conv[l], w_out[l], g_post_mix[l],
            g_pre_ffn[l], w_up[l], ffn_conv_w[l], ffn_conv_b[l], w_down[l], g_post_ffn[l])
    return h[:, N_META:, :]


import jax as _jax
import jax.numpy as _jnp

TWIN_FORMAT = 'train_step'
FWD_PARAMS = ['x', 'meta_tokens', 'g_pre_mix', 'w_in', 'lru_conv_w', 'lru_conv_b', 'lru_wa', 'lru_ba', 'lru_wx', 'lru_bx', 'lru_lambda', 'conv_w', 'conv_b', 'conv_ln_g', 'conv_ln_b', 'g_out_lru', 'g_out_conv', 'w_out', 'g_post_mix', 'g_pre_ffn', 'w_up', 'ffn_conv_w', 'ffn_conv_b', 'w_down', 'g_post_ffn']
TWIN_WEIGHTS = ['meta_tokens', 'g_pre_mix', 'w_in', 'lru_conv_w', 'lru_conv_b', 'lru_wa', 'lru_ba', 'lru_wx', 'lru_bx', 'lru_lambda', 'conv_w', 'conv_b', 'conv_ln_g', 'conv_ln_b', 'g_out_lru', 'g_out_conv', 'w_out', 'g_post_mix', 'g_pre_ffn', 'w_up', 'ffn_conv_w', 'ffn_conv_b', 'w_down', 'g_post_ffn']
TWIN_DIFF_INPUT = 'x'
TWIN_INPUTS = ['x', 'meta_tokens', 'g_pre_mix', 'w_in', 'lru_conv_w', 'lru_conv_b', 'lru_wa', 'lru_ba', 'lru_wx', 'lru_bx', 'lru_lambda', 'conv_w', 'conv_b', 'conv_ln_g', 'conv_ln_b', 'g_out_lru', 'g_out_conv', 'w_out', 'g_post_mix', 'g_pre_ffn', 'w_up', 'ffn_conv_w', 'ffn_conv_b', 'w_down', 'g_post_ffn', 'loss_target', 'm_meta_tokens', 'm_g_pre_mix', 'm_w_in', 'm_lru_conv_w', 'm_lru_conv_b', 'm_lru_wa', 'm_lru_ba', 'm_lru_wx', 'm_lru_bx', 'm_lru_lambda', 'm_conv_w', 'm_conv_b', 'm_conv_ln_g', 'm_conv_ln_b', 'm_g_out_lru', 'm_g_out_conv', 'm_w_out', 'm_g_post_mix', 'm_g_pre_ffn', 'm_w_up', 'm_ffn_conv_w', 'm_ffn_conv_b', 'm_w_down', 'm_g_post_ffn', 'v_meta_tokens', 'v_g_pre_mix', 'v_w_in', 'v_lru_conv_w', 'v_lru_conv_b', 'v_lru_wa', 'v_lru_ba', 'v_lru_wx', 'v_lru_bx', 'v_lru_lambda', 'v_conv_w', 'v_conv_b', 'v_conv_ln_g', 'v_conv_ln_b', 'v_g_out_lru', 'v_g_out_conv', 'v_w_out', 'v_g_post_mix', 'v_g_pre_ffn', 'v_w_up', 'v_ffn_conv_w', 'v_ffn_conv_b', 'v_w_down', 'v_g_post_ffn']
TWIN_OUTPUTS = ['loss', 'grad_x', 'grad_meta_tokens', 'grad_g_pre_mix', 'grad_w_in', 'grad_lru_conv_w', 'grad_lru_conv_b', 'grad_lru_wa', 'grad_lru_ba', 'grad_lru_wx', 'grad_lru_bx', 'grad_lru_lambda', 'grad_conv_w', 'grad_conv_b', 'grad_conv_ln_g', 'grad_conv_ln_b', 'grad_g_out_lru', 'grad_g_out_conv', 'grad_w_out', 'grad_g_post_mix', 'grad_g_pre_ffn', 'grad_w_up', 'grad_ffn_conv_w', 'grad_ffn_conv_b', 'grad_w_down', 'grad_g_post_ffn', 'delta_meta_tokens', 'delta_g_pre_mix', 'delta_w_in', 'delta_lru_conv_w', 'delta_lru_conv_b', 'delta_lru_wa', 'delta_lru_ba', 'delta_lru_wx', 'delta_lru_bx', 'delta_lru_lambda', 'delta_conv_w', 'delta_conv_b', 'delta_conv_ln_g', 'delta_conv_ln_b', 'delta_g_out_lru', 'delta_g_out_conv', 'delta_w_out', 'delta_g_post_mix', 'delta_g_pre_ffn', 'delta_w_up', 'delta_ffn_conv_w', 'delta_ffn_conv_b', 'delta_w_down', 'delta_g_post_ffn', 'new_m_meta_tokens', 'new_m_g_pre_mix', 'new_m_w_in', 'new_m_lru_conv_w', 'new_m_lru_conv_b', 'new_m_lru_wa', 'new_m_lru_ba', 'new_m_lru_wx', 'new_m_lru_bx', 'new_m_lru_lambda', 'new_m_conv_w', 'new_m_conv_b', 'new_m_conv_ln_g', 'new_m_conv_ln_b', 'new_m_g_out_lru', 'new_m_g_out_conv', 'new_m_w_out', 'new_m_g_post_mix', 'new_m_g_pre_ffn', 'new_m_w_up', 'new_m_ffn_conv_w', 'new_m_ffn_conv_b', 'new_m_w_down', 'new_m_g_post_ffn', 'new_v_meta_tokens', 'new_v_g_pre_mix', 'new_v_w_in', 'new_v_lru_conv_w', 'new_v_lru_conv_b', 'new_v_lru_wa', 'new_v_lru_ba', 'new_v_lru_wx', 'new_v_lru_bx', 'new_v_lru_lambda', 'new_v_conv_w', 'new_v_conv_b', 'new_v_conv_ln_g', 'new_v_conv_ln_b', 'new_v_g_out_lru', 'new_v_g_out_conv', 'new_v_w_out', 'new_v_g_post_mix', 'new_v_g_pre_ffn', 'new_v_w_up', 'new_v_ffn_conv_w', 'new_v_ffn_conv_b', 'new_v_w_down', 'new_v_g_post_ffn']
TWIN_LEAF_KINDS = {'loss': 'loss', 'grad_x': 'grad_x', 'grad_meta_tokens': 'grad_w', 'grad_g_pre_mix': 'grad_w', 'grad_w_in': 'grad_w', 'grad_lru_conv_w': 'grad_w', 'grad_lru_conv_b': 'grad_w', 'grad_lru_wa': 'grad_w', 'grad_lru_ba': 'grad_w', 'grad_lru_wx': 'grad_w', 'grad_lru_bx': 'grad_w', 'grad_lru_lambda': 'grad_w', 'grad_conv_w': 'grad_w', 'grad_conv_b': 'grad_w', 'grad_conv_ln_g': 'grad_w', 'grad_conv_ln_b': 'grad_w', 'grad_g_out_lru': 'grad_w', 'grad_g_out_conv': 'grad_w', 'grad_w_out': 'grad_w', 'grad_g_post_mix': 'grad_w', 'grad_g_pre_ffn': 'grad_w', 'grad_w_up': 'grad_w', 'grad_ffn_conv_w': 'grad_w', 'grad_ffn_conv_b': 'grad_w', 'grad_w_down': 'grad_w', 'grad_g_post_ffn': 'grad_w', 'delta_meta_tokens': 'delta_w', 'delta_g_pre_mix': 'delta_w', 'delta_w_in': 'delta_w', 'delta_lru_conv_w': 'delta_w', 'delta_lru_conv_b': 'delta_w', 'delta_lru_wa': 'delta_w', 'delta_lru_ba': 'delta_w', 'delta_lru_wx': 'delta_w', 'delta_lru_bx': 'delta_w', 'delta_lru_lambda': 'delta_w', 'delta_conv_w': 'delta_w', 'delta_conv_b': 'delta_w', 'delta_conv_ln_g': 'delta_w', 'delta_conv_ln_b': 'delta_w', 'delta_g_out_lru': 'delta_w', 'delta_g_out_conv': 'delta_w', 'delta_w_out': 'delta_w', 'delta_g_post_mix': 'delta_w', 'delta_g_pre_ffn': 'delta_w', 'delta_w_up': 'delta_w', 'delta_ffn_conv_w': 'delta_w', 'delta_ffn_conv_b': 'delta_w', 'delta_w_down': 'delta_w', 'delta_g_post_ffn': 'delta_w', 'new_m_meta_tokens': 'new_m', 'new_m_g_pre_mix': 'new_m', 'new_m_w_in': 'new_m', 'new_m_lru_conv_w': 'new_m', 'new_m_lru_conv_b': 'new_m', 'new_m_lru_wa': 'new_m', 'new_m_lru_ba': 'new_m', 'new_m_lru_wx': 'new_m', 'new_m_lru_bx': 'new_m', 'new_m_lru_lambda': 'new_m', 'new_m_conv_w': 'new_m', 'new_m_conv_b': 'new_m', 'new_m_conv_ln_g': 'new_m', 'new_m_conv_ln_b': 'new_m', 'new_m_g_out_lru': 'new_m', 'new_m_g_out_conv': 'new_m', 'new_m_w_out': 'new_m', 'new_m_g_post_mix': 'new_m', 'new_m_g_pre_ffn': 'new_m', 'new_m_w_up': 'new_m', 'new_m_ffn_conv_w': 'new_m', 'new_m_ffn_conv_b': 'new_m', 'new_m_w_down': 'new_m', 'new_m_g_post_ffn': 'new_m', 'new_v_meta_tokens': 'new_v', 'new_v_g_pre_mix': 'new_v', 'new_v_w_in': 'new_v', 'new_v_lru_conv_w': 'new_v', 'new_v_lru_conv_b': 'new_v', 'new_v_lru_wa': 'new_v', 'new_v_lru_ba': 'new_v', 'new_v_lru_wx': 'new_v', 'new_v_lru_bx': 'new_v', 'new_v_lru_lambda': 'new_v', 'new_v_conv_w': 'new_v', 'new_v_conv_b': 'new_v', 'new_v_conv_ln_g': 'new_v', 'new_v_conv_ln_b': 'new_v', 'new_v_g_out_lru': 'new_v', 'new_v_g_out_conv': 'new_v', 'new_v_w_out': 'new_v', 'new_v_g_post_mix': 'new_v', 'new_v_g_pre_ffn': 'new_v', 'new_v_w_up': 'new_v', 'new_v_ffn_conv_w': 'new_v', 'new_v_ffn_conv_b': 'new_v', 'new_v_w_down': 'new_v', 'new_v_g_post_ffn': 'new_v'}


def _forward(args):
    return _fwd_reference(*[args[k] for k in FWD_PARAMS])


def _output_shape():
    out = _jax.eval_shape(lambda: _forward(_fwd_setup_inputs(0)))
    return out.shape, out.dtype

N_MICROBATCH = 1
ADAM_LR = 0.001
ADAM_B1 = 0.9
ADAM_B2 = 0.999
ADAM_EPS = 1e-08
ADAM_WD = 0.01
ADAM_STEP = 10
PER_EXAMPLE_BATCH_AXIS = {'x': 0, 'loss_target': 0}
SHARED_INPUTS = []
_WEIGHT_DTYPES = {'meta_tokens': _jnp.float32, 'g_pre_mix': _jnp.float32, 'w_in': _jnp.float32, 'lru_conv_w': _jnp.float32, 'lru_conv_b': _jnp.float32, 'lru_wa': _jnp.float32, 'lru_ba': _jnp.float32, 'lru_wx': _jnp.float32, 'lru_bx': _jnp.float32, 'lru_lambda': _jnp.float32, 'conv_w': _jnp.float32, 'conv_b': _jnp.float32, 'conv_ln_g': _jnp.float32, 'conv_ln_b': _jnp.float32, 'g_out_lru': _jnp.float32, 'g_out_conv': _jnp.float32, 'w_out': _jnp.float32, 'g_post_mix': _jnp.float32, 'g_pre_ffn': _jnp.float32, 'w_up': _jnp.float32, 'ffn_conv_w': _jnp.float32, 'ffn_conv_b': _jnp.float32, 'w_down': _jnp.float32, 'g_post_ffn': _jnp.float32}
MOMENT_SCALE = {'meta_tokens': 4.628626e-01, 'g_pre_mix': 8.768077e+00, 'w_in': 4.991864e+00, 'lru_conv_w': 1.122728e+01, 'lru_conv_b': 1.453275e+02, 'lru_wa': 3.841702e+00, 'lru_ba': 3.201147e+00, 'lru_wx': 7.488360e+00, 'lru_bx': 4.637013e+00, 'lru_lambda': 6.186807e+00, 'conv_w': 4.660281e+00, 'conv_b': 7.816891e+01, 'conv_ln_g': 3.028682e+01, 'conv_ln_b': 4.439602e+01, 'g_out_lru': 1.442977e+01, 'g_out_conv': 1.730104e+01, 'w_out': 1.654200e+01, 'g_post_mix': 6.651489e+01, 'g_pre_ffn': 4.265834e+00, 'w_up': 1.635995e+00, 'ffn_conv_w': 2.145956e+00, 'ffn_conv_b': 1.267526e+01, 'w_down': 4.125243e+00, 'g_post_ffn': 6.413363e+01}


def _to_microbatches(a, axis):
    t = _jnp.moveaxis(a, axis, 0)
    t = t.reshape((N_MICROBATCH, t.shape[0] // N_MICROBATCH) + t.shape[1:])
    return _jnp.moveaxis(t, 1, axis + 1)


def setup_inputs(seed: int = 0) -> dict:
    inp = _fwd_setup_inputs(seed)
    key = _jax.random.fold_in(_jax.random.key(seed), 7919)
    shape, _ = _output_shape()
    out = dict(inp)
    out["loss_target"] = _jax.random.normal(_jax.random.fold_in(key, 0), shape, _jnp.float32)
    for i, name in enumerate(TWIN_WEIGHTS):
        w = inp[name].astype(_jnp.float32)
        if MOMENT_SCALE is None:
            s = _jnp.sqrt(_jnp.mean(_jnp.square(w)) + 1e-30)
        else:
            s = MOMENT_SCALE[name]
        km, kv = _jax.random.split(_jax.random.fold_in(key, i + 1))
        out[name] = w
        out["m_" + name] = s * _jax.random.normal(km, w.shape, _jnp.float32)
        out["v_" + name] = (s * s) * _jax.random.uniform(kv, w.shape, _jnp.float32, 0.5, 1.5)
    if N_MICROBATCH > 1:
        for name, axis in PER_EXAMPLE_BATCH_AXIS.items():
            out[name] = _to_microbatches(out[name], axis)
    return {'x': out['x'], 'meta_tokens': out['meta_tokens'], 'g_pre_mix': out['g_pre_mix'], 'w_in': out['w_in'], 'lru_conv_w': out['lru_conv_w'], 'lru_conv_b': out['lru_conv_b'], 'lru_wa': out['lru_wa'], 'lru_ba': out['lru_ba'], 'lru_wx': out['lru_wx'], 'lru_bx': out['lru_bx'], 'lru_lambda': out['lru_lambda'], 'conv_w': out['conv_w'], 'conv_b': out['conv_b'], 'conv_ln_g': out['conv_ln_g'], 'conv_ln_b': out['conv_ln_b'], 'g_out_lru': out['g_out_lru'], 'g_out_conv': out['g_out_conv'], 'w_out': out['w_out'], 'g_post_mix': out['g_post_mix'], 'g_pre_ffn': out['g_pre_ffn'], 'w_up': out['w_up'], 'ffn_conv_w': out['ffn_conv_w'], 'ffn_conv_b': out['ffn_conv_b'], 'w_down': out['w_down'], 'g_post_ffn': out['g_post_ffn'], 'loss_target': out['loss_target'], 'm_meta_tokens': out['m_meta_tokens'], 'm_g_pre_mix': out['m_g_pre_mix'], 'm_w_in': out['m_w_in'], 'm_lru_conv_w': out['m_lru_conv_w'], 'm_lru_conv_b': out['m_lru_conv_b'], 'm_lru_wa': out['m_lru_wa'], 'm_lru_ba': out['m_lru_ba'], 'm_lru_wx': out['m_lru_wx'], 'm_lru_bx': out['m_lru_bx'], 'm_lru_lambda': out['m_lru_lambda'], 'm_conv_w': out['m_conv_w'], 'm_conv_b': out['m_conv_b'], 'm_conv_ln_g': out['m_conv_ln_g'], 'm_conv_ln_b': out['m_conv_ln_b'], 'm_g_out_lru': out['m_g_out_lru'], 'm_g_out_conv': out['m_g_out_conv'], 'm_w_out': out['m_w_out'], 'm_g_post_mix': out['m_g_post_mix'], 'm_g_pre_ffn': out['m_g_pre_ffn'], 'm_w_up': out['m_w_up'], 'm_ffn_conv_w': out['m_ffn_conv_w'], 'm_ffn_conv_b': out['m_ffn_conv_b'], 'm_w_down': out['m_w_down'], 'm_g_post_ffn': out['m_g_post_ffn'], 'v_meta_tokens': out['v_meta_tokens'], 'v_g_pre_mix': out['v_g_pre_mix'], 'v_w_in': out['v_w_in'], 'v_lru_conv_w': out['v_lru_conv_w'], 'v_lru_conv_b': out['v_lru_conv_b'], 'v_lru_wa': out['v_lru_wa'], 'v_lru_ba': out['v_lru_ba'], 'v_lru_wx': out['v_lru_wx'], 'v_lru_bx': out['v_lru_bx'], 'v_lru_lambda': out['v_lru_lambda'], 'v_conv_w': out['v_conv_w'], 'v_conv_b': out['v_conv_b'], 'v_conv_ln_g': out['v_conv_ln_g'], 'v_conv_ln_b': out['v_conv_ln_b'], 'v_g_out_lru': out['v_g_out_lru'], 'v_g_out_conv': out['v_g_out_conv'], 'v_w_out': out['v_w_out'], 'v_g_post_mix': out['v_g_post_mix'], 'v_g_pre_ffn': out['v_g_pre_ffn'], 'v_w_up': out['v_w_up'], 'v_ffn_conv_w': out['v_ffn_conv_w'], 'v_ffn_conv_b': out['v_ffn_conv_b'], 'v_w_down': out['v_w_down'], 'v_g_post_ffn': out['v_g_post_ffn']}


def _loss(weights, diff, rest, loss_target):
    with _jax.named_scope("forward"):
        args = {**rest, TWIN_DIFF_INPUT: diff, **{k: w.astype(_WEIGHT_DTYPES[k]) for k, w in weights.items()}}
        y = _forward(args)
    with _jax.named_scope("loss_head"):
        err = _jnp.square(y.astype(_jnp.float32) - loss_target)
        return 0.5 * _jnp.sum(_jnp.mean(err, axis=-1)) if err.ndim else 0.5 * err


def _adamw(w, g, m, v):
    m = ADAM_B1 * m + (1.0 - ADAM_B1) * g
    v = ADAM_B2 * v + (1.0 - ADAM_B2) * _jnp.square(g)
    m_hat = m / (1.0 - ADAM_B1 ** ADAM_STEP)
    v_hat = v / (1.0 - ADAM_B2 ** ADAM_STEP)
    delta = -ADAM_LR * (m_hat / (_jnp.sqrt(v_hat) + ADAM_EPS) + ADAM_WD * w)
    return delta, m, v


def reference(x, meta_tokens, g_pre_mix, w_in, lru_conv_w, lru_conv_b, lru_wa, lru_ba, lru_wx, lru_bx, lru_lambda, conv_w, conv_b, conv_ln_g, conv_ln_b, g_out_lru, g_out_conv, w_out, g_post_mix, g_pre_ffn, w_up, ffn_conv_w, ffn_conv_b, w_down, g_post_ffn, loss_target, m_meta_tokens, m_g_pre_mix, m_w_in, m_lru_conv_w, m_lru_conv_b, m_lru_wa, m_lru_ba, m_lru_wx, m_lru_bx, m_lru_lambda, m_conv_w, m_conv_b, m_conv_ln_g, m_conv_ln_b, m_g_out_lru, m_g_out_conv, m_w_out, m_g_post_mix, m_g_pre_ffn, m_w_up, m_ffn_conv_w, m_ffn_conv_b, m_w_down, m_g_post_ffn, v_meta_tokens, v_g_pre_mix, v_w_in, v_lru_conv_w, v_lru_conv_b, v_lru_wa, v_lru_ba, v_lru_wx, v_lru_bx, v_lru_lambda, v_conv_w, v_conv_b, v_conv_ln_g, v_conv_ln_b, v_g_out_lru, v_g_out_conv, v_w_out, v_g_post_mix, v_g_pre_ffn, v_w_up, v_ffn_conv_w, v_ffn_conv_b, v_w_down, v_g_post_ffn):
    given = dict(x=x, meta_tokens=meta_tokens, g_pre_mix=g_pre_mix, w_in=w_in, lru_conv_w=lru_conv_w, lru_conv_b=lru_conv_b, lru_wa=lru_wa, lru_ba=lru_ba, lru_wx=lru_wx, lru_bx=lru_bx, lru_lambda=lru_lambda, conv_w=conv_w, conv_b=conv_b, conv_ln_g=conv_ln_g, conv_ln_b=conv_ln_b, g_out_lru=g_out_lru, g_out_conv=g_out_conv, w_out=w_out, g_post_mix=g_post_mix, g_pre_ffn=g_pre_ffn, w_up=w_up, ffn_conv_w=ffn_conv_w, ffn_conv_b=ffn_conv_b, w_down=w_down, g_post_ffn=g_post_ffn, loss_target=loss_target, m_meta_tokens=m_meta_tokens, m_g_pre_mix=m_g_pre_mix, m_w_in=m_w_in, m_lru_conv_w=m_lru_conv_w, m_lru_conv_b=m_lru_conv_b, m_lru_wa=m_lru_wa, m_lru_ba=m_lru_ba, m_lru_wx=m_lru_wx, m_lru_bx=m_lru_bx, m_lru_lambda=m_lru_lambda, m_conv_w=m_conv_w, m_conv_b=m_conv_b, m_conv_ln_g=m_conv_ln_g, m_conv_ln_b=m_conv_ln_b, m_g_out_lru=m_g_out_lru, m_g_out_conv=m_g_out_conv, m_w_out=m_w_out, m_g_post_mix=m_g_post_mix, m_g_pre_ffn=m_g_pre_ffn, m_w_up=m_w_up, m_ffn_conv_w=m_ffn_conv_w, m_ffn_conv_b=m_ffn_conv_b, m_w_down=m_w_down, m_g_post_ffn=m_g_post_ffn, v_meta_tokens=v_meta_tokens, v_g_pre_mix=v_g_pre_mix, v_w_in=v_w_in, v_lru_conv_w=v_lru_conv_w, v_lru_conv_b=v_lru_conv_b, v_lru_wa=v_lru_wa, v_lru_ba=v_lru_ba, v_lru_wx=v_lru_wx, v_lru_bx=v_lru_bx, v_lru_lambda=v_lru_lambda, v_conv_w=v_conv_w, v_conv_b=v_conv_b, v_conv_ln_g=v_conv_ln_g, v_conv_ln_b=v_conv_ln_b, v_g_out_lru=v_g_out_lru, v_g_out_conv=v_g_out_conv, v_w_out=v_w_out, v_g_post_mix=v_g_post_mix, v_g_pre_ffn=v_g_pre_ffn, v_w_up=v_w_up, v_ffn_conv_w=v_ffn_conv_w, v_ffn_conv_b=v_ffn_conv_b, v_w_down=v_w_down, v_g_post_ffn=v_g_post_ffn)
    weights = {n: given[n] for n in TWIN_WEIGHTS}
    shared = {n: given[n] for n in SHARED_INPUTS}
    per_example = {n: given[n] for n in ['x']}
    grad_fn = _jax.value_and_grad(_loss, argnums=(0, 1))

    def one_microbatch(ex, loss_target):
        ex = dict(ex)
        diff = ex.pop(TWIN_DIFF_INPUT)
        return grad_fn(weights, diff, {**shared, **ex}, loss_target)

    if N_MICROBATCH == 1:
        loss, (grad_w, grad_x) = one_microbatch(per_example, given["loss_target"])
    else:
        def body(carry, xs):
            loss_sum, grad_sum = carry
            l_k, (gw_k, gx_k) = one_microbatch(xs[0], xs[1])
            with _jax.named_scope("update"):
                return (loss_sum + l_k, _jax.tree.map(_jnp.add, grad_sum, gw_k)), gx_k

        init = (_jnp.zeros((), _jnp.float32), _jax.tree.map(_jnp.zeros_like, weights))
        (loss, grad_w), grad_x = _jax.lax.scan(body, init, (per_example, given["loss_target"]))
    with _jax.named_scope("update"):
        delta_w, new_m, new_v = {}, {}, {}
        for n in TWIN_WEIGHTS:
            delta_w[n], new_m[n], new_v[n] = _adamw(weights[n], grad_w[n], given["m_" + n], given["v_" + n])
    return (loss, grad_x, *[grad_w[n] for n in TWIN_WEIGHTS], *[delta_w[n] for n in TWIN_WEIGHTS],
            *[new_m[n] for n in TWIN_WEIGHTS], *[new_v[n] for n in TWIN_WEIGHTS])
```

```python
import math

import jax
import jax.numpy as jnp
from jax import lax
from jax.experimental import pallas as pl
from jax.experimental.pallas import tpu as pltpu

F32 = jnp.float32
_MXU = jnp.bfloat16
_TM_MM = 768
_TM_SEQ = 256
_NB = 768
_VMEM_LIMIT = 56 * 1024 * 1024
_ADAM_BLOCK_ELEMS = 128 * 1024
_LANES = 128
_N_DEV = 8

EPS = 1e-6
N_META = 16
LRU_C = 8.0
CONV_GROUPS = 4
HALO_S = 8
HALO_L = 32
ADAM_LR, ADAM_B1, ADAM_B2, ADAM_EPS, ADAM_WD, ADAM_STEP = 0.001, 0.9, 0.999, 1e-08, 0.01, 10
_GELU_K0 = math.sqrt(2.0 / math.pi)
_GELU_K1 = 0.044715

W_NAMES = ['meta_tokens', 'g_pre_mix', 'w_in', 'lru_conv_w', 'lru_conv_b', 'lru_wa', 'lru_ba', 'lru_wx', 'lru_bx',
           'lru_lambda', 'conv_w', 'conv_b', 'conv_ln_g', 'conv_ln_b', 'g_out_lru', 'g_out_conv', 'w_out',
           'g_post_mix', 'g_pre_ffn', 'w_up', 'ffn_conv_w', 'ffn_conv_b', 'w_down', 'g_post_ffn']
BIG = ['w_in', 'w_out', 'w_up', 'w_down']
SMALL_SHARDED = ['meta_tokens', 'lru_conv_w', 'conv_w', 'ffn_conv_w']
REPLICATED = [n for n in W_NAMES if n not in BIG and n not in SMALL_SHARDED]


def _params():
    return pltpu.CompilerParams(vmem_limit_bytes=_VMEM_LIMIT)


def _pick(n, pref):
    if n <= pref:
        return n
    best = None
    for b in range(_LANES, pref + 1, _LANES):
        if n % b == 0:
            best = b
    assert best is not None, (n, pref)
    return best


def _pick_rows(n, pref):
    if n <= pref:
        return n
    best = None
    for b in range(8, pref + 1, 8):
        if n % b == 0:
            best = b
    assert best is not None, (n, pref)
    return best


def _lspec(l, *dims):
    zeros = (0,) * len(dims)
    return pl.BlockSpec((None,) + tuple(dims), lambda *_: (l,) + zeros)


def _sds(shape, dtype):
    return jax.ShapeDtypeStruct(tuple(shape), dtype)


def _rms_fwd(x, g):
    r = lax.rsqrt(jnp.mean(x * x, axis=-1, keepdims=True) + EPS)
    return (x * r) * g


def _rms_bwd(x, g, dy):
    r = lax.rsqrt(jnp.mean(x * x, axis=-1, keepdims=True) + EPS)
    xh = x * r
    dg = jnp.sum(dy * xh, axis=0, keepdims=True)
    dxh = dy * g
    dx = r * (dxh - xh * jnp.mean(dxh * xh, axis=-1, keepdims=True))
    return dx, dg


def _gelu(x):
    t = jnp.tanh(_GELU_K0 * (x + _GELU_K1 * (x * x * x)))
    return 0.5 * x * (1.0 + t), t


def _gelu_grad(x, t):
    return 0.5 * (1.0 + t) + 0.5 * x * (1.0 - t * t) * (_GELU_K0 * (1.0 + 3.0 * _GELU_K1 * x * x))


def _log1p(e):
    u = 1.0 + e
    return jnp.where(u == 1.0, e, jnp.log(u) * (e / (u - 1.0)))


def _softplus(z):
    return jnp.maximum(z, 0.0) + _log1p(jnp.exp(-jnp.abs(z)))


def _one_minus_exp(x):
    p = -x * (1.0 + x * (0.5 + x * (1.0 / 6 + x * (1.0 / 24 + x * (1.0 / 120 + x * (1.0 / 720))))))
    return jnp.where(x > -0.125, p, 1.0 - jnp.exp(x))


def _shift_down(x, s, fill, row):
    return jnp.where(row >= s, pltpu.roll(x, s, 0), fill)


def _shift_up(x, s, fill, row):
    n = x.shape[0]
    return jnp.where(row < n - s, pltpu.roll(x, n - s, 0), fill)


def _bd_mm(xb, w_ref, heads, hd):
    return jnp.concatenate(
        [jnp.dot(xb[:, h * hd:(h + 1) * hd], w_ref[h].astype(_MXU), preferred_element_type=F32)
         for h in range(heads)], axis=-1)


def _bd_mm_t(db, w_ref, heads, hd):
    return jnp.concatenate(
        [lax.dot_general(db[:, h * hd:(h + 1) * hd], w_ref[h].astype(_MXU), (((1,), (1,)), ((), ())),
                         preferred_element_type=F32)
         for h in range(heads)], axis=-1)


def _gates(xc, wa_ref, ba, wx_ref, bx, lam, heads, hd):
    xcb = xc.astype(_MXU)
    ra = jax.nn.sigmoid(_bd_mm(xcb, wa_ref, heads, hd) + ba)
    ri = jax.nn.sigmoid(_bd_mm(xcb, wx_ref, heads, hd) + bx)
    sp = _softplus(-lam)
    la = (-LRU_C) * ra * sp
    a = jnp.exp(la)
    mult = jnp.sqrt(_one_minus_exp(2.0 * la))
    return xcb, ra, ri, sp, a, mult


def _group_norm(cc, groups):
    gs = cc.shape[-1] // groups
    outs, rss = [], []
    for g in range(groups):
        seg = cc[:, g * gs:(g + 1) * gs]
        mu = jnp.mean(seg, axis=-1, keepdims=True)
        d = seg - mu
        rs = lax.rsqrt(jnp.mean(d * d, axis=-1, keepdims=True) + EPS)
        outs.append(d * rs)
        rss.append(rs)
    return jnp.concatenate(outs, axis=-1), rss


def _norm_mm(x, g3, w, l, name):
    T, D = x.shape
    N = w.shape[2]
    tm, nb = _TM_MM, _pick(N, _NB)

    def body(x_ref, g_ref, w_ref, o_ref, z_ref):
        @pl.when(pl.program_id(1) == 0)
        def _():
            z_ref[...] = _rms_fwd(x_ref[...], g_ref[...]).astype(z_ref.dtype)

        o_ref[...] = jnp.dot(z_ref[...], w_ref[...], preferred_element_type=F32)

    return pl.pallas_call(
        body, name=name, grid=(T // tm, N // nb),
        in_specs=[pl.BlockSpec((tm, D), lambda i, j: (i, 0)), _lspec(l, 1, D),
                  pl.BlockSpec((None, D, nb), lambda i, j: (l, 0, j))],
        out_specs=[pl.BlockSpec((tm, nb), lambda i, j: (i, j)), pl.BlockSpec((tm, D), lambda i, j: (i, 0))],
        out_shape=[_sds((T, N), F32), _sds((T, D), _MXU)],
        compiler_params=_params())(x, g3, w)


def _mixers_fwd(proj, p, l, name):
    T, D3 = proj.shape
    D = D3 // 3
    Dc = D // 2
    heads, hd = p['lru_wa'].shape[1], p['lru_wa'].shape[2]
    tm = _TM_SEQ
    kl, kc = p['lru_conv_w'].shape[1], p['conv_w'].shape[1]

    def body(proj_ref, wl_ref, bl_ref, wa_ref, ba_ref, wx_ref, bx_ref, lam_ref, wc_ref, bc_ref, lg_ref, lb_ref,
             xc_ref, hs_ref, ya_ref, cc_ref, yb_ref, xbuf, hcar, cbuf):
        @pl.when(pl.program_id(0) == 0)
        def _():
            xbuf[0:HALO_S, :] = jnp.zeros((HALO_S, D), F32)
            cbuf[0:HALO_L, :] = jnp.zeros((HALO_L, Dc), F32)
            hcar[...] = jnp.zeros((1, D), F32)

        xbuf[HALO_S:HALO_S + tm, :] = proj_ref[:, 0:D]
        xc = bl_ref[...] + wl_ref[0:1, :] * xbuf[pl.ds(HALO_S - (kl - 1), tm), :]
        for k in range(1, kl):
            xc = xc + wl_ref[k:k + 1, :] * xbuf[pl.ds(HALO_S - (kl - 1) + k, tm), :]
        xbuf[0:HALO_S, :] = xbuf[tm:tm + HALO_S, :]
        xc_ref[...] = xc
        _, ra, ri, sp, a, mult = _gates(xc, wa_ref, ba_ref[...], wx_ref, bx_ref[...], lam_ref[...], heads, hd)
        u = mult * (ri * xc)
        row = lax.broadcasted_iota(jnp.int32, (tm, D), 0)
        s = 1
        while s < tm:
            u = u + a * _shift_down(u, s, 0.0, row)
            a = a * _shift_down(a, s, 1.0, row)
            s *= 2
        hs = u + a * hcar[...]
        hs_ref[...] = hs
        hcar[...] = hs_ref[pl.ds(tm - 1, 1), :]
        gg, _ = _gelu(proj_ref[:, D:2 * D])
        ya_ref[...] = hs * gg

        cbuf[HALO_L:HALO_L + tm, :] = proj_ref[:, 2 * D:2 * D + Dc] * jax.nn.sigmoid(proj_ref[:, 2 * D + Dc:3 * D])
        cc = bc_ref[...] + wc_ref[0:1, :] * cbuf[pl.ds(HALO_L - (kc - 1), tm), :]
        for k in range(1, kc):
            cc = cc + wc_ref[k:k + 1, :] * cbuf[pl.ds(HALO_L - (kc - 1) + k, tm), :]
        cbuf[0:HALO_L, :] = cbuf[tm:tm + HALO_L, :]
        cc_ref[...] = cc
        nrm, _ = _group_norm(cc, CONV_GROUPS)
        cl = nrm * lg_ref[...] + lb_ref[...]
        yb_ref[...] = cl * jax.nn.sigmoid(cl)

    row_d = pl.BlockSpec((tm, D), lambda i: (i, 0))
    row_c = pl.BlockSpec((tm, Dc), lambda i: (i, 0))
    return pl.pallas_call(
        body, name=name, grid=(T // tm,),
        in_specs=[pl.BlockSpec((tm, D3), lambda i: (i, 0)),
                  _lspec(l, kl, D), _lspec(l, 1, D), _lspec(l, heads, hd, hd), _lspec(l, 1, D),
                  _lspec(l, heads, hd, hd), _lspec(l, 1, D), _lspec(l, 1, D),
                  _lspec(l, kc, Dc), _lspec(l, 1, Dc), _lspec(l, 1, Dc), _lspec(l, 1, Dc)],
        out_specs=[row_d, row_d, row_d, row_c, row_c],
        out_shape=[_sds((T, D), F32), _sds((T, D), F32), _sds((T, D), F32), _sds((T, Dc), F32), _sds((T, Dc), F32)],
        scratch_shapes=[pltpu.VMEM((HALO_S + tm, D), F32), pltpu.VMEM((1, D), F32), pltpu.VMEM((HALO_L + tm, Dc), F32)],
        compiler_params=_params(),
    )(proj, p['lru_conv_w'], p['lru_conv_b'], p['lru_wa'], p['lru_ba'], p['lru_wx'], p['lru_bx'], p['lru_lambda'],
      p['conv_w'], p['conv_b'], p['conv_ln_g'], p['conv_ln_b'])


def _mix_out(ya, yb, h, p, wout, l, name):
    T, D = ya.shape
    Dc = yb.shape[1]
    tm = _TM_SEQ

    def body(ya_ref, yb_ref, h_ref, gl_ref, gc_ref, gp_ref, w_ref, y_ref, o_ref, hm_ref):
        y = jnp.concatenate([_rms_fwd(ya_ref[...], gl_ref[...]), _rms_fwd(yb_ref[...], gc_ref[...])],
                            axis=-1).astype(_MXU)
        y_ref[...] = y
        o = jnp.dot(y, w_ref[...], preferred_element_type=F32)
        o_ref[...] = o
        hm_ref[...] = h_ref[...] + _rms_fwd(o, gp_ref[...])

    row_d = pl.BlockSpec((tm, D), lambda i: (i, 0))
    return pl.pallas_call(
        body, name=name, grid=(T // tm,),
        in_specs=[row_d, pl.BlockSpec((tm, Dc), lambda i: (i, 0)), row_d,
                  _lspec(l, 1, D), _lspec(l, 1, Dc), _lspec(l, 1, D), _lspec(l, D + Dc, D)],
        out_specs=[pl.BlockSpec((tm, D + Dc), lambda i: (i, 0)), row_d, row_d],
        out_shape=[_sds((T, D + Dc), _MXU), _sds((T, D), F32), _sds((T, D), F32)],
        compiler_params=_params(),
    )(ya, yb, h, p['g_out_lru'], p['g_out_conv'], p['g_post_mix'], wout)


def _ffn_fwd(up0, hmid, p, wdown, l, name):
    T, F2 = up0.shape
    Fh = F2 // 2
    D = hmid.shape[1]
    tm = _TM_SEQ
    cw = _pick(Fh, _NB)
    kf = p['ffn_conv_w'].shape[1]

    def body(up_ref, hm_ref, wf_ref, bf_ref, g_ref, wd_ref, a2_ref, f_ref, ho_ref, ubuf):
        @pl.when(pl.program_id(0) == 0)
        def _():
            ubuf[0:HALO_S, :] = jnp.zeros((HALO_S, F2), F32)

        ubuf[HALO_S:HALO_S + tm, :] = up_ref[...]

        def conv(cs):
            acc = bf_ref[:, cs]
            for k in range(kf):
                acc = acc + wf_ref[k:k + 1, cs] * ubuf[pl.ds(HALO_S - (kf - 1) + k, tm), cs]
            return acc

        f = None
        for c in range(Fh // cw):
            gs = slice(c * cw, (c + 1) * cw)
            gg, _ = _gelu(conv(gs))
            a2 = (gg * conv(slice(Fh + c * cw, Fh + (c + 1) * cw))).astype(_MXU)
            a2_ref[:, gs] = a2
            part = jnp.dot(a2, wd_ref[gs, :], preferred_element_type=F32)
            f = part if f is None else f + part
        ubuf[0:HALO_S, :] = ubuf[tm:tm + HALO_S, :]
        f_ref[...] = f
        ho_ref[...] = hm_ref[...] + _rms_fwd(f, g_ref[...])

    row_d = pl.BlockSpec((tm, D), lambda i: (i, 0))
    return pl.pallas_call(
        body, name=name, grid=(T // tm,),
        in_specs=[pl.BlockSpec((tm, F2), lambda i: (i, 0)), row_d, _lspec(l, kf, F2), _lspec(l, 1, F2),
                  _lspec(l, 1, D), _lspec(l, Fh, D)],
        out_specs=[pl.BlockSpec((tm, Fh), lambda i: (i, 0)), row_d, row_d],
        out_shape=[_sds((T, Fh), _MXU), _sds((T, D), F32), _sds((T, D), F32)],
        scratch_shapes=[pltpu.VMEM((HALO_S + tm, F2), F32)],
        compiler_params=_params(),
    )(up0, hmid, p['ffn_conv_w'], p['ffn_conv_b'], p['g_post_ffn'], wdown)


def _loss_head(h, tgt, n_real, name):
    T, D = h.shape
    tm = _TM_SEQ

    def body(h_ref, t_ref, loss_ref, dh_ref):
        i = pl.program_id(0)

        @pl.when(i == 0)
        def _():
            loss_ref[...] = jnp.zeros_like(loss_ref)

        row = i * tm + lax.broadcasted_iota(jnp.int32, (tm, D), 0)
        e = jnp.where((row >= N_META) & (row < N_META + n_real), h_ref[...] - t_ref[...], 0.0)
        dh_ref[...] = e * (1.0 / D)
        loss_ref[...] += 0.5 * jnp.sum(jnp.mean(e * e, axis=-1, keepdims=True), axis=0, keepdims=True)

    row_d = pl.BlockSpec((tm, D), lambda i: (i, 0))
    return pl.pallas_call(
        body, name=name, grid=(T // tm,),
        in_specs=[row_d, row_d],
        out_specs=[pl.BlockSpec((8, _LANES), lambda i: (0, 0)), row_d],
        out_shape=[_sds((8, _LANES), F32), _sds((T, D), F32)],
        compiler_params=_params())(h, tgt)


def _ffn_bwd(dh, f, up0, p, wdown, l, name):
    T, F2 = up0.shape
    Fh = F2 // 2
    D = dh.shape[1]
    tm = _TM_SEQ
    nt = T // tm
    cw = _pick(Fh, _NB)
    kf = p['ffn_conv_w'].shape[1]
    per8 = tm // 8

    def body(dh_ref, f_ref, up_ref, upp_ref, wf_ref, bf_ref, g_ref, wd_ref,
             df_ref, dup_ref, dwf_ref, dbf_ref, dg_ref, ubuf, dbuf):
        i = pl.program_id(0)
        r = nt - 1 - i

        @pl.when(i == 0)
        def _():
            dbuf[tm:tm + HALO_S, :] = jnp.zeros((HALO_S, F2), F32)
            dwf_ref[...] = jnp.zeros_like(dwf_ref)
            dbf_ref[...] = jnp.zeros_like(dbf_ref)
            dg_ref[...] = jnp.zeros_like(dg_ref)

        df, dg = _rms_bwd(f_ref[...], g_ref[...], dh_ref[...])
        dg_ref[...] += dg
        dfb = df.astype(_MXU)
        df_ref[...] = dfb
        ubuf[0:HALO_S, :] = jnp.where(r == 0, 0.0, upp_ref[...])
        ubuf[HALO_S:HALO_S + tm, :] = up_ref[...]

        def conv(cs):
            acc = bf_ref[:, cs]
            for k in range(kf):
                acc = acc + wf_ref[k:k + 1, cs] * ubuf[pl.ds(HALO_S - (kf - 1) + k, tm), cs]
            return acc

        for c in range(Fh // cw):
            gs = slice(c * cw, (c + 1) * cw)
            us = slice(Fh + c * cw, Fh + (c + 1) * cw)
            ug = conv(gs)
            gg, t = _gelu(ug)
            da2 = lax.dot_general(dfb, wd_ref[gs, :], (((1,), (1,)), ((), ())), preferred_element_type=F32)
            dbuf[0:tm, gs] = da2 * conv(us) * _gelu_grad(ug, t)
            dbuf[0:tm, us] = da2 * gg
        for c in range(F2 // cw):
            cs = slice(c * cw, (c + 1) * cw)
            upc = up_ref[:, cs]
            dup = None
            for k in range(kf):
                dsh = dbuf[pl.ds(kf - 1 - k, tm), cs]
                term = wf_ref[k:k + 1, cs] * dsh
                dup = term if dup is None else dup + term
                dwf_ref[k:k + 1, cs] += jnp.sum(dsh * upc, axis=0, keepdims=True)
            dbf_ref[:, cs] += jnp.sum(dbuf[0:tm, cs], axis=0, keepdims=True)
            dup_ref[:, cs] = dup.astype(_MXU)
        dbuf[tm:tm + HALO_S, :] = dbuf[0:HALO_S, :]

    rev_d = pl.BlockSpec((tm, D), lambda i: (nt - 1 - i, 0))
    rev_f = pl.BlockSpec((tm, F2), lambda i: (nt - 1 - i, 0))
    prev8 = pl.BlockSpec((8, F2), lambda i: (jnp.maximum((nt - 1 - i) * per8 - 1, 0), 0))
    full = lambda *s: pl.BlockSpec(s, lambda i: (0,) * len(s))
    return pl.pallas_call(
        body, name=name, grid=(nt,),
        in_specs=[rev_d, rev_d, rev_f, prev8, _lspec(l, kf, F2), _lspec(l, 1, F2), _lspec(l, 1, D), _lspec(l, Fh, D)],
        out_specs=[rev_d, rev_f, full(kf, F2), full(1, F2), full(1, D)],
        out_shape=[_sds((T, D), _MXU), _sds((T, F2), _MXU), _sds((kf, F2), F32), _sds((1, F2), F32), _sds((1, D), F32)],
        scratch_shapes=[pltpu.VMEM((HALO_S + tm, F2), F32), pltpu.VMEM((tm + HALO_S, F2), F32)],
        compiler_params=_params(),
    )(dh, f, up0, up0, p['ffn_conv_w'], p['ffn_conv_b'], p['g_post_ffn'], wdown)


def _mm_nt_norm_bwd(dy, w, x, g3, dres, l, name):
    T, N = dy.shape
    D = x.shape[1]
    tm, nb = _TM_MM, _pick(N, _NB)
    nj = N // nb

    def body(dy_ref, w_ref, x_ref, g_ref, dr_ref, dh_ref, dg_ref, acc):
        i, j = pl.program_id(0), pl.program_id(1)

        @pl.when((i == 0) & (j == 0))
        def _():
            dg_ref[...] = jnp.zeros_like(dg_ref)

        part = lax.dot_general(dy_ref[...], w_ref[...], (((1,), (1,)), ((), ())), preferred_element_type=F32)

        @pl.when(j == 0)
        def _():
            acc[...] = part

        @pl.when(j > 0)
        def _():
            acc[...] += part

        @pl.when(j == nj - 1)
        def _():
            dx, dg = _rms_bwd(x_ref[...], g_ref[...], acc[...])
            dh_ref[...] = dr_ref[...] + dx
            dg_ref[...] += dg

    row_d = pl.BlockSpec((tm, D), lambda i, j: (i, 0))
    return pl.pallas_call(
        body, name=name, grid=(T // tm, nj),
        in_specs=[pl.BlockSpec((tm, nb), lambda i, j: (i, j)), pl.BlockSpec((None, D, nb), lambda i, j: (l, 0, j)),
                  row_d, _lspec(l, 1, D), row_d],
        out_specs=[row_d, pl.BlockSpec((1, D), lambda i, j: (0, 0))],
        out_shape=[_sds((T, D), F32), _sds((1, D), F32)],
        scratch_shapes=[pltpu.VMEM((tm, D), F32)],
        compiler_params=_params())(dy, w, x, g3, dres)


def _mm_tn(xs, dy, l, n_layers, buf, name):
    T, K = xs.shape
    N = dy.shape[1]
    tm = _TM_MM
    kb = K if K <= 1024 else _pick(K, _NB)
    nb = _pick(N, _NB)

    def body(*refs):
        x_ref, dy_ref, o_ref = refs[0], refs[1], refs[-1]

        @pl.when(pl.program_id(2) == 0)
        def _():
            o_ref[...] = jnp.zeros_like(o_ref)

        o_ref[...] += lax.dot_general(x_ref[...], dy_ref[...], (((0,), (0,)), ((), ())), preferred_element_type=F32)

    in_specs = [pl.BlockSpec((tm, kb), lambda a, b, t: (t, a)), pl.BlockSpec((tm, nb), lambda a, b, t: (t, b))]
    args = [xs, dy]
    aliases = {}
    if buf is not None:
        in_specs.append(pl.BlockSpec(memory_space=pl.ANY))
        args.append(buf)
        aliases = {2: 0}
    return pl.pallas_call(
        body, name=name, grid=(K // kb, N // nb, T // tm),
        in_specs=in_specs,
        out_specs=pl.BlockSpec((None, kb, nb), lambda a, b, t: (l, a, b)),
        out_shape=_sds((n_layers, K, N), F32),
        input_output_aliases=aliases,
        compiler_params=_params())(*args)


def _mix_bwd(dhm, o, ya, yb, p, wout, l, name):
    T, D = ya.shape
    Dc = yb.shape[1]
    tm = _TM_SEQ

    def body(dh_ref, o_ref, ya_ref, yb_ref, gp_ref, gl_ref, gc_ref, w_ref,
             do_ref, dya_ref, dyb_ref, dgp_ref, dgl_ref, dgc_ref):
        @pl.when(pl.program_id(0) == 0)
        def _():
            dgp_ref[...] = jnp.zeros_like(dgp_ref)
            dgl_ref[...] = jnp.zeros_like(dgl_ref)
            dgc_ref[...] = jnp.zeros_like(dgc_ref)

        do, dgp = _rms_bwd(o_ref[...], gp_ref[...], dh_ref[...])
        dgp_ref[...] += dgp
        dob = do.astype(_MXU)
        do_ref[...] = dob
        dy = lax.dot_general(dob, w_ref[...], (((1,), (1,)), ((), ())), preferred_element_type=F32)
        dya, dgl = _rms_bwd(ya_ref[...], gl_ref[...], dy[:, 0:D])
        dyb, dgc = _rms_bwd(yb_ref[...], gc_ref[...], dy[:, D:D + Dc])
        dya_ref[...] = dya
        dyb_ref[...] = dyb
        dgl_ref[...] += dgl
        dgc_ref[...] += dgc

    row_d = pl.BlockSpec((tm, D), lambda i: (i, 0))
    row_c = pl.BlockSpec((tm, Dc), lambda i: (i, 0))
    full = lambda *s: pl.BlockSpec(s, lambda i: (0,) * len(s))
    return pl.pallas_call(
        body, name=name, grid=(T // tm,),
        in_specs=[row_d, row_d, row_d, row_c, _lspec(l, 1, D), _lspec(l, 1, D), _lspec(l, 1, Dc), _lspec(l, D + Dc, D)],
        out_specs=[row_d, row_d, row_c, full(1, D), full(1, D), full(1, Dc)],
        out_shape=[_sds((T, D), _MXU), _sds((T, D), F32), _sds((T, Dc), F32),
                   _sds((1, D), F32), _sds((1, D), F32), _sds((1, Dc), F32)],
        compiler_params=_params(),
    )(dhm, o, ya, yb, p['g_post_mix'], p['g_out_lru'], p['g_out_conv'], wout)


def _mixers_bwd(dya, dyb, proj, xc, hs, cc, p, l, name):
    T, D3 = proj.shape
    D = D3 // 3
    Dc = D // 2
    heads, hd = p['lru_wa'].shape[1], p['lru_wa'].shape[2]
    tm = _TM_SEQ
    nt = T // tm
    per8 = tm // 8
    kl, kc = p['lru_conv_w'].shape[1], p['conv_w'].shape[1]

    def body(dya_ref, dyb_ref, proj_ref, xc_ref, hs_ref, hsp_ref, cc_ref,
             wl_ref, wa_ref, ba_ref, wx_ref, bx_ref, lam_ref, wc_ref, lg_ref, lb_ref,
             dproj_ref, dwl_ref, dbl_ref, dwa_ref, dba_ref, dwx_ref, dbx_ref, dlam_ref,
             dwc_ref, dbc_ref, dlg_ref, dlb_ref, gcar, dxbuf, dcbuf, tmp):
        i = pl.program_id(0)
        r = nt - 1 - i

        @pl.when(i == 0)
        def _():
            gcar[...] = jnp.zeros((1, D), F32)
            dxbuf[tm:tm + HALO_S, :] = jnp.zeros((HALO_S, D), F32)
            dcbuf[tm:tm + HALO_L, :] = jnp.zeros((HALO_L, Dc), F32)
            for ref in (dwl_ref, dbl_ref, dwa_ref, dba_ref, dwx_ref, dbx_ref, dlam_ref, dwc_ref, dbc_ref,
                        dlg_ref, dlb_ref):
                ref[...] = jnp.zeros_like(ref)

        dya_v = dya_ref[...]
        hs = hs_ref[...]
        gl = proj_ref[:, D:2 * D]
        gg, tg = _gelu(gl)
        dproj_ref[:, D:2 * D] = (dya_v * hs * _gelu_grad(gl, tg)).astype(_MXU)
        dhs = dya_v * gg
        xc = xc_ref[...]
        lam = lam_ref[...]
        xcb, ra, ri, sp, a, mult = _gates(xc, wa_ref, ba_ref[...], wx_ref, bx_ref[...], lam, heads, hd)
        row = lax.broadcasted_iota(jnp.int32, (tm, D), 0)
        cf = _shift_up(a, 1, 1.0, row)
        g = dhs
        s = 1
        while s < tm:
            g = g + cf * _shift_up(g, s, 0.0, row)
            cf = cf * _shift_up(cf, s, 1.0, row)
            s *= 2
        g = g + cf * gcar[...]
        tmp[...] = a * g
        gcar[...] = tmp[0:1, :]
        hprev = jnp.where(r == 0, 0.0, hsp_ref[7:8, :])
        da = g * jnp.where(row >= 1, pltpu.roll(hs, 1, 0), hprev)
        gx = g * xc
        dxc = g * mult * ri
        dla = da * a - (gx * ri) * (a * a) / mult
        dlam_ref[...] += jnp.sum(dla * ra, axis=0, keepdims=True) * (LRU_C * jax.nn.sigmoid(-lam))
        dpa = (dla * ((-LRU_C) * sp)) * ra * (1.0 - ra)
        dpx = (gx * mult) * ri * (1.0 - ri)
        dba_ref[...] += jnp.sum(dpa, axis=0, keepdims=True)
        dbx_ref[...] += jnp.sum(dpx, axis=0, keepdims=True)
        dpab, dpxb = dpa.astype(_MXU), dpx.astype(_MXU)
        for h in range(heads):
            hsl = slice(h * hd, (h + 1) * hd)
            dwa_ref[h] += lax.dot_general(xcb[:, hsl], dpab[:, hsl], (((0,), (0,)), ((), ())),
                                          preferred_element_type=F32)
            dwx_ref[h] += lax.dot_general(xcb[:, hsl], dpxb[:, hsl], (((0,), (0,)), ((), ())),
                                          preferred_element_type=F32)
        dxc = dxc + _bd_mm_t(dpab, wa_ref, heads, hd) + _bd_mm_t(dpxb, wx_ref, heads, hd)
        dbl_ref[...] += jnp.sum(dxc, axis=0, keepdims=True)
        dxbuf[0:tm, :] = dxc
        xl = proj_ref[:, 0:D]
        dxl = None
        for k in range(kl):
            dsh = dxbuf[pl.ds(kl - 1 - k, tm), :]
            term = wl_ref[k:k + 1, :] * dsh
            dxl = term if dxl is None else dxl + term
            dwl_ref[k:k + 1, :] += jnp.sum(dsh * xl, axis=0, keepdims=True)
        dxbuf[tm:tm + HALO_S, :] = dxbuf[0:HALO_S, :]
        dproj_ref[:, 0:D] = dxl.astype(_MXU)

        ca = proj_ref[:, 2 * D:2 * D + Dc]
        sg = jax.nn.sigmoid(proj_ref[:, 2 * D + Dc:3 * D])
        cg = ca * sg
        nrm, rss = _group_norm(cc_ref[...], CONV_GROUPS)
        lg = lg_ref[...]
        cl = nrm * lg + lb_ref[...]
        sc = jax.nn.sigmoid(cl)
        dcl = dyb_ref[...] * (sc * (1.0 + cl * (1.0 - sc)))
        dlg_ref[...] += jnp.sum(dcl * nrm, axis=0, keepdims=True)
        dlb_ref[...] += jnp.sum(dcl, axis=0, keepdims=True)
        dnrm = dcl * lg
        gsz = Dc // CONV_GROUPS
        parts = []
        for gi in range(CONV_GROUPS):
            sl = slice(gi * gsz, (gi + 1) * gsz)
            dn, nn = dnrm[:, sl], nrm[:, sl]
            parts.append(rss[gi] * (dn - jnp.mean(dn, axis=-1, keepdims=True)
                                    - nn * jnp.mean(dn * nn, axis=-1, keepdims=True)))
        dcc = jnp.concatenate(parts, axis=-1)
        dbc_ref[...] += jnp.sum(dcc, axis=0, keepdims=True)
        dcbuf[0:tm, :] = dcc
        dcg = None
        for k in range(kc):
            dsh = dcbuf[pl.ds(kc - 1 - k, tm), :]
            term = wc_ref[k:k + 1, :] * dsh
            dcg = term if dcg is None else dcg + term
            dwc_ref[k:k + 1, :] += jnp.sum(dsh * cg, axis=0, keepdims=True)
        dcbuf[tm:tm + HALO_L, :] = dcbuf[0:HALO_L, :]
        dproj_ref[:, 2 * D:2 * D + Dc] = (dcg * sg).astype(_MXU)
        dproj_ref[:, 2 * D + Dc:3 * D] = (dcg * ca * sg * (1.0 - sg)).astype(_MXU)

    rev_d = pl.BlockSpec((tm, D), lambda i: (nt - 1 - i, 0))
    rev_c = pl.BlockSpec((tm, Dc), lambda i: (nt - 1 - i, 0))
    rev_p = pl.BlockSpec((tm, D3), lambda i: (nt - 1 - i, 0))
    prev8 = pl.BlockSpec((8, D), lambda i: (jnp.maximum((nt - 1 - i) * per8 - 1, 0), 0))
    full = lambda *s: pl.BlockSpec(s, lambda i: (0,) * len(s))
    return pl.pallas_call(
        body, name=name, grid=(nt,),
        in_specs=[rev_d, rev_c, rev_p, rev_d, rev_d, prev8, rev_c,
                  _lspec(l, kl, D), _lspec(l, heads, hd, hd), _lspec(l, 1, D), _lspec(l, heads, hd, hd),
                  _lspec(l, 1, D), _lspec(l, 1, D), _lspec(l, kc, Dc), _lspec(l, 1, Dc), _lspec(l, 1, Dc)],
        out_specs=[rev_p, full(kl, D), full(1, D), full(heads, hd, hd), full(1, D), full(heads, hd, hd), full(1, D),
                   full(1, D), full(kc, Dc), full(1, Dc), full(1, Dc), full(1, Dc)],
        out_shape=[_sds((T, D3), _MXU), _sds((kl, D), F32), _sds((1, D), F32), _sds((heads, hd, hd), F32),
                   _sds((1, D), F32), _sds((heads, hd, hd), F32), _sds((1, D), F32), _sds((1, D), F32),
                   _sds((kc, Dc), F32), _sds((1, Dc), F32), _sds((1, Dc), F32), _sds((1, Dc), F32)],
        scratch_shapes=[pltpu.VMEM((1, D), F32), pltpu.VMEM((tm + HALO_S, D), F32),
                        pltpu.VMEM((tm + HALO_L, Dc), F32), pltpu.VMEM((tm, D), F32)],
        compiler_params=_params(),
    )(dya, dyb, proj, xc, hs, hs, cc, p['lru_conv_w'], p['lru_wa'], p['lru_ba'], p['lru_wx'], p['lru_bx'],
      p['lru_lambda'], p['conv_w'], p['conv_ln_g'], p['conv_ln_b'])


_MESH = pl.DeviceIdType.MESH
_ANY = pl.BlockSpec(memory_space=pl.ANY)


def _all_gather(blocks, name):
    n = len(blocks)

    def body(*refs):
        ins, outs = refs[:n], refs[n:2 * n]
        send_sems, recv_sems, local_sems = refs[2 * n:]
        x, y, c = lax.axis_index("x"), lax.axis_index("y"), lax.axis_index("c")
        me, sibling = (x, y, c), (x, y, 1 - c)
        chips = [(1 - x, y), (x, 1 - y), (1 - x, 1 - y)]
        started = []
        for a in range(n):
            def slot(dev, a=a):
                return outs[a].at[4 * dev[0] + 2 * dev[1] + dev[2]]

            def copy(k, block, to, src=None, a=a, slot=slot):
                return pltpu.make_async_remote_copy(
                    src_ref=slot(block) if src is None else src, dst_ref=slot(block),
                    send_sem=send_sems.at[7 * a + k], recv_sem=recv_sems.at[7 * a + k],
                    device_id=to, device_id_type=_MESH)

            mine = pltpu.make_async_copy(ins[a], slot(me), local_sems.at[a])
            mine.start()
            first = [copy(0, me, sibling, src=ins[a])]
            first += [copy(1 + j, me, (*chip, c), src=ins[a]) for j, chip in enumerate(chips)]
            for cp in first:
                cp.start()
            started.append((mine, first, copy))
        for a in range(n):
            mine, first, copy = started[a]
            passed = [copy(4 + j, (*chip, c), sibling) for j, chip in enumerate(chips)]
            for j, chip in enumerate(chips):
                copy(1 + j, (*chip, c), me).wait_recv()
                passed[j].start()
            copy(0, sibling, me).wait_recv()
            for j, chip in enumerate(chips):
                copy(4 + j, (*chip, 1 - c), me).wait_recv()
            for cp in first + passed:
                cp.wait_send()
            mine.wait()

    return pl.pallas_call(
        body, name=name,
        in_specs=[_ANY] * n, out_specs=[_ANY] * n,
        out_shape=[_sds((_N_DEV,) + b.shape, b.dtype) for b in blocks],
        scratch_shapes=[pltpu.SemaphoreType.DMA((7 * n,)), pltpu.SemaphoreType.DMA((7 * n,)),
                        pltpu.SemaphoreType.DMA((n,))],
    )(*blocks)


def _exchange(d_in, d_out, d_up, d_down, small, name):
    arrs = [d_in, d_out, d_up, d_down, small]
    n = len(arrs)
    col_sharded = [True, False, True, False]
    piece_shapes = []
    for a in range(4):
        L, K, N = arrs[a].shape
        piece_shapes.append((L, K, N // _N_DEV) if col_sharded[a] else (L, K // _N_DEV, N))
    piece_shapes.append(small.shape[1:])

    def body(*refs):
        ins, outs = refs[:n], refs[n:2 * n]
        send_sems, recv_sems, local_sems = refs[2 * n:]
        me = 4 * lax.axis_index("x") + 2 * lax.axis_index("y") + lax.axis_index("c")

        def piece(a, j):
            if a == 4:
                return ins[a].at[j]
            w = piece_shapes[a][2] if col_sharded[a] else piece_shapes[a][1]
            if col_sharded[a]:
                return ins[a].at[:, :, pl.ds(j * w, w)]
            return ins[a].at[:, pl.ds(j * w, w), :]

        def remote(a, j):
            return pltpu.make_async_remote_copy(
                src_ref=piece(a, j), dst_ref=outs[a].at[me],
                send_sem=send_sems.at[_N_DEV * a + j], recv_sem=recv_sems.at[_N_DEV * a + me],
                device_id=(j >> 2, (j >> 1) & 1, j & 1), device_id_type=_MESH)

        def arrival(a, s):
            return pltpu.make_async_remote_copy(
                src_ref=piece(a, s), dst_ref=outs[a].at[s],
                send_sem=send_sems.at[_N_DEV * a + s], recv_sem=recv_sems.at[_N_DEV * a + s],
                device_id=(s >> 2, (s >> 1) & 1, s & 1), device_id_type=_MESH)

        def local(a, j):
            return pltpu.make_async_copy(piece(a, j), outs[a].at[j], local_sems.at[a])

        for j in range(_N_DEV):
            for a in range(n):
                @pl.when(me != j)
                def _(a=a, j=j):
                    remote(a, j).start()

                @pl.when(me == j)
                def _(a=a, j=j):
                    local(a, j).start()
        for j in range(_N_DEV):
            for a in range(n):
                @pl.when(me != j)
                def _(a=a, j=j):
                    arrival(a, j).wait_recv()
                    remote(a, j).wait_send()

                @pl.when(me == j)
                def _(a=a, j=j):
                    local(a, j).wait()

    return pl.pallas_call(
        body, name=name,
        in_specs=[_ANY] * n, out_specs=[_ANY] * n,
        out_shape=[_sds((_N_DEV,) + tuple(s), F32) for s in piece_shapes],
        scratch_shapes=[pltpu.SemaphoreType.DMA((_N_DEV * n,)), pltpu.SemaphoreType.DMA((_N_DEV * n,)),
                        pltpu.SemaphoreType.DMA((n,))],
    )(*arrs)


def _sum_sources(recv, name):
    _, R, C = recv.shape
    rb = _pick_rows(R, 1024)

    def body(r_ref, o_ref):
        acc = r_ref[0]
        for s in range(1, _N_DEV):
            acc = acc + r_ref[s]
        o_ref[...] = acc

    return pl.pallas_call(
        body, name=name, grid=(R // rb,),
        in_specs=[pl.BlockSpec((_N_DEV, rb, C), lambda i: (0, i, 0))],
        out_specs=pl.BlockSpec((rb, C), lambda i: (i, 0)),
        out_shape=_sds((R, C), F32), compiler_params=_params())(recv)


def _adamw(g, w, m, v, name):
    R, C = w.shape
    summed = g.ndim == 3
    rb = _pick_rows(R, max(8, min(512, _ADAM_BLOCK_ELEMS // C)))
    c1 = 1.0 - ADAM_B1 ** ADAM_STEP
    c2 = 1.0 - ADAM_B2 ** ADAM_STEP

    def body(g_ref, w_ref, m_ref, v_ref, go_ref, d_ref, mo_ref, vo_ref):
        if summed:
            gv = g_ref[0]
            for s in range(1, _N_DEV):
                gv = gv + g_ref[s]
        else:
            gv = g_ref[...]
        go_ref[...] = gv
        mn = ADAM_B1 * m_ref[...] + (1.0 - ADAM_B1) * gv
        vn = ADAM_B2 * v_ref[...] + (1.0 - ADAM_B2) * (gv * gv)
        mo_ref[...] = mn
        vo_ref[...] = vn
        d_ref[...] = (-ADAM_LR) * ((mn / c1) / (jnp.sqrt(vn / c2) + ADAM_EPS) + ADAM_WD * w_ref[...])

    blk = pl.BlockSpec((rb, C), lambda i: (i, 0))
    gspec = pl.BlockSpec((_N_DEV, rb, C), lambda i: (0, i, 0)) if summed else blk
    return pl.pallas_call(
        body, name=name, grid=(R // rb,),
        in_specs=[gspec, blk, blk, blk], out_specs=[blk, blk, blk, blk],
        out_shape=[_sds((R, C), F32)] * 4, compiler_params=_params())(g, w, m, v)


def _pack_rows(flat_parts, dtype, row_mult):
    flat = jnp.concatenate([f.reshape(-1).astype(dtype) for f in flat_parts])
    n = flat.shape[0]
    per = _LANES * row_mult
    padded = -(-n // per) * per
    if padded != n:
        flat = jnp.concatenate([flat, jnp.zeros((padded - n,), dtype)])
    return flat.reshape(-1, _LANES)


def _unpack(flat, shapes):
    out, off = [], 0
    for s in shapes:
        n = math.prod(s)
        out.append(flat[off:off + n].reshape(s))
        off += n
    return out


def _to_pieces(full):
    n = full.shape[-1] // _N_DEV
    t = full.reshape(full.shape[:-1] + (_N_DEV, n))
    return jnp.moveaxis(t, -2, 0).reshape(_N_DEV, -1)


def _from_gathered(seg, shard_shape, axis):
    t = seg.reshape((_N_DEV,) + tuple(shard_shape))
    t = jnp.moveaxis(t, 0, axis)
    shape = list(shard_shape)
    shape[axis] *= _N_DEV
    return t.reshape(shape)


def kernel(x, meta_tokens, g_pre_mix, w_in, lru_conv_w, lru_conv_b, lru_wa, lru_ba, lru_wx, lru_bx, lru_lambda, conv_w, conv_b, conv_ln_g, conv_ln_b, g_out_lru, g_out_conv, w_out, g_post_mix, g_pre_ffn, w_up, ffn_conv_w, ffn_conv_b, w_down, g_post_ffn, loss_target, m_meta_tokens, m_g_pre_mix, m_w_in, m_lru_conv_w, m_lru_conv_b, m_lru_wa, m_lru_ba, m_lru_wx, m_lru_bx, m_lru_lambda, m_conv_w, m_conv_b, m_conv_ln_g, m_conv_ln_b, m_g_out_lru, m_g_out_conv, m_w_out, m_g_post_mix, m_g_pre_ffn, m_w_up, m_ffn_conv_w, m_ffn_conv_b, m_w_down, m_g_post_ffn, v_meta_tokens, v_g_pre_mix, v_w_in, v_lru_conv_w, v_lru_conv_b, v_lru_wa, v_lru_ba, v_lru_wx, v_lru_bx, v_lru_lambda, v_conv_w, v_conv_b, v_conv_ln_g, v_conv_ln_b, v_g_out_lru, v_g_out_conv, v_w_out, v_g_post_mix, v_g_pre_ffn, v_w_up, v_ffn_conv_w, v_ffn_conv_b, v_w_down, v_g_post_ffn):
    given = dict(locals())
    W = {n: given[n] for n in W_NAMES}
    M = {n: given['m_' + n] for n in W_NAMES}
    V = {n: given['v_' + n] for n in W_NAMES}
    S, D = x.shape[1], x.shape[2]
    L = g_pre_mix.shape[0]
    Dc = D // 2
    step = math.lcm(_TM_MM, _TM_SEQ)
    T = -(-(N_META + S) // step) * step

    big_pack = _pack_rows([W[n] for n in BIG], _MXU, 16)
    small_pack = _pack_rows([W[n] for n in SMALL_SHARDED], F32, 8)
    big_g, small_g = _all_gather([big_pack, small_pack], "gather_weights")
    big_segs = _unpack_cols(big_g.reshape(_N_DEV, -1), [W[n].shape for n in BIG])
    small_segs = _unpack_cols(small_g.reshape(_N_DEV, -1), [W[n].shape for n in SMALL_SHARDED])
    full = {}
    for n, seg in zip(BIG, big_segs):
        full[n] = _from_gathered(seg, W[n].shape, 2 if n in ('w_in', 'w_up') else 1)
    for n, seg in zip(SMALL_SHARDED, small_segs):
        full[n] = _from_gathered(seg, W[n].shape, W[n].ndim - 1)

    p = {}
    for n in W_NAMES:
        if n in BIG or n == 'meta_tokens':
            continue
        a = full[n] if n in full else W[n]
        p[n] = a.reshape(L, 1, a.shape[1]) if a.ndim == 2 else a

    pad_rows = T - N_META - S
    h = jnp.concatenate([full['meta_tokens'], x[0], jnp.zeros((pad_rows, D), F32)], axis=0)
    tgt = jnp.concatenate([jnp.zeros((N_META, D), F32), loss_target[0], jnp.zeros((pad_rows, D), F32)], axis=0)

    saved = []
    for l in range(L):
        proj, z1 = _norm_mm(h, p['g_pre_mix'], full['w_in'], l, f"in_proj_l{l}")
        xc, hs, ya, cc, yb = _mixers_fwd(proj, p, l, f"mixers_fwd_l{l}")
        y, o, hmid = _mix_out(ya, yb, h, p, full['w_out'], l, f"mix_out_l{l}")
        up0, z2 = _norm_mm(hmid, p['g_pre_ffn'], full['w_up'], l, f"up_proj_l{l}")
        a2, f, hout = _ffn_fwd(up0, hmid, p, full['w_down'], l, f"ffn_fwd_l{l}")
        saved.append(dict(h=h, z1=z1, proj=proj, xc=xc, hs=hs, ya=ya, cc=cc, yb=yb, y=y, o=o, hmid=hmid,
                          z2=z2, up0=up0, a2=a2, f=f))
        h = hout
    loss_tile, dh = _loss_head(h, tgt, S, "loss_head")
    loss = lax.psum(loss_tile[0, 0], ("x", "y", "c"))

    small_g_names = ['g_pre_mix', 'lru_conv_w', 'lru_conv_b', 'lru_wa', 'lru_ba', 'lru_wx', 'lru_bx', 'lru_lambda',
                     'conv_w', 'conv_b', 'conv_ln_g', 'conv_ln_b', 'g_out_lru', 'g_out_conv', 'g_post_mix',
                     'g_pre_ffn', 'ffn_conv_w', 'ffn_conv_b', 'g_post_ffn']
    per_layer = {n: [None] * L for n in small_g_names}
    d_big = {n: None for n in BIG}
    for l in reversed(range(L)):
        sv = saved[l]
        df, dup0, dwf, dbf, dgpf = _ffn_bwd(dh, sv['f'], sv['up0'], p, full['w_down'], l, f"ffn_bwd_l{l}")
        d_big['w_down'] = _mm_tn(sv['a2'], df, l, L, d_big['w_down'], f"dw_down_l{l}")
        d_big['w_up'] = _mm_tn(sv['z2'], dup0, l, L, d_big['w_up'], f"dw_up_l{l}")
        dhm, dgpre = _mm_nt_norm_bwd(dup0, full['w_up'], sv['hmid'], p['g_pre_ffn'], dh, l, f"up_bwd_l{l}")
        do, dya, dyb, dgpm, dgol, dgoc = _mix_bwd(dhm, sv['o'], sv['ya'], sv['yb'], p, full['w_out'], l,
                                                  f"mix_bwd_l{l}")
        d_big['w_out'] = _mm_tn(sv['y'], do, l, L, d_big['w_out'], f"dw_out_l{l}")
        (dproj, dwl, dbl, dwa, dba, dwx, dbx, dlam, dwc, dbc, dlg, dlb) = _mixers_bwd(
            dya, dyb, sv['proj'], sv['xc'], sv['hs'], sv['cc'], p, l, f"mixers_bwd_l{l}")
        d_big['w_in'] = _mm_tn(sv['z1'], dproj, l, L, d_big['w_in'], f"dw_in_l{l}")
        dh, dgpmix = _mm_nt_norm_bwd(dproj, full['w_in'], sv['h'], p['g_pre_mix'], dhm, l, f"in_bwd_l{l}")
        for n, val in (('g_pre_mix', dgpmix), ('lru_conv_w', dwl), ('lru_conv_b', dbl), ('lru_wa', dwa),
                       ('lru_ba', dba), ('lru_wx', dwx), ('lru_bx', dbx), ('lru_lambda', dlam), ('conv_w', dwc),
                       ('conv_b', dbc), ('conv_ln_g', dlg), ('conv_ln_b', dlb), ('g_out_lru', dgol),
                       ('g_out_conv', dgoc), ('g_post_mix', dgpm), ('g_pre_ffn', dgpre), ('ffn_conv_w', dwf),
                       ('ffn_conv_b', dbf), ('g_post_ffn', dgpf)):
            per_layer[n][l] = val
    grad_x = dh[N_META:N_META + S][None]
    partial = {n: jnp.stack(per_layer[n]).reshape((L,) + tuple(
        (full[n] if n in full else W[n]).shape[1:])) for n in small_g_names}
    partial['meta_tokens'] = dh[0:N_META]

    shard_pack = jnp.concatenate([_to_pieces(partial[n]) for n in SMALL_SHARDED], axis=1)
    n_sh = shard_pack.shape[1]
    rs = -(-n_sh // (8 * _LANES)) * 8
    shard_pack = jnp.concatenate([shard_pack, jnp.zeros((_N_DEV, rs * _LANES - n_sh), F32)], axis=1)
    rep_flat = jnp.concatenate([partial[n].reshape(-1) for n in REPLICATED])
    n_rep = rep_flat.shape[0]
    rr = -(-n_rep // (_N_DEV * 8 * _LANES)) * 8
    rep_flat = jnp.concatenate([rep_flat, jnp.zeros((_N_DEV * rr * _LANES - n_rep,), F32)])
    small_send = jnp.concatenate([shard_pack, rep_flat.reshape(_N_DEV, rr * _LANES)], axis=1)
    small_send = small_send.reshape(_N_DEV, rs + rr, _LANES)
    r_in, r_out, r_up, r_down, r_small = _exchange(d_big['w_in'], d_big['w_out'], d_big['w_up'], d_big['w_down'],
                                                   small_send, "grad_exchange")
    small_red = _sum_sources(r_small, "sum_small")
    (rep_g,) = _all_gather([small_red[rs:]], "gather_replicated_grads")
    rep_g = rep_g.reshape(_N_DEV * rr, _LANES)

    out = {}
    for n, recv in (('w_in', r_in), ('w_out', r_out), ('w_up', r_up), ('w_down', r_down)):
        shp = W[n].shape
        c2 = shp[-1]
        res = _adamw(recv.reshape(_N_DEV, -1, c2), W[n].reshape(-1, c2), M[n].reshape(-1, c2),
                     V[n].reshape(-1, c2), f"adamw_{n}")
        out[n] = [r.reshape(shp) for r in res]
    sh_shapes = [W[n].shape for n in SMALL_SHARDED]
    res = _adamw(small_red[:rs], _pack_rows([W[n] for n in SMALL_SHARDED], F32, 8),
                 _pack_rows([M[n] for n in SMALL_SHARDED], F32, 8),
                 _pack_rows([V[n] for n in SMALL_SHARDED], F32, 8), "adamw_small_sharded")
    for k in range(4):
        for n, val in zip(SMALL_SHARDED, _unpack(res[k].reshape(-1), sh_shapes)):
            out.setdefault(n, [None] * 4)[k] = val
    rep_shapes = [W[n].shape for n in REPLICATED]
    res = _adamw(rep_g, _pack_rows([W[n] for n in REPLICATED], F32, 8 * _N_DEV),
                 _pack_rows([M[n] for n in REPLICATED], F32, 8 * _N_DEV),
                 _pack_rows([V[n] for n in REPLICATED], F32, 8 * _N_DEV), "adamw_replicated")
    for k in range(4):
        for n, val in zip(REPLICATED, _unpack(res[k].reshape(-1), rep_shapes)):
            out.setdefault(n, [None] * 4)[k] = val

    return (loss, grad_x, *[out[n][0] for n in W_NAMES], *[out[n][1] for n in W_NAMES],
            *[out[n][2] for n in W_NAMES], *[out[n][3] for n in W_NAMES])


def _unpack_cols(gathered, shapes):
    out, off = [], 0
    for s in shapes:
        n = math.prod(s)
        out.append(gathered[:, off:off + n])
        off += n
    return out
```

```python
import math

import jax
import jax.numpy as jnp
from jax import lax
from jax.experimental import pallas as pl
from jax.experimental.pallas import tpu as pltpu

F32 = jnp.float32
_MXU = jnp.bfloat16
_TM_MM = 768
_TM_SEQ = 256
_NB = 768
_VMEM_LIMIT = 56 * 1024 * 1024
_ADAM_BLOCK_ELEMS = 128 * 1024
_LANES = 128
_N_DEV = 8

EPS = 1e-6
N_META = 16
LRU_C = 8.0
CONV_GROUPS = 4
HALO_S = 8
HALO_L = 32
ADAM_LR, ADAM_B1, ADAM_B2, ADAM_EPS, ADAM_WD, ADAM_STEP = 0.001, 0.9, 0.999, 1e-08, 0.01, 10
_GELU_K0 = math.sqrt(2.0 / math.pi)
_GELU_K1 = 0.044715

W_NAMES = ['meta_tokens', 'g_pre_mix', 'w_in', 'lru_conv_w', 'lru_conv_b', 'lru_wa', 'lru_ba', 'lru_wx', 'lru_bx',
           'lru_lambda', 'conv_w', 'conv_b', 'conv_ln_g', 'conv_ln_b', 'g_out_lru', 'g_out_conv', 'w_out',
           'g_post_mix', 'g_pre_ffn', 'w_up', 'ffn_conv_w', 'ffn_conv_b', 'w_down', 'g_post_ffn']
BIG = ['w_in', 'w_out', 'w_up', 'w_down']
SMALL_SHARDED = ['meta_tokens', 'lru_conv_w', 'conv_w', 'ffn_conv_w']
REPLICATED = [n for n in W_NAMES if n not in BIG and n not in SMALL_SHARDED]


def _params():
    return pltpu.CompilerParams(vmem_limit_bytes=_VMEM_LIMIT)


def _pick(n, pref):
    if n <= pref:
        return n
    best = None
    for b in range(_LANES, pref + 1, _LANES):
        if n % b == 0:
            best = b
    assert best is not None, (n, pref)
    return best


def _pick_rows(n, pref):
    if n <= pref:
        return n
    best = None
    for b in range(8, pref + 1, 8):
        if n % b == 0:
            best = b
    assert best is not None, (n, pref)
    return best


def _lspec(l, *dims):
    zeros = (0,) * len(dims)
    return pl.BlockSpec((None,) + tuple(dims), lambda *_: (l,) + zeros)


def _sds(shape, dtype):
    return jax.ShapeDtypeStruct(tuple(shape), dtype)


def _rms_fwd(x, g):
    r = lax.rsqrt(jnp.mean(x * x, axis=-1, keepdims=True) + EPS)
    return (x * r) * g


def _rms_bwd(x, g, dy):
    r = lax.rsqrt(jnp.mean(x * x, axis=-1, keepdims=True) + EPS)
    xh = x * r
    dg = jnp.sum(dy * xh, axis=0, keepdims=True)
    dxh = dy * g
    dx = r * (dxh - xh * jnp.mean(dxh * xh, axis=-1, keepdims=True))
    return dx, dg


def _gelu(x):
    t = jnp.tanh(_GELU_K0 * (x + _GELU_K1 * (x * x * x)))
    return 0.5 * x * (1.0 + t), t


def _gelu_grad(x, t):
    return 0.5 * (1.0 + t) + 0.5 * x * (1.0 - t * t) * (_GELU_K0 * (1.0 + 3.0 * _GELU_K1 * x * x))


def _log1p(e):
    u = 1.0 + e
    return jnp.where(u == 1.0, e, jnp.log(u) * (e / (u - 1.0)))


def _softplus(z):
    return jnp.maximum(z, 0.0) + _log1p(jnp.exp(-jnp.abs(z)))


def _one_minus_exp(x):
    p = -x * (1.0 + x * (0.5 + x * (1.0 / 6 + x * (1.0 / 24 + x * (1.0 / 120 + x * (1.0 / 720))))))
    return jnp.where(x > -0.125, p, 1.0 - jnp.exp(x))


def _shift_down(x, s, fill, row):
    return jnp.where(row >= s, pltpu.roll(x, s, 0), fill)


def _shift_up(x, s, fill, row):
    n = x.shape[0]
    return jnp.where(row < n - s, pltpu.roll(x, n - s, 0), fill)


def _bd_mm(xb, w_ref, heads, hd):
    return jnp.concatenate(
        [jnp.dot(xb[:, h * hd:(h + 1) * hd], w_ref[h].astype(_MXU), preferred_element_type=F32)
         for h in range(heads)], axis=-1)


def _bd_mm_t(db, w_ref, heads, hd):
    return jnp.concatenate(
        [lax.dot_general(db[:, h * hd:(h + 1) * hd], w_ref[h].astype(_MXU), (((1,), (1,)), ((), ())),
                         preferred_element_type=F32)
         for h in range(heads)], axis=-1)


def _gates(xc, wa_ref, ba, wx_ref, bx, lam, heads, hd):
    xcb = xc.astype(_MXU)
    ra = jax.nn.sigmoid(_bd_mm(xcb, wa_ref, heads, hd) + ba)
    ri = jax.nn.sigmoid(_bd_mm(xcb, wx_ref, heads, hd) + bx)
    sp = _softplus(-lam)
    la = (-LRU_C) * ra * sp
    a = jnp.exp(la)
    mult = jnp.sqrt(_one_minus_exp(2.0 * la))
    return xcb, ra, ri, sp, a, mult


def _group_norm(cc, groups):
    gs = cc.shape[-1] // groups
    outs, rss = [], []
    for g in range(groups):
        seg = cc[:, g * gs:(g + 1) * gs]
        mu = jnp.mean(seg, axis=-1, keepdims=True)
        d = seg - mu
        rs = lax.rsqrt(jnp.mean(d * d, axis=-1, keepdims=True) + EPS)
        outs.append(d * rs)
        rss.append(rs)
    return jnp.concatenate(outs, axis=-1), rss


_MESH = pl.DeviceIdType.MESH
_ANY = pl.BlockSpec(memory_space=pl.ANY)


def _my_index():
    return 4 * lax.axis_index("x") + 2 * lax.axis_index("y") + lax.axis_index("c")


class _GatherPlan:
    def __init__(self, blocks):
        self.blocks = list(blocks)

    def operands(self):
        return self.blocks

    def out_shape(self):
        return [_sds((_N_DEV,) + b.shape, b.dtype) for b in self.blocks]

    def scratch(self):
        n = len(self.blocks)
        return [pltpu.SemaphoreType.DMA((7 * n,)), pltpu.SemaphoreType.DMA((7 * n,)), pltpu.SemaphoreType.DMA((n,))]

    def _copies(self, a, ins, outs, sems):
        send_sems, recv_sems, local_sems = sems
        x, y, c = lax.axis_index("x"), lax.axis_index("y"), lax.axis_index("c")
        me, sibling = (x, y, c), (x, y, 1 - c)
        chips = [(1 - x, y), (x, 1 - y), (1 - x, 1 - y)]

        def slot(dev):
            return outs[a].at[4 * dev[0] + 2 * dev[1] + dev[2]]

        def copy(k, block, to, src=None):
            return pltpu.make_async_remote_copy(
                src_ref=slot(block) if src is None else src, dst_ref=slot(block),
                send_sem=send_sems.at[7 * a + k], recv_sem=recv_sems.at[7 * a + k],
                device_id=to, device_id_type=_MESH)

        mine = pltpu.make_async_copy(ins[a], slot(me), local_sems.at[a])
        first = [copy(0, me, sibling, src=ins[a])]
        first += [copy(1 + j, me, (*chip, c), src=ins[a]) for j, chip in enumerate(chips)]
        passed = [copy(4 + j, (*chip, c), sibling) for j, chip in enumerate(chips)]
        from_chips = [copy(1 + j, (*chip, c), me) for j, chip in enumerate(chips)]
        from_sibling = [copy(0, sibling, me)] + [copy(4 + j, (*chip, 1 - c), me) for j, chip in enumerate(chips)]
        return mine, first, passed, from_chips, from_sibling

    def start(self, ins, outs, sems):
        for a in range(len(self.blocks)):
            mine, first, _, _, _ = self._copies(a, ins, outs, sems)
            mine.start()
            for cp in first:
                cp.start()

    def forward(self, ins, outs, sems):
        for a in range(len(self.blocks)):
            _, _, passed, from_chips, _ = self._copies(a, ins, outs, sems)
            for j in range(3):
                from_chips[j].wait_recv()
                passed[j].start()

    def finish(self, ins, outs, sems):
        for a in range(len(self.blocks)):
            mine, first, passed, _, from_sibling = self._copies(a, ins, outs, sems)
            for cp in from_sibling:
                cp.wait_recv()
            for cp in first + passed:
                cp.wait_send()
            mine.wait()

    def begin(self, i, steps, ins, outs, sems):
        @pl.when(i == 0)
        def _():
            self.start(ins, outs, sems)

        @pl.when(i == steps // 2)
        def _():
            self.forward(ins, outs, sems)

    def end(self, i, steps, ins, outs, sems):
        @pl.when(i == steps - 1)
        def _():
            self.finish(ins, outs, sems)


class _ExchangePlan:
    def __init__(self, arrs, kinds):
        self.arrs, self.kinds = list(arrs), list(kinds)

    def _piece_shape(self, a):
        shp = self.arrs[a].shape
        if self.kinds[a] == 'cols':
            return (shp[0], shp[1] // _N_DEV)
        if self.kinds[a] == 'rows':
            return (shp[0] // _N_DEV, shp[1])
        return tuple(shp[1:])

    def operands(self):
        return self.arrs

    def out_shape(self):
        return [_sds((_N_DEV,) + self._piece_shape(a), F32) for a in range(len(self.arrs))]

    def scratch(self):
        n = len(self.arrs)
        return [pltpu.SemaphoreType.DMA((_N_DEV * n,)), pltpu.SemaphoreType.DMA((_N_DEV * n,)),
                pltpu.SemaphoreType.DMA((n,))]

    def _copies(self, ins, outs, sems):
        send_sems, recv_sems, local_sems = sems
        me = _my_index()

        def piece(a, j):
            ps = self._piece_shape(a)
            if self.kinds[a] == 'cols':
                return ins[a].at[:, pl.ds(j * ps[1], ps[1])]
            if self.kinds[a] == 'rows':
                return ins[a].at[pl.ds(j * ps[0], ps[0]), :]
            return ins[a].at[j]

        def remote(a, j):
            return pltpu.make_async_remote_copy(
                src_ref=piece(a, j), dst_ref=outs[a].at[me],
                send_sem=send_sems.at[_N_DEV * a + j], recv_sem=recv_sems.at[_N_DEV * a + me],
                device_id=(j >> 2, (j >> 1) & 1, j & 1), device_id_type=_MESH)

        def arrival(a, s):
            return pltpu.make_async_remote_copy(
                src_ref=piece(a, s), dst_ref=outs[a].at[s],
                send_sem=send_sems.at[_N_DEV * a + s], recv_sem=recv_sems.at[_N_DEV * a + s],
                device_id=(s >> 2, (s >> 1) & 1, s & 1), device_id_type=_MESH)

        def local(a, j):
            return pltpu.make_async_copy(piece(a, j), outs[a].at[j], local_sems.at[a])

        return me, remote, arrival, local

    def start(self, ins, outs, sems):
        me, remote, _, local = self._copies(ins, outs, sems)
        for j in range(_N_DEV):
            for a in range(len(self.arrs)):
                @pl.when(me != j)
                def _(a=a, j=j):
                    remote(a, j).start()

                @pl.when(me == j)
                def _(a=a, j=j):
                    local(a, j).start()

    def finish(self, ins, outs, sems):
        me, remote, arrival, local = self._copies(ins, outs, sems)
        for j in range(_N_DEV):
            for a in range(len(self.arrs)):
                @pl.when(me != j)
                def _(a=a, j=j):
                    arrival(a, j).wait_recv()
                    remote(a, j).wait_send()

                @pl.when(me == j)
                def _(a=a, j=j):
                    local(a, j).wait()

    def forward(self, ins, outs, sems):
        pass

    def begin(self, i, steps, ins, outs, sems):
        @pl.when(i == 0)
        def _():
            self.start(ins, outs, sems)

    def end(self, i, steps, ins, outs, sems):
        @pl.when(i == steps - 1)
        def _():
            self.finish(ins, outs, sems)


def _ride(main, plan, name, steps, in_specs, args, out_specs, out_shape, scratch_shapes):
    n_in, n_out, n_sc = len(in_specs), len(out_specs), len(scratch_shapes)
    p_args = plan.operands() if plan else []
    p_out = plan.out_shape() if plan else []
    p_sc = plan.scratch() if plan else []

    def body(*refs):
        k = 0
        ins = refs[k:k + n_in]; k += n_in
        p_ins = refs[k:k + len(p_args)]; k += len(p_args)
        outs = refs[k:k + n_out]; k += n_out
        p_outs = refs[k:k + len(p_out)]; k += len(p_out)
        scr = refs[k:k + n_sc]; k += n_sc
        sems = refs[k:]
        i = pl.program_id(0)
        if plan:
            plan.begin(i, steps, p_ins, p_outs, sems)
        main(*ins, *outs, *scr)
        if plan:
            plan.end(i, steps, p_ins, p_outs, sems)

    res = pl.pallas_call(
        body, name=name, grid=(steps,),
        in_specs=list(in_specs) + [_ANY] * len(p_args),
        out_specs=list(out_specs) + [_ANY] * len(p_out),
        out_shape=list(out_shape) + p_out,
        scratch_shapes=list(scratch_shapes) + p_sc,
        compiler_params=_params())(*args, *p_args)
    return res[:n_out], res[n_out:]


def _run_plan(plan, name):
    n_args, n_out = len(plan.operands()), len(plan.out_shape())

    def body(*refs):
        ins, outs, sems = refs[:n_args], refs[n_args:n_args + n_out], refs[n_args + n_out:]
        plan.start(ins, outs, sems)
        plan.forward(ins, outs, sems)
        plan.finish(ins, outs, sems)

    return pl.pallas_call(
        body, name=name, in_specs=[_ANY] * n_args, out_specs=[_ANY] * n_out,
        out_shape=plan.out_shape(), scratch_shapes=plan.scratch())(*plan.operands())


def _norm_mm(x, g3, w, l, name):
    T, D = x.shape
    N = w.shape[1]
    tm, nb = _TM_MM, _pick(N, _NB)

    def body(x_ref, g_ref, w_ref, o_ref, z_ref):
        @pl.when(pl.program_id(1) == 0)
        def _():
            z_ref[...] = _rms_fwd(x_ref[...], g_ref[...]).astype(z_ref.dtype)

        o_ref[...] = jnp.dot(z_ref[...], w_ref[...], preferred_element_type=F32)

    return pl.pallas_call(
        body, name=name, grid=(T // tm, N // nb),
        in_specs=[pl.BlockSpec((tm, D), lambda i, j: (i, 0)), _lspec(l, 1, D),
                  pl.BlockSpec((D, nb), lambda i, j: (0, j))],
        out_specs=[pl.BlockSpec((tm, nb), lambda i, j: (i, j)), pl.BlockSpec((tm, D), lambda i, j: (i, 0))],
        out_shape=[_sds((T, N), F32), _sds((T, D), _MXU)],
        compiler_params=_params())(x, g3, w)


def _tap31(buf, sbuf, off, tm):
    a, b = divmod(off, 8)
    if b == 0:
        return buf[pl.ds(off, tm), :]
    return sbuf[b - 1, pl.ds(8 * a, tm), :]


def _fill_shifted(buf, sbuf, rows):
    for b in range(1, 8):
        sbuf[b - 1, :, :] = buf[pl.ds(b, rows), :]


def _mixers_fwd(proj, p, l, name, plan=None):
    T, D3 = proj.shape
    D = D3 // 3
    Dc = D // 2
    heads, hd = p['lru_wa'].shape[1], p['lru_wa'].shape[2]
    tm = _TM_SEQ
    kl, kc = p['lru_conv_w'].shape[1], p['conv_w'].shape[1]

    def body(proj_ref, wl_ref, bl_ref, wa_ref, ba_ref, wx_ref, bx_ref, lam_ref, wc_ref, bc_ref, lg_ref, lb_ref,
             xc_ref, hs_ref, ya_ref, cc_ref, yb_ref, xbuf, hcar, cbuf, sbuf):
        @pl.when(pl.program_id(0) == 0)
        def _():
            xbuf[0:HALO_S, :] = jnp.zeros((HALO_S, D), F32)
            cbuf[0:HALO_L, :] = jnp.zeros((HALO_L, Dc), F32)
            hcar[...] = jnp.zeros((1, D), F32)

        xbuf[HALO_S:HALO_S + tm, :] = proj_ref[:, 0:D]
        xc = bl_ref[...] + wl_ref[0:1, :] * xbuf[pl.ds(HALO_S - (kl - 1), tm), :]
        for k in range(1, kl):
            xc = xc + wl_ref[k:k + 1, :] * xbuf[pl.ds(HALO_S - (kl - 1) + k, tm), :]
        xbuf[0:HALO_S, :] = xbuf[tm:tm + HALO_S, :]
        xc_ref[...] = xc
        _, ra, ri, sp, a, mult = _gates(xc, wa_ref, ba_ref[...], wx_ref, bx_ref[...], lam_ref[...], heads, hd)
        u = mult * (ri * xc)
        row = lax.broadcasted_iota(jnp.int32, (tm, D), 0)
        s = 1
        while s < tm:
            u = u + a * _shift_down(u, s, 0.0, row)
            a = a * _shift_down(a, s, 1.0, row)
            s *= 2
        hs = u + a * hcar[...]
        hs_ref[...] = hs
        hcar[...] = hs_ref[pl.ds(tm - 1, 1), :]
        gg, _ = _gelu(proj_ref[:, D:2 * D])
        ya_ref[...] = hs * gg

        cbuf[HALO_L:HALO_L + tm, :] = proj_ref[:, 2 * D:2 * D + Dc] * jax.nn.sigmoid(proj_ref[:, 2 * D + Dc:3 * D])
        _fill_shifted(cbuf, sbuf, tm + HALO_L - 8)
        cc = bc_ref[...] + wc_ref[0:1, :] * _tap31(cbuf, sbuf, HALO_L - (kc - 1), tm)
        for k in range(1, kc):
            cc = cc + wc_ref[k:k + 1, :] * _tap31(cbuf, sbuf, HALO_L - (kc - 1) + k, tm)
        cbuf[0:HALO_L, :] = cbuf[tm:tm + HALO_L, :]
        cc_ref[...] = cc
        nrm, _ = _group_norm(cc, CONV_GROUPS)
        cl = nrm * lg_ref[...] + lb_ref[...]
        yb_ref[...] = cl * jax.nn.sigmoid(cl)

    row_d = pl.BlockSpec((tm, D), lambda i: (i, 0))
    row_c = pl.BlockSpec((tm, Dc), lambda i: (i, 0))
    return _ride(
        body, plan, name, T // tm,
        in_specs=[pl.BlockSpec((tm, D3), lambda i: (i, 0)),
                  _lspec(l, kl, D), _lspec(l, 1, D), _lspec(l, heads, hd, hd), _lspec(l, 1, D),
                  _lspec(l, heads, hd, hd), _lspec(l, 1, D), _lspec(l, 1, D),
                  _lspec(l, kc, Dc), _lspec(l, 1, Dc), _lspec(l, 1, Dc), _lspec(l, 1, Dc)],
        args=(proj, p['lru_conv_w'], p['lru_conv_b'], p['lru_wa'], p['lru_ba'], p['lru_wx'], p['lru_bx'],
              p['lru_lambda'], p['conv_w'], p['conv_b'], p['conv_ln_g'], p['conv_ln_b']),
        out_specs=[row_d, row_d, row_d, row_c, row_c],
        out_shape=[_sds((T, D), F32), _sds((T, D), F32), _sds((T, D), F32), _sds((T, Dc), F32), _sds((T, Dc), F32)],
        scratch_shapes=[pltpu.VMEM((HALO_S + tm, D), F32), pltpu.VMEM((1, D), F32),
                        pltpu.VMEM((HALO_L + tm, Dc), F32), pltpu.VMEM((7, tm + HALO_L - 8, Dc), F32)])


def _mix_out(ya, yb, h, p, wout, l, name):
    T, D = ya.shape
    Dc = yb.shape[1]
    tm = _TM_SEQ

    def body(ya_ref, yb_ref, h_ref, gl_ref, gc_ref, gp_ref, w_ref, y_ref, o_ref, hm_ref):
        y = jnp.concatenate([_rms_fwd(ya_ref[...], gl_ref[...]), _rms_fwd(yb_ref[...], gc_ref[...])],
                            axis=-1).astype(_MXU)
        y_ref[...] = y
        o = jnp.dot(y, w_ref[...], preferred_element_type=F32)
        o_ref[...] = o
        hm_ref[...] = h_ref[...] + _rms_fwd(o, gp_ref[...])

    row_d = pl.BlockSpec((tm, D), lambda i: (i, 0))
    return pl.pallas_call(
        body, name=name, grid=(T // tm,),
        in_specs=[row_d, pl.BlockSpec((tm, Dc), lambda i: (i, 0)), row_d,
                  _lspec(l, 1, D), _lspec(l, 1, Dc), _lspec(l, 1, D), pl.BlockSpec((D + Dc, D), lambda i: (0, 0))],
        out_specs=[pl.BlockSpec((tm, D + Dc), lambda i: (i, 0)), row_d, row_d],
        out_shape=[_sds((T, D + Dc), _MXU), _sds((T, D), F32), _sds((T, D), F32)],
        compiler_params=_params(),
    )(ya, yb, h, p['g_out_lru'], p['g_out_conv'], p['g_post_mix'], wout)


def _ffn_fwd(up0, hmid, p, wdown, l, name):
    T, F2 = up0.shape
    Fh = F2 // 2
    D = hmid.shape[1]
    tm = _TM_SEQ
    cw = _pick(Fh, _NB)
    kf = p['ffn_conv_w'].shape[1]

    def body(up_ref, hm_ref, wf_ref, bf_ref, g_ref, wd_ref, a2_ref, f_ref, ho_ref, ubuf):
        @pl.when(pl.program_id(0) == 0)
        def _():
            ubuf[0:HALO_S, :] = jnp.zeros((HALO_S, F2), F32)

        ubuf[HALO_S:HALO_S + tm, :] = up_ref[...]

        def conv(cs):
            acc = bf_ref[:, cs]
            for k in range(kf):
                acc = acc + wf_ref[k:k + 1, cs] * ubuf[pl.ds(HALO_S - (kf - 1) + k, tm), cs]
            return acc

        f = None
        for c in range(Fh // cw):
            gs = slice(c * cw, (c + 1) * cw)
            gg, _ = _gelu(conv(gs))
            a2 = (gg * conv(slice(Fh + c * cw, Fh + (c + 1) * cw))).astype(_MXU)
            a2_ref[:, gs] = a2
            part = jnp.dot(a2, wd_ref[gs, :], preferred_element_type=F32)
            f = part if f is None else f + part
        ubuf[0:HALO_S, :] = ubuf[tm:tm + HALO_S, :]
        f_ref[...] = f
        ho_ref[...] = hm_ref[...] + _rms_fwd(f, g_ref[...])

    row_d = pl.BlockSpec((tm, D), lambda i: (i, 0))
    return pl.pallas_call(
        body, name=name, grid=(T // tm,),
        in_specs=[pl.BlockSpec((tm, F2), lambda i: (i, 0)), row_d, _lspec(l, kf, F2), _lspec(l, 1, F2),
                  _lspec(l, 1, D), pl.BlockSpec((Fh, D), lambda i: (0, 0))],
        out_specs=[pl.BlockSpec((tm, Fh), lambda i: (i, 0)), row_d, row_d],
        out_shape=[_sds((T, Fh), _MXU), _sds((T, D), F32), _sds((T, D), F32)],
        scratch_shapes=[pltpu.VMEM((HALO_S + tm, F2), F32)],
        compiler_params=_params(),
    )(up0, hmid, p['ffn_conv_w'], p['ffn_conv_b'], p['g_post_ffn'], wdown)


def _loss_head(h, tgt, n_real, name):
    T, D = h.shape
    tm = _TM_SEQ

    def body(h_ref, t_ref, loss_ref, dh_ref):
        i = pl.program_id(0)

        @pl.when(i == 0)
        def _():
            loss_ref[...] = jnp.zeros_like(loss_ref)

        row = i * tm + lax.broadcasted_iota(jnp.int32, (tm, D), 0)
        e = jnp.where((row >= N_META) & (row < N_META + n_real), h_ref[...] - t_ref[...], 0.0)
        dh_ref[...] = e * (1.0 / D)
        loss_ref[...] += 0.5 * jnp.sum(jnp.mean(e * e, axis=-1, keepdims=True), axis=0, keepdims=True)

    row_d = pl.BlockSpec((tm, D), lambda i: (i, 0))
    return pl.pallas_call(
        body, name=name, grid=(T // tm,),
        in_specs=[row_d, row_d],
        out_specs=[pl.BlockSpec((8, _LANES), lambda i: (0, 0)), row_d],
        out_shape=[_sds((8, _LANES), F32), _sds((T, D), F32)],
        compiler_params=_params())(h, tgt)


def _ffn_bwd(dh, f, up0, p, wdown, l, name, plan=None):
    T, F2 = up0.shape
    Fh = F2 // 2
    D = dh.shape[1]
    tm = _TM_SEQ
    nt = T // tm
    cw = _pick(Fh, _NB)
    kf = p['ffn_conv_w'].shape[1]
    per8 = tm // 8

    def body(dh_ref, f_ref, up_ref, upp_ref, wf_ref, bf_ref, g_ref, wd_ref,
             df_ref, dup_ref, dwf_ref, dbf_ref, dg_ref, ubuf, dbuf):
        i = pl.program_id(0)
        r = nt - 1 - i

        @pl.when(i == 0)
        def _():
            dbuf[tm:tm + HALO_S, :] = jnp.zeros((HALO_S, F2), F32)
            dwf_ref[...] = jnp.zeros_like(dwf_ref)
            dbf_ref[...] = jnp.zeros_like(dbf_ref)
            dg_ref[...] = jnp.zeros_like(dg_ref)

        df, dg = _rms_bwd(f_ref[...], g_ref[...], dh_ref[...])
        dg_ref[...] += dg
        dfb = df.astype(_MXU)
        df_ref[...] = dfb
        ubuf[0:HALO_S, :] = jnp.where(r == 0, 0.0, upp_ref[...])
        ubuf[HALO_S:HALO_S + tm, :] = up_ref[...]

        def conv(cs):
            acc = bf_ref[:, cs]
            for k in range(kf):
                acc = acc + wf_ref[k:k + 1, cs] * ubuf[pl.ds(HALO_S - (kf - 1) + k, tm), cs]
            return acc

        for c in range(Fh // cw):
            gs = slice(c * cw, (c + 1) * cw)
            us = slice(Fh + c * cw, Fh + (c + 1) * cw)
            ug = conv(gs)
            gg, t = _gelu(ug)
            da2 = lax.dot_general(dfb, wd_ref[gs, :], (((1,), (1,)), ((), ())), preferred_element_type=F32)
            dbuf[0:tm, gs] = da2 * conv(us) * _gelu_grad(ug, t)
            dbuf[0:tm, us] = da2 * gg
        for c in range(F2 // cw):
            cs = slice(c * cw, (c + 1) * cw)
            upc = up_ref[:, cs]
            dup = None
            for k in range(kf):
                dsh = dbuf[pl.ds(kf - 1 - k, tm), cs]
                term = wf_ref[k:k + 1, cs] * dsh
                dup = term if dup is None else dup + term
                dwf_ref[k:k + 1, cs] += jnp.sum(dsh * upc, axis=0, keepdims=True)
            dbf_ref[:, cs] += jnp.sum(dbuf[0:tm, cs], axis=0, keepdims=True)
            dup_ref[:, cs] = dup.astype(_MXU)
        dbuf[tm:tm + HALO_S, :] = dbuf[0:HALO_S, :]

    rev_d = pl.BlockSpec((tm, D), lambda i: (nt - 1 - i, 0))
    rev_f = pl.BlockSpec((tm, F2), lambda i: (nt - 1 - i, 0))
    prev8 = pl.BlockSpec((8, F2), lambda i: (jnp.maximum((nt - 1 - i) * per8 - 1, 0), 0))
    full = lambda *s: pl.BlockSpec(s, lambda i: (0,) * len(s))
    return _ride(
        body, plan, name, nt,
        in_specs=[rev_d, rev_d, rev_f, prev8, _lspec(l, kf, F2), _lspec(l, 1, F2), _lspec(l, 1, D),
                  pl.BlockSpec((Fh, D), lambda i: (0, 0))],
        args=(dh, f, up0, up0, p['ffn_conv_w'], p['ffn_conv_b'], p['g_post_ffn'], wdown),
        out_specs=[rev_d, rev_f, full(kf, F2), full(1, F2), full(1, D)],
        out_shape=[_sds((T, D), _MXU), _sds((T, F2), _MXU), _sds((kf, F2), F32), _sds((1, F2), F32), _sds((1, D), F32)],
        scratch_shapes=[pltpu.VMEM((HALO_S + tm, F2), F32), pltpu.VMEM((tm + HALO_S, F2), F32)])


def _mm_nt_norm_bwd(dy, w, x, g3, dres, l, name):
    T, N = dy.shape
    D = x.shape[1]
    tm, nb = _TM_MM, _pick(N, _NB)
    nj = N // nb

    def body(dy_ref, w_ref, x_ref, g_ref, dr_ref, dh_ref, dg_ref, acc):
        i, j = pl.program_id(0), pl.program_id(1)

        @pl.when((i == 0) & (j == 0))
        def _():
            dg_ref[...] = jnp.zeros_like(dg_ref)

        part = lax.dot_general(dy_ref[...], w_ref[...], (((1,), (1,)), ((), ())), preferred_element_type=F32)

        @pl.when(j == 0)
        def _():
            acc[...] = part

        @pl.when(j > 0)
        def _():
            acc[...] += part

        @pl.when(j == nj - 1)
        def _():
            dx, dg = _rms_bwd(x_ref[...], g_ref[...], acc[...])
            dh_ref[...] = dr_ref[...] + dx
            dg_ref[...] += dg

    row_d = pl.BlockSpec((tm, D), lambda i, j: (i, 0))
    return pl.pallas_call(
        body, name=name, grid=(T // tm, nj),
        in_specs=[pl.BlockSpec((tm, nb), lambda i, j: (i, j)), pl.BlockSpec((D, nb), lambda i, j: (0, j)),
                  row_d, _lspec(l, 1, D), row_d],
        out_specs=[row_d, pl.BlockSpec((1, D), lambda i, j: (0, 0))],
        out_shape=[_sds((T, D), F32), _sds((1, D), F32)],
        scratch_shapes=[pltpu.VMEM((tm, D), F32)],
        compiler_params=_params())(dy, w, x, g3, dres)


def _mm_tn(xs, dy, name):
    T, K = xs.shape
    N = dy.shape[1]
    tm = _TM_MM
    kb = K if K <= 1024 else _pick(K, _NB)
    nb = _pick(N, _NB)

    def body(x_ref, dy_ref, o_ref):
        @pl.when(pl.program_id(2) == 0)
        def _():
            o_ref[...] = jnp.zeros_like(o_ref)

        o_ref[...] += lax.dot_general(x_ref[...], dy_ref[...], (((0,), (0,)), ((), ())), preferred_element_type=F32)

    return pl.pallas_call(
        body, name=name, grid=(K // kb, N // nb, T // tm),
        in_specs=[pl.BlockSpec((tm, kb), lambda a, b, t: (t, a)), pl.BlockSpec((tm, nb), lambda a, b, t: (t, b))],
        out_specs=pl.BlockSpec((kb, nb), lambda a, b, t: (a, b)),
        out_shape=_sds((K, N), F32),
        compiler_params=_params())(xs, dy)


def _mix_bwd(dhm, o, ya, yb, p, wout, l, name):
    T, D = ya.shape
    Dc = yb.shape[1]
    tm = _TM_SEQ

    def body(dh_ref, o_ref, ya_ref, yb_ref, gp_ref, gl_ref, gc_ref, w_ref,
             do_ref, dya_ref, dyb_ref, dgp_ref, dgl_ref, dgc_ref):
        @pl.when(pl.program_id(0) == 0)
        def _():
            dgp_ref[...] = jnp.zeros_like(dgp_ref)
            dgl_ref[...] = jnp.zeros_like(dgl_ref)
            dgc_ref[...] = jnp.zeros_like(dgc_ref)

        do, dgp = _rms_bwd(o_ref[...], gp_ref[...], dh_ref[...])
        dgp_ref[...] += dgp
        dob = do.astype(_MXU)
        do_ref[...] = dob
        dy = lax.dot_general(dob, w_ref[...], (((1,), (1,)), ((), ())), preferred_element_type=F32)
        dya, dgl = _rms_bwd(ya_ref[...], gl_ref[...], dy[:, 0:D])
        dyb, dgc = _rms_bwd(yb_ref[...], gc_ref[...], dy[:, D:D + Dc])
        dya_ref[...] = dya
        dyb_ref[...] = dyb
        dgl_ref[...] += dgl
        dgc_ref[...] += dgc

    row_d = pl.BlockSpec((tm, D), lambda i: (i, 0))
    row_c = pl.BlockSpec((tm, Dc), lambda i: (i, 0))
    full = lambda *s: pl.BlockSpec(s, lambda i: (0,) * len(s))
    return pl.pallas_call(
        body, name=name, grid=(T // tm,),
        in_specs=[row_d, row_d, row_d, row_c, _lspec(l, 1, D), _lspec(l, 1, D), _lspec(l, 1, Dc),
                  pl.BlockSpec((D + Dc, D), lambda i: (0, 0))],
        out_specs=[row_d, row_d, row_c, full(1, D), full(1, D), full(1, Dc)],
        out_shape=[_sds((T, D), _MXU), _sds((T, D), F32), _sds((T, Dc), F32),
                   _sds((1, D), F32), _sds((1, D), F32), _sds((1, Dc), F32)],
        compiler_params=_params(),
    )(dhm, o, ya, yb, p['g_post_mix'], p['g_out_lru'], p['g_out_conv'], wout)


def _mixers_bwd(dya, dyb, proj, xc, hs, cc, p, l, name, plan=None):
    T, D3 = proj.shape
    D = D3 // 3
    Dc = D // 2
    heads, hd = p['lru_wa'].shape[1], p['lru_wa'].shape[2]
    tm = _TM_SEQ
    nt = T // tm
    per8 = tm // 8
    kl, kc = p['lru_conv_w'].shape[1], p['conv_w'].shape[1]

    def body(dya_ref, dyb_ref, proj_ref, xc_ref, hs_ref, hsp_ref, cc_ref,
             wl_ref, wa_ref, ba_ref, wx_ref, bx_ref, lam_ref, wc_ref, lg_ref, lb_ref,
             dproj_ref, dwl_ref, dbl_ref, dwa_ref, dba_ref, dwx_ref, dbx_ref, dlam_ref,
             dwc_ref, dbc_ref, dlg_ref, dlb_ref, gcar, dxbuf, dcbuf, tmp, sbuf):
        i = pl.program_id(0)
        r = nt - 1 - i

        @pl.when(i == 0)
        def _():
            gcar[...] = jnp.zeros((1, D), F32)
            dxbuf[tm:tm + HALO_S, :] = jnp.zeros((HALO_S, D), F32)
            dcbuf[tm:tm + HALO_L, :] = jnp.zeros((HALO_L, Dc), F32)
            for ref in (dwl_ref, dbl_ref, dwa_ref, dba_ref, dwx_ref, dbx_ref, dlam_ref, dwc_ref, dbc_ref,
                        dlg_ref, dlb_ref):
                ref[...] = jnp.zeros_like(ref)

        dya_v = dya_ref[...]
        hs = hs_ref[...]
        gl = proj_ref[:, D:2 * D]
        gg, tg = _gelu(gl)
        dproj_ref[:, D:2 * D] = (dya_v * hs * _gelu_grad(gl, tg)).astype(_MXU)
        dhs = dya_v * gg
        xc = xc_ref[...]
        lam = lam_ref[...]
        xcb, ra, ri, sp, a, mult = _gates(xc, wa_ref, ba_ref[...], wx_ref, bx_ref[...], lam, heads, hd)
        row = lax.broadcasted_iota(jnp.int32, (tm, D), 0)
        cf = _shift_up(a, 1, 1.0, row)
        g = dhs
        s = 1
        while s < tm:
            g = g + cf * _shift_up(g, s, 0.0, row)
            cf = cf * _shift_up(cf, s, 1.0, row)
            s *= 2
        g = g + cf * gcar[...]
        tmp[...] = a * g
        gcar[...] = tmp[0:1, :]
        hprev = jnp.where(r == 0, 0.0, hsp_ref[7:8, :])
        da = g * jnp.where(row >= 1, pltpu.roll(hs, 1, 0), hprev)
        gx = g * xc
        dxc = g * mult * ri
        dla = da * a - (gx * ri) * (a * a) / mult
        dlam_ref[...] += jnp.sum(dla * ra, axis=0, keepdims=True) * (LRU_C * jax.nn.sigmoid(-lam))
        dpa = (dla * ((-LRU_C) * sp)) * ra * (1.0 - ra)
        dpx = (gx * mult) * ri * (1.0 - ri)
        dba_ref[...] += jnp.sum(dpa, axis=0, keepdims=True)
        dbx_ref[...] += jnp.sum(dpx, axis=0, keepdims=True)
        dpab, dpxb = dpa.astype(_MXU), dpx.astype(_MXU)
        for h in range(heads):
            hsl = slice(h * hd, (h + 1) * hd)
            dwa_ref[h] += lax.dot_general(xcb[:, hsl], dpab[:, hsl], (((0,), (0,)), ((), ())),
                                          preferred_element_type=F32)
            dwx_ref[h] += lax.dot_general(xcb[:, hsl], dpxb[:, hsl], (((0,), (0,)), ((), ())),
                                          preferred_element_type=F32)
        dxc = dxc + _bd_mm_t(dpab, wa_ref, heads, hd) + _bd_mm_t(dpxb, wx_ref, heads, hd)
        dbl_ref[...] += jnp.sum(dxc, axis=0, keepdims=True)
        dxbuf[0:tm, :] = dxc
        xl = proj_ref[:, 0:D]
        dxl = None
        for k in range(kl):
            dsh = dxbuf[pl.ds(kl - 1 - k, tm), :]
            term = wl_ref[k:k + 1, :] * dsh
            dxl = term if dxl is None else dxl + term
            dwl_ref[k:k + 1, :] += jnp.sum(dsh * xl, axis=0, keepdims=True)
        dxbuf[tm:tm + HALO_S, :] = dxbuf[0:HALO_S, :]
        dproj_ref[:, 0:D] = dxl.astype(_MXU)

        ca = proj_ref[:, 2 * D:2 * D + Dc]
        sg = jax.nn.sigmoid(proj_ref[:, 2 * D + Dc:3 * D])
        cg = ca * sg
        nrm, rss = _group_norm(cc_ref[...], CONV_GROUPS)
        lg = lg_ref[...]
        cl = nrm * lg + lb_ref[...]
        sc = jax.nn.sigmoid(cl)
        dcl = dyb_ref[...] * (sc * (1.0 + cl * (1.0 - sc)))
        dlg_ref[...] += jnp.sum(dcl * nrm, axis=0, keepdims=True)
        dlb_ref[...] += jnp.sum(dcl, axis=0, keepdims=True)
        dnrm = dcl * lg
        gsz = Dc // CONV_GROUPS
        parts = []
        for gi in range(CONV_GROUPS):
            sl = slice(gi * gsz, (gi + 1) * gsz)
            dn, nn = dnrm[:, sl], nrm[:, sl]
            parts.append(rss[gi] * (dn - jnp.mean(dn, axis=-1, keepdims=True)
                                    - nn * jnp.mean(dn * nn, axis=-1, keepdims=True)))
        dcc = jnp.concatenate(parts, axis=-1)
        dbc_ref[...] += jnp.sum(dcc, axis=0, keepdims=True)
        dcbuf[0:tm, :] = dcc
        _fill_shifted(dcbuf, sbuf, tm + HALO_L - 8)
        dcg = None
        for k in range(kc):
            dsh = _tap31(dcbuf, sbuf, kc - 1 - k, tm)
            term = wc_ref[k:k + 1, :] * dsh
            dcg = term if dcg is None else dcg + term
            dwc_ref[k:k + 1, :] += jnp.sum(dsh * cg, axis=0, keepdims=True)
        dcbuf[tm:tm + HALO_L, :] = dcbuf[0:HALO_L, :]
        dproj_ref[:, 2 * D:2 * D + Dc] = (dcg * sg).astype(_MXU)
        dproj_ref[:, 2 * D + Dc:3 * D] = (dcg * ca * sg * (1.0 - sg)).astype(_MXU)

    rev_d = pl.BlockSpec((tm, D), lambda i: (nt - 1 - i, 0))
    rev_c = pl.BlockSpec((tm, Dc), lambda i: (nt - 1 - i, 0))
    rev_p = pl.BlockSpec((tm, D3), lambda i: (nt - 1 - i, 0))
    prev8 = pl.BlockSpec((8, D), lambda i: (jnp.maximum((nt - 1 - i) * per8 - 1, 0), 0))
    full = lambda *s: pl.BlockSpec(s, lambda i: (0,) * len(s))
    return _ride(
        body, plan, name, nt,
        in_specs=[rev_d, rev_c, rev_p, rev_d, rev_d, prev8, rev_c,
                  _lspec(l, kl, D), _lspec(l, heads, hd, hd), _lspec(l, 1, D), _lspec(l, heads, hd, hd),
                  _lspec(l, 1, D), _lspec(l, 1, D), _lspec(l, kc, Dc), _lspec(l, 1, Dc), _lspec(l, 1, Dc)],
        args=(dya, dyb, proj, xc, hs, hs, cc, p['lru_conv_w'], p['lru_wa'], p['lru_ba'], p['lru_wx'], p['lru_bx'],
              p['lru_lambda'], p['conv_w'], p['conv_ln_g'], p['conv_ln_b']),
        out_specs=[rev_p, full(kl, D), full(1, D), full(heads, hd, hd), full(1, D), full(heads, hd, hd), full(1, D),
                   full(1, D), full(kc, Dc), full(1, Dc), full(1, Dc), full(1, Dc)],
        out_shape=[_sds((T, D3), _MXU), _sds((kl, D), F32), _sds((1, D), F32), _sds((heads, hd, hd), F32),
                   _sds((1, D), F32), _sds((heads, hd, hd), F32), _sds((1, D), F32), _sds((1, D), F32),
                   _sds((kc, Dc), F32), _sds((1, Dc), F32), _sds((1, Dc), F32), _sds((1, Dc), F32)],
        scratch_shapes=[pltpu.VMEM((1, D), F32), pltpu.VMEM((tm + HALO_S, D), F32),
                        pltpu.VMEM((tm + HALO_L, Dc), F32), pltpu.VMEM((tm, D), F32),
                        pltpu.VMEM((7, tm + HALO_L - 8, Dc), F32)])


def _sum_sources(recv, name):
    _, R, C = recv.shape
    rb = _pick_rows(R, 1024)

    def body(r_ref, o_ref):
        acc = r_ref[0]
        for s in range(1, _N_DEV):
            acc = acc + r_ref[s]
        o_ref[...] = acc

    return pl.pallas_call(
        body, name=name, grid=(R // rb,),
        in_specs=[pl.BlockSpec((_N_DEV, rb, C), lambda i: (0, i, 0))],
        out_specs=pl.BlockSpec((rb, C), lambda i: (i, 0)),
        out_shape=_sds((R, C), F32), compiler_params=_params())(recv)


def _adamw(g, w, m, v, name):
    R, C = w.shape
    summed = g.ndim == 3
    rb = _pick_rows(R, max(8, min(512, _ADAM_BLOCK_ELEMS // C)))
    c1 = 1.0 - ADAM_B1 ** ADAM_STEP
    c2 = 1.0 - ADAM_B2 ** ADAM_STEP

    def body(g_ref, w_ref, m_ref, v_ref, go_ref, d_ref, mo_ref, vo_ref):
        if summed:
            gv = g_ref[0]
            for s in range(1, _N_DEV):
                gv = gv + g_ref[s]
        else:
            gv = g_ref[...]
        go_ref[...] = gv
        mn = ADAM_B1 * m_ref[...] + (1.0 - ADAM_B1) * gv
        vn = ADAM_B2 * v_ref[...] + (1.0 - ADAM_B2) * (gv * gv)
        mo_ref[...] = mn
        vo_ref[...] = vn
        d_ref[...] = (-ADAM_LR) * ((mn / c1) / (jnp.sqrt(vn / c2) + ADAM_EPS) + ADAM_WD * w_ref[...])

    blk = pl.BlockSpec((rb, C), lambda i: (i, 0))
    gspec = pl.BlockSpec((_N_DEV, rb, C), lambda i: (0, i, 0)) if summed else blk
    return pl.pallas_call(
        body, name=name, grid=(R // rb,),
        in_specs=[gspec, blk, blk, blk], out_specs=[blk, blk, blk, blk],
        out_shape=[_sds((R, C), F32)] * 4, compiler_params=_params())(g, w, m, v)


def _adamw_layers(recvs, w, m, v, name):
    L, R, C = w.shape
    rb = _pick_rows(R, max(8, _ADAM_BLOCK_ELEMS // (4 * C)))
    c1 = 1.0 - ADAM_B1 ** ADAM_STEP
    c2 = 1.0 - ADAM_B2 ** ADAM_STEP

    def body(*refs):
        r_refs = refs[:L]
        w_ref, m_ref, v_ref, go_ref, d_ref, mo_ref, vo_ref = refs[L:]
        for l in range(L):
            gv = r_refs[l][0]
            for s in range(1, _N_DEV):
                gv = gv + r_refs[l][s]
            go_ref[l] = gv
            mn = ADAM_B1 * m_ref[l] + (1.0 - ADAM_B1) * gv
            vn = ADAM_B2 * v_ref[l] + (1.0 - ADAM_B2) * (gv * gv)
            mo_ref[l] = mn
            vo_ref[l] = vn
            d_ref[l] = (-ADAM_LR) * ((mn / c1) / (jnp.sqrt(vn / c2) + ADAM_EPS) + ADAM_WD * w_ref[l])

    blk = pl.BlockSpec((L, rb, C), lambda i: (0, i, 0))
    return pl.pallas_call(
        body, name=name, grid=(R // rb,),
        in_specs=[pl.BlockSpec((_N_DEV, rb, C), lambda i: (0, i, 0))] * L + [blk, blk, blk],
        out_specs=[blk, blk, blk, blk],
        out_shape=[_sds((L, R, C), F32)] * 4, compiler_params=_params())(*recvs, w, m, v)


def _pack_rows(flat_parts, dtype, row_mult):
    flat = jnp.concatenate([f.reshape(-1).astype(dtype) for f in flat_parts])
    n = flat.shape[0]
    per = _LANES * row_mult
    padded = -(-n // per) * per
    if padded != n:
        flat = jnp.concatenate([flat, jnp.zeros((padded - n,), dtype)])
    return flat.reshape(-1, _LANES)


def _unpack(flat, shapes):
    out, off = [], 0
    for s in shapes:
        n = math.prod(s)
        out.append(flat[off:off + n].reshape(s))
        off += n
    return out


def _to_pieces(full):
    n = full.shape[-1] // _N_DEV
    t = full.reshape(full.shape[:-1] + (_N_DEV, n))
    return jnp.moveaxis(t, -2, 0).reshape(_N_DEV, -1)


def _from_gathered(seg, shard_shape, axis):
    t = seg.reshape((_N_DEV,) + tuple(shard_shape))
    t = jnp.moveaxis(t, 0, axis)
    shape = list(shard_shape)
    shape[axis] *= _N_DEV
    return t.reshape(shape)


def kernel(x, meta_tokens, g_pre_mix, w_in, lru_conv_w, lru_conv_b, lru_wa, lru_ba, lru_wx, lru_bx, lru_lambda, conv_w, conv_b, conv_ln_g, conv_ln_b, g_out_lru, g_out_conv, w_out, g_post_mix, g_pre_ffn, w_up, ffn_conv_w, ffn_conv_b, w_down, g_post_ffn, loss_target, m_meta_tokens, m_g_pre_mix, m_w_in, m_lru_conv_w, m_lru_conv_b, m_lru_wa, m_lru_ba, m_lru_wx, m_lru_bx, m_lru_lambda, m_conv_w, m_conv_b, m_conv_ln_g, m_conv_ln_b, m_g_out_lru, m_g_out_conv, m_w_out, m_g_post_mix, m_g_pre_ffn, m_w_up, m_ffn_conv_w, m_ffn_conv_b, m_w_down, m_g_post_ffn, v_meta_tokens, v_g_pre_mix, v_w_in, v_lru_conv_w, v_lru_conv_b, v_lru_wa, v_lru_ba, v_lru_wx, v_lru_bx, v_lru_lambda, v_conv_w, v_conv_b, v_conv_ln_g, v_conv_ln_b, v_g_out_lru, v_g_out_conv, v_w_out, v_g_post_mix, v_g_pre_ffn, v_w_up, v_ffn_conv_w, v_ffn_conv_b, v_w_down, v_g_post_ffn):
    given = dict(locals())
    W = {n: given[n] for n in W_NAMES}
    M = {n: given['m_' + n] for n in W_NAMES}
    V = {n: given['v_' + n] for n in W_NAMES}
    S, D = x.shape[1], x.shape[2]
    L = g_pre_mix.shape[0]
    Dc = D // 2
    step = math.lcm(_TM_MM, _TM_SEQ)
    T = -(-(N_META + S) // step) * step

    def layer_pack(l):
        return _pack_rows([W[n][l] for n in BIG], _MXU, 16)

    def layer_weights(gathered):
        segs = _unpack_cols(gathered.reshape(_N_DEV, -1), [W[n].shape[1:] for n in BIG])
        return {n: _from_gathered(seg, W[n].shape[1:], 1 if n in ('w_in', 'w_up') else 0)
                for n, seg in zip(BIG, segs)}

    small_pack = _pack_rows([W[n] for n in SMALL_SHARDED], F32, 8)
    big_g, small_g = _run_plan(_GatherPlan([layer_pack(0), small_pack]), "gather_weights_l0")
    small_segs = _unpack_cols(small_g.reshape(_N_DEV, -1), [W[n].shape for n in SMALL_SHARDED])
    full = {}
    for n, seg in zip(SMALL_SHARDED, small_segs):
        full[n] = _from_gathered(seg, W[n].shape, W[n].ndim - 1)

    p = {}
    for n in W_NAMES:
        if n in BIG or n == 'meta_tokens':
            continue
        a = full[n] if n in full else W[n]
        p[n] = a.reshape(L, 1, a.shape[1]) if a.ndim == 2 else a

    pad_rows = T - N_META - S
    h = jnp.concatenate([full['meta_tokens'], x[0], jnp.zeros((pad_rows, D), F32)], axis=0)
    tgt = jnp.concatenate([jnp.zeros((N_META, D), F32), loss_target[0], jnp.zeros((pad_rows, D), F32)], axis=0)

    saved = []
    wl = layer_weights(big_g)
    for l in range(L):
        plan = _GatherPlan([layer_pack(l + 1)]) if l + 1 < L else None
        proj, z1 = _norm_mm(h, p['g_pre_mix'], wl['w_in'], l, f"in_proj_l{l}")
        (xc, hs, ya, cc, yb), nxt = _mixers_fwd(proj, p, l, f"mixers_fwd_l{l}", plan)
        y, o, hmid = _mix_out(ya, yb, h, p, wl['w_out'], l, f"mix_out_l{l}")
        up0, z2 = _norm_mm(hmid, p['g_pre_ffn'], wl['w_up'], l, f"up_proj_l{l}")
        a2, f, hout = _ffn_fwd(up0, hmid, p, wl['w_down'], l, f"ffn_fwd_l{l}")
        saved.append(dict(h=h, z1=z1, proj=proj, xc=xc, hs=hs, ya=ya, cc=cc, yb=yb, y=y, o=o, hmid=hmid,
                          z2=z2, up0=up0, a2=a2, f=f, w=wl))
        h = hout
        if plan:
            wl = layer_weights(nxt[0])
    loss_tile, dh = _loss_head(h, tgt, S, "loss_head")
    loss = lax.psum(loss_tile[0, 0], ("x", "y", "c"))

    small_g_names = ['g_pre_mix', 'lru_conv_w', 'lru_conv_b', 'lru_wa', 'lru_ba', 'lru_wx', 'lru_bx', 'lru_lambda',
                     'conv_w', 'conv_b', 'conv_ln_g', 'conv_ln_b', 'g_out_lru', 'g_out_conv', 'g_post_mix',
                     'g_pre_ffn', 'ffn_conv_w', 'ffn_conv_b', 'g_post_ffn']
    per_layer = {n: [None] * L for n in small_g_names}
    recv = {n: [None] * L for n in BIG}
    d_in_pending = None
    for l in reversed(range(L)):
        sv = saved[l]
        wl = sv['w']
        plan = _ExchangePlan([d_in_pending], ['cols']) if d_in_pending is not None else None
        (df, dup0, dwf, dbf, dgpf), got = _ffn_bwd(dh, sv['f'], sv['up0'], p, wl['w_down'], l, f"ffn_bwd_l{l}", plan)
        if plan:
            recv['w_in'][l + 1] = got[0]
        d_down = _mm_tn(sv['a2'], df, f"dw_down_l{l}")
        d_up = _mm_tn(sv['z2'], dup0, f"dw_up_l{l}")
        dhm, dgpre = _mm_nt_norm_bwd(dup0, wl['w_up'], sv['hmid'], p['g_pre_ffn'], dh, l, f"up_bwd_l{l}")
        do, dya, dyb, dgpm, dgol, dgoc = _mix_bwd(dhm, sv['o'], sv['ya'], sv['yb'], p, wl['w_out'], l,
                                                  f"mix_bwd_l{l}")
        d_out = _mm_tn(sv['y'], do, f"dw_out_l{l}")
        plan = _ExchangePlan([d_down, d_up, d_out], ['rows', 'cols', 'rows'])
        (dproj, dwl, dbl, dwa, dba, dwx, dbx, dlam, dwc, dbc, dlg, dlb), got = _mixers_bwd(
            dya, dyb, sv['proj'], sv['xc'], sv['hs'], sv['cc'], p, l, f"mixers_bwd_l{l}", plan)
        recv['w_down'][l], recv['w_up'][l], recv['w_out'][l] = got
        d_in_pending = _mm_tn(sv['z1'], dproj, f"dw_in_l{l}")
        dh, dgpmix = _mm_nt_norm_bwd(dproj, wl['w_in'], sv['h'], p['g_pre_mix'], dhm, l, f"in_bwd_l{l}")
        for n, val in (('g_pre_mix', dgpmix), ('lru_conv_w', dwl), ('lru_conv_b', dbl), ('lru_wa', dwa),
                       ('lru_ba', dba), ('lru_wx', dwx), ('lru_bx', dbx), ('lru_lambda', dlam), ('conv_w', dwc),
                       ('conv_b', dbc), ('conv_ln_g', dlg), ('conv_ln_b', dlb), ('g_out_lru', dgol),
                       ('g_out_conv', dgoc), ('g_post_mix', dgpm), ('g_pre_ffn', dgpre), ('ffn_conv_w', dwf),
                       ('ffn_conv_b', dbf), ('g_post_ffn', dgpf)):
            per_layer[n][l] = val
    grad_x = dh[N_META:N_META + S][None]
    partial = {n: jnp.stack(per_layer[n]).reshape((L,) + tuple(
        (full[n] if n in full else W[n]).shape[1:])) for n in small_g_names}
    partial['meta_tokens'] = dh[0:N_META]

    shard_pack = jnp.concatenate([_to_pieces(partial[n]) for n in SMALL_SHARDED], axis=1)
    n_sh = shard_pack.shape[1]
    rs = -(-n_sh // (8 * _LANES)) * 8
    shard_pack = jnp.concatenate([shard_pack, jnp.zeros((_N_DEV, rs * _LANES - n_sh), F32)], axis=1)
    rep_flat = jnp.concatenate([partial[n].reshape(-1) for n in REPLICATED])
    n_rep = rep_flat.shape[0]
    rr = -(-n_rep // (_N_DEV * 8 * _LANES)) * 8
    rep_flat = jnp.concatenate([rep_flat, jnp.zeros((_N_DEV * rr * _LANES - n_rep,), F32)])
    small_send = jnp.concatenate([shard_pack, rep_flat.reshape(_N_DEV, rr * _LANES)], axis=1)
    small_send = small_send.reshape(_N_DEV, rs + rr, _LANES)
    recv['w_in'][0], r_small = _run_plan(_ExchangePlan([d_in_pending, small_send], ['cols', 'slots']),
                                         "grad_exchange_last")
    small_red = _sum_sources(r_small, "sum_small")
    (rep_g,) = _run_plan(_GatherPlan([small_red[rs:]]), "gather_replicated_grads")
    rep_g = rep_g.reshape(_N_DEV * rr, _LANES)

    out = {}
    for n in BIG:
        out[n] = list(_adamw_layers(recv[n], W[n], M[n], V[n], f"adamw_{n}"))
    sh_shapes = [W[n].shape for n in SMALL_SHARDED]
    res = _adamw(small_red[:rs], _pack_rows([W[n] for n in SMALL_SHARDED], F32, 8),
                 _pack_rows([M[n] for n in SMALL_SHARDED], F32, 8),
                 _pack_rows([V[n] for n in SMALL_SHARDED], F32, 8), "adamw_small_sharded")
    for k in range(4):
        for n, val in zip(SMALL_SHARDED, _unpack(res[k].reshape(-1), sh_shapes)):
            out.setdefault(n, [None] * 4)[k] = val
    rep_shapes = [W[n].shape for n in REPLICATED]
    res = _adamw(rep_g, _pack_rows([W[n] for n in REPLICATED], F32, 8 * _N_DEV),
                 _pack_rows([M[n] for n in REPLICATED], F32, 8 * _N_DEV),
                 _pack_rows([V[n] for n in REPLICATED], F32, 8 * _N_DEV), "adamw_replicated")
    for k in range(4):
        for n, val in zip(REPLICATED, _unpack(res[k].reshape(-1), rep_shapes)):
            out.setdefault(n, [None] * 4)[k] = val

    return (loss, grad_x, *[out[n][0] for n in W_NAMES], *[out[n][1] for n in W_NAMES],
            *[out[n][2] for n in W_NAMES], *[out[n][3] for n in W_NAMES])


def _unpack_cols(gathered, shapes):
    out, off = [], 0
    for s in shapes:
        n = math.prod(s)
        out.append(gathered[:, off:off + n])
        off += n
    return out
```

```python
import math

import jax
import jax.numpy as jnp
from jax import lax
from jax.experimental import pallas as pl
from jax.experimental.pallas import tpu as pltpu

F32 = jnp.float32
_MXU = jnp.bfloat16
_TM_MM = 768
_TM_SEQ = 256
_NB = 768
_VMEM_LIMIT = 56 * 1024 * 1024
_ADAM_BLOCK_ELEMS = 128 * 1024
_LANES = 128
_N_DEV = 8

EPS = 1e-6
N_META = 16
LRU_C = 8.0
CONV_GROUPS = 4
ADAM_LR, ADAM_B1, ADAM_B2, ADAM_EPS, ADAM_WD, ADAM_STEP = 0.001, 0.9, 0.999, 1e-08, 0.01, 10
_GELU_K0 = math.sqrt(2.0 / math.pi)
_GELU_K1 = 0.044715

W_NAMES = ['meta_tokens', 'g_pre_mix', 'w_in', 'lru_conv_w', 'lru_conv_b', 'lru_wa', 'lru_ba', 'lru_wx', 'lru_bx',
           'lru_lambda', 'conv_w', 'conv_b', 'conv_ln_g', 'conv_ln_b', 'g_out_lru', 'g_out_conv', 'w_out',
           'g_post_mix', 'g_pre_ffn', 'w_up', 'ffn_conv_w', 'ffn_conv_b', 'w_down', 'g_post_ffn']
BIG = ['w_in', 'w_out', 'w_up', 'w_down']
SMALL_SHARDED = ['meta_tokens', 'lru_conv_w', 'conv_w', 'ffn_conv_w']
REPLICATED = [n for n in W_NAMES if n not in BIG and n not in SMALL_SHARDED]


def _params():
    return pltpu.CompilerParams(vmem_limit_bytes=_VMEM_LIMIT)


def _pick(n, pref):
    if n <= pref:
        return n
    best = None
    for b in range(_LANES, pref + 1, _LANES):
        if n % b == 0:
            best = b
    assert best is not None, (n, pref)
    return best


def _pick_rows(n, pref):
    if n <= pref:
        return n
    best = None
    for b in range(8, pref + 1, 8):
        if n % b == 0:
            best = b
    assert best is not None, (n, pref)
    return best


def _lspec(l, *dims):
    zeros = (0,) * len(dims)
    return pl.BlockSpec((None,) + tuple(dims), lambda *_: (l,) + zeros)


def _sds(shape, dtype):
    return jax.ShapeDtypeStruct(tuple(shape), dtype)


def _rms_fwd(x, g):
    r = lax.rsqrt(jnp.mean(x * x, axis=-1, keepdims=True) + EPS)
    return (x * r) * g


def _rms_bwd(x, g, dy):
    r = lax.rsqrt(jnp.mean(x * x, axis=-1, keepdims=True) + EPS)
    xh = x * r
    dg = jnp.sum(dy * xh, axis=0, keepdims=True)
    dxh = dy * g
    dx = r * (dxh - xh * jnp.mean(dxh * xh, axis=-1, keepdims=True))
    return dx, dg


def _gelu(x):
    t = jnp.tanh(_GELU_K0 * (x + _GELU_K1 * (x * x * x)))
    return 0.5 * x * (1.0 + t), t


def _gelu_grad(x, t):
    return 0.5 * (1.0 + t) + 0.5 * x * (1.0 - t * t) * (_GELU_K0 * (1.0 + 3.0 * _GELU_K1 * x * x))


def _log1p(e):
    u = 1.0 + e
    return jnp.where(u == 1.0, e, jnp.log(u) * (e / (u - 1.0)))


def _softplus(z):
    return jnp.maximum(z, 0.0) + _log1p(jnp.exp(-jnp.abs(z)))


def _one_minus_exp(x):
    p = -x * (1.0 + x * (0.5 + x * (1.0 / 6 + x * (1.0 / 24 + x * (1.0 / 120 + x * (1.0 / 720))))))
    return jnp.where(x > -0.125, p, 1.0 - jnp.exp(x))


def _shift_down(x, s, fill, row):
    return jnp.where(row >= s, pltpu.roll(x, s, 0), fill)


def _shift_up(x, s, fill, row):
    n = x.shape[0]
    return jnp.where(row < n - s, pltpu.roll(x, n - s, 0), fill)


def _bd_mm(xb, w_ref, heads, hd):
    return jnp.concatenate(
        [jnp.dot(xb[:, h * hd:(h + 1) * hd], w_ref[h].astype(_MXU), preferred_element_type=F32)
         for h in range(heads)], axis=-1)


def _bd_mm_t(db, w_ref, heads, hd):
    return jnp.concatenate(
        [lax.dot_general(db[:, h * hd:(h + 1) * hd], w_ref[h].astype(_MXU), (((1,), (1,)), ((), ())),
                         preferred_element_type=F32)
         for h in range(heads)], axis=-1)


def _gates(xc, wa_ref, ba, wx_ref, bx, lam, heads, hd):
    xcb = xc.astype(_MXU)
    ra = jax.nn.sigmoid(_bd_mm(xcb, wa_ref, heads, hd) + ba)
    ri = jax.nn.sigmoid(_bd_mm(xcb, wx_ref, heads, hd) + bx)
    sp = _softplus(-lam)
    la = (-LRU_C) * ra * sp
    a = jnp.exp(la)
    mult = jnp.sqrt(_one_minus_exp(2.0 * la))
    return xcb, ra, ri, sp, a, mult


def _group_norm(cc, groups):
    gs = cc.shape[-1] // groups
    outs, rss = [], []
    for g in range(groups):
        seg = cc[:, g * gs:(g + 1) * gs]
        mu = jnp.mean(seg, axis=-1, keepdims=True)
        d = seg - mu
        rs = lax.rsqrt(jnp.mean(d * d, axis=-1, keepdims=True) + EPS)
        outs.append(d * rs)
        rss.append(rs)
    return jnp.concatenate(outs, axis=-1), rss


_MESH = pl.DeviceIdType.MESH
_ANY = pl.BlockSpec(memory_space=pl.ANY)


def _my_index():
    return 4 * lax.axis_index("x") + 2 * lax.axis_index("y") + lax.axis_index("c")


class _GatherPlan:
    def __init__(self, blocks):
        self.blocks = list(blocks)

    def operands(self):
        return self.blocks

    def out_shape(self):
        return [_sds((_N_DEV,) + b.shape, b.dtype) for b in self.blocks]

    def scratch(self):
        n = len(self.blocks)
        return [pltpu.SemaphoreType.DMA((7 * n,)), pltpu.SemaphoreType.DMA((7 * n,)), pltpu.SemaphoreType.DMA((n,))]

    def _copies(self, a, ins, outs, sems):
        send_sems, recv_sems, local_sems = sems
        x, y, c = lax.axis_index("x"), lax.axis_index("y"), lax.axis_index("c")
        me, sibling = (x, y, c), (x, y, 1 - c)
        chips = [(1 - x, y), (x, 1 - y), (1 - x, 1 - y)]

        def slot(dev):
            return outs[a].at[4 * dev[0] + 2 * dev[1] + dev[2]]

        def copy(k, block, to, src=None):
            return pltpu.make_async_remote_copy(
                src_ref=slot(block) if src is None else src, dst_ref=slot(block),
                send_sem=send_sems.at[7 * a + k], recv_sem=recv_sems.at[7 * a + k],
                device_id=to, device_id_type=_MESH)

        mine = pltpu.make_async_copy(ins[a], slot(me), local_sems.at[a])
        first = [copy(0, me, sibling, src=ins[a])]
        first += [copy(1 + j, me, (*chip, c), src=ins[a]) for j, chip in enumerate(chips)]
        passed = [copy(4 + j, (*chip, c), sibling) for j, chip in enumerate(chips)]
        from_chips = [copy(1 + j, (*chip, c), me) for j, chip in enumerate(chips)]
        from_sibling = [copy(0, sibling, me)] + [copy(4 + j, (*chip, 1 - c), me) for j, chip in enumerate(chips)]
        return mine, first, passed, from_chips, from_sibling

    def start(self, ins, outs, sems):
        for a in range(len(self.blocks)):
            mine, first, _, _, _ = self._copies(a, ins, outs, sems)
            mine.start()
            for cp in first:
                cp.start()

    def forward(self, ins, outs, sems):
        for a in range(len(self.blocks)):
            _, _, passed, from_chips, _ = self._copies(a, ins, outs, sems)
            for j in range(3):
                from_chips[j].wait_recv()
                passed[j].start()

    def finish(self, ins, outs, sems):
        for a in range(len(self.blocks)):
            mine, first, passed, _, from_sibling = self._copies(a, ins, outs, sems)
            for cp in from_sibling:
                cp.wait_recv()
            for cp in first + passed:
                cp.wait_send()
            mine.wait()

    def begin(self, i, steps, ins, outs, sems):
        @pl.when(i == 0)
        def _():
            self.start(ins, outs, sems)

        @pl.when(i == (3 * steps) // 4)
        def _():
            self.forward(ins, outs, sems)

    def end(self, i, steps, ins, outs, sems):
        @pl.when(i == steps - 1)
        def _():
            self.finish(ins, outs, sems)


class _ExchangePlan:
    def __init__(self, arrs, kinds):
        self.arrs, self.kinds = list(arrs), list(kinds)

    def _piece_shape(self, a):
        shp = self.arrs[a].shape
        if self.kinds[a] == 'cols':
            return (shp[0], shp[1] // _N_DEV)
        if self.kinds[a] == 'rows':
            return (shp[0] // _N_DEV, shp[1])
        return tuple(shp[1:])

    def operands(self):
        return self.arrs

    def out_shape(self):
        return [_sds((_N_DEV,) + self._piece_shape(a), F32) for a in range(len(self.arrs))]

    def scratch(self):
        n = len(self.arrs)
        return [pltpu.SemaphoreType.DMA((_N_DEV * n,)), pltpu.SemaphoreType.DMA((_N_DEV * n,)),
                pltpu.SemaphoreType.DMA((n,))]

    def _copies(self, ins, outs, sems):
        send_sems, recv_sems, local_sems = sems
        me = _my_index()

        def piece(a, j):
            ps = self._piece_shape(a)
            if self.kinds[a] == 'cols':
                return ins[a].at[:, pl.ds(j * ps[1], ps[1])]
            if self.kinds[a] == 'rows':
                return ins[a].at[pl.ds(j * ps[0], ps[0]), :]
            return ins[a].at[j]

        def remote(a, j):
            return pltpu.make_async_remote_copy(
                src_ref=piece(a, j), dst_ref=outs[a].at[me],
                send_sem=send_sems.at[_N_DEV * a + j], recv_sem=recv_sems.at[_N_DEV * a + me],
                device_id=(j >> 2, (j >> 1) & 1, j & 1), device_id_type=_MESH)

        def arrival(a, s):
            return pltpu.make_async_remote_copy(
                src_ref=piece(a, s), dst_ref=outs[a].at[s],
                send_sem=send_sems.at[_N_DEV * a + s], recv_sem=recv_sems.at[_N_DEV * a + s],
                device_id=(s >> 2, (s >> 1) & 1, s & 1), device_id_type=_MESH)

        def local(a, j):
            return pltpu.make_async_copy(piece(a, j), outs[a].at[j], local_sems.at[a])

        return me, remote, arrival, local

    def start(self, ins, outs, sems):
        me, remote, _, local = self._copies(ins, outs, sems)
        for j in range(_N_DEV):
            for a in range(len(self.arrs)):
                @pl.when(me != j)
                def _(a=a, j=j):
                    remote(a, j).start()

                @pl.when(me == j)
                def _(a=a, j=j):
                    local(a, j).start()

    def finish(self, ins, outs, sems):
        me, remote, arrival, local = self._copies(ins, outs, sems)
        for j in range(_N_DEV):
            for a in range(len(self.arrs)):
                @pl.when(me != j)
                def _(a=a, j=j):
                    arrival(a, j).wait_recv()
                    remote(a, j).wait_send()

                @pl.when(me == j)
                def _(a=a, j=j):
                    local(a, j).wait()

    def forward(self, ins, outs, sems):
        pass

    def begin(self, i, steps, ins, outs, sems):
        @pl.when(i == 0)
        def _():
            self.start(ins, outs, sems)

    def end(self, i, steps, ins, outs, sems):
        @pl.when(i == steps - 1)
        def _():
            self.finish(ins, outs, sems)


def _ride(main, plan, name, grid, in_specs, args, out_specs, out_shape, scratch_shapes):
    grid = (grid,) if isinstance(grid, int) else tuple(grid)
    steps = math.prod(grid)
    n_in, n_out, n_sc = len(in_specs), len(out_specs), len(scratch_shapes)
    p_args = plan.operands() if plan else []
    p_out = plan.out_shape() if plan else []
    p_sc = plan.scratch() if plan else []

    def body(*refs):
        k = 0
        ins = refs[k:k + n_in]; k += n_in
        p_ins = refs[k:k + len(p_args)]; k += len(p_args)
        outs = refs[k:k + n_out]; k += n_out
        p_outs = refs[k:k + len(p_out)]; k += len(p_out)
        scr = refs[k:k + n_sc]; k += n_sc
        sems = refs[k:]
        i = pl.program_id(0)
        for axis in range(1, len(grid)):
            i = i * grid[axis] + pl.program_id(axis)
        if plan:
            plan.begin(i, steps, p_ins, p_outs, sems)
        main(*ins, *outs, *scr)
        if plan:
            plan.end(i, steps, p_ins, p_outs, sems)

    res = pl.pallas_call(
        body, name=name, grid=grid,
        in_specs=list(in_specs) + [_ANY] * len(p_args),
        out_specs=list(out_specs) + [_ANY] * len(p_out),
        out_shape=list(out_shape) + p_out,
        scratch_shapes=list(scratch_shapes) + p_sc,
        compiler_params=_params())(*args, *p_args)
    return res[:n_out], res[n_out:]


def _run_plan(plan, name):
    n_args, n_out = len(plan.operands()), len(plan.out_shape())

    def body(*refs):
        ins, outs, sems = refs[:n_args], refs[n_args:n_args + n_out], refs[n_args + n_out:]
        plan.start(ins, outs, sems)
        plan.forward(ins, outs, sems)
        plan.finish(ins, outs, sems)

    return pl.pallas_call(
        body, name=name, in_specs=[_ANY] * n_args, out_specs=[_ANY] * n_out,
        out_shape=plan.out_shape(), scratch_shapes=plan.scratch())(*plan.operands())


def _norm_mm(x, g3, w, l, name, plan=None):
    T, D = x.shape
    N = w.shape[1]
    tm, nb = _TM_MM, _pick(N, _NB)

    def body(x_ref, g_ref, w_ref, o_ref, z_ref):
        @pl.when(pl.program_id(1) == 0)
        def _():
            z_ref[...] = _rms_fwd(x_ref[...], g_ref[...]).astype(z_ref.dtype)

        o_ref[...] = jnp.dot(z_ref[...], w_ref[...], preferred_element_type=F32)

    return _ride(
        body, plan, name, (T // tm, N // nb),
        in_specs=[pl.BlockSpec((tm, D), lambda i, j: (i, 0)), _lspec(l, 1, D),
                  pl.BlockSpec((D, nb), lambda i, j: (0, j))],
        args=(x, g3, w),
        out_specs=[pl.BlockSpec((tm, nb), lambda i, j: (i, j)), pl.BlockSpec((tm, D), lambda i, j: (i, 0))],
        out_shape=[_sds((T, N), F32), _sds((T, D), _MXU)],
        scratch_shapes=[])


def _perm(a, tm):
    T, C = a.shape
    return a.reshape(T // tm, 8, tm // 8, C).transpose(0, 2, 1, 3).reshape(T, C)


def _unperm(a, tm):
    T, C = a.shape
    return a.reshape(T // tm, tm // 8, 8, C).transpose(0, 2, 1, 3).reshape(T, C)


def _wrap_prev(prev_z, z):
    n = z.shape[0]
    sub = lax.broadcasted_iota(jnp.int32, z.shape, 0) & 7
    return jnp.where(sub == 0, pltpu.roll(prev_z, n - 7, 0), pltpu.roll(z, 1, 0))


def _wrap_next(next_z, z):
    n = z.shape[0]
    sub = lax.broadcasted_iota(jnp.int32, z.shape, 0) & 7
    return jnp.where(sub == 7, pltpu.roll(next_z, 7, 0), pltpu.roll(z, n - 1, 0))


def _fill_causal(buf, halo, x, tm):
    nh = halo.shape[0]
    buf[nh:nh + tm, :] = x
    z = buf[tm:tm + nh, :]
    buf[0:nh, :] = _wrap_prev(halo[...], z)
    halo[...] = z


def _fill_anticausal(buf, halo, dy, tm):
    nh = halo.shape[0]
    buf[0:tm, :] = dy
    z = buf[0:nh, :]
    buf[tm:tm + nh, :] = _wrap_next(halo[...], z)
    halo[...] = z


def _scan_fwd(a, u, abuf, ubuf, hcar, tm):
    D = a.shape[1]
    G = tm // 8
    abuf[...] = a
    ubuf[...] = u

    def step(j, c):
        h, pr = c
        r = pl.multiple_of(j * 8, 8)
        aj = abuf[pl.ds(r, 8), :]
        h = aj * h + ubuf[pl.ds(r, 8), :]
        pr = aj * pr
        ubuf[pl.ds(r, 8), :] = h
        abuf[pl.ds(r, 8), :] = pr
        return h, pr

    sub8 = lax.broadcasted_iota(jnp.int32, (8, D), 0)
    e, q = lax.fori_loop(1, G, step, (ubuf[0:8, :], abuf[0:8, :]))
    for s in (1, 2, 4):
        e = e + q * _shift_down(e, s, 0.0, sub8)
        q = q * _shift_down(q, s, 1.0, sub8)
    e = e + q * hcar[...]
    cin = jnp.where(sub8 == 0, hcar[...], pltpu.roll(e, 1, 0))
    return ubuf[...] + abuf[...] * jnp.tile(cin, (G, 1))


def _scan_bwd(a, d, abuf, gbuf, gcar, tmp8, tm):
    D = a.shape[1]
    G = tm // 8
    abuf[...] = a
    gbuf[...] = d

    def step(k, c):
        g_next, a_next, r_j = c
        r = pl.multiple_of((G - 1 - k) * 8, 8)
        aj = abuf[pl.ds(r, 8), :]
        g = gbuf[pl.ds(r, 8), :] + a_next * g_next
        gbuf[pl.ds(r, 8), :] = g
        abuf[pl.ds(r, 8), :] = r_j
        return g, aj, aj * r_j

    last = 8 * (G - 1)
    a_last = abuf[last:last + 8, :]
    abuf[last:last + 8, :] = jnp.ones((8, D), F32)
    g0, a0, _ = lax.fori_loop(1, G, step, (gbuf[last:last + 8, :], a_last, a_last))
    r0 = abuf[0:8, :]
    sub8 = lax.broadcasted_iota(jnp.int32, (8, D), 0)
    x, q = a0 * g0, a0 * r0
    for s in (1, 2, 4):
        x = x + q * _shift_up(x, s, 0.0, sub8)
        q = q * _shift_up(q, s, 1.0, sub8)
    x = x + q * gcar[...]
    cin = jnp.where(sub8 == 7, gcar[...], pltpu.roll(x, 7, 0))
    g = gbuf[...] + abuf[...] * jnp.tile(cin, (G, 1))
    tmp8[...] = x
    gcar[...] = tmp8[0:1, :]
    return g


def _mixers_fwd(proj, p, l, name, plan=None):
    T, D3 = proj.shape
    D = D3 // 3
    Dc = D // 2
    heads, hd = p['lru_wa'].shape[1], p['lru_wa'].shape[2]
    tm = _TM_SEQ
    kl, kc = p['lru_conv_w'].shape[1], p['conv_w'].shape[1]
    nhl, nhc = 8 * (kl - 1), 8 * (kc - 1)
    assert nhc <= tm

    def body(proj_ref, wl_ref, bl_ref, wa_ref, ba_ref, wx_ref, bx_ref, lam_ref, wc_ref, bc_ref, lg_ref, lb_ref,
             xc_ref, hs_ref, ya_ref, cc_ref, yb_ref, xbuf, xhalo, hcar, cbuf, chalo, abuf, ubuf):
        @pl.when(pl.program_id(0) == 0)
        def _():
            xhalo[...] = jnp.zeros_like(xhalo)
            chalo[...] = jnp.zeros_like(chalo)
            hcar[...] = jnp.zeros_like(hcar)

        _fill_causal(xbuf, xhalo, proj_ref[:, 0:D], tm)
        xc = bl_ref[...] + wl_ref[0:1, :] * xbuf[0:tm, :]
        for k in range(1, kl):
            xc = xc + wl_ref[k:k + 1, :] * xbuf[8 * k:8 * k + tm, :]
        xc_ref[...] = xc
        _, ra, ri, sp, a, mult = _gates(xc, wa_ref, ba_ref[...], wx_ref, bx_ref[...], lam_ref[...], heads, hd)
        hs = _scan_fwd(a, mult * (ri * xc), abuf, ubuf, hcar, tm)
        hs_ref[...] = hs
        hcar[...] = hs_ref[pl.ds(tm - 1, 1), :]
        gg, _ = _gelu(proj_ref[:, D:2 * D])
        ya_ref[...] = hs * gg

        _fill_causal(cbuf, chalo,
                     proj_ref[:, 2 * D:2 * D + Dc] * jax.nn.sigmoid(proj_ref[:, 2 * D + Dc:3 * D]), tm)
        cc = bc_ref[...] + wc_ref[0:1, :] * cbuf[0:tm, :]
        for k in range(1, kc):
            cc = cc + wc_ref[k:k + 1, :] * cbuf[8 * k:8 * k + tm, :]
        cc_ref[...] = cc
        nrm, _ = _group_norm(cc, CONV_GROUPS)
        cl = nrm * lg_ref[...] + lb_ref[...]
        yb_ref[...] = cl * jax.nn.sigmoid(cl)

    row_d = pl.BlockSpec((tm, D), lambda i: (i, 0))
    row_c = pl.BlockSpec((tm, Dc), lambda i: (i, 0))
    return _ride(
        body, plan, name, T // tm,
        in_specs=[pl.BlockSpec((tm, D3), lambda i: (i, 0)),
                  _lspec(l, kl, D), _lspec(l, 1, D), _lspec(l, heads, hd, hd), _lspec(l, 1, D),
                  _lspec(l, heads, hd, hd), _lspec(l, 1, D), _lspec(l, 1, D),
                  _lspec(l, kc, Dc), _lspec(l, 1, Dc), _lspec(l, 1, Dc), _lspec(l, 1, Dc)],
        args=(proj, p['lru_conv_w'], p['lru_conv_b'], p['lru_wa'], p['lru_ba'], p['lru_wx'], p['lru_bx'],
              p['lru_lambda'], p['conv_w'], p['conv_b'], p['conv_ln_g'], p['conv_ln_b']),
        out_specs=[row_d, row_d, row_d, row_c, row_c],
        out_shape=[_sds((T, D), F32), _sds((T, D), F32), _sds((T, D), F32), _sds((T, Dc), F32), _sds((T, Dc), F32)],
        scratch_shapes=[pltpu.VMEM((nhl + tm, D), F32), pltpu.VMEM((nhl, D), F32), pltpu.VMEM((1, D), F32),
                        pltpu.VMEM((nhc + tm, Dc), F32), pltpu.VMEM((nhc, Dc), F32),
                        pltpu.VMEM((tm, D), F32), pltpu.VMEM((tm, D), F32)])


def _mix_out(ya, yb, h, p, wout, l, name):
    T, D = ya.shape
    Dc = yb.shape[1]
    tm = _TM_SEQ

    def body(ya_ref, yb_ref, h_ref, gl_ref, gc_ref, gp_ref, w_ref, y_ref, o_ref, hm_ref):
        y = jnp.concatenate([_rms_fwd(ya_ref[...], gl_ref[...]), _rms_fwd(yb_ref[...], gc_ref[...])],
                            axis=-1).astype(_MXU)
        y_ref[...] = y
        o = jnp.dot(y, w_ref[...], preferred_element_type=F32)
        o_ref[...] = o
        hm_ref[...] = h_ref[...] + _rms_fwd(o, gp_ref[...])

    row_d = pl.BlockSpec((tm, D), lambda i: (i, 0))
    return pl.pallas_call(
        body, name=name, grid=(T // tm,),
        in_specs=[row_d, pl.BlockSpec((tm, Dc), lambda i: (i, 0)), row_d,
                  _lspec(l, 1, D), _lspec(l, 1, Dc), _lspec(l, 1, D), pl.BlockSpec((D + Dc, D), lambda i: (0, 0))],
        out_specs=[pl.BlockSpec((tm, D + Dc), lambda i: (i, 0)), row_d, row_d],
        out_shape=[_sds((T, D + Dc), _MXU), _sds((T, D), F32), _sds((T, D), F32)],
        compiler_params=_params(),
    )(ya, yb, h, p['g_out_lru'], p['g_out_conv'], p['g_post_mix'], wout)


def _ffn_fwd(up0, hmid, p, wdown, l, name):
    T, F2 = up0.shape
    Fh = F2 // 2
    D = hmid.shape[1]
    tm = _TM_SEQ
    cw = _pick(Fh, _NB)
    kf = p['ffn_conv_w'].shape[1]

    nh = 8 * (kf - 1)

    def body(up_ref, hm_ref, wf_ref, bf_ref, g_ref, wd_ref, a2_ref, f_ref, ho_ref, ubuf, uhalo):
        @pl.when(pl.program_id(0) == 0)
        def _():
            uhalo[...] = jnp.zeros_like(uhalo)

        _fill_causal(ubuf, uhalo, up_ref[...], tm)

        def conv(cs):
            acc = bf_ref[:, cs]
            for k in range(kf):
                acc = acc + wf_ref[k:k + 1, cs] * ubuf[8 * k:8 * k + tm, cs]
            return acc

        f = None
        for c in range(Fh // cw):
            gs = slice(c * cw, (c + 1) * cw)
            gg, _ = _gelu(conv(gs))
            a2 = (gg * conv(slice(Fh + c * cw, Fh + (c + 1) * cw))).astype(_MXU)
            a2_ref[:, gs] = a2
            part = jnp.dot(a2, wd_ref[gs, :], preferred_element_type=F32)
            f = part if f is None else f + part
        f_ref[...] = f
        ho_ref[...] = hm_ref[...] + _rms_fwd(f, g_ref[...])

    row_d = pl.BlockSpec((tm, D), lambda i: (i, 0))
    return pl.pallas_call(
        body, name=name, grid=(T // tm,),
        in_specs=[pl.BlockSpec((tm, F2), lambda i: (i, 0)), row_d, _lspec(l, kf, F2), _lspec(l, 1, F2),
                  _lspec(l, 1, D), pl.BlockSpec((Fh, D), lambda i: (0, 0))],
        out_specs=[pl.BlockSpec((tm, Fh), lambda i: (i, 0)), row_d, row_d],
        out_shape=[_sds((T, Fh), _MXU), _sds((T, D), F32), _sds((T, D), F32)],
        scratch_shapes=[pltpu.VMEM((nh + tm, F2), F32), pltpu.VMEM((nh, F2), F32)],
        compiler_params=_params(),
    )(up0, hmid, p['ffn_conv_w'], p['ffn_conv_b'], p['g_post_ffn'], wdown)


def _loss_head(h, tgt, n_real, name):
    T, D = h.shape
    tm = _TM_SEQ

    def body(h_ref, t_ref, loss_ref, dh_ref):
        i = pl.program_id(0)

        @pl.when(i == 0)
        def _():
            loss_ref[...] = jnp.zeros_like(loss_ref)

        pos = lax.broadcasted_iota(jnp.int32, (tm, D), 0)
        row = i * tm + (pos & 7) * (tm // 8) + (pos >> 3)
        e = jnp.where((row >= N_META) & (row < N_META + n_real), h_ref[...] - t_ref[...], 0.0)
        dh_ref[...] = e * (1.0 / D)
        loss_ref[...] += 0.5 * jnp.sum(jnp.mean(e * e, axis=-1, keepdims=True), axis=0, keepdims=True)

    row_d = pl.BlockSpec((tm, D), lambda i: (i, 0))
    return pl.pallas_call(
        body, name=name, grid=(T // tm,),
        in_specs=[row_d, row_d],
        out_specs=[pl.BlockSpec((8, _LANES), lambda i: (0, 0)), row_d],
        out_shape=[_sds((8, _LANES), F32), _sds((T, D), F32)],
        compiler_params=_params())(h, tgt)


def _ffn_bwd(dh, f, up0, p, wdown, l, name, plan=None):
    T, F2 = up0.shape
    Fh = F2 // 2
    D = dh.shape[1]
    tm = _TM_SEQ
    nt = T // tm
    cw = _pick(Fh, _NB)
    kf = p['ffn_conv_w'].shape[1]
    nh = 8 * (kf - 1)
    assert tm % nh == 0

    def body(dh_ref, f_ref, up_ref, upp_ref, wf_ref, bf_ref, g_ref, wd_ref,
             df_ref, dup_ref, dwf_ref, dbf_ref, dg_ref, ubuf, dbuf, dhalo):
        i = pl.program_id(0)
        r = nt - 1 - i

        @pl.when(i == 0)
        def _():
            dhalo[...] = jnp.zeros_like(dhalo)
            dwf_ref[...] = jnp.zeros_like(dwf_ref)
            dbf_ref[...] = jnp.zeros_like(dbf_ref)
            dg_ref[...] = jnp.zeros_like(dg_ref)

        df, dg = _rms_bwd(f_ref[...], g_ref[...], dh_ref[...])
        dg_ref[...] += dg
        dfb = df.astype(_MXU)
        df_ref[...] = dfb
        ubuf[nh:nh + tm, :] = up_ref[...]
        ubuf[0:nh, :] = _wrap_prev(jnp.where(r == 0, 0.0, upp_ref[...]), up_ref[tm - nh:tm, :])

        def conv(cs):
            acc = bf_ref[:, cs]
            for k in range(kf):
                acc = acc + wf_ref[k:k + 1, cs] * ubuf[8 * k:8 * k + tm, cs]
            return acc

        for c in range(Fh // cw):
            gs = slice(c * cw, (c + 1) * cw)
            us = slice(Fh + c * cw, Fh + (c + 1) * cw)
            ug = conv(gs)
            gg, t = _gelu(ug)
            da2 = lax.dot_general(dfb, wd_ref[gs, :], (((1,), (1,)), ((), ())), preferred_element_type=F32)
            dbuf[0:tm, gs] = da2 * conv(us) * _gelu_grad(ug, t)
            dbuf[0:tm, us] = da2 * gg
        z = dbuf[0:nh, :]
        dbuf[tm:tm + nh, :] = _wrap_next(dhalo[...], z)
        dhalo[...] = z
        for c in range(F2 // cw):
            cs = slice(c * cw, (c + 1) * cw)
            upc = up_ref[:, cs]
            dup = None
            for k in range(kf):
                dsh = dbuf[8 * (kf - 1 - k):8 * (kf - 1 - k) + tm, cs]
                term = wf_ref[k:k + 1, cs] * dsh
                dup = term if dup is None else dup + term
                dwf_ref[k:k + 1, cs] += jnp.sum(dsh * upc, axis=0, keepdims=True)
            dbf_ref[:, cs] += jnp.sum(dbuf[0:tm, cs], axis=0, keepdims=True)
            dup_ref[:, cs] = dup.astype(_MXU)

    rev_d = pl.BlockSpec((tm, D), lambda i: (nt - 1 - i, 0))
    rev_f = pl.BlockSpec((tm, F2), lambda i: (nt - 1 - i, 0))
    prev8 = pl.BlockSpec((nh, F2), lambda i: (jnp.maximum((nt - 1 - i) * (tm // nh) - 1, 0), 0))
    full = lambda *s: pl.BlockSpec(s, lambda i: (0,) * len(s))
    return _ride(
        body, plan, name, nt,
        in_specs=[rev_d, rev_d, rev_f, prev8, _lspec(l, kf, F2), _lspec(l, 1, F2), _lspec(l, 1, D),
                  pl.BlockSpec((Fh, D), lambda i: (0, 0))],
        args=(dh, f, up0, up0, p['ffn_conv_w'], p['ffn_conv_b'], p['g_post_ffn'], wdown),
        out_specs=[rev_d, rev_f, full(kf, F2), full(1, F2), full(1, D)],
        out_shape=[_sds((T, D), _MXU), _sds((T, F2), _MXU), _sds((kf, F2), F32), _sds((1, F2), F32), _sds((1, D), F32)],
        scratch_shapes=[pltpu.VMEM((nh + tm, F2), F32), pltpu.VMEM((tm + nh, F2), F32), pltpu.VMEM((nh, F2), F32)])


def _mm_nt_norm_bwd(dy, w, x, g3, dres, l, name, plan=None):
    T, N = dy.shape
    D = x.shape[1]
    tm, nb = _TM_MM, _pick(N, _NB)
    nj = N // nb

    def body(dy_ref, w_ref, x_ref, g_ref, dr_ref, dh_ref, dg_ref, acc):
        i, j = pl.program_id(0), pl.program_id(1)

        @pl.when((i == 0) & (j == 0))
        def _():
            dg_ref[...] = jnp.zeros_like(dg_ref)

        part = lax.dot_general(dy_ref[...], w_ref[...], (((1,), (1,)), ((), ())), preferred_element_type=F32)

        @pl.when(j == 0)
        def _():
            acc[...] = part

        @pl.when(j > 0)
        def _():
            acc[...] += part

        @pl.when(j == nj - 1)
        def _():
            dx, dg = _rms_bwd(x_ref[...], g_ref[...], acc[...])
            dh_ref[...] = dr_ref[...] + dx
            dg_ref[...] += dg

    row_d = pl.BlockSpec((tm, D), lambda i, j: (i, 0))
    return _ride(
        body, plan, name, (T // tm, nj),
        in_specs=[pl.BlockSpec((tm, nb), lambda i, j: (i, j)), pl.BlockSpec((D, nb), lambda i, j: (0, j)),
                  row_d, _lspec(l, 1, D), row_d],
        args=(dy, w, x, g3, dres),
        out_specs=[row_d, pl.BlockSpec((1, D), lambda i, j: (0, 0))],
        out_shape=[_sds((T, D), F32), _sds((1, D), F32)],
        scratch_shapes=[pltpu.VMEM((tm, D), F32)])


def _mm_tn(xs, dy, name):
    T, K = xs.shape
    N = dy.shape[1]
    tm = _TM_MM
    kb = K if K <= 1024 else _pick(K, _NB)
    nb = _pick(N, _NB)

    def body(x_ref, dy_ref, o_ref):
        @pl.when(pl.program_id(2) == 0)
        def _():
            o_ref[...] = jnp.zeros_like(o_ref)

        o_ref[...] += lax.dot_general(x_ref[...], dy_ref[...], (((0,), (0,)), ((), ())), preferred_element_type=F32)

    return pl.pallas_call(
        body, name=name, grid=(K // kb, N // nb, T // tm),
        in_specs=[pl.BlockSpec((tm, kb), lambda a, b, t: (t, a)), pl.BlockSpec((tm, nb), lambda a, b, t: (t, b))],
        out_specs=pl.BlockSpec((kb, nb), lambda a, b, t: (a, b)),
        out_shape=_sds((K, N), F32),
        compiler_params=_params())(xs, dy)


def _mix_bwd(dhm, o, ya, yb, p, wout, l, name):
    T, D = ya.shape
    Dc = yb.shape[1]
    tm = _TM_SEQ

    def body(dh_ref, o_ref, ya_ref, yb_ref, gp_ref, gl_ref, gc_ref, w_ref,
             do_ref, dya_ref, dyb_ref, dgp_ref, dgl_ref, dgc_ref):
        @pl.when(pl.program_id(0) == 0)
        def _():
            dgp_ref[...] = jnp.zeros_like(dgp_ref)
            dgl_ref[...] = jnp.zeros_like(dgl_ref)
            dgc_ref[...] = jnp.zeros_like(dgc_ref)

        do, dgp = _rms_bwd(o_ref[...], gp_ref[...], dh_ref[...])
        dgp_ref[...] += dgp
        dob = do.astype(_MXU)
        do_ref[...] = dob
        dy = lax.dot_general(dob, w_ref[...], (((1,), (1,)), ((), ())), preferred_element_type=F32)
        dya, dgl = _rms_bwd(ya_ref[...], gl_ref[...], dy[:, 0:D])
        dyb, dgc = _rms_bwd(yb_ref[...], gc_ref[...], dy[:, D:D + Dc])
        dya_ref[...] = dya
        dyb_ref[...] = dyb
        dgl_ref[...] += dgl
        dgc_ref[...] += dgc

    row_d = pl.BlockSpec((tm, D), lambda i: (i, 0))
    row_c = pl.BlockSpec((tm, Dc), lambda i: (i, 0))
    full = lambda *s: pl.BlockSpec(s, lambda i: (0,) * len(s))
    return pl.pallas_call(
        body, name=name, grid=(T // tm,),
        in_specs=[row_d, row_d, row_d, row_c, _lspec(l, 1, D), _lspec(l, 1, D), _lspec(l, 1, Dc),
                  pl.BlockSpec((D + Dc, D), lambda i: (0, 0))],
        out_specs=[row_d, row_d, row_c, full(1, D), full(1, D), full(1, Dc)],
        out_shape=[_sds((T, D), _MXU), _sds((T, D), F32), _sds((T, Dc), F32),
                   _sds((1, D), F32), _sds((1, D), F32), _sds((1, Dc), F32)],
        compiler_params=_params(),
    )(dhm, o, ya, yb, p['g_post_mix'], p['g_out_lru'], p['g_out_conv'], wout)


def _mixers_bwd(dya, dyb, proj, xc, hs, cc, p, l, name, plan=None):
    T, D3 = proj.shape
    D = D3 // 3
    Dc = D // 2
    heads, hd = p['lru_wa'].shape[1], p['lru_wa'].shape[2]
    tm = _TM_SEQ
    nt = T // tm
    per8 = tm // 8
    kl, kc = p['lru_conv_w'].shape[1], p['conv_w'].shape[1]
    nhl, nhc = 8 * (kl - 1), 8 * (kc - 1)
    assert nhc <= tm

    def body(dya_ref, dyb_ref, proj_ref, xc_ref, hs_ref, hsp_ref, cc_ref,
             wl_ref, wa_ref, ba_ref, wx_ref, bx_ref, lam_ref, wc_ref, lg_ref, lb_ref,
             dproj_ref, dwl_ref, dbl_ref, dwa_ref, dba_ref, dwx_ref, dbx_ref, dlam_ref,
             dwc_ref, dbc_ref, dlg_ref, dlb_ref, gcar, dxbuf, dxhalo, dcbuf, dchalo, abuf, gbuf, hbuf, tmp8):
        i = pl.program_id(0)
        r = nt - 1 - i

        @pl.when(i == 0)
        def _():
            for ref in (gcar, dxhalo, dchalo, dwl_ref, dbl_ref, dwa_ref, dba_ref, dwx_ref, dbx_ref, dlam_ref,
                        dwc_ref, dbc_ref, dlg_ref, dlb_ref):
                ref[...] = jnp.zeros_like(ref)

        dya_v = dya_ref[...]
        hs = hs_ref[...]
        gl = proj_ref[:, D:2 * D]
        gg, tg = _gelu(gl)
        dproj_ref[:, D:2 * D] = (dya_v * hs * _gelu_grad(gl, tg)).astype(_MXU)
        dhs = dya_v * gg
        xc = xc_ref[...]
        lam = lam_ref[...]
        xcb, ra, ri, sp, a, mult = _gates(xc, wa_ref, ba_ref[...], wx_ref, bx_ref[...], lam, heads, hd)
        g = _scan_bwd(a, dhs, abuf, gbuf, gcar, tmp8, tm)
        sub8 = lax.broadcasted_iota(jnp.int32, (8, D), 0)
        hbuf[8:8 + tm, :] = hs
        hbuf[0:8, :] = jnp.where(sub8 == 0, jnp.where(r == 0, 0.0, hsp_ref[7:8, :]),
                                 pltpu.roll(hs_ref[tm - 8:tm, :], 1, 0))
        da = g * hbuf[0:tm, :]
        gx = g * xc
        dxc = g * mult * ri
        dla = da * a - (gx * ri) * (a * a) / mult
        dlam_ref[...] += jnp.sum(dla * ra, axis=0, keepdims=True) * (LRU_C * jax.nn.sigmoid(-lam))
        dpa = (dla * ((-LRU_C) * sp)) * ra * (1.0 - ra)
        dpx = (gx * mult) * ri * (1.0 - ri)
        dba_ref[...] += jnp.sum(dpa, axis=0, keepdims=True)
        dbx_ref[...] += jnp.sum(dpx, axis=0, keepdims=True)
        dpab, dpxb = dpa.astype(_MXU), dpx.astype(_MXU)
        for h in range(heads):
            hsl = slice(h * hd, (h + 1) * hd)
            dwa_ref[h] += lax.dot_general(xcb[:, hsl], dpab[:, hsl], (((0,), (0,)), ((), ())),
                                          preferred_element_type=F32)
            dwx_ref[h] += lax.dot_general(xcb[:, hsl], dpxb[:, hsl], (((0,), (0,)), ((), ())),
                                          preferred_element_type=F32)
        dxc = dxc + _bd_mm_t(dpab, wa_ref, heads, hd) + _bd_mm_t(dpxb, wx_ref, heads, hd)
        dbl_ref[...] += jnp.sum(dxc, axis=0, keepdims=True)
        _fill_anticausal(dxbuf, dxhalo, dxc, tm)
        xl = proj_ref[:, 0:D]
        dxl = None
        for k in range(kl):
            dsh = dxbuf[8 * (kl - 1 - k):8 * (kl - 1 - k) + tm, :]
            term = wl_ref[k:k + 1, :] * dsh
            dxl = term if dxl is None else dxl + term
            dwl_ref[k:k + 1, :] += jnp.sum(dsh * xl, axis=0, keepdims=True)
        dproj_ref[:, 0:D] = dxl.astype(_MXU)

        ca = proj_ref[:, 2 * D:2 * D + Dc]
        sg = jax.nn.sigmoid(proj_ref[:, 2 * D + Dc:3 * D])
        cg = ca * sg
        nrm, rss = _group_norm(cc_ref[...], CONV_GROUPS)
        lg = lg_ref[...]
        cl = nrm * lg + lb_ref[...]
        sc = jax.nn.sigmoid(cl)
        dcl = dyb_ref[...] * (sc * (1.0 + cl * (1.0 - sc)))
        dlg_ref[...] += jnp.sum(dcl * nrm, axis=0, keepdims=True)
        dlb_ref[...] += jnp.sum(dcl, axis=0, keepdims=True)
        dnrm = dcl * lg
        gsz = Dc // CONV_GROUPS
        parts = []
        for gi in range(CONV_GROUPS):
            sl = slice(gi * gsz, (gi + 1) * gsz)
            dn, nn = dnrm[:, sl], nrm[:, sl]
            parts.append(rss[gi] * (dn - jnp.mean(dn, axis=-1, keepdims=True)
                                    - nn * jnp.mean(dn * nn, axis=-1, keepdims=True)))
        dcc = jnp.concatenate(parts, axis=-1)
        dbc_ref[...] += jnp.sum(dcc, axis=0, keepdims=True)
        _fill_anticausal(dcbuf, dchalo, dcc, tm)
        dcg = None
        for k in range(kc):
            dsh = dcbuf[8 * (kc - 1 - k):8 * (kc - 1 - k) + tm, :]
            term = wc_ref[k:k + 1, :] * dsh
            dcg = term if dcg is None else dcg + term
            dwc_ref[k:k + 1, :] += jnp.sum(dsh * cg, axis=0, keepdims=True)
        dproj_ref[:, 2 * D:2 * D + Dc] = (dcg * sg).astype(_MXU)
        dproj_ref[:, 2 * D + Dc:3 * D] = (dcg * ca * sg * (1.0 - sg)).astype(_MXU)

    rev_d = pl.BlockSpec((tm, D), lambda i: (nt - 1 - i, 0))
    rev_c = pl.BlockSpec((tm, Dc), lambda i: (nt - 1 - i, 0))
    rev_p = pl.BlockSpec((tm, D3), lambda i: (nt - 1 - i, 0))
    prev8 = pl.BlockSpec((8, D), lambda i: (jnp.maximum((nt - 1 - i) * per8 - 1, 0), 0))
    full = lambda *s: pl.BlockSpec(s, lambda i: (0,) * len(s))
    return _ride(
        body, plan, name, nt,
        in_specs=[rev_d, rev_c, rev_p, rev_d, rev_d, prev8, rev_c,
                  _lspec(l, kl, D), _lspec(l, heads, hd, hd), _lspec(l, 1, D), _lspec(l, heads, hd, hd),
                  _lspec(l, 1, D), _lspec(l, 1, D), _lspec(l, kc, Dc), _lspec(l, 1, Dc), _lspec(l, 1, Dc)],
        args=(dya, dyb, proj, xc, hs, hs, cc, p['lru_conv_w'], p['lru_wa'], p['lru_ba'], p['lru_wx'], p['lru_bx'],
              p['lru_lambda'], p['conv_w'], p['conv_ln_g'], p['conv_ln_b']),
        out_specs=[rev_p, full(kl, D), full(1, D), full(heads, hd, hd), full(1, D), full(heads, hd, hd), full(1, D),
                   full(1, D), full(kc, Dc), full(1, Dc), full(1, Dc), full(1, Dc)],
        out_shape=[_sds((T, D3), _MXU), _sds((kl, D), F32), _sds((1, D), F32), _sds((heads, hd, hd), F32),
                   _sds((1, D), F32), _sds((heads, hd, hd), F32), _sds((1, D), F32), _sds((1, D), F32),
                   _sds((kc, Dc), F32), _sds((1, Dc), F32), _sds((1, Dc), F32), _sds((1, Dc), F32)],
        scratch_shapes=[pltpu.VMEM((1, D), F32), pltpu.VMEM((tm + nhl, D), F32), pltpu.VMEM((nhl, D), F32),
                        pltpu.VMEM((tm + nhc, Dc), F32), pltpu.VMEM((nhc, Dc), F32),
                        pltpu.VMEM((tm, D), F32), pltpu.VMEM((tm, D), F32), pltpu.VMEM((8 + tm, D), F32),
                        pltpu.VMEM((8, D), F32)])


def _sum_sources(recv, name):
    _, R, C = recv.shape
    rb = _pick_rows(R, 1024)

    def body(r_ref, o_ref):
        acc = r_ref[0]
        for s in range(1, _N_DEV):
            acc = acc + r_ref[s]
        o_ref[...] = acc

    return pl.pallas_call(
        body, name=name, grid=(R // rb,),
        in_specs=[pl.BlockSpec((_N_DEV, rb, C), lambda i: (0, i, 0))],
        out_specs=pl.BlockSpec((rb, C), lambda i: (i, 0)),
        out_shape=_sds((R, C), F32), compiler_params=_params())(recv)


def _adamw(g, w, m, v, name):
    R, C = w.shape
    summed = g.ndim == 3
    rb = _pick_rows(R, max(8, min(512, _ADAM_BLOCK_ELEMS // C)))
    c1 = 1.0 - ADAM_B1 ** ADAM_STEP
    c2 = 1.0 - ADAM_B2 ** ADAM_STEP

    def body(g_ref, w_ref, m_ref, v_ref, go_ref, d_ref, mo_ref, vo_ref):
        if summed:
            gv = g_ref[0]
            for s in range(1, _N_DEV):
                gv = gv + g_ref[s]
        else:
            gv = g_ref[...]
        go_ref[...] = gv
        mn = ADAM_B1 * m_ref[...] + (1.0 - ADAM_B1) * gv
        vn = ADAM_B2 * v_ref[...] + (1.0 - ADAM_B2) * (gv * gv)
        mo_ref[...] = mn
        vo_ref[...] = vn
        d_ref[...] = (-ADAM_LR) * ((mn / c1) / (jnp.sqrt(vn / c2) + ADAM_EPS) + ADAM_WD * w_ref[...])

    blk = pl.BlockSpec((rb, C), lambda i: (i, 0))
    gspec = pl.BlockSpec((_N_DEV, rb, C), lambda i: (0, i, 0)) if summed else blk
    return pl.pallas_call(
        body, name=name, grid=(R // rb,),
        in_specs=[gspec, blk, blk, blk], out_specs=[blk, blk, blk, blk],
        out_shape=[_sds((R, C), F32)] * 4, compiler_params=_params())(g, w, m, v)


def _adamw_layers(recvs, w, m, v, name):
    L, R, C = w.shape
    rb = _pick_rows(R, max(8, _ADAM_BLOCK_ELEMS // (4 * C)))
    c1 = 1.0 - ADAM_B1 ** ADAM_STEP
    c2 = 1.0 - ADAM_B2 ** ADAM_STEP

    def body(*refs):
        r_refs = refs[:L]
        w_ref, m_ref, v_ref, go_ref, d_ref, mo_ref, vo_ref = refs[L:]
        for l in range(L):
            gv = r_refs[l][0]
            for s in range(1, _N_DEV):
                gv = gv + r_refs[l][s]
            go_ref[l] = gv
            mn = ADAM_B1 * m_ref[l] + (1.0 - ADAM_B1) * gv
            vn = ADAM_B2 * v_ref[l] + (1.0 - ADAM_B2) * (gv * gv)
            mo_ref[l] = mn
            vo_ref[l] = vn
            d_ref[l] = (-ADAM_LR) * ((mn / c1) / (jnp.sqrt(vn / c2) + ADAM_EPS) + ADAM_WD * w_ref[l])

    blk = pl.BlockSpec((L, rb, C), lambda i: (0, i, 0))
    return pl.pallas_call(
        body, name=name, grid=(R // rb,),
        in_specs=[pl.BlockSpec((_N_DEV, rb, C), lambda i: (0, i, 0))] * L + [blk, blk, blk],
        out_specs=[blk, blk, blk, blk],
        out_shape=[_sds((L, R, C), F32)] * 4, compiler_params=_params())(*recvs, w, m, v)


def _pack_rows(flat_parts, dtype, row_mult):
    flat = jnp.concatenate([f.reshape(-1).astype(dtype) for f in flat_parts])
    n = flat.shape[0]
    per = _LANES * row_mult
    padded = -(-n // per) * per
    if padded != n:
        flat = jnp.concatenate([flat, jnp.zeros((padded - n,), dtype)])
    return flat.reshape(-1, _LANES)


def _unpack(flat, shapes):
    out, off = [], 0
    for s in shapes:
        n = math.prod(s)
        out.append(flat[off:off + n].reshape(s))
        off += n
    return out


def _to_pieces(full):
    n = full.shape[-1] // _N_DEV
    t = full.reshape(full.shape[:-1] + (_N_DEV, n))
    return jnp.moveaxis(t, -2, 0).reshape(_N_DEV, -1)


def _from_gathered(seg, shard_shape, axis):
    t = seg.reshape((_N_DEV,) + tuple(shard_shape))
    t = jnp.moveaxis(t, 0, axis)
    shape = list(shard_shape)
    shape[axis] *= _N_DEV
    return t.reshape(shape)


def kernel(x, meta_tokens, g_pre_mix, w_in, lru_conv_w, lru_conv_b, lru_wa, lru_ba, lru_wx, lru_bx, lru_lambda, conv_w, conv_b, conv_ln_g, conv_ln_b, g_out_lru, g_out_conv, w_out, g_post_mix, g_pre_ffn, w_up, ffn_conv_w, ffn_conv_b, w_down, g_post_ffn, loss_target, m_meta_tokens, m_g_pre_mix, m_w_in, m_lru_conv_w, m_lru_conv_b, m_lru_wa, m_lru_ba, m_lru_wx, m_lru_bx, m_lru_lambda, m_conv_w, m_conv_b, m_conv_ln_g, m_conv_ln_b, m_g_out_lru, m_g_out_conv, m_w_out, m_g_post_mix, m_g_pre_ffn, m_w_up, m_ffn_conv_w, m_ffn_conv_b, m_w_down, m_g_post_ffn, v_meta_tokens, v_g_pre_mix, v_w_in, v_lru_conv_w, v_lru_conv_b, v_lru_wa, v_lru_ba, v_lru_wx, v_lru_bx, v_lru_lambda, v_conv_w, v_conv_b, v_conv_ln_g, v_conv_ln_b, v_g_out_lru, v_g_out_conv, v_w_out, v_g_post_mix, v_g_pre_ffn, v_w_up, v_ffn_conv_w, v_ffn_conv_b, v_w_down, v_g_post_ffn):
    given = dict(locals())
    W = {n: given[n] for n in W_NAMES}
    M = {n: given['m_' + n] for n in W_NAMES}
    V = {n: given['v_' + n] for n in W_NAMES}
    S, D = x.shape[1], x.shape[2]
    L = g_pre_mix.shape[0]
    Dc = D // 2
    step = math.lcm(_TM_MM, _TM_SEQ)
    T = -(-(N_META + S) // step) * step

    def layer_pack(l):
        return _pack_rows([W[n][l] for n in BIG], _MXU, 16)

    def layer_weights(gathered):
        segs = _unpack_cols(gathered.reshape(_N_DEV, -1), [W[n].shape[1:] for n in BIG])
        return {n: _from_gathered(seg, W[n].shape[1:], 1 if n in ('w_in', 'w_up') else 0)
                for n, seg in zip(BIG, segs)}

    small_pack = _pack_rows([W[n] for n in SMALL_SHARDED], F32, 8)
    big_g, small_g = _run_plan(_GatherPlan([layer_pack(0), small_pack]), "gather_weights_l0")
    small_segs = _unpack_cols(small_g.reshape(_N_DEV, -1), [W[n].shape for n in SMALL_SHARDED])
    full = {}
    for n, seg in zip(SMALL_SHARDED, small_segs):
        full[n] = _from_gathered(seg, W[n].shape, W[n].ndim - 1)

    p = {}
    for n in W_NAMES:
        if n in BIG or n == 'meta_tokens':
            continue
        a = full[n] if n in full else W[n]
        p[n] = a.reshape(L, 1, a.shape[1]) if a.ndim == 2 else a

    pad_rows = T - N_META - S
    h = _perm(jnp.concatenate([full['meta_tokens'], x[0], jnp.zeros((pad_rows, D), F32)], axis=0), _TM_SEQ)
    tgt = _perm(jnp.concatenate([jnp.zeros((N_META, D), F32), loss_target[0], jnp.zeros((pad_rows, D), F32)],
                                axis=0), _TM_SEQ)

    saved = []
    wl = layer_weights(big_g)
    for l in range(L):
        plan = _GatherPlan([layer_pack(l + 1)]) if l + 1 < L else None
        (proj, z1), _ = _norm_mm(h, p['g_pre_mix'], wl['w_in'], l, f"in_proj_l{l}")
        (xc, hs, ya, cc, yb), _ = _mixers_fwd(proj, p, l, f"mixers_fwd_l{l}")
        y, o, hmid = _mix_out(ya, yb, h, p, wl['w_out'], l, f"mix_out_l{l}")
        (up0, z2), nxt = _norm_mm(hmid, p['g_pre_ffn'], wl['w_up'], l, f"up_proj_l{l}", plan)
        a2, f, hout = _ffn_fwd(up0, hmid, p, wl['w_down'], l, f"ffn_fwd_l{l}")
        saved.append(dict(h=h, z1=z1, proj=proj, xc=xc, hs=hs, ya=ya, cc=cc, yb=yb, y=y, o=o, hmid=hmid,
                          z2=z2, up0=up0, a2=a2, f=f, w=wl))
        h = hout
        if plan:
            wl = layer_weights(nxt[0])
    loss_tile, dh = _loss_head(h, tgt, S, "loss_head")
    loss = lax.psum(loss_tile[0, 0], ("x", "y", "c"))

    small_g_names = ['g_pre_mix', 'lru_conv_w', 'lru_conv_b', 'lru_wa', 'lru_ba', 'lru_wx', 'lru_bx', 'lru_lambda',
                     'conv_w', 'conv_b', 'conv_ln_g', 'conv_ln_b', 'g_out_lru', 'g_out_conv', 'g_post_mix',
                     'g_pre_ffn', 'ffn_conv_w', 'ffn_conv_b', 'g_post_ffn']
    per_layer = {n: [None] * L for n in small_g_names}
    recv = {n: [None] * L for n in BIG}
    d_in_pending = None
    for l in reversed(range(L)):
        sv = saved[l]
        wl = sv['w']
        plan = _ExchangePlan([d_in_pending], ['cols']) if d_in_pending is not None else None
        (df, dup0, dwf, dbf, dgpf), got = _ffn_bwd(dh, sv['f'], sv['up0'], p, wl['w_down'], l, f"ffn_bwd_l{l}", plan)
        if plan:
            recv['w_in'][l + 1] = got[0]
        d_down = _mm_tn(sv['a2'], df, f"dw_down_l{l}")
        d_up = _mm_tn(sv['z2'], dup0, f"dw_up_l{l}")
        (dhm, dgpre), got = _mm_nt_norm_bwd(dup0, wl['w_up'], sv['hmid'], p['g_pre_ffn'], dh, l, f"up_bwd_l{l}",
                                            _ExchangePlan([d_down], ['rows']))
        recv['w_down'][l] = got[0]
        do, dya, dyb, dgpm, dgol, dgoc = _mix_bwd(dhm, sv['o'], sv['ya'], sv['yb'], p, wl['w_out'], l,
                                                  f"mix_bwd_l{l}")
        d_out = _mm_tn(sv['y'], do, f"dw_out_l{l}")
        (dproj, dwl, dbl, dwa, dba, dwx, dbx, dlam, dwc, dbc, dlg, dlb), got = _mixers_bwd(
            dya, dyb, sv['proj'], sv['xc'], sv['hs'], sv['cc'], p, l, f"mixers_bwd_l{l}",
            _ExchangePlan([d_up], ['cols']))
        recv['w_up'][l] = got[0]
        d_in_pending = _mm_tn(sv['z1'], dproj, f"dw_in_l{l}")
        (dh, dgpmix), got = _mm_nt_norm_bwd(dproj, wl['w_in'], sv['h'], p['g_pre_mix'], dhm, l, f"in_bwd_l{l}",
                                            _ExchangePlan([d_out], ['rows']))
        recv['w_out'][l] = got[0]
        for n, val in (('g_pre_mix', dgpmix), ('lru_conv_w', dwl), ('lru_conv_b', dbl), ('lru_wa', dwa),
                       ('lru_ba', dba), ('lru_wx', dwx), ('lru_bx', dbx), ('lru_lambda', dlam), ('conv_w', dwc),
                       ('conv_b', dbc), ('conv_ln_g', dlg), ('conv_ln_b', dlb), ('g_out_lru', dgol),
                       ('g_out_conv', dgoc), ('g_post_mix', dgpm), ('g_pre_ffn', dgpre), ('ffn_conv_w', dwf),
                       ('ffn_conv_b', dbf), ('g_post_ffn', dgpf)):
            per_layer[n][l] = val
    dh = _unperm(dh, _TM_SEQ)
    grad_x = dh[N_META:N_META + S][None]
    partial = {n: jnp.stack(per_layer[n]).reshape((L,) + tuple(
        (full[n] if n in full else W[n]).shape[1:])) for n in small_g_names}
    partial['meta_tokens'] = dh[0:N_META]

    shard_pack = jnp.concatenate([_to_pieces(partial[n]) for n in SMALL_SHARDED], axis=1)
    n_sh = shard_pack.shape[1]
    rs = -(-n_sh // (8 * _LANES)) * 8
    shard_pack = jnp.concatenate([shard_pack, jnp.zeros((_N_DEV, rs * _LANES - n_sh), F32)], axis=1)
    rep_flat = jnp.concatenate([partial[n].reshape(-1) for n in REPLICATED])
    n_rep = rep_flat.shape[0]
    rr = -(-n_rep // (_N_DEV * 8 * _LANES)) * 8
    rep_flat = jnp.concatenate([rep_flat, jnp.zeros((_N_DEV * rr * _LANES - n_rep,), F32)])
    small_send = jnp.concatenate([shard_pack, rep_flat.reshape(_N_DEV, rr * _LANES)], axis=1)
    small_send = small_send.reshape(_N_DEV, rs + rr, _LANES)
    recv['w_in'][0], r_small = _run_plan(_ExchangePlan([d_in_pending, small_send], ['cols', 'slots']),
                                         "grad_exchange_last")
    small_red = _sum_sources(r_small, "sum_small")
    (rep_g,) = _run_plan(_GatherPlan([small_red[rs:]]), "gather_replicated_grads")
    rep_g = rep_g.reshape(_N_DEV * rr, _LANES)

    out = {}
    for n in BIG:
        out[n] = list(_adamw_layers(recv[n], W[n], M[n], V[n], f"adamw_{n}"))
    sh_shapes = [W[n].shape for n in SMALL_SHARDED]
    res = _adamw(small_red[:rs], _pack_rows([W[n] for n in SMALL_SHARDED], F32, 8),
                 _pack_rows([M[n] for n in SMALL_SHARDED], F32, 8),
                 _pack_rows([V[n] for n in SMALL_SHARDED], F32, 8), "adamw_small_sharded")
    for k in range(4):
        for n, val in zip(SMALL_SHARDED, _unpack(res[k].reshape(-1), sh_shapes)):
            out.setdefault(n, [None] * 4)[k] = val
    rep_shapes = [W[n].shape for n in REPLICATED]
    res = _adamw(rep_g, _pack_rows([W[n] for n in REPLICATED], F32, 8 * _N_DEV),
                 _pack_rows([M[n] for n in REPLICATED], F32, 8 * _N_DEV),
                 _pack_rows([V[n] for n in REPLICATED], F32, 8 * _N_DEV), "adamw_replicated")
    for k in range(4):
        for n, val in zip(REPLICATED, _unpack(res[k].reshape(-1), rep_shapes)):
            out.setdefault(n, [None] * 4)[k] = val

    return (loss, grad_x, *[out[n][0] for n in W_NAMES], *[out[n][1] for n in W_NAMES],
            *[out[n][2] for n in W_NAMES], *[out[n][3] for n in W_NAMES])


def _unpack_cols(gathered, shapes):
    out, off = [], 0
    for s in shapes:
        n = math.prod(s)
        out.append(gathered[:, off:off + n])
        off += n
    return out
```

```python
import math

import jax
import jax.numpy as jnp
from jax import lax
from jax.experimental import pallas as pl
from jax.experimental.pallas import tpu as pltpu

F32 = jnp.float32
_MXU = jnp.bfloat16
_TM_MM = 768
_TM_FWD = 1408
_TM_DW = 2816
_TM_SEQ = 256
_NB = 768
_NB_DX = 1536
_VMEM_LIMIT = 56 * 1024 * 1024
_ADAM_BLOCK_ELEMS = 128 * 1024
_LANES = 128
_N_DEV = 8

EPS = 1e-6
N_META = 16
LRU_C = 8.0
CONV_GROUPS = 4
ADAM_LR, ADAM_B1, ADAM_B2, ADAM_EPS, ADAM_WD, ADAM_STEP = 0.001, 0.9, 0.999, 1e-08, 0.01, 10
_GELU_K0 = math.sqrt(2.0 / math.pi)
_GELU_K1 = 0.044715

W_NAMES = ['meta_tokens', 'g_pre_mix', 'w_in', 'lru_conv_w', 'lru_conv_b', 'lru_wa', 'lru_ba', 'lru_wx', 'lru_bx',
           'lru_lambda', 'conv_w', 'conv_b', 'conv_ln_g', 'conv_ln_b', 'g_out_lru', 'g_out_conv', 'w_out',
           'g_post_mix', 'g_pre_ffn', 'w_up', 'ffn_conv_w', 'ffn_conv_b', 'w_down', 'g_post_ffn']
BIG = ['w_in', 'w_out', 'w_up', 'w_down']
SMALL_SHARDED = ['meta_tokens', 'lru_conv_w', 'conv_w', 'ffn_conv_w']
REPLICATED = [n for n in W_NAMES if n not in BIG and n not in SMALL_SHARDED]


def _params():
    return pltpu.CompilerParams(vmem_limit_bytes=_VMEM_LIMIT)


def _pick(n, pref):
    if n <= pref:
        return n
    best = None
    for b in range(_LANES, pref + 1, _LANES):
        if n % b == 0:
            best = b
    assert best is not None, (n, pref)
    return best


def _pick_rows(n, pref):
    if n <= pref:
        return n
    best = None
    for b in range(8, pref + 1, 8):
        if n % b == 0:
            best = b
    assert best is not None, (n, pref)
    return best


def _lspec(l, *dims):
    zeros = (0,) * len(dims)
    return pl.BlockSpec((None,) + tuple(dims), lambda *_: (l,) + zeros)


def _sds(shape, dtype):
    return jax.ShapeDtypeStruct(tuple(shape), dtype)


def _rms_fwd(x, g):
    r = lax.rsqrt(jnp.mean(x * x, axis=-1, keepdims=True) + EPS)
    return (x * r) * g


def _rms_bwd(x, g, dy):
    r = lax.rsqrt(jnp.mean(x * x, axis=-1, keepdims=True) + EPS)
    xh = x * r
    dg = jnp.sum(dy * xh, axis=0, keepdims=True)
    dxh = dy * g
    dx = r * (dxh - xh * jnp.mean(dxh * xh, axis=-1, keepdims=True))
    return dx, dg


def _gelu(x):
    t = jnp.tanh(_GELU_K0 * (x + _GELU_K1 * (x * x * x)))
    return 0.5 * x * (1.0 + t), t


def _gelu_grad(x, t):
    return 0.5 * (1.0 + t) + 0.5 * x * (1.0 - t * t) * (_GELU_K0 * (1.0 + 3.0 * _GELU_K1 * x * x))


def _log1p(e):
    u = 1.0 + e
    return jnp.where(u == 1.0, e, jnp.log(u) * (e / (u - 1.0)))


def _softplus(z):
    return jnp.maximum(z, 0.0) + _log1p(jnp.exp(-jnp.abs(z)))


def _one_minus_exp(x):
    p = -x * (1.0 + x * (0.5 + x * (1.0 / 6 + x * (1.0 / 24 + x * (1.0 / 120 + x * (1.0 / 720))))))
    return jnp.where(x > -0.125, p, 1.0 - jnp.exp(x))


def _shift_down(x, s, fill, row):
    return jnp.where(row >= s, pltpu.roll(x, s, 0), fill)


def _shift_up(x, s, fill, row):
    n = x.shape[0]
    return jnp.where(row < n - s, pltpu.roll(x, n - s, 0), fill)


def _bd_mm(xb, w_ref, heads, hd):
    return jnp.concatenate(
        [jnp.dot(xb[:, h * hd:(h + 1) * hd], w_ref[h].astype(_MXU), preferred_element_type=F32)
         for h in range(heads)], axis=-1)


def _bd_mm_t(db, w_ref, heads, hd):
    return jnp.concatenate(
        [lax.dot_general(db[:, h * hd:(h + 1) * hd], w_ref[h].astype(_MXU), (((1,), (1,)), ((), ())),
                         preferred_element_type=F32)
         for h in range(heads)], axis=-1)


def _gates(xc, wa_ref, ba, wx_ref, bx, lam, heads, hd):
    xcb = xc.astype(_MXU)
    ra = jax.nn.sigmoid(_bd_mm(xcb, wa_ref, heads, hd) + ba)
    ri = jax.nn.sigmoid(_bd_mm(xcb, wx_ref, heads, hd) + bx)
    sp = _softplus(-lam)
    la = (-LRU_C) * ra * sp
    a = jnp.exp(la)
    mult = jnp.sqrt(_one_minus_exp(2.0 * la))
    return xcb, ra, ri, sp, a, mult


def _group_norm(cc, groups):
    gs = cc.shape[-1] // groups
    outs, rss = [], []
    for g in range(groups):
        seg = cc[:, g * gs:(g + 1) * gs]
        mu = jnp.mean(seg, axis=-1, keepdims=True)
        d = seg - mu
        rs = lax.rsqrt(jnp.mean(d * d, axis=-1, keepdims=True) + EPS)
        outs.append(d * rs)
        rss.append(rs)
    return jnp.concatenate(outs, axis=-1), rss


_MESH = pl.DeviceIdType.MESH
_ANY = pl.BlockSpec(memory_space=pl.ANY)


def _my_index():
    return 4 * lax.axis_index("x") + 2 * lax.axis_index("y") + lax.axis_index("c")


class _GatherPlan:
    def __init__(self, blocks):
        self.blocks = list(blocks)

    def operands(self):
        return self.blocks

    def out_shape(self):
        return [_sds((_N_DEV,) + b.shape, b.dtype) for b in self.blocks]

    def scratch(self):
        n = len(self.blocks)
        return [pltpu.SemaphoreType.DMA((7 * n,)), pltpu.SemaphoreType.DMA((7 * n,)), pltpu.SemaphoreType.DMA((n,))]

    def _copies(self, a, ins, outs, sems):
        send_sems, recv_sems, local_sems = sems
        x, y, c = lax.axis_index("x"), lax.axis_index("y"), lax.axis_index("c")
        me, sibling = (x, y, c), (x, y, 1 - c)
        chips = [(1 - x, y), (x, 1 - y), (1 - x, 1 - y)]

        def slot(dev):
            return outs[a].at[4 * dev[0] + 2 * dev[1] + dev[2]]

        def copy(k, block, to, src=None):
            return pltpu.make_async_remote_copy(
                src_ref=slot(block) if src is None else src, dst_ref=slot(block),
                send_sem=send_sems.at[7 * a + k], recv_sem=recv_sems.at[7 * a + k],
                device_id=to, device_id_type=_MESH)

        mine = pltpu.make_async_copy(ins[a], slot(me), local_sems.at[a])
        first = [copy(0, me, sibling, src=ins[a])]
        first += [copy(1 + j, me, (*chip, c), src=ins[a]) for j, chip in enumerate(chips)]
        passed = [copy(4 + j, (*chip, c), sibling) for j, chip in enumerate(chips)]
        from_chips = [copy(1 + j, (*chip, c), me) for j, chip in enumerate(chips)]
        from_sibling = [copy(0, sibling, me)] + [copy(4 + j, (*chip, 1 - c), me) for j, chip in enumerate(chips)]
        return mine, first, passed, from_chips, from_sibling

    def start(self, ins, outs, sems):
        for a in range(len(self.blocks)):
            mine, first, _, _, _ = self._copies(a, ins, outs, sems)
            mine.start()
            for cp in first:
                cp.start()

    def forward(self, ins, outs, sems):
        for a in range(len(self.blocks)):
            _, _, passed, from_chips, _ = self._copies(a, ins, outs, sems)
            for j in range(3):
                from_chips[j].wait_recv()
                passed[j].start()

    def finish(self, ins, outs, sems):
        for a in range(len(self.blocks)):
            mine, first, passed, _, from_sibling = self._copies(a, ins, outs, sems)
            for cp in from_sibling:
                cp.wait_recv()
            for cp in first + passed:
                cp.wait_send()
            mine.wait()

    def begin(self, i, steps, ins, outs, sems):
        @pl.when(i == 0)
        def _():
            self.start(ins, outs, sems)

        @pl.when(i == (3 * steps) // 4)
        def _():
            self.forward(ins, outs, sems)

    def end(self, i, steps, ins, outs, sems):
        @pl.when(i == steps - 1)
        def _():
            self.finish(ins, outs, sems)


class _ExchangePlan:
    def __init__(self, arrs, kinds):
        self.arrs, self.kinds = list(arrs), list(kinds)

    def _piece_shape(self, a):
        shp = self.arrs[a].shape
        if self.kinds[a] == 'cols':
            return (shp[0], shp[1] // _N_DEV)
        if self.kinds[a] == 'rows':
            return (shp[0] // _N_DEV, shp[1])
        return tuple(shp[1:])

    def operands(self):
        return self.arrs

    def out_shape(self):
        return [_sds((_N_DEV,) + self._piece_shape(a), F32) for a in range(len(self.arrs))]

    def scratch(self):
        n = len(self.arrs)
        return [pltpu.SemaphoreType.DMA((_N_DEV * n,)), pltpu.SemaphoreType.DMA((_N_DEV * n,)),
                pltpu.SemaphoreType.DMA((n,))]

    def _copies(self, ins, outs, sems):
        send_sems, recv_sems, local_sems = sems
        me = _my_index()

        def piece(a, j):
            ps = self._piece_shape(a)
            if self.kinds[a] == 'cols':
                return ins[a].at[:, pl.ds(j * ps[1], ps[1])]
            if self.kinds[a] == 'rows':
                return ins[a].at[pl.ds(j * ps[0], ps[0]), :]
            return ins[a].at[j]

        def remote(a, j):
            return pltpu.make_async_remote_copy(
                src_ref=piece(a, j), dst_ref=outs[a].at[me],
                send_sem=send_sems.at[_N_DEV * a + j], recv_sem=recv_sems.at[_N_DEV * a + me],
                device_id=(j >> 2, (j >> 1) & 1, j & 1), device_id_type=_MESH)

        def arrival(a, s):
            return pltpu.make_async_remote_copy(
                src_ref=piece(a, s), dst_ref=outs[a].at[s],
                send_sem=send_sems.at[_N_DEV * a + s], recv_sem=recv_sems.at[_N_DEV * a + s],
                device_id=(s >> 2, (s >> 1) & 1, s & 1), device_id_type=_MESH)

        def local(a, j):
            return pltpu.make_async_copy(piece(a, j), outs[a].at[j], local_sems.at[a])

        return me, remote, arrival, local

    def start(self, ins, outs, sems):
        me, remote, _, local = self._copies(ins, outs, sems)
        for j in range(_N_DEV):
            for a in range(len(self.arrs)):
                @pl.when(me != j)
                def _(a=a, j=j):
                    remote(a, j).start()

                @pl.when(me == j)
                def _(a=a, j=j):
                    local(a, j).start()

    def finish(self, ins, outs, sems):
        me, remote, arrival, local = self._copies(ins, outs, sems)
        for j in range(_N_DEV):
            for a in range(len(self.arrs)):
                @pl.when(me != j)
                def _(a=a, j=j):
                    arrival(a, j).wait_recv()
                    remote(a, j).wait_send()

                @pl.when(me == j)
                def _(a=a, j=j):
                    local(a, j).wait()

    def forward(self, ins, outs, sems):
        pass

    def begin(self, i, steps, ins, outs, sems):
        @pl.when(i == 0)
        def _():
            self.start(ins, outs, sems)

    def end(self, i, steps, ins, outs, sems):
        @pl.when(i == steps - 1)
        def _():
            self.finish(ins, outs, sems)


def _ride(main, plan, name, grid, in_specs, args, out_specs, out_shape, scratch_shapes):
    grid = (grid,) if isinstance(grid, int) else tuple(grid)
    steps = math.prod(grid)
    n_in, n_out, n_sc = len(in_specs), len(out_specs), len(scratch_shapes)
    p_args = plan.operands() if plan else []
    p_out = plan.out_shape() if plan else []
    p_sc = plan.scratch() if plan else []

    def body(*refs):
        k = 0
        ins = refs[k:k + n_in]; k += n_in
        p_ins = refs[k:k + len(p_args)]; k += len(p_args)
        outs = refs[k:k + n_out]; k += n_out
        p_outs = refs[k:k + len(p_out)]; k += len(p_out)
        scr = refs[k:k + n_sc]; k += n_sc
        sems = refs[k:]
        i = pl.program_id(0)
        for axis in range(1, len(grid)):
            i = i * grid[axis] + pl.program_id(axis)
        if plan:
            plan.begin(i, steps, p_ins, p_outs, sems)
        main(*ins, *outs, *scr)
        if plan:
            plan.end(i, steps, p_ins, p_outs, sems)

    res = pl.pallas_call(
        body, name=name, grid=grid,
        in_specs=list(in_specs) + [_ANY] * len(p_args),
        out_specs=list(out_specs) + [_ANY] * len(p_out),
        out_shape=list(out_shape) + p_out,
        scratch_shapes=list(scratch_shapes) + p_sc,
        compiler_params=_params())(*args, *p_args)
    return res[:n_out], res[n_out:]


def _run_plan(plan, name):
    n_args, n_out = len(plan.operands()), len(plan.out_shape())

    def body(*refs):
        ins, outs, sems = refs[:n_args], refs[n_args:n_args + n_out], refs[n_args + n_out:]
        plan.start(ins, outs, sems)
        plan.forward(ins, outs, sems)
        plan.finish(ins, outs, sems)

    return pl.pallas_call(
        body, name=name, in_specs=[_ANY] * n_args, out_specs=[_ANY] * n_out,
        out_shape=plan.out_shape(), scratch_shapes=plan.scratch())(*plan.operands())


def _norm_mm(x, g3, w, l, name, plan=None):
    T, D = x.shape
    N = w.shape[1]
    tm, nb = _TM_FWD, _pick(N, _NB)

    def body(x_ref, g_ref, w_ref, o_ref, z_ref):
        @pl.when(pl.program_id(1) == 0)
        def _():
            z_ref[...] = _rms_fwd(x_ref[...], g_ref[...]).astype(z_ref.dtype)

        o_ref[...] = jnp.dot(z_ref[...], w_ref[...], preferred_element_type=F32)

    return _ride(
        body, plan, name, (T // tm, N // nb),
        in_specs=[pl.BlockSpec((tm, D), lambda i, j: (i, 0)), _lspec(l, 1, D),
                  pl.BlockSpec((D, nb), lambda i, j: (0, j))],
        args=(x, g3, w),
        out_specs=[pl.BlockSpec((tm, nb), lambda i, j: (i, j)), pl.BlockSpec((tm, D), lambda i, j: (i, 0))],
        out_shape=[_sds((T, N), F32), _sds((T, D), _MXU)],
        scratch_shapes=[])


def _perm(a, tm):
    T, C = a.shape
    return a.reshape(T // tm, 8, tm // 8, C).transpose(0, 2, 1, 3).reshape(T, C)


def _unperm(a, tm):
    T, C = a.shape
    return a.reshape(T // tm, tm // 8, 8, C).transpose(0, 2, 1, 3).reshape(T, C)


def _wrap_prev(prev_z, z):
    n = z.shape[0]
    sub = lax.broadcasted_iota(jnp.int32, z.shape, 0) & 7
    return jnp.where(sub == 0, pltpu.roll(prev_z, n - 7, 0), pltpu.roll(z, 1, 0))


def _wrap_next(next_z, z):
    n = z.shape[0]
    sub = lax.broadcasted_iota(jnp.int32, z.shape, 0) & 7
    return jnp.where(sub == 7, pltpu.roll(next_z, 7, 0), pltpu.roll(z, n - 1, 0))


def _fill_causal(buf, halo, x, tm):
    nh = halo.shape[0]
    buf[nh:nh + tm, :] = x
    z = buf[tm:tm + nh, :]
    buf[0:nh, :] = _wrap_prev(halo[...], z)
    halo[...] = z


def _fill_anticausal(buf, halo, dy, tm):
    nh = halo.shape[0]
    buf[0:tm, :] = dy
    z = buf[0:nh, :]
    buf[tm:tm + nh, :] = _wrap_next(halo[...], z)
    halo[...] = z


def _scan_fwd(a, u, abuf, ubuf, hcar, tm):
    D = a.shape[1]
    G = tm // 8
    abuf[...] = a
    ubuf[...] = u

    def step(j, c):
        h, pr = c
        r = pl.multiple_of(j * 8, 8)
        aj = abuf[pl.ds(r, 8), :]
        h = aj * h + ubuf[pl.ds(r, 8), :]
        pr = aj * pr
        ubuf[pl.ds(r, 8), :] = h
        abuf[pl.ds(r, 8), :] = pr
        return h, pr

    sub8 = lax.broadcasted_iota(jnp.int32, (8, D), 0)
    e, q = lax.fori_loop(1, G, step, (ubuf[0:8, :], abuf[0:8, :]))
    for s in (1, 2, 4):
        e = e + q * _shift_down(e, s, 0.0, sub8)
        q = q * _shift_down(q, s, 1.0, sub8)
    e = e + q * hcar[...]
    cin = jnp.where(sub8 == 0, hcar[...], pltpu.roll(e, 1, 0))
    return ubuf[...] + abuf[...] * jnp.tile(cin, (G, 1))


def _scan_bwd(a, d, abuf, gbuf, gcar, tmp8, tm):
    D = a.shape[1]
    G = tm // 8
    abuf[...] = a
    gbuf[...] = d

    def step(k, c):
        g_next, a_next, r_j = c
        r = pl.multiple_of((G - 1 - k) * 8, 8)
        aj = abuf[pl.ds(r, 8), :]
        g = gbuf[pl.ds(r, 8), :] + a_next * g_next
        gbuf[pl.ds(r, 8), :] = g
        abuf[pl.ds(r, 8), :] = r_j
        return g, aj, aj * r_j

    last = 8 * (G - 1)
    a_last = abuf[last:last + 8, :]
    abuf[last:last + 8, :] = jnp.ones((8, D), F32)
    g0, a0, _ = lax.fori_loop(1, G, step, (gbuf[last:last + 8, :], a_last, a_last))
    r0 = abuf[0:8, :]
    sub8 = lax.broadcasted_iota(jnp.int32, (8, D), 0)
    x, q = a0 * g0, a0 * r0
    for s in (1, 2, 4):
        x = x + q * _shift_up(x, s, 0.0, sub8)
        q = q * _shift_up(q, s, 1.0, sub8)
    x = x + q * gcar[...]
    cin = jnp.where(sub8 == 7, gcar[...], pltpu.roll(x, 7, 0))
    g = gbuf[...] + abuf[...] * jnp.tile(cin, (G, 1))
    tmp8[...] = x
    gcar[...] = tmp8[0:1, :]
    return g


def _mixers_fwd(proj, p, l, name, plan=None):
    T, D3 = proj.shape
    D = D3 // 3
    Dc = D // 2
    heads, hd = p['lru_wa'].shape[1], p['lru_wa'].shape[2]
    tm = _TM_SEQ
    kl, kc = p['lru_conv_w'].shape[1], p['conv_w'].shape[1]
    nhl, nhc = 8 * (kl - 1), 8 * (kc - 1)
    assert nhc <= tm

    def body(proj_ref, wl_ref, bl_ref, wa_ref, ba_ref, wx_ref, bx_ref, lam_ref, wc_ref, bc_ref, lg_ref, lb_ref,
             xc_ref, hs_ref, ya_ref, cc_ref, yb_ref, xbuf, xhalo, hcar, cbuf, chalo, abuf, ubuf):
        @pl.when(pl.program_id(0) == 0)
        def _():
            xhalo[...] = jnp.zeros_like(xhalo)
            chalo[...] = jnp.zeros_like(chalo)
            hcar[...] = jnp.zeros_like(hcar)

        _fill_causal(xbuf, xhalo, proj_ref[:, 0:D], tm)
        xc = bl_ref[...] + wl_ref[0:1, :] * xbuf[0:tm, :]
        for k in range(1, kl):
            xc = xc + wl_ref[k:k + 1, :] * xbuf[8 * k:8 * k + tm, :]
        xc_ref[...] = xc
        _, ra, ri, sp, a, mult = _gates(xc, wa_ref, ba_ref[...], wx_ref, bx_ref[...], lam_ref[...], heads, hd)
        hs = _scan_fwd(a, mult * (ri * xc), abuf, ubuf, hcar, tm)
        hs_ref[...] = hs
        hcar[...] = hs_ref[pl.ds(tm - 1, 1), :]
        gg, _ = _gelu(proj_ref[:, D:2 * D])
        ya_ref[...] = hs * gg

        _fill_causal(cbuf, chalo,
                     proj_ref[:, 2 * D:2 * D + Dc] * jax.nn.sigmoid(proj_ref[:, 2 * D + Dc:3 * D]), tm)
        cc = bc_ref[...] + wc_ref[0:1, :] * cbuf[0:tm, :]
        for k in range(1, kc):
            cc = cc + wc_ref[k:k + 1, :] * cbuf[8 * k:8 * k + tm, :]
        cc_ref[...] = cc
        nrm, _ = _group_norm(cc, CONV_GROUPS)
        cl = nrm * lg_ref[...] + lb_ref[...]
        yb_ref[...] = cl * jax.nn.sigmoid(cl)

    row_d = pl.BlockSpec((tm, D), lambda i: (i, 0))
    row_c = pl.BlockSpec((tm, Dc), lambda i: (i, 0))
    return _ride(
        body, plan, name, T // tm,
        in_specs=[pl.BlockSpec((tm, D3), lambda i: (i, 0)),
                  _lspec(l, kl, D), _lspec(l, 1, D), _lspec(l, heads, hd, hd), _lspec(l, 1, D),
                  _lspec(l, heads, hd, hd), _lspec(l, 1, D), _lspec(l, 1, D),
                  _lspec(l, kc, Dc), _lspec(l, 1, Dc), _lspec(l, 1, Dc), _lspec(l, 1, Dc)],
        args=(proj, p['lru_conv_w'], p['lru_conv_b'], p['lru_wa'], p['lru_ba'], p['lru_wx'], p['lru_bx'],
              p['lru_lambda'], p['conv_w'], p['conv_b'], p['conv_ln_g'], p['conv_ln_b']),
        out_specs=[row_d, row_d, row_d, row_c, row_c],
        out_shape=[_sds((T, D), F32), _sds((T, D), F32), _sds((T, D), F32), _sds((T, Dc), F32), _sds((T, Dc), F32)],
        scratch_shapes=[pltpu.VMEM((nhl + tm, D), F32), pltpu.VMEM((nhl, D), F32), pltpu.VMEM((1, D), F32),
                        pltpu.VMEM((nhc + tm, Dc), F32), pltpu.VMEM((nhc, Dc), F32),
                        pltpu.VMEM((tm, D), F32), pltpu.VMEM((tm, D), F32)])


def _mix_out(ya, yb, h, p, wout, l, name):
    T, D = ya.shape
    Dc = yb.shape[1]
    tm = _TM_SEQ

    def body(ya_ref, yb_ref, h_ref, gl_ref, gc_ref, gp_ref, w_ref, y_ref, o_ref, hm_ref):
        y = jnp.concatenate([_rms_fwd(ya_ref[...], gl_ref[...]), _rms_fwd(yb_ref[...], gc_ref[...])],
                            axis=-1).astype(_MXU)
        y_ref[...] = y
        o = jnp.dot(y, w_ref[...], preferred_element_type=F32)
        o_ref[...] = o
        hm_ref[...] = h_ref[...] + _rms_fwd(o, gp_ref[...])

    row_d = pl.BlockSpec((tm, D), lambda i: (i, 0))
    return pl.pallas_call(
        body, name=name, grid=(T // tm,),
        in_specs=[row_d, pl.BlockSpec((tm, Dc), lambda i: (i, 0)), row_d,
                  _lspec(l, 1, D), _lspec(l, 1, Dc), _lspec(l, 1, D), pl.BlockSpec((D + Dc, D), lambda i: (0, 0))],
        out_specs=[pl.BlockSpec((tm, D + Dc), lambda i: (i, 0)), row_d, row_d],
        out_shape=[_sds((T, D + Dc), _MXU), _sds((T, D), F32), _sds((T, D), F32)],
        compiler_params=_params(),
    )(ya, yb, h, p['g_out_lru'], p['g_out_conv'], p['g_post_mix'], wout)


def _ffn_fwd(up0, hmid, p, wdown, l, name):
    T, F2 = up0.shape
    Fh = F2 // 2
    D = hmid.shape[1]
    tm = _TM_SEQ
    cw = _pick(Fh, _NB)
    kf = p['ffn_conv_w'].shape[1]

    nh = 8 * (kf - 1)

    def body(up_ref, hm_ref, wf_ref, bf_ref, g_ref, wd_ref, a2_ref, f_ref, ho_ref, ubuf, uhalo):
        @pl.when(pl.program_id(0) == 0)
        def _():
            uhalo[...] = jnp.zeros_like(uhalo)

        _fill_causal(ubuf, uhalo, up_ref[...], tm)

        def conv(cs):
            acc = bf_ref[:, cs]
            for k in range(kf):
                acc = acc + wf_ref[k:k + 1, cs] * ubuf[8 * k:8 * k + tm, cs]
            return acc

        f = None
        for c in range(Fh // cw):
            gs = slice(c * cw, (c + 1) * cw)
            gg, _ = _gelu(conv(gs))
            a2 = (gg * conv(slice(Fh + c * cw, Fh + (c + 1) * cw))).astype(_MXU)
            a2_ref[:, gs] = a2
            part = jnp.dot(a2, wd_ref[gs, :], preferred_element_type=F32)
            f = part if f is None else f + part
        f_ref[...] = f
        ho_ref[...] = hm_ref[...] + _rms_fwd(f, g_ref[...])

    row_d = pl.BlockSpec((tm, D), lambda i: (i, 0))
    return pl.pallas_call(
        body, name=name, grid=(T // tm,),
        in_specs=[pl.BlockSpec((tm, F2), lambda i: (i, 0)), row_d, _lspec(l, kf, F2), _lspec(l, 1, F2),
                  _lspec(l, 1, D), pl.BlockSpec((Fh, D), lambda i: (0, 0))],
        out_specs=[pl.BlockSpec((tm, Fh), lambda i: (i, 0)), row_d, row_d],
        out_shape=[_sds((T, Fh), _MXU), _sds((T, D), F32), _sds((T, D), F32)],
        scratch_shapes=[pltpu.VMEM((nh + tm, F2), F32), pltpu.VMEM((nh, F2), F32)],
        compiler_params=_params(),
    )(up0, hmid, p['ffn_conv_w'], p['ffn_conv_b'], p['g_post_ffn'], wdown)


def _loss_head(h, tgt, n_real, name):
    T, D = h.shape
    tm = _TM_SEQ

    def body(h_ref, t_ref, loss_ref, dh_ref):
        i = pl.program_id(0)

        @pl.when(i == 0)
        def _():
            loss_ref[...] = jnp.zeros_like(loss_ref)

        pos = lax.broadcasted_iota(jnp.int32, (tm, D), 0)
        row = i * tm + (pos & 7) * (tm // 8) + (pos >> 3)
        e = jnp.where((row >= N_META) & (row < N_META + n_real), h_ref[...] - t_ref[...], 0.0)
        dh_ref[...] = e * (1.0 / D)
        loss_ref[...] += 0.5 * jnp.sum(jnp.mean(e * e, axis=-1, keepdims=True), axis=0, keepdims=True)

    row_d = pl.BlockSpec((tm, D), lambda i: (i, 0))
    return pl.pallas_call(
        body, name=name, grid=(T // tm,),
        in_specs=[row_d, row_d],
        out_specs=[pl.BlockSpec((8, _LANES), lambda i: (0, 0)), row_d],
        out_shape=[_sds((8, _LANES), F32), _sds((T, D), F32)],
        compiler_params=_params())(h, tgt)


def _ffn_bwd(dh, f, up0, p, wdown, l, name, plan=None):
    T, F2 = up0.shape
    Fh = F2 // 2
    D = dh.shape[1]
    tm = _TM_SEQ
    nt = T // tm
    cw = _pick(Fh, _NB)
    kf = p['ffn_conv_w'].shape[1]
    nh = 8 * (kf - 1)
    assert tm % nh == 0

    def body(dh_ref, f_ref, up_ref, upp_ref, wf_ref, bf_ref, g_ref, wd_ref,
             df_ref, dup_ref, dwf_ref, dbf_ref, dg_ref, ubuf, dbuf, dhalo):
        i = pl.program_id(0)
        r = nt - 1 - i

        @pl.when(i == 0)
        def _():
            dhalo[...] = jnp.zeros_like(dhalo)
            dwf_ref[...] = jnp.zeros_like(dwf_ref)
            dbf_ref[...] = jnp.zeros_like(dbf_ref)
            dg_ref[...] = jnp.zeros_like(dg_ref)

        df, dg = _rms_bwd(f_ref[...], g_ref[...], dh_ref[...])
        dg_ref[...] += dg
        dfb = df.astype(_MXU)
        df_ref[...] = dfb
        ubuf[nh:nh + tm, :] = up_ref[...]
        ubuf[0:nh, :] = _wrap_prev(jnp.where(r == 0, 0.0, upp_ref[...]), up_ref[tm - nh:tm, :])

        def conv(cs):
            acc = bf_ref[:, cs]
            for k in range(kf):
                acc = acc + wf_ref[k:k + 1, cs] * ubuf[8 * k:8 * k + tm, cs]
            return acc

        for c in range(Fh // cw):
            gs = slice(c * cw, (c + 1) * cw)
            us = slice(Fh + c * cw, Fh + (c + 1) * cw)
            ug = conv(gs)
            gg, t = _gelu(ug)
            da2 = lax.dot_general(dfb, wd_ref[gs, :], (((1,), (1,)), ((), ())), preferred_element_type=F32)
            dbuf[0:tm, gs] = da2 * conv(us) * _gelu_grad(ug, t)
            dbuf[0:tm, us] = da2 * gg
        z = dbuf[0:nh, :]
        dbuf[tm:tm + nh, :] = _wrap_next(dhalo[...], z)
        dhalo[...] = z
        for c in range(F2 // cw):
            cs = slice(c * cw, (c + 1) * cw)
            upc = up_ref[:, cs]
            dup = None
            for k in range(kf):
                dsh = dbuf[8 * (kf - 1 - k):8 * (kf - 1 - k) + tm, cs]
                term = wf_ref[k:k + 1, cs] * dsh
                dup = term if dup is None else dup + term
                dwf_ref[k:k + 1, cs] += jnp.sum(dsh * upc, axis=0, keepdims=True)
            dbf_ref[:, cs] += jnp.sum(dbuf[0:tm, cs], axis=0, keepdims=True)
            dup_ref[:, cs] = dup.astype(_MXU)

    rev_d = pl.BlockSpec((tm, D), lambda i: (nt - 1 - i, 0))
    rev_f = pl.BlockSpec((tm, F2), lambda i: (nt - 1 - i, 0))
    prev8 = pl.BlockSpec((nh, F2), lambda i: (jnp.maximum((nt - 1 - i) * (tm // nh) - 1, 0), 0))
    full = lambda *s: pl.BlockSpec(s, lambda i: (0,) * len(s))
    return _ride(
        body, plan, name, nt,
        in_specs=[rev_d, rev_d, rev_f, prev8, _lspec(l, kf, F2), _lspec(l, 1, F2), _lspec(l, 1, D),
                  pl.BlockSpec((Fh, D), lambda i: (0, 0))],
        args=(dh, f, up0, up0, p['ffn_conv_w'], p['ffn_conv_b'], p['g_post_ffn'], wdown),
        out_specs=[rev_d, rev_f, full(kf, F2), full(1, F2), full(1, D)],
        out_shape=[_sds((T, D), _MXU), _sds((T, F2), _MXU), _sds((kf, F2), F32), _sds((1, F2), F32), _sds((1, D), F32)],
        scratch_shapes=[pltpu.VMEM((nh + tm, F2), F32), pltpu.VMEM((tm + nh, F2), F32), pltpu.VMEM((nh, F2), F32)])


def _mm_nt_norm_bwd(dy, w, x, g3, dres, l, name, plan=None):
    T, N = dy.shape
    D = x.shape[1]
    tm, nb = _TM_MM, _pick(N, _NB_DX)
    nj = N // nb

    def body(dy_ref, w_ref, x_ref, g_ref, dr_ref, dh_ref, dg_ref, acc):
        i, j = pl.program_id(0), pl.program_id(1)

        @pl.when((i == 0) & (j == 0))
        def _():
            dg_ref[...] = jnp.zeros_like(dg_ref)

        part = lax.dot_general(dy_ref[...], w_ref[...], (((1,), (1,)), ((), ())), preferred_element_type=F32)

        @pl.when(j == 0)
        def _():
            acc[...] = part

        @pl.when(j > 0)
        def _():
            acc[...] += part

        @pl.when(j == nj - 1)
        def _():
            dx, dg = _rms_bwd(x_ref[...], g_ref[...], acc[...])
            dh_ref[...] = dr_ref[...] + dx
            dg_ref[...] += dg

    row_d = pl.BlockSpec((tm, D), lambda i, j: (i, 0))
    return _ride(
        body, plan, name, (T // tm, nj),
        in_specs=[pl.BlockSpec((tm, nb), lambda i, j: (i, j)), pl.BlockSpec((D, nb), lambda i, j: (0, j)),
                  row_d, _lspec(l, 1, D), row_d],
        args=(dy, w, x, g3, dres),
        out_specs=[row_d, pl.BlockSpec((1, D), lambda i, j: (0, 0))],
        out_shape=[_sds((T, D), F32), _sds((1, D), F32)],
        scratch_shapes=[pltpu.VMEM((tm, D), F32)])


def _mm_tn(xs, dy, name):
    T, K = xs.shape
    N = dy.shape[1]
    tm = _TM_DW
    kb = K if K <= 1024 else _pick(K, _NB)
    nb = _pick(N, _NB)

    def body(x_ref, dy_ref, o_ref):
        @pl.when(pl.program_id(2) == 0)
        def _():
            o_ref[...] = jnp.zeros_like(o_ref)

        o_ref[...] += lax.dot_general(x_ref[...], dy_ref[...], (((0,), (0,)), ((), ())), preferred_element_type=F32)

    return pl.pallas_call(
        body, name=name, grid=(K // kb, N // nb, T // tm),
        in_specs=[pl.BlockSpec((tm, kb), lambda a, b, t: (t, a)), pl.BlockSpec((tm, nb), lambda a, b, t: (t, b))],
        out_specs=pl.BlockSpec((kb, nb), lambda a, b, t: (a, b)),
        out_shape=_sds((K, N), F32),
        compiler_params=_params())(xs, dy)


def _mix_bwd(dhm, o, ya, yb, p, wout, l, name):
    T, D = ya.shape
    Dc = yb.shape[1]
    tm = _TM_SEQ

    def body(dh_ref, o_ref, ya_ref, yb_ref, gp_ref, gl_ref, gc_ref, w_ref,
             do_ref, dya_ref, dyb_ref, dgp_ref, dgl_ref, dgc_ref):
        @pl.when(pl.program_id(0) == 0)
        def _():
            dgp_ref[...] = jnp.zeros_like(dgp_ref)
            dgl_ref[...] = jnp.zeros_like(dgl_ref)
            dgc_ref[...] = jnp.zeros_like(dgc_ref)

        do, dgp = _rms_bwd(o_ref[...], gp_ref[...], dh_ref[...])
        dgp_ref[...] += dgp
        dob = do.astype(_MXU)
        do_ref[...] = dob
        dy = lax.dot_general(dob, w_ref[...], (((1,), (1,)), ((), ())), preferred_element_type=F32)
        dya, dgl = _rms_bwd(ya_ref[...], gl_ref[...], dy[:, 0:D])
        dyb, dgc = _rms_bwd(yb_ref[...], gc_ref[...], dy[:, D:D + Dc])
        dya_ref[...] = dya
        dyb_ref[...] = dyb
        dgl_ref[...] += dgl
        dgc_ref[...] += dgc

    row_d = pl.BlockSpec((tm, D), lambda i: (i, 0))
    row_c = pl.BlockSpec((tm, Dc), lambda i: (i, 0))
    full = lambda *s: pl.BlockSpec(s, lambda i: (0,) * len(s))
    return pl.pallas_call(
        body, name=name, grid=(T // tm,),
        in_specs=[row_d, row_d, row_d, row_c, _lspec(l, 1, D), _lspec(l, 1, D), _lspec(l, 1, Dc),
                  pl.BlockSpec((D + Dc, D), lambda i: (0, 0))],
        out_specs=[row_d, row_d, row_c, full(1, D), full(1, D), full(1, Dc)],
        out_shape=[_sds((T, D), _MXU), _sds((T, D), F32), _sds((T, Dc), F32),
                   _sds((1, D), F32), _sds((1, D), F32), _sds((1, Dc), F32)],
        compiler_params=_params(),
    )(dhm, o, ya, yb, p['g_post_mix'], p['g_out_lru'], p['g_out_conv'], wout)


def _mixers_bwd(dya, dyb, proj, xc, hs, cc, p, l, name, plan=None):
    T, D3 = proj.shape
    D = D3 // 3
    Dc = D // 2
    heads, hd = p['lru_wa'].shape[1], p['lru_wa'].shape[2]
    tm = _TM_SEQ
    nt = T // tm
    per8 = tm // 8
    kl, kc = p['lru_conv_w'].shape[1], p['conv_w'].shape[1]
    nhl, nhc = 8 * (kl - 1), 8 * (kc - 1)
    assert nhc <= tm

    def body(dya_ref, dyb_ref, proj_ref, xc_ref, hs_ref, hsp_ref, cc_ref,
             wl_ref, wa_ref, ba_ref, wx_ref, bx_ref, lam_ref, wc_ref, lg_ref, lb_ref,
             dproj_ref, dwl_ref, dbl_ref, dwa_ref, dba_ref, dwx_ref, dbx_ref, dlam_ref,
             dwc_ref, dbc_ref, dlg_ref, dlb_ref, gcar, dxbuf, dxhalo, dcbuf, dchalo, abuf, gbuf, hbuf, tmp8):
        i = pl.program_id(0)
        r = nt - 1 - i

        @pl.when(i == 0)
        def _():
            for ref in (gcar, dxhalo, dchalo, dwl_ref, dbl_ref, dwa_ref, dba_ref, dwx_ref, dbx_ref, dlam_ref,
                        dwc_ref, dbc_ref, dlg_ref, dlb_ref):
                ref[...] = jnp.zeros_like(ref)

        dya_v = dya_ref[...]
        hs = hs_ref[...]
        gl = proj_ref[:, D:2 * D]
        gg, tg = _gelu(gl)
        dproj_ref[:, D:2 * D] = (dya_v * hs * _gelu_grad(gl, tg)).astype(_MXU)
        dhs = dya_v * gg
        xc = xc_ref[...]
        lam = lam_ref[...]
        xcb, ra, ri, sp, a, mult = _gates(xc, wa_ref, ba_ref[...], wx_ref, bx_ref[...], lam, heads, hd)
        g = _scan_bwd(a, dhs, abuf, gbuf, gcar, tmp8, tm)
        sub8 = lax.broadcasted_iota(jnp.int32, (8, D), 0)
        hbuf[8:8 + tm, :] = hs
        hbuf[0:8, :] = jnp.where(sub8 == 0, jnp.where(r == 0, 0.0, hsp_ref[7:8, :]),
                                 pltpu.roll(hs_ref[tm - 8:tm, :], 1, 0))
        da = g * hbuf[0:tm, :]
        gx = g * xc
        dxc = g * mult * ri
        dla = da * a - (gx * ri) * (a * a) / mult
        dlam_ref[...] += jnp.sum(dla * ra, axis=0, keepdims=True) * (LRU_C * jax.nn.sigmoid(-lam))
        dpa = (dla * ((-LRU_C) * sp)) * ra * (1.0 - ra)
        dpx = (gx * mult) * ri * (1.0 - ri)
        dba_ref[...] += jnp.sum(dpa, axis=0, keepdims=True)
        dbx_ref[...] += jnp.sum(dpx, axis=0, keepdims=True)
        dpab, dpxb = dpa.astype(_MXU), dpx.astype(_MXU)
        for h in range(heads):
            hsl = slice(h * hd, (h + 1) * hd)
            dwa_ref[h] += lax.dot_general(xcb[:, hsl], dpab[:, hsl], (((0,), (0,)), ((), ())),
                                          preferred_element_type=F32)
            dwx_ref[h] += lax.dot_general(xcb[:, hsl], dpxb[:, hsl], (((0,), (0,)), ((), ())),
                                          preferred_element_type=F32)
        dxc = dxc + _bd_mm_t(dpab, wa_ref, heads, hd) + _bd_mm_t(dpxb, wx_ref, heads, hd)
        dbl_ref[...] += jnp.sum(dxc, axis=0, keepdims=True)
        _fill_anticausal(dxbuf, dxhalo, dxc, tm)
        xl = proj_ref[:, 0:D]
        dxl = None
        for k in range(kl):
            dsh = dxbuf[8 * (kl - 1 - k):8 * (kl - 1 - k) + tm, :]
            term = wl_ref[k:k + 1, :] * dsh
            dxl = term if dxl is None else dxl + term
            dwl_ref[k:k + 1, :] += jnp.sum(dsh * xl, axis=0, keepdims=True)
        dproj_ref[:, 0:D] = dxl.astype(_MXU)

        ca = proj_ref[:, 2 * D:2 * D + Dc]
        sg = jax.nn.sigmoid(proj_ref[:, 2 * D + Dc:3 * D])
        cg = ca * sg
        nrm, rss = _group_norm(cc_ref[...], CONV_GROUPS)
        lg = lg_ref[...]
        cl = nrm * lg + lb_ref[...]
        sc = jax.nn.sigmoid(cl)
        dcl = dyb_ref[...] * (sc * (1.0 + cl * (1.0 - sc)))
        dlg_ref[...] += jnp.sum(dcl * nrm, axis=0, keepdims=True)
        dlb_ref[...] += jnp.sum(dcl, axis=0, keepdims=True)
        dnrm = dcl * lg
        gsz = Dc // CONV_GROUPS
        parts = []
        for gi in range(CONV_GROUPS):
            sl = slice(gi * gsz, (gi + 1) * gsz)
            dn, nn = dnrm[:, sl], nrm[:, sl]
            parts.append(rss[gi] * (dn - jnp.mean(dn, axis=-1, keepdims=True)
                                    - nn * jnp.mean(dn * nn, axis=-1, keepdims=True)))
        dcc = jnp.concatenate(parts, axis=-1)
        dbc_ref[...] += jnp.sum(dcc, axis=0, keepdims=True)
        _fill_anticausal(dcbuf, dchalo, dcc, tm)
        dcg = None
        for k in range(kc):
            dsh = dcbuf[8 * (kc - 1 - k):8 * (kc - 1 - k) + tm, :]
            term = wc_ref[k:k + 1, :] * dsh
            dcg = term if dcg is None else dcg + term
            dwc_ref[k:k + 1, :] += jnp.sum(dsh * cg, axis=0, keepdims=True)
        dproj_ref[:, 2 * D:2 * D + Dc] = (dcg * sg).astype(_MXU)
        dproj_ref[:, 2 * D + Dc:3 * D] = (dcg * ca * sg * (1.0 - sg)).astype(_MXU)

    rev_d = pl.BlockSpec((tm, D), lambda i: (nt - 1 - i, 0))
    rev_c = pl.BlockSpec((tm, Dc), lambda i: (nt - 1 - i, 0))
    rev_p = pl.BlockSpec((tm, D3), lambda i: (nt - 1 - i, 0))
    prev8 = pl.BlockSpec((8, D), lambda i: (jnp.maximum((nt - 1 - i) * per8 - 1, 0), 0))
    full = lambda *s: pl.BlockSpec(s, lambda i: (0,) * len(s))
    return _ride(
        body, plan, name, nt,
        in_specs=[rev_d, rev_c, rev_p, rev_d, rev_d, prev8, rev_c,
                  _lspec(l, kl, D), _lspec(l, heads, hd, hd), _lspec(l, 1, D), _lspec(l, heads, hd, hd),
                  _lspec(l, 1, D), _lspec(l, 1, D), _lspec(l, kc, Dc), _lspec(l, 1, Dc), _lspec(l, 1, Dc)],
        args=(dya, dyb, proj, xc, hs, hs, cc, p['lru_conv_w'], p['lru_wa'], p['lru_ba'], p['lru_wx'], p['lru_bx'],
              p['lru_lambda'], p['conv_w'], p['conv_ln_g'], p['conv_ln_b']),
        out_specs=[rev_p, full(kl, D), full(1, D), full(heads, hd, hd), full(1, D), full(heads, hd, hd), full(1, D),
                   full(1, D), full(kc, Dc), full(1, Dc), full(1, Dc), full(1, Dc)],
        out_shape=[_sds((T, D3), _MXU), _sds((kl, D), F32), _sds((1, D), F32), _sds((heads, hd, hd), F32),
                   _sds((1, D), F32), _sds((heads, hd, hd), F32), _sds((1, D), F32), _sds((1, D), F32),
                   _sds((kc, Dc), F32), _sds((1, Dc), F32), _sds((1, Dc), F32), _sds((1, Dc), F32)],
        scratch_shapes=[pltpu.VMEM((1, D), F32), pltpu.VMEM((tm + nhl, D), F32), pltpu.VMEM((nhl, D), F32),
                        pltpu.VMEM((tm + nhc, Dc), F32), pltpu.VMEM((nhc, Dc), F32),
                        pltpu.VMEM((tm, D), F32), pltpu.VMEM((tm, D), F32), pltpu.VMEM((8 + tm, D), F32),
                        pltpu.VMEM((8, D), F32)])


def _sum_sources(recv, name):
    _, R, C = recv.shape
    rb = _pick_rows(R, 1024)

    def body(r_ref, o_ref):
        acc = r_ref[0]
        for s in range(1, _N_DEV):
            acc = acc + r_ref[s]
        o_ref[...] = acc

    return pl.pallas_call(
        body, name=name, grid=(R // rb,),
        in_specs=[pl.BlockSpec((_N_DEV, rb, C), lambda i: (0, i, 0))],
        out_specs=pl.BlockSpec((rb, C), lambda i: (i, 0)),
        out_shape=_sds((R, C), F32), compiler_params=_params())(recv)


def _adamw(g, w, m, v, name):
    R, C = w.shape
    summed = g.ndim == 3
    rb = _pick_rows(R, max(8, min(512, _ADAM_BLOCK_ELEMS // C)))
    c1 = 1.0 - ADAM_B1 ** ADAM_STEP
    c2 = 1.0 - ADAM_B2 ** ADAM_STEP

    def body(g_ref, w_ref, m_ref, v_ref, go_ref, d_ref, mo_ref, vo_ref):
        if summed:
            gv = g_ref[0]
            for s in range(1, _N_DEV):
                gv = gv + g_ref[s]
        else:
            gv = g_ref[...]
        go_ref[...] = gv
        mn = ADAM_B1 * m_ref[...] + (1.0 - ADAM_B1) * gv
        vn = ADAM_B2 * v_ref[...] + (1.0 - ADAM_B2) * (gv * gv)
        mo_ref[...] = mn
        vo_ref[...] = vn
        d_ref[...] = (-ADAM_LR) * ((mn / c1) / (jnp.sqrt(vn / c2) + ADAM_EPS) + ADAM_WD * w_ref[...])

    blk = pl.BlockSpec((rb, C), lambda i: (i, 0))
    gspec = pl.BlockSpec((_N_DEV, rb, C), lambda i: (0, i, 0)) if summed else blk
    return pl.pallas_call(
        body, name=name, grid=(R // rb,),
        in_specs=[gspec, blk, blk, blk], out_specs=[blk, blk, blk, blk],
        out_shape=[_sds((R, C), F32)] * 4, compiler_params=_params())(g, w, m, v)


def _adamw_layers(recvs, w, m, v, name):
    L, R, C = w.shape
    rb = _pick_rows(R, max(8, _ADAM_BLOCK_ELEMS // (4 * C)))
    c1 = 1.0 - ADAM_B1 ** ADAM_STEP
    c2 = 1.0 - ADAM_B2 ** ADAM_STEP

    def body(*refs):
        r_refs = refs[:L]
        w_ref, m_ref, v_ref, go_ref, d_ref, mo_ref, vo_ref = refs[L:]
        for l in range(L):
            gv = r_refs[l][0]
            for s in range(1, _N_DEV):
                gv = gv + r_refs[l][s]
            go_ref[l] = gv
            mn = ADAM_B1 * m_ref[l] + (1.0 - ADAM_B1) * gv
            vn = ADAM_B2 * v_ref[l] + (1.0 - ADAM_B2) * (gv * gv)
            mo_ref[l] = mn
            vo_ref[l] = vn
            d_ref[l] = (-ADAM_LR) * ((mn / c1) / (jnp.sqrt(vn / c2) + ADAM_EPS) + ADAM_WD * w_ref[l])

    blk = pl.BlockSpec((L, rb, C), lambda i: (0, i, 0))
    return pl.pallas_call(
        body, name=name, grid=(R // rb,),
        in_specs=[pl.BlockSpec((_N_DEV, rb, C), lambda i: (0, i, 0))] * L + [blk, blk, blk],
        out_specs=[blk, blk, blk, blk],
        out_shape=[_sds((L, R, C), F32)] * 4, compiler_params=_params())(*recvs, w, m, v)


def _pack_rows(flat_parts, dtype, row_mult):
    flat = jnp.concatenate([f.reshape(-1).astype(dtype) for f in flat_parts])
    n = flat.shape[0]
    per = _LANES * row_mult
    padded = -(-n // per) * per
    if padded != n:
        flat = jnp.concatenate([flat, jnp.zeros((padded - n,), dtype)])
    return flat.reshape(-1, _LANES)


def _unpack(flat, shapes):
    out, off = [], 0
    for s in shapes:
        n = math.prod(s)
        out.append(flat[off:off + n].reshape(s))
        off += n
    return out


def _to_pieces(full):
    n = full.shape[-1] // _N_DEV
    t = full.reshape(full.shape[:-1] + (_N_DEV, n))
    return jnp.moveaxis(t, -2, 0).reshape(_N_DEV, -1)


def _from_gathered(seg, shard_shape, axis):
    t = seg.reshape((_N_DEV,) + tuple(shard_shape))
    t = jnp.moveaxis(t, 0, axis)
    shape = list(shard_shape)
    shape[axis] *= _N_DEV
    return t.reshape(shape)


def kernel(x, meta_tokens, g_pre_mix, w_in, lru_conv_w, lru_conv_b, lru_wa, lru_ba, lru_wx, lru_bx, lru_lambda, conv_w, conv_b, conv_ln_g, conv_ln_b, g_out_lru, g_out_conv, w_out, g_post_mix, g_pre_ffn, w_up, ffn_conv_w, ffn_conv_b, w_down, g_post_ffn, loss_target, m_meta_tokens, m_g_pre_mix, m_w_in, m_lru_conv_w, m_lru_conv_b, m_lru_wa, m_lru_ba, m_lru_wx, m_lru_bx, m_lru_lambda, m_conv_w, m_conv_b, m_conv_ln_g, m_conv_ln_b, m_g_out_lru, m_g_out_conv, m_w_out, m_g_post_mix, m_g_pre_ffn, m_w_up, m_ffn_conv_w, m_ffn_conv_b, m_w_down, m_g_post_ffn, v_meta_tokens, v_g_pre_mix, v_w_in, v_lru_conv_w, v_lru_conv_b, v_lru_wa, v_lru_ba, v_lru_wx, v_lru_bx, v_lru_lambda, v_conv_w, v_conv_b, v_conv_ln_g, v_conv_ln_b, v_g_out_lru, v_g_out_conv, v_w_out, v_g_post_mix, v_g_pre_ffn, v_w_up, v_ffn_conv_w, v_ffn_conv_b, v_w_down, v_g_post_ffn):
    given = dict(locals())
    W = {n: given[n] for n in W_NAMES}
    M = {n: given['m_' + n] for n in W_NAMES}
    V = {n: given['v_' + n] for n in W_NAMES}
    S, D = x.shape[1], x.shape[2]
    L = g_pre_mix.shape[0]
    Dc = D // 2
    step = math.lcm(_TM_MM, _TM_SEQ, _TM_FWD, _TM_DW)
    T = -(-(N_META + S) // step) * step

    first, rest = ['w_in'], ['w_out', 'w_up', 'w_down']

    def layer_pack(l, names):
        return _pack_rows([W[n][l] for n in names], _MXU, 16)

    def layer_weights(gathered, names):
        segs = _unpack_cols(gathered.reshape(_N_DEV, -1), [W[n].shape[1:] for n in names])
        return {n: _from_gathered(seg, W[n].shape[1:], 1 if n in ('w_in', 'w_up') else 0)
                for n, seg in zip(names, segs)}

    small_pack = _pack_rows([W[n] for n in SMALL_SHARDED], F32, 8)
    big_g, small_g = _run_plan(_GatherPlan([layer_pack(0, first), small_pack]), "gather_weights_first")
    small_segs = _unpack_cols(small_g.reshape(_N_DEV, -1), [W[n].shape for n in SMALL_SHARDED])
    full = {}
    for n, seg in zip(SMALL_SHARDED, small_segs):
        full[n] = _from_gathered(seg, W[n].shape, W[n].ndim - 1)

    p = {}
    for n in W_NAMES:
        if n in BIG or n == 'meta_tokens':
            continue
        a = full[n] if n in full else W[n]
        p[n] = a.reshape(L, 1, a.shape[1]) if a.ndim == 2 else a

    pad_rows = T - N_META - S
    h = _perm(jnp.concatenate([full['meta_tokens'], x[0], jnp.zeros((pad_rows, D), F32)], axis=0), _TM_SEQ)
    tgt = _perm(jnp.concatenate([jnp.zeros((N_META, D), F32), loss_target[0], jnp.zeros((pad_rows, D), F32)],
                                axis=0), _TM_SEQ)

    saved = []
    wl = layer_weights(big_g, first)
    for l in range(L):
        plan = _GatherPlan([layer_pack(0, rest)]) if l == 0 else None
        (proj, z1), nxt = _norm_mm(h, p['g_pre_mix'], wl['w_in'], l, f"in_proj_l{l}", plan)
        if plan:
            wl = {**wl, **layer_weights(nxt[0], rest)}
        plan = _GatherPlan([layer_pack(l + 1, BIG)]) if l + 1 < L else None
        (xc, hs, ya, cc, yb), _ = _mixers_fwd(proj, p, l, f"mixers_fwd_l{l}")
        y, o, hmid = _mix_out(ya, yb, h, p, wl['w_out'], l, f"mix_out_l{l}")
        (up0, z2), nxt = _norm_mm(hmid, p['g_pre_ffn'], wl['w_up'], l, f"up_proj_l{l}", plan)
        a2, f, hout = _ffn_fwd(up0, hmid, p, wl['w_down'], l, f"ffn_fwd_l{l}")
        saved.append(dict(h=h, z1=z1, proj=proj, xc=xc, hs=hs, ya=ya, cc=cc, yb=yb, y=y, o=o, hmid=hmid,
                          z2=z2, up0=up0, a2=a2, f=f, w=wl))
        h = hout
        if plan:
            wl = layer_weights(nxt[0], BIG)
    loss_tile, dh = _loss_head(h, tgt, S, "loss_head")
    loss = lax.psum(loss_tile[0, 0], ("x", "y", "c"))

    small_g_names = ['g_pre_mix', 'lru_conv_w', 'lru_conv_b', 'lru_wa', 'lru_ba', 'lru_wx', 'lru_bx', 'lru_lambda',
                     'conv_w', 'conv_b', 'conv_ln_g', 'conv_ln_b', 'g_out_lru', 'g_out_conv', 'g_post_mix',
                     'g_pre_ffn', 'ffn_conv_w', 'ffn_conv_b', 'g_post_ffn']
    per_layer = {n: [None] * L for n in small_g_names}
    recv = {n: [None] * L for n in BIG}
    d_in_pending = None
    for l in reversed(range(L)):
        sv = saved[l]
        wl = sv['w']
        plan = _ExchangePlan([d_in_pending], ['cols']) if d_in_pending is not None else None
        (df, dup0, dwf, dbf, dgpf), got = _ffn_bwd(dh, sv['f'], sv['up0'], p, wl['w_down'], l, f"ffn_bwd_l{l}", plan)
        if plan:
            recv['w_in'][l + 1] = got[0]
        d_down = _mm_tn(sv['a2'], df, f"dw_down_l{l}")
        d_up = _mm_tn(sv['z2'], dup0, f"dw_up_l{l}")
        (dhm, dgpre), got = _mm_nt_norm_bwd(dup0, wl['w_up'], sv['hmid'], p['g_pre_ffn'], dh, l, f"up_bwd_l{l}",
                                            _ExchangePlan([d_down], ['rows']))
        recv['w_down'][l] = got[0]
        do, dya, dyb, dgpm, dgol, dgoc = _mix_bwd(dhm, sv['o'], sv['ya'], sv['yb'], p, wl['w_out'], l,
                                                  f"mix_bwd_l{l}")
        d_out = _mm_tn(sv['y'], do, f"dw_out_l{l}")
        (dproj, dwl, dbl, dwa, dba, dwx, dbx, dlam, dwc, dbc, dlg, dlb), got = _mixers_bwd(
            dya, dyb, sv['proj'], sv['xc'], sv['hs'], sv['cc'], p, l, f"mixers_bwd_l{l}",
            _ExchangePlan([d_up], ['cols']))
        recv['w_up'][l] = got[0]
        d_in = _mm_tn(sv['z1'], dproj, f"dw_in_l{l}")
        plan = _ExchangePlan([d_out, d_in], ['rows', 'cols']) if l == 0 else _ExchangePlan([d_out], ['rows'])
        (dh, dgpmix), got = _mm_nt_norm_bwd(dproj, wl['w_in'], sv['h'], p['g_pre_mix'], dhm, l, f"in_bwd_l{l}", plan)
        recv['w_out'][l] = got[0]
        if l == 0:
            recv['w_in'][0] = got[1]
        d_in_pending = d_in
        for n, val in (('g_pre_mix', dgpmix), ('lru_conv_w', dwl), ('lru_conv_b', dbl), ('lru_wa', dwa),
                       ('lru_ba', dba), ('lru_wx', dwx), ('lru_bx', dbx), ('lru_lambda', dlam), ('conv_w', dwc),
                       ('conv_b', dbc), ('conv_ln_g', dlg), ('conv_ln_b', dlb), ('g_out_lru', dgol),
                       ('g_out_conv', dgoc), ('g_post_mix', dgpm), ('g_pre_ffn', dgpre), ('ffn_conv_w', dwf),
                       ('ffn_conv_b', dbf), ('g_post_ffn', dgpf)):
            per_layer[n][l] = val
    dh = _unperm(dh, _TM_SEQ)
    grad_x = dh[N_META:N_META + S][None]
    partial = {n: jnp.stack(per_layer[n]).reshape((L,) + tuple(
        (full[n] if n in full else W[n]).shape[1:])) for n in small_g_names}
    partial['meta_tokens'] = dh[0:N_META]

    shard_pack = jnp.concatenate([_to_pieces(partial[n]) for n in SMALL_SHARDED], axis=1)
    n_sh = shard_pack.shape[1]
    rs = -(-n_sh // (8 * _LANES)) * 8
    shard_pack = jnp.concatenate([shard_pack, jnp.zeros((_N_DEV, rs * _LANES - n_sh), F32)], axis=1)
    rep_flat = jnp.concatenate([partial[n].reshape(-1) for n in REPLICATED])
    n_rep = rep_flat.shape[0]
    rr = -(-n_rep // (_N_DEV * 8 * _LANES)) * 8
    rep_flat = jnp.concatenate([rep_flat, jnp.zeros((_N_DEV * rr * _LANES - n_rep,), F32)])
    small_send = jnp.concatenate([shard_pack, rep_flat.reshape(_N_DEV, rr * _LANES)], axis=1)
    small_send = small_send.reshape(_N_DEV, rs + rr, _LANES)
    (r_small,) = _run_plan(_ExchangePlan([small_send], ['slots']), "grad_exchange_small")
    small_red = _sum_sources(r_small, "sum_small")
    (rep_g,) = _run_plan(_GatherPlan([small_red[rs:]]), "gather_replicated_grads")
    rep_g = rep_g.reshape(_N_DEV * rr, _LANES)

    out = {}
    for n in BIG:
        out[n] = list(_adamw_layers(recv[n], W[n], M[n], V[n], f"adamw_{n}"))
    sh_shapes = [W[n].shape for n in SMALL_SHARDED]
    res = _adamw(small_red[:rs], _pack_rows([W[n] for n in SMALL_SHARDED], F32, 8),
                 _pack_rows([M[n] for n in SMALL_SHARDED], F32, 8),
                 _pack_rows([V[n] for n in SMALL_SHARDED], F32, 8), "adamw_small_sharded")
    for k in range(4):
        for n, val in zip(SMALL_SHARDED, _unpack(res[k].reshape(-1), sh_shapes)):
            out.setdefault(n, [None] * 4)[k] = val
    rep_shapes = [W[n].shape for n in REPLICATED]
    res = _adamw(rep_g, _pack_rows([W[n] for n in REPLICATED], F32, 8 * _N_DEV),
                 _pack_rows([M[n] for n in REPLICATED], F32, 8 * _N_DEV),
                 _pack_rows([V[n] for n in REPLICATED], F32, 8 * _N_DEV), "adamw_replicated")
    for k in range(4):
        for n, val in zip(REPLICATED, _unpack(res[k].reshape(-1), rep_shapes)):
            out.setdefault(n, [None] * 4)[k] = val

    return (loss, grad_x, *[out[n][0] for n in W_NAMES], *[out[n][1] for n in W_NAMES],
            *[out[n][2] for n in W_NAMES], *[out[n][3] for n in W_NAMES])


def _unpack_cols(gathered, shapes):
    out, off = [], 0
    for s in shapes:
        n = math.prod(s)
        out.append(gathered[:, off:off + n])
        off += n
    return out
```

```python
import math

import jax
import jax.numpy as jnp
from jax import lax
from jax.experimental import pallas as pl
from jax.experimental.pallas import tpu as pltpu

F32 = jnp.float32
_MXU = jnp.bfloat16
_TM_MM = 768
_TM_DW = 2816
_TM_SEQ = 256
_NB = 768
_NB_DX = 1536
_VMEM_LIMIT = 56 * 1024 * 1024
_ADAM_BLOCK_ELEMS = 128 * 1024
_LANES = 128
_N_DEV = 8

EPS = 1e-6
N_META = 16
LRU_C = 8.0
CONV_GROUPS = 4
ADAM_LR, ADAM_B1, ADAM_B2, ADAM_EPS, ADAM_WD, ADAM_STEP = 0.001, 0.9, 0.999, 1e-08, 0.01, 10
_GELU_K0 = math.sqrt(2.0 / math.pi)
_GELU_K1 = 0.044715

W_NAMES = ['meta_tokens', 'g_pre_mix', 'w_in', 'lru_conv_w', 'lru_conv_b', 'lru_wa', 'lru_ba', 'lru_wx', 'lru_bx',
           'lru_lambda', 'conv_w', 'conv_b', 'conv_ln_g', 'conv_ln_b', 'g_out_lru', 'g_out_conv', 'w_out',
           'g_post_mix', 'g_pre_ffn', 'w_up', 'ffn_conv_w', 'ffn_conv_b', 'w_down', 'g_post_ffn']
BIG = ['w_in', 'w_out', 'w_up', 'w_down']
SMALL_SHARDED = ['meta_tokens', 'lru_conv_w', 'conv_w', 'ffn_conv_w']
REPLICATED = [n for n in W_NAMES if n not in BIG and n not in SMALL_SHARDED]


def _params():
    return pltpu.CompilerParams(vmem_limit_bytes=_VMEM_LIMIT)


def _pick(n, pref):
    if n <= pref:
        return n
    best = None
    for b in range(_LANES, pref + 1, _LANES):
        if n % b == 0:
            best = b
    assert best is not None, (n, pref)
    return best


def _pick_rows(n, pref):
    if n <= pref:
        return n
    best = None
    for b in range(8, pref + 1, 8):
        if n % b == 0:
            best = b
    assert best is not None, (n, pref)
    return best


def _lspec(l, *dims):
    zeros = (0,) * len(dims)
    return pl.BlockSpec((None,) + tuple(dims), lambda *_: (l,) + zeros)


def _sds(shape, dtype):
    return jax.ShapeDtypeStruct(tuple(shape), dtype)


def _rms_fwd(x, g):
    r = lax.rsqrt(jnp.mean(x * x, axis=-1, keepdims=True) + EPS)
    return (x * r) * g


def _rms_bwd(x, g, dy):
    r = lax.rsqrt(jnp.mean(x * x, axis=-1, keepdims=True) + EPS)
    xh = x * r
    dg = jnp.sum(dy * xh, axis=0, keepdims=True)
    dxh = dy * g
    dx = r * (dxh - xh * jnp.mean(dxh * xh, axis=-1, keepdims=True))
    return dx, dg


def _gelu(x):
    t = jnp.tanh(_GELU_K0 * (x + _GELU_K1 * (x * x * x)))
    return 0.5 * x * (1.0 + t), t


def _gelu_grad(x, t):
    return 0.5 * (1.0 + t) + 0.5 * x * (1.0 - t * t) * (_GELU_K0 * (1.0 + 3.0 * _GELU_K1 * x * x))


def _log1p(e):
    u = 1.0 + e
    return jnp.where(u == 1.0, e, jnp.log(u) * (e / (u - 1.0)))


def _softplus(z):
    return jnp.maximum(z, 0.0) + _log1p(jnp.exp(-jnp.abs(z)))


def _one_minus_exp(x):
    p = -x * (1.0 + x * (0.5 + x * (1.0 / 6 + x * (1.0 / 24 + x * (1.0 / 120 + x * (1.0 / 720))))))
    return jnp.where(x > -0.125, p, 1.0 - jnp.exp(x))


def _shift_down(x, s, fill, row):
    return jnp.where(row >= s, pltpu.roll(x, s, 0), fill)


def _shift_up(x, s, fill, row):
    n = x.shape[0]
    return jnp.where(row < n - s, pltpu.roll(x, n - s, 0), fill)


def _bd_mm(xb, w_ref, heads, hd):
    return jnp.concatenate(
        [jnp.dot(xb[:, h * hd:(h + 1) * hd], w_ref[h].astype(_MXU), preferred_element_type=F32)
         for h in range(heads)], axis=-1)


def _bd_mm_t(db, w_ref, heads, hd):
    return jnp.concatenate(
        [lax.dot_general(db[:, h * hd:(h + 1) * hd], w_ref[h].astype(_MXU), (((1,), (1,)), ((), ())),
                         preferred_element_type=F32)
         for h in range(heads)], axis=-1)


def _gates(xc, wa_ref, ba, wx_ref, bx, lam, heads, hd):
    xcb = xc.astype(_MXU)
    ra = jax.nn.sigmoid(_bd_mm(xcb, wa_ref, heads, hd) + ba)
    ri = jax.nn.sigmoid(_bd_mm(xcb, wx_ref, heads, hd) + bx)
    sp = _softplus(-lam)
    la = (-LRU_C) * ra * sp
    a = jnp.exp(la)
    mult = jnp.sqrt(_one_minus_exp(2.0 * la))
    return xcb, ra, ri, sp, a, mult


def _group_norm(cc, groups):
    gs = cc.shape[-1] // groups
    outs, rss = [], []
    for g in range(groups):
        seg = cc[:, g * gs:(g + 1) * gs]
        mu = jnp.mean(seg, axis=-1, keepdims=True)
        d = seg - mu
        rs = lax.rsqrt(jnp.mean(d * d, axis=-1, keepdims=True) + EPS)
        outs.append(d * rs)
        rss.append(rs)
    return jnp.concatenate(outs, axis=-1), rss


_MESH = pl.DeviceIdType.MESH
_ANY = pl.BlockSpec(memory_space=pl.ANY)


def _my_index():
    return 4 * lax.axis_index("x") + 2 * lax.axis_index("y") + lax.axis_index("c")


class _GatherPlan:
    def __init__(self, blocks):
        self.blocks = list(blocks)

    def operands(self):
        return self.blocks

    def out_shape(self):
        return [_sds((_N_DEV,) + b.shape, b.dtype) for b in self.blocks]

    def scratch(self):
        n = len(self.blocks)
        return [pltpu.SemaphoreType.DMA((7 * n,)), pltpu.SemaphoreType.DMA((7 * n,)), pltpu.SemaphoreType.DMA((n,))]

    def _copies(self, a, ins, outs, sems):
        send_sems, recv_sems, local_sems = sems
        x, y, c = lax.axis_index("x"), lax.axis_index("y"), lax.axis_index("c")
        me, sibling = (x, y, c), (x, y, 1 - c)
        chips = [(1 - x, y), (x, 1 - y), (1 - x, 1 - y)]

        def slot(dev):
            return outs[a].at[4 * dev[0] + 2 * dev[1] + dev[2]]

        def copy(k, block, to, src=None):
            return pltpu.make_async_remote_copy(
                src_ref=slot(block) if src is None else src, dst_ref=slot(block),
                send_sem=send_sems.at[7 * a + k], recv_sem=recv_sems.at[7 * a + k],
                device_id=to, device_id_type=_MESH)

        mine = pltpu.make_async_copy(ins[a], slot(me), local_sems.at[a])
        first = [copy(0, me, sibling, src=ins[a])]
        first += [copy(1 + j, me, (*chip, c), src=ins[a]) for j, chip in enumerate(chips)]
        passed = [copy(4 + j, (*chip, c), sibling) for j, chip in enumerate(chips)]
        from_chips = [copy(1 + j, (*chip, c), me) for j, chip in enumerate(chips)]
        from_sibling = [copy(0, sibling, me)] + [copy(4 + j, (*chip, 1 - c), me) for j, chip in enumerate(chips)]
        return mine, first, passed, from_chips, from_sibling

    def start(self, ins, outs, sems):
        for a in range(len(self.blocks)):
            mine, first, _, _, _ = self._copies(a, ins, outs, sems)
            mine.start()
            for cp in first:
                cp.start()

    def forward(self, ins, outs, sems):
        for a in range(len(self.blocks)):
            _, _, passed, from_chips, _ = self._copies(a, ins, outs, sems)
            for j in range(3):
                from_chips[j].wait_recv()
                passed[j].start()

    def finish(self, ins, outs, sems):
        for a in range(len(self.blocks)):
            mine, first, passed, _, from_sibling = self._copies(a, ins, outs, sems)
            for cp in from_sibling:
                cp.wait_recv()
            for cp in first + passed:
                cp.wait_send()
            mine.wait()

    def begin(self, i, steps, ins, outs, sems):
        @pl.when(i == 0)
        def _():
            self.start(ins, outs, sems)

        @pl.when(i == (3 * steps) // 4)
        def _():
            self.forward(ins, outs, sems)

    def end(self, i, steps, ins, outs, sems):
        @pl.when(i == steps - 1)
        def _():
            self.finish(ins, outs, sems)


class _ExchangePlan:
    def __init__(self, arrs, kinds):
        self.arrs, self.kinds = list(arrs), list(kinds)

    def _piece_shape(self, a):
        shp = self.arrs[a].shape
        if self.kinds[a] == 'cols':
            return (shp[0], shp[1] // _N_DEV)
        if self.kinds[a] == 'rows':
            return (shp[0] // _N_DEV, shp[1])
        return tuple(shp[1:])

    def operands(self):
        return self.arrs

    def out_shape(self):
        return [_sds((_N_DEV,) + self._piece_shape(a), F32) for a in range(len(self.arrs))]

    def scratch(self):
        n = len(self.arrs)
        return [pltpu.SemaphoreType.DMA((_N_DEV * n,)), pltpu.SemaphoreType.DMA((_N_DEV * n,)),
                pltpu.SemaphoreType.DMA((n,))]

    def _copies(self, ins, outs, sems):
        send_sems, recv_sems, local_sems = sems
        me = _my_index()

        def piece(a, j):
            ps = self._piece_shape(a)
            if self.kinds[a] == 'cols':
                return ins[a].at[:, pl.ds(j * ps[1], ps[1])]
            if self.kinds[a] == 'rows':
                return ins[a].at[pl.ds(j * ps[0], ps[0]), :]
            return ins[a].at[j]

        def remote(a, j):
            return pltpu.make_async_remote_copy(
                src_ref=piece(a, j), dst_ref=outs[a].at[me],
                send_sem=send_sems.at[_N_DEV * a + j], recv_sem=recv_sems.at[_N_DEV * a + me],
                device_id=(j >> 2, (j >> 1) & 1, j & 1), device_id_type=_MESH)

        def arrival(a, s):
            return pltpu.make_async_remote_copy(
                src_ref=piece(a, s), dst_ref=outs[a].at[s],
                send_sem=send_sems.at[_N_DEV * a + s], recv_sem=recv_sems.at[_N_DEV * a + s],
                device_id=(s >> 2, (s >> 1) & 1, s & 1), device_id_type=_MESH)

        def local(a, j):
            return pltpu.make_async_copy(piece(a, j), outs[a].at[j], local_sems.at[a])

        return me, remote, arrival, local

    def start(self, ins, outs, sems):
        me, remote, _, local = self._copies(ins, outs, sems)
        for j in range(_N_DEV):
            for a in range(len(self.arrs)):
                @pl.when(me != j)
                def _(a=a, j=j):
                    remote(a, j).start()

                @pl.when(me == j)
                def _(a=a, j=j):
                    local(a, j).start()

    def finish(self, ins, outs, sems):
        me, remote, arrival, local = self._copies(ins, outs, sems)
        for j in range(_N_DEV):
            for a in range(len(self.arrs)):
                @pl.when(me != j)
                def _(a=a, j=j):
                    arrival(a, j).wait_recv()
                    remote(a, j).wait_send()

                @pl.when(me == j)
                def _(a=a, j=j):
                    local(a, j).wait()

    def forward(self, ins, outs, sems):
        pass

    def begin(self, i, steps, ins, outs, sems):
        @pl.when(i == 0)
        def _():
            self.start(ins, outs, sems)

    def end(self, i, steps, ins, outs, sems):
        @pl.when(i == steps - 1)
        def _():
            self.finish(ins, outs, sems)


def _ride(main, plan, name, grid, in_specs, args, out_specs, out_shape, scratch_shapes):
    grid = (grid,) if isinstance(grid, int) else tuple(grid)
    steps = math.prod(grid)
    n_in, n_out, n_sc = len(in_specs), len(out_specs), len(scratch_shapes)
    p_args = plan.operands() if plan else []
    p_out = plan.out_shape() if plan else []
    p_sc = plan.scratch() if plan else []

    def body(*refs):
        k = 0
        ins = refs[k:k + n_in]; k += n_in
        p_ins = refs[k:k + len(p_args)]; k += len(p_args)
        outs = refs[k:k + n_out]; k += n_out
        p_outs = refs[k:k + len(p_out)]; k += len(p_out)
        scr = refs[k:k + n_sc]; k += n_sc
        sems = refs[k:]
        i = pl.program_id(0)
        for axis in range(1, len(grid)):
            i = i * grid[axis] + pl.program_id(axis)
        if plan:
            plan.begin(i, steps, p_ins, p_outs, sems)
        main(*ins, *outs, *scr)
        if plan:
            plan.end(i, steps, p_ins, p_outs, sems)

    res = pl.pallas_call(
        body, name=name, grid=grid,
        in_specs=list(in_specs) + [_ANY] * len(p_args),
        out_specs=list(out_specs) + [_ANY] * len(p_out),
        out_shape=list(out_shape) + p_out,
        scratch_shapes=list(scratch_shapes) + p_sc,
        compiler_params=_params())(*args, *p_args)
    return res[:n_out], res[n_out:]


def _run_plan(plan, name):
    n_args, n_out = len(plan.operands()), len(plan.out_shape())

    def body(*refs):
        ins, outs, sems = refs[:n_args], refs[n_args:n_args + n_out], refs[n_args + n_out:]
        plan.start(ins, outs, sems)
        plan.forward(ins, outs, sems)
        plan.finish(ins, outs, sems)

    return pl.pallas_call(
        body, name=name, in_specs=[_ANY] * n_args, out_specs=[_ANY] * n_out,
        out_shape=plan.out_shape(), scratch_shapes=plan.scratch())(*plan.operands())


def _perm(a, tm):
    T, C = a.shape
    return a.reshape(T // tm, 8, tm // 8, C).transpose(0, 2, 1, 3).reshape(T, C)


def _unperm(a, tm):
    T, C = a.shape
    return a.reshape(T // tm, tm // 8, 8, C).transpose(0, 2, 1, 3).reshape(T, C)


def _wrap_prev(prev_z, z):
    n = z.shape[0]
    sub = lax.broadcasted_iota(jnp.int32, z.shape, 0) & 7
    return jnp.where(sub == 0, pltpu.roll(prev_z, n - 7, 0), pltpu.roll(z, 1, 0))


def _wrap_next(next_z, z):
    n = z.shape[0]
    sub = lax.broadcasted_iota(jnp.int32, z.shape, 0) & 7
    return jnp.where(sub == 7, pltpu.roll(next_z, 7, 0), pltpu.roll(z, n - 1, 0))


def _fill_causal(buf, halo, x, tm):
    nh = halo.shape[0]
    buf[nh:nh + tm, :] = x
    z = buf[tm:tm + nh, :]
    buf[0:nh, :] = _wrap_prev(halo[...], z)
    halo[...] = z


def _fill_anticausal(buf, halo, dy, tm):
    nh = halo.shape[0]
    buf[0:tm, :] = dy
    z = buf[0:nh, :]
    buf[tm:tm + nh, :] = _wrap_next(halo[...], z)
    halo[...] = z


def _scan_fwd(a, u, abuf, ubuf, hcar, tm):
    D = a.shape[1]
    G = tm // 8
    abuf[...] = a
    ubuf[...] = u

    def step(j, c):
        h, pr = c
        r = pl.multiple_of(j * 8, 8)
        aj = abuf[pl.ds(r, 8), :]
        h = aj * h + ubuf[pl.ds(r, 8), :]
        pr = aj * pr
        ubuf[pl.ds(r, 8), :] = h
        abuf[pl.ds(r, 8), :] = pr
        return h, pr

    sub8 = lax.broadcasted_iota(jnp.int32, (8, D), 0)
    e, q = lax.fori_loop(1, G, step, (ubuf[0:8, :], abuf[0:8, :]))
    for s in (1, 2, 4):
        e = e + q * _shift_down(e, s, 0.0, sub8)
        q = q * _shift_down(q, s, 1.0, sub8)
    e = e + q * hcar[...]
    cin = jnp.where(sub8 == 0, hcar[...], pltpu.roll(e, 1, 0))
    return ubuf[...] + abuf[...] * jnp.tile(cin, (G, 1))


def _scan_bwd(a, d, abuf, gbuf, gcar, tmp8, tm):
    D = a.shape[1]
    G = tm // 8
    abuf[...] = a
    gbuf[...] = d

    def step(k, c):
        g_next, a_next, r_j = c
        r = pl.multiple_of((G - 1 - k) * 8, 8)
        aj = abuf[pl.ds(r, 8), :]
        g = gbuf[pl.ds(r, 8), :] + a_next * g_next
        gbuf[pl.ds(r, 8), :] = g
        abuf[pl.ds(r, 8), :] = r_j
        return g, aj, aj * r_j

    last = 8 * (G - 1)
    a_last = abuf[last:last + 8, :]
    abuf[last:last + 8, :] = jnp.ones((8, D), F32)
    g0, a0, _ = lax.fori_loop(1, G, step, (gbuf[last:last + 8, :], a_last, a_last))
    r0 = abuf[0:8, :]
    sub8 = lax.broadcasted_iota(jnp.int32, (8, D), 0)
    x, q = a0 * g0, a0 * r0
    for s in (1, 2, 4):
        x = x + q * _shift_up(x, s, 0.0, sub8)
        q = q * _shift_up(q, s, 1.0, sub8)
    x = x + q * gcar[...]
    cin = jnp.where(sub8 == 7, gcar[...], pltpu.roll(x, 7, 0))
    g = gbuf[...] + abuf[...] * jnp.tile(cin, (G, 1))
    tmp8[...] = x
    gcar[...] = tmp8[0:1, :]
    return g


def _mixers_fwd(h, p, win, l, name, plan=None):
    T, D = h.shape
    D3 = win.shape[1]
    Dc = D // 2
    heads, hd = p['lru_wa'].shape[1], p['lru_wa'].shape[2]
    tm = _TM_SEQ
    kl, kc = p['lru_conv_w'].shape[1], p['conv_w'].shape[1]
    nhl, nhc = 8 * (kl - 1), 8 * (kc - 1)
    assert nhc <= tm

    def body(h_ref, gz_ref, wl_ref, bl_ref, wa_ref, ba_ref, wx_ref, bx_ref, lam_ref, wc_ref, bc_ref, lg_ref, lb_ref,
             win_hbm, z_ref, xc_ref, hs_ref, ya_ref, cc_ref, yb_ref,
             xbuf, xhalo, hcar, cbuf, chalo, abuf, ubuf, proj_ref, win_ref, wsem):
        _load_weights(pl.program_id(0), [(win_hbm, win_ref)], wsem)

        @pl.when(pl.program_id(0) == 0)
        def _():
            xhalo[...] = jnp.zeros_like(xhalo)
            chalo[...] = jnp.zeros_like(chalo)
            hcar[...] = jnp.zeros_like(hcar)

        z = _rms_fwd(h_ref[...], gz_ref[...]).astype(_MXU)
        z_ref[...] = z
        _up_into(proj_ref, z, win_ref, 0, tm, _pick(D3, _NB))

        _fill_causal(xbuf, xhalo, proj_ref[:, 0:D], tm)
        xc = bl_ref[...] + wl_ref[0:1, :] * xbuf[0:tm, :]
        for k in range(1, kl):
            xc = xc + wl_ref[k:k + 1, :] * xbuf[8 * k:8 * k + tm, :]
        xc_ref[...] = xc
        _, ra, ri, sp, a, mult = _gates(xc, wa_ref, ba_ref[...], wx_ref, bx_ref[...], lam_ref[...], heads, hd)
        hs = _scan_fwd(a, mult * (ri * xc), abuf, ubuf, hcar, tm)
        hs_ref[...] = hs
        hcar[...] = hs_ref[pl.ds(tm - 1, 1), :]
        gg, _ = _gelu(proj_ref[:, D:2 * D])
        ya_ref[...] = hs * gg

        _fill_causal(cbuf, chalo,
                     proj_ref[:, 2 * D:2 * D + Dc] * jax.nn.sigmoid(proj_ref[:, 2 * D + Dc:3 * D]), tm)
        cc = bc_ref[...] + wc_ref[0:1, :] * cbuf[0:tm, :]
        for k in range(1, kc):
            cc = cc + wc_ref[k:k + 1, :] * cbuf[8 * k:8 * k + tm, :]
        cc_ref[...] = cc
        nrm, _ = _group_norm(cc, CONV_GROUPS)
        cl = nrm * lg_ref[...] + lb_ref[...]
        yb_ref[...] = cl * jax.nn.sigmoid(cl)

    row_d = pl.BlockSpec((tm, D), lambda i: (i, 0))
    row_c = pl.BlockSpec((tm, Dc), lambda i: (i, 0))
    return _ride(
        body, plan, name, T // tm,
        in_specs=[row_d, _lspec(l, 1, D),
                  _lspec(l, kl, D), _lspec(l, 1, D), _lspec(l, heads, hd, hd), _lspec(l, 1, D),
                  _lspec(l, heads, hd, hd), _lspec(l, 1, D), _lspec(l, 1, D),
                  _lspec(l, kc, Dc), _lspec(l, 1, Dc), _lspec(l, 1, Dc), _lspec(l, 1, Dc), _ANY],
        args=(h, p['g_pre_mix'], p['lru_conv_w'], p['lru_conv_b'], p['lru_wa'], p['lru_ba'], p['lru_wx'], p['lru_bx'],
              p['lru_lambda'], p['conv_w'], p['conv_b'], p['conv_ln_g'], p['conv_ln_b'], win),
        out_specs=[row_d, row_d, row_d, row_d, row_c, row_c],
        out_shape=[_sds((T, D), _MXU), _sds((T, D), F32), _sds((T, D), F32), _sds((T, D), F32),
                   _sds((T, Dc), F32), _sds((T, Dc), F32)],
        scratch_shapes=[pltpu.VMEM((nhl + tm, D), F32), pltpu.VMEM((nhl, D), F32), pltpu.VMEM((1, D), F32),
                        pltpu.VMEM((nhc + tm, Dc), F32), pltpu.VMEM((nhc, Dc), F32),
                        pltpu.VMEM((tm, D), F32), pltpu.VMEM((tm, D), F32),
                        pltpu.VMEM((tm, D3), F32), pltpu.VMEM((D, D3), _MXU), pltpu.SemaphoreType.DMA((1,))])


def _mix_out(ya, yb, h, p, wout, l, name):
    T, D = ya.shape
    Dc = yb.shape[1]
    tm = _TM_SEQ

    def body(ya_ref, yb_ref, h_ref, gl_ref, gc_ref, gp_ref, w_ref, y_ref, o_ref, hm_ref):
        y = jnp.concatenate([_rms_fwd(ya_ref[...], gl_ref[...]), _rms_fwd(yb_ref[...], gc_ref[...])],
                            axis=-1).astype(_MXU)
        y_ref[...] = y
        o = jnp.dot(y, w_ref[...], preferred_element_type=F32)
        o_ref[...] = o
        hm_ref[...] = h_ref[...] + _rms_fwd(o, gp_ref[...])

    row_d = pl.BlockSpec((tm, D), lambda i: (i, 0))
    return pl.pallas_call(
        body, name=name, grid=(T // tm,),
        in_specs=[row_d, pl.BlockSpec((tm, Dc), lambda i: (i, 0)), row_d,
                  _lspec(l, 1, D), _lspec(l, 1, Dc), _lspec(l, 1, D), pl.BlockSpec((D + Dc, D), lambda i: (0, 0))],
        out_specs=[pl.BlockSpec((tm, D + Dc), lambda i: (i, 0)), row_d, row_d],
        out_shape=[_sds((T, D + Dc), _MXU), _sds((T, D), F32), _sds((T, D), F32)],
        compiler_params=_params(),
    )(ya, yb, h, p['g_out_lru'], p['g_out_conv'], p['g_post_mix'], wout)


def _load_weights(i, pairs, sems):
    @pl.when(i == 0)
    def _():
        copies = [pltpu.make_async_copy(src, dst, sems.at[k]) for k, (src, dst) in enumerate(pairs)]
        for cp in copies:
            cp.start()
        for cp in copies:
            cp.wait()


def _up_into(ubuf, z, wup_v, nh, tm, cw):
    for c in range(wup_v.shape[1] // cw):
        cs = slice(c * cw, (c + 1) * cw)
        ubuf[nh:nh + tm, cs] = jnp.dot(z, wup_v[:, cs], preferred_element_type=F32)


def _ffn_fwd(hmid, p, wup, wdown, l, name, plan=None):
    T, D = hmid.shape
    F2 = wup.shape[1]
    Fh = F2 // 2
    tm = _TM_SEQ
    cw = _pick(Fh, _NB)
    kf = p['ffn_conv_w'].shape[1]
    nh = 8 * (kf - 1)

    def body(hm_ref, gz_ref, wf_ref, bf_ref, g_ref, wu_hbm, wd_hbm, z_ref, a2_ref, f_ref, ho_ref,
             ubuf, uhalo, wu_ref, wd_ref, wsem):
        i = pl.program_id(0)
        _load_weights(i, [(wu_hbm, wu_ref), (wd_hbm, wd_ref)], wsem)

        @pl.when(i == 0)
        def _():
            uhalo[...] = jnp.zeros_like(uhalo)

        z = _rms_fwd(hm_ref[...], gz_ref[...]).astype(_MXU)
        z_ref[...] = z
        _up_into(ubuf, z, wu_ref, nh, tm, cw)
        tail = ubuf[tm:tm + nh, :]
        ubuf[0:nh, :] = _wrap_prev(uhalo[...], tail)
        uhalo[...] = tail

        def conv(cs):
            acc = bf_ref[:, cs]
            for k in range(kf):
                acc = acc + wf_ref[k:k + 1, cs] * ubuf[8 * k:8 * k + tm, cs]
            return acc

        f = None
        for c in range(Fh // cw):
            gs = slice(c * cw, (c + 1) * cw)
            gg, _ = _gelu(conv(gs))
            a2 = (gg * conv(slice(Fh + c * cw, Fh + (c + 1) * cw))).astype(_MXU)
            a2_ref[:, gs] = a2
            part = jnp.dot(a2, wd_ref[gs, :], preferred_element_type=F32)
            f = part if f is None else f + part
        f_ref[...] = f
        ho_ref[...] = hm_ref[...] + _rms_fwd(f, g_ref[...])

    row_d = pl.BlockSpec((tm, D), lambda i: (i, 0))
    return _ride(
        body, plan, name, T // tm,
        in_specs=[row_d, _lspec(l, 1, D), _lspec(l, kf, F2), _lspec(l, 1, F2), _lspec(l, 1, D), _ANY, _ANY],
        args=(hmid, p['g_pre_ffn'], p['ffn_conv_w'], p['ffn_conv_b'], p['g_post_ffn'], wup, wdown),
        out_specs=[row_d, pl.BlockSpec((tm, Fh), lambda i: (i, 0)), row_d, row_d],
        out_shape=[_sds((T, D), _MXU), _sds((T, Fh), _MXU), _sds((T, D), F32), _sds((T, D), F32)],
        scratch_shapes=[pltpu.VMEM((nh + tm, F2), F32), pltpu.VMEM((nh, F2), F32),
                        pltpu.VMEM((D, F2), _MXU), pltpu.VMEM((Fh, D), _MXU), pltpu.SemaphoreType.DMA((2,))])


def _loss_head(h, tgt, n_real, name):
    T, D = h.shape
    tm = _TM_SEQ

    def body(h_ref, t_ref, loss_ref, dh_ref):
        i = pl.program_id(0)

        @pl.when(i == 0)
        def _():
            loss_ref[...] = jnp.zeros_like(loss_ref)

        pos = lax.broadcasted_iota(jnp.int32, (tm, D), 0)
        row = i * tm + (pos & 7) * (tm // 8) + (pos >> 3)
        e = jnp.where((row >= N_META) & (row < N_META + n_real), h_ref[...] - t_ref[...], 0.0)
        dh_ref[...] = e * (1.0 / D)
        loss_ref[...] += 0.5 * jnp.sum(jnp.mean(e * e, axis=-1, keepdims=True), axis=0, keepdims=True)

    row_d = pl.BlockSpec((tm, D), lambda i: (i, 0))
    return pl.pallas_call(
        body, name=name, grid=(T // tm,),
        in_specs=[row_d, row_d],
        out_specs=[pl.BlockSpec((8, _LANES), lambda i: (0, 0)), row_d],
        out_shape=[_sds((8, _LANES), F32), _sds((T, D), F32)],
        compiler_params=_params())(h, tgt)


def _ffn_bwd(dh, f, z2, p, wup, wdown, l, name, plan=None):
    T, D = dh.shape
    F2 = wup.shape[1]
    Fh = F2 // 2
    tm = _TM_SEQ
    nt = T // tm
    cw = _pick(Fh, _NB)
    kf = p['ffn_conv_w'].shape[1]
    nh = 8 * (kf - 1)
    assert tm % nh == 0

    def body(dh_ref, f_ref, z_ref, zp_ref, wf_ref, bf_ref, g_ref, wu_hbm, wd_hbm,
             df_ref, dup_ref, dwf_ref, dbf_ref, dg_ref, ubuf, dbuf, dhalo, wu_ref, wd_ref, wsem):
        i = pl.program_id(0)
        r = nt - 1 - i
        _load_weights(i, [(wu_hbm, wu_ref), (wd_hbm, wd_ref)], wsem)

        @pl.when(i == 0)
        def _():
            dhalo[...] = jnp.zeros_like(dhalo)
            dwf_ref[...] = jnp.zeros_like(dwf_ref)
            dbf_ref[...] = jnp.zeros_like(dbf_ref)
            dg_ref[...] = jnp.zeros_like(dg_ref)

        df, dg = _rms_bwd(f_ref[...], g_ref[...], dh_ref[...])
        dg_ref[...] += dg
        dfb = df.astype(_MXU)
        df_ref[...] = dfb
        zb, zpb = z_ref[...], zp_ref[...]

        def conv(cs):
            ubuf[nh:nh + tm, cs] = jnp.dot(zb, wu_ref[:, cs], preferred_element_type=F32)
            up_prev = jnp.dot(zpb, wu_ref[:, cs], preferred_element_type=F32)
            ubuf[0:nh, cs] = _wrap_prev(jnp.where(r == 0, 0.0, up_prev), ubuf[tm:tm + nh, cs])
            acc = bf_ref[:, cs]
            for k in range(kf):
                acc = acc + wf_ref[k:k + 1, cs] * ubuf[8 * k:8 * k + tm, cs]
            return acc

        for c in range(Fh // cw):
            gs = slice(c * cw, (c + 1) * cw)
            us = slice(Fh + c * cw, Fh + (c + 1) * cw)
            ug = conv(gs)
            gg, t = _gelu(ug)
            da2 = lax.dot_general(dfb, wd_ref[gs, :], (((1,), (1,)), ((), ())), preferred_element_type=F32)
            dbuf[0:tm, gs] = da2 * conv(us) * _gelu_grad(ug, t)
            dbuf[0:tm, us] = da2 * gg
        z = dbuf[0:nh, :]
        dbuf[tm:tm + nh, :] = _wrap_next(dhalo[...], z)
        dhalo[...] = z
        for c in range(F2 // cw):
            cs = slice(c * cw, (c + 1) * cw)
            upc = ubuf[nh:nh + tm, cs]
            dup = None
            for k in range(kf):
                dsh = dbuf[8 * (kf - 1 - k):8 * (kf - 1 - k) + tm, cs]
                term = wf_ref[k:k + 1, cs] * dsh
                dup = term if dup is None else dup + term
                dwf_ref[k:k + 1, cs] += jnp.sum(dsh * upc, axis=0, keepdims=True)
            dbf_ref[:, cs] += jnp.sum(dbuf[0:tm, cs], axis=0, keepdims=True)
            dup_ref[:, cs] = dup.astype(_MXU)

    rev_d = pl.BlockSpec((tm, D), lambda i: (nt - 1 - i, 0))
    rev_f = pl.BlockSpec((tm, F2), lambda i: (nt - 1 - i, 0))
    prev = pl.BlockSpec((nh, D), lambda i: (jnp.maximum((nt - 1 - i) * (tm // nh) - 1, 0), 0))
    full = lambda *s: pl.BlockSpec(s, lambda i: (0,) * len(s))
    return _ride(
        body, plan, name, nt,
        in_specs=[rev_d, rev_d, rev_d, prev, _lspec(l, kf, F2), _lspec(l, 1, F2), _lspec(l, 1, D), _ANY, _ANY],
        args=(dh, f, z2, z2, p['ffn_conv_w'], p['ffn_conv_b'], p['g_post_ffn'], wup, wdown),
        out_specs=[rev_d, rev_f, full(kf, F2), full(1, F2), full(1, D)],
        out_shape=[_sds((T, D), _MXU), _sds((T, F2), _MXU), _sds((kf, F2), F32), _sds((1, F2), F32), _sds((1, D), F32)],
        scratch_shapes=[pltpu.VMEM((nh + tm, F2), F32), pltpu.VMEM((tm + nh, F2), F32), pltpu.VMEM((nh, F2), F32),
                        pltpu.VMEM((D, F2), _MXU), pltpu.VMEM((Fh, D), _MXU), pltpu.SemaphoreType.DMA((2,))])


def _mm_nt_norm_bwd(dy, w, x, g3, dres, l, name, plan=None):
    T, N = dy.shape
    D = x.shape[1]
    tm, nb = _TM_MM, _pick(N, _NB_DX)
    nj = N // nb

    def body(dy_ref, w_ref, x_ref, g_ref, dr_ref, dh_ref, dg_ref, acc):
        i, j = pl.program_id(0), pl.program_id(1)

        @pl.when((i == 0) & (j == 0))
        def _():
            dg_ref[...] = jnp.zeros_like(dg_ref)

        part = lax.dot_general(dy_ref[...], w_ref[...], (((1,), (1,)), ((), ())), preferred_element_type=F32)

        @pl.when(j == 0)
        def _():
            acc[...] = part

        @pl.when(j > 0)
        def _():
            acc[...] += part

        @pl.when(j == nj - 1)
        def _():
            dx, dg = _rms_bwd(x_ref[...], g_ref[...], acc[...])
            dh_ref[...] = dr_ref[...] + dx
            dg_ref[...] += dg

    row_d = pl.BlockSpec((tm, D), lambda i, j: (i, 0))
    return _ride(
        body, plan, name, (T // tm, nj),
        in_specs=[pl.BlockSpec((tm, nb), lambda i, j: (i, j)), pl.BlockSpec((D, nb), lambda i, j: (0, j)),
                  row_d, _lspec(l, 1, D), row_d],
        args=(dy, w, x, g3, dres),
        out_specs=[row_d, pl.BlockSpec((1, D), lambda i, j: (0, 0))],
        out_shape=[_sds((T, D), F32), _sds((1, D), F32)],
        scratch_shapes=[pltpu.VMEM((tm, D), F32)])


def _mm_tn(xs, dy, name):
    T, K = xs.shape
    N = dy.shape[1]
    tm = _TM_DW
    kb = K if K <= 1024 else _pick(K, _NB)
    nb = _pick(N, _NB)

    def body(x_ref, dy_ref, o_ref):
        @pl.when(pl.program_id(2) == 0)
        def _():
            o_ref[...] = jnp.zeros_like(o_ref)

        o_ref[...] += lax.dot_general(x_ref[...], dy_ref[...], (((0,), (0,)), ((), ())), preferred_element_type=F32)

    return pl.pallas_call(
        body, name=name, grid=(K // kb, N // nb, T // tm),
        in_specs=[pl.BlockSpec((tm, kb), lambda a, b, t: (t, a)), pl.BlockSpec((tm, nb), lambda a, b, t: (t, b))],
        out_specs=pl.BlockSpec((kb, nb), lambda a, b, t: (a, b)),
        out_shape=_sds((K, N), F32),
        compiler_params=_params())(xs, dy)


def _mix_bwd(dhm, o, ya, yb, p, wout, l, name):
    T, D = ya.shape
    Dc = yb.shape[1]
    tm = _TM_SEQ

    def body(dh_ref, o_ref, ya_ref, yb_ref, gp_ref, gl_ref, gc_ref, w_ref,
             do_ref, dya_ref, dyb_ref, dgp_ref, dgl_ref, dgc_ref):
        @pl.when(pl.program_id(0) == 0)
        def _():
            dgp_ref[...] = jnp.zeros_like(dgp_ref)
            dgl_ref[...] = jnp.zeros_like(dgl_ref)
            dgc_ref[...] = jnp.zeros_like(dgc_ref)

        do, dgp = _rms_bwd(o_ref[...], gp_ref[...], dh_ref[...])
        dgp_ref[...] += dgp
        dob = do.astype(_MXU)
        do_ref[...] = dob
        dy = lax.dot_general(dob, w_ref[...], (((1,), (1,)), ((), ())), preferred_element_type=F32)
        dya, dgl = _rms_bwd(ya_ref[...], gl_ref[...], dy[:, 0:D])
        dyb, dgc = _rms_bwd(yb_ref[...], gc_ref[...], dy[:, D:D + Dc])
        dya_ref[...] = dya
        dyb_ref[...] = dyb
        dgl_ref[...] += dgl
        dgc_ref[...] += dgc

    row_d = pl.BlockSpec((tm, D), lambda i: (i, 0))
    row_c = pl.BlockSpec((tm, Dc), lambda i: (i, 0))
    full = lambda *s: pl.BlockSpec(s, lambda i: (0,) * len(s))
    return pl.pallas_call(
        body, name=name, grid=(T // tm,),
        in_specs=[row_d, row_d, row_d, row_c, _lspec(l, 1, D), _lspec(l, 1, D), _lspec(l, 1, Dc),
                  pl.BlockSpec((D + Dc, D), lambda i: (0, 0))],
        out_specs=[row_d, row_d, row_c, full(1, D), full(1, D), full(1, Dc)],
        out_shape=[_sds((T, D), _MXU), _sds((T, D), F32), _sds((T, Dc), F32),
                   _sds((1, D), F32), _sds((1, D), F32), _sds((1, Dc), F32)],
        compiler_params=_params(),
    )(dhm, o, ya, yb, p['g_post_mix'], p['g_out_lru'], p['g_out_conv'], wout)


def _mixers_bwd(dya, dyb, z1, xc, hs, cc, p, win, l, name, plan=None):
    T, D = z1.shape
    D3 = win.shape[1]
    Dc = D // 2
    heads, hd = p['lru_wa'].shape[1], p['lru_wa'].shape[2]
    tm = _TM_SEQ
    nt = T // tm
    per8 = tm // 8
    kl, kc = p['lru_conv_w'].shape[1], p['conv_w'].shape[1]
    nhl, nhc = 8 * (kl - 1), 8 * (kc - 1)
    assert nhc <= tm

    def body(dya_ref, dyb_ref, z_ref, xc_ref, hs_ref, hsp_ref, cc_ref,
             wl_ref, wa_ref, ba_ref, wx_ref, bx_ref, lam_ref, wc_ref, lg_ref, lb_ref, win_hbm,
             dproj_ref, dwl_ref, dbl_ref, dwa_ref, dba_ref, dwx_ref, dbx_ref, dlam_ref,
             dwc_ref, dbc_ref, dlg_ref, dlb_ref, gcar, dxbuf, dxhalo, dcbuf, dchalo, abuf, gbuf, hbuf, tmp8,
             proj_ref, win_ref, wsem):
        i = pl.program_id(0)
        r = nt - 1 - i
        _load_weights(i, [(win_hbm, win_ref)], wsem)

        @pl.when(i == 0)
        def _():
            for ref in (gcar, dxhalo, dchalo, dwl_ref, dbl_ref, dwa_ref, dba_ref, dwx_ref, dbx_ref, dlam_ref,
                        dwc_ref, dbc_ref, dlg_ref, dlb_ref):
                ref[...] = jnp.zeros_like(ref)

        _up_into(proj_ref, z_ref[...], win_ref, 0, tm, _pick(D3, _NB))

        dya_v = dya_ref[...]
        hs = hs_ref[...]
        gl = proj_ref[:, D:2 * D]
        gg, tg = _gelu(gl)
        dproj_ref[:, D:2 * D] = (dya_v * hs * _gelu_grad(gl, tg)).astype(_MXU)
        dhs = dya_v * gg
        xc = xc_ref[...]
        lam = lam_ref[...]
        xcb, ra, ri, sp, a, mult = _gates(xc, wa_ref, ba_ref[...], wx_ref, bx_ref[...], lam, heads, hd)
        g = _scan_bwd(a, dhs, abuf, gbuf, gcar, tmp8, tm)
        sub8 = lax.broadcasted_iota(jnp.int32, (8, D), 0)
        hbuf[8:8 + tm, :] = hs
        hbuf[0:8, :] = jnp.where(sub8 == 0, jnp.where(r == 0, 0.0, hsp_ref[7:8, :]),
                                 pltpu.roll(hs_ref[tm - 8:tm, :], 1, 0))
        da = g * hbuf[0:tm, :]
        gx = g * xc
        dxc = g * mult * ri
        dla = da * a - (gx * ri) * (a * a) / mult
        dlam_ref[...] += jnp.sum(dla * ra, axis=0, keepdims=True) * (LRU_C * jax.nn.sigmoid(-lam))
        dpa = (dla * ((-LRU_C) * sp)) * ra * (1.0 - ra)
        dpx = (gx * mult) * ri * (1.0 - ri)
        dba_ref[...] += jnp.sum(dpa, axis=0, keepdims=True)
        dbx_ref[...] += jnp.sum(dpx, axis=0, keepdims=True)
        dpab, dpxb = dpa.astype(_MXU), dpx.astype(_MXU)
        for h in range(heads):
            hsl = slice(h * hd, (h + 1) * hd)
            dwa_ref[h] += lax.dot_general(xcb[:, hsl], dpab[:, hsl], (((0,), (0,)), ((), ())),
                                          preferred_element_type=F32)
            dwx_ref[h] += lax.dot_general(xcb[:, hsl], dpxb[:, hsl], (((0,), (0,)), ((), ())),
                                          preferred_element_type=F32)
        dxc = dxc + _bd_mm_t(dpab, wa_ref, heads, hd) + _bd_mm_t(dpxb, wx_ref, heads, hd)
        dbl_ref[...] += jnp.sum(dxc, axis=0, keepdims=True)
        _fill_anticausal(dxbuf, dxhalo, dxc, tm)
        xl = proj_ref[:, 0:D]
        dxl = None
        for k in range(kl):
            dsh = dxbuf[8 * (kl - 1 - k):8 * (kl - 1 - k) + tm, :]
            term = wl_ref[k:k + 1, :] * dsh
            dxl = term if dxl is None else dxl + term
            dwl_ref[k:k + 1, :] += jnp.sum(dsh * xl, axis=0, keepdims=True)
        dproj_ref[:, 0:D] = dxl.astype(_MXU)

        ca = proj_ref[:, 2 * D:2 * D + Dc]
        sg = jax.nn.sigmoid(proj_ref[:, 2 * D + Dc:3 * D])
        cg = ca * sg
        nrm, rss = _group_norm(cc_ref[...], CONV_GROUPS)
        lg = lg_ref[...]
        cl = nrm * lg + lb_ref[...]
        sc = jax.nn.sigmoid(cl)
        dcl = dyb_ref[...] * (sc * (1.0 + cl * (1.0 - sc)))
        dlg_ref[...] += jnp.sum(dcl * nrm, axis=0, keepdims=True)
        dlb_ref[...] += jnp.sum(dcl, axis=0, keepdims=True)
        dnrm = dcl * lg
        gsz = Dc // CONV_GROUPS
        parts = []
        for gi in range(CONV_GROUPS):
            sl = slice(gi * gsz, (gi + 1) * gsz)
            dn, nn = dnrm[:, sl], nrm[:, sl]
            parts.append(rss[gi] * (dn - jnp.mean(dn, axis=-1, keepdims=True)
                                    - nn * jnp.mean(dn * nn, axis=-1, keepdims=True)))
        dcc = jnp.concatenate(parts, axis=-1)
        dbc_ref[...] += jnp.sum(dcc, axis=0, keepdims=True)
        _fill_anticausal(dcbuf, dchalo, dcc, tm)
        dcg = None
        for k in range(kc):
            dsh = dcbuf[8 * (kc - 1 - k):8 * (kc - 1 - k) + tm, :]
            term = wc_ref[k:k + 1, :] * dsh
            dcg = term if dcg is None else dcg + term
            dwc_ref[k:k + 1, :] += jnp.sum(dsh * cg, axis=0, keepdims=True)
        dproj_ref[:, 2 * D:2 * D + Dc] = (dcg * sg).astype(_MXU)
        dproj_ref[:, 2 * D + Dc:3 * D] = (dcg * ca * sg * (1.0 - sg)).astype(_MXU)

    rev_d = pl.BlockSpec((tm, D), lambda i: (nt - 1 - i, 0))
    rev_c = pl.BlockSpec((tm, Dc), lambda i: (nt - 1 - i, 0))
    rev_p = pl.BlockSpec((tm, D3), lambda i: (nt - 1 - i, 0))
    prev8 = pl.BlockSpec((8, D), lambda i: (jnp.maximum((nt - 1 - i) * per8 - 1, 0), 0))
    full = lambda *s: pl.BlockSpec(s, lambda i: (0,) * len(s))
    return _ride(
        body, plan, name, nt,
        in_specs=[rev_d, rev_c, rev_d, rev_d, rev_d, prev8, rev_c,
                  _lspec(l, kl, D), _lspec(l, heads, hd, hd), _lspec(l, 1, D), _lspec(l, heads, hd, hd),
                  _lspec(l, 1, D), _lspec(l, 1, D), _lspec(l, kc, Dc), _lspec(l, 1, Dc), _lspec(l, 1, Dc), _ANY],
        args=(dya, dyb, z1, xc, hs, hs, cc, p['lru_conv_w'], p['lru_wa'], p['lru_ba'], p['lru_wx'], p['lru_bx'],
              p['lru_lambda'], p['conv_w'], p['conv_ln_g'], p['conv_ln_b'], win),
        out_specs=[rev_p, full(kl, D), full(1, D), full(heads, hd, hd), full(1, D), full(heads, hd, hd), full(1, D),
                   full(1, D), full(kc, Dc), full(1, Dc), full(1, Dc), full(1, Dc)],
        out_shape=[_sds((T, D3), _MXU), _sds((kl, D), F32), _sds((1, D), F32), _sds((heads, hd, hd), F32),
                   _sds((1, D), F32), _sds((heads, hd, hd), F32), _sds((1, D), F32), _sds((1, D), F32),
                   _sds((kc, Dc), F32), _sds((1, Dc), F32), _sds((1, Dc), F32), _sds((1, Dc), F32)],
        scratch_shapes=[pltpu.VMEM((1, D), F32), pltpu.VMEM((tm + nhl, D), F32), pltpu.VMEM((nhl, D), F32),
                        pltpu.VMEM((tm + nhc, Dc), F32), pltpu.VMEM((nhc, Dc), F32),
                        pltpu.VMEM((tm, D), F32), pltpu.VMEM((tm, D), F32), pltpu.VMEM((8 + tm, D), F32),
                        pltpu.VMEM((8, D), F32),
                        pltpu.VMEM((tm, D3), F32), pltpu.VMEM((D, D3), _MXU), pltpu.SemaphoreType.DMA((1,))])


def _sum_sources(recv, name):
    _, R, C = recv.shape
    rb = _pick_rows(R, 1024)

    def body(r_ref, o_ref):
        acc = r_ref[0]
        for s in range(1, _N_DEV):
            acc = acc + r_ref[s]
        o_ref[...] = acc

    return pl.pallas_call(
        body, name=name, grid=(R // rb,),
        in_specs=[pl.BlockSpec((_N_DEV, rb, C), lambda i: (0, i, 0))],
        out_specs=pl.BlockSpec((rb, C), lambda i: (i, 0)),
        out_shape=_sds((R, C), F32), compiler_params=_params())(recv)


def _adamw(g, w, m, v, name):
    R, C = w.shape
    summed = g.ndim == 3
    rb = _pick_rows(R, max(8, min(512, _ADAM_BLOCK_ELEMS // C)))
    c1 = 1.0 - ADAM_B1 ** ADAM_STEP
    c2 = 1.0 - ADAM_B2 ** ADAM_STEP

    def body(g_ref, w_ref, m_ref, v_ref, go_ref, d_ref, mo_ref, vo_ref):
        if summed:
            gv = g_ref[0]
            for s in range(1, _N_DEV):
                gv = gv + g_ref[s]
        else:
            gv = g_ref[...]
        go_ref[...] = gv
        mn = ADAM_B1 * m_ref[...] + (1.0 - ADAM_B1) * gv
        vn = ADAM_B2 * v_ref[...] + (1.0 - ADAM_B2) * (gv * gv)
        mo_ref[...] = mn
        vo_ref[...] = vn
        d_ref[...] = (-ADAM_LR) * ((mn / c1) / (jnp.sqrt(vn / c2) + ADAM_EPS) + ADAM_WD * w_ref[...])

    blk = pl.BlockSpec((rb, C), lambda i: (i, 0))
    gspec = pl.BlockSpec((_N_DEV, rb, C), lambda i: (0, i, 0)) if summed else blk
    return pl.pallas_call(
        body, name=name, grid=(R // rb,),
        in_specs=[gspec, blk, blk, blk], out_specs=[blk, blk, blk, blk],
        out_shape=[_sds((R, C), F32)] * 4, compiler_params=_params())(g, w, m, v)


def _adamw_layers(recvs, w, m, v, name):
    L, R, C = w.shape
    rb = _pick_rows(R, max(8, _ADAM_BLOCK_ELEMS // (4 * C)))
    c1 = 1.0 - ADAM_B1 ** ADAM_STEP
    c2 = 1.0 - ADAM_B2 ** ADAM_STEP

    def body(*refs):
        r_refs = refs[:L]
        w_ref, m_ref, v_ref, go_ref, d_ref, mo_ref, vo_ref = refs[L:]
        for l in range(L):
            gv = r_refs[l][0]
            for s in range(1, _N_DEV):
                gv = gv + r_refs[l][s]
            go_ref[l] = gv
            mn = ADAM_B1 * m_ref[l] + (1.0 - ADAM_B1) * gv
            vn = ADAM_B2 * v_ref[l] + (1.0 - ADAM_B2) * (gv * gv)
            mo_ref[l] = mn
            vo_ref[l] = vn
            d_ref[l] = (-ADAM_LR) * ((mn / c1) / (jnp.sqrt(vn / c2) + ADAM_EPS) + ADAM_WD * w_ref[l])

    blk = pl.BlockSpec((L, rb, C), lambda i: (0, i, 0))
    return pl.pallas_call(
        body, name=name, grid=(R // rb,),
        in_specs=[pl.BlockSpec((_N_DEV, rb, C), lambda i: (0, i, 0))] * L + [blk, blk, blk],
        out_specs=[blk, blk, blk, blk],
        out_shape=[_sds((L, R, C), F32)] * 4, compiler_params=_params())(*recvs, w, m, v)


def _pack_rows(flat_parts, dtype, row_mult):
    flat = jnp.concatenate([f.reshape(-1).astype(dtype) for f in flat_parts])
    n = flat.shape[0]
    per = _LANES * row_mult
    padded = -(-n // per) * per
    if padded != n:
        flat = jnp.concatenate([flat, jnp.zeros((padded - n,), dtype)])
    return flat.reshape(-1, _LANES)


def _unpack(flat, shapes):
    out, off = [], 0
    for s in shapes:
        n = math.prod(s)
        out.append(flat[off:off + n].reshape(s))
        off += n
    return out


def _to_pieces(full):
    n = full.shape[-1] // _N_DEV
    t = full.reshape(full.shape[:-1] + (_N_DEV, n))
    return jnp.moveaxis(t, -2, 0).reshape(_N_DEV, -1)


def _from_gathered(seg, shard_shape, axis):
    t = seg.reshape((_N_DEV,) + tuple(shard_shape))
    t = jnp.moveaxis(t, 0, axis)
    shape = list(shard_shape)
    shape[axis] *= _N_DEV
    return t.reshape(shape)


def kernel(x, meta_tokens, g_pre_mix, w_in, lru_conv_w, lru_conv_b, lru_wa, lru_ba, lru_wx, lru_bx, lru_lambda, conv_w, conv_b, conv_ln_g, conv_ln_b, g_out_lru, g_out_conv, w_out, g_post_mix, g_pre_ffn, w_up, ffn_conv_w, ffn_conv_b, w_down, g_post_ffn, loss_target, m_meta_tokens, m_g_pre_mix, m_w_in, m_lru_conv_w, m_lru_conv_b, m_lru_wa, m_lru_ba, m_lru_wx, m_lru_bx, m_lru_lambda, m_conv_w, m_conv_b, m_conv_ln_g, m_conv_ln_b, m_g_out_lru, m_g_out_conv, m_w_out, m_g_post_mix, m_g_pre_ffn, m_w_up, m_ffn_conv_w, m_ffn_conv_b, m_w_down, m_g_post_ffn, v_meta_tokens, v_g_pre_mix, v_w_in, v_lru_conv_w, v_lru_conv_b, v_lru_wa, v_lru_ba, v_lru_wx, v_lru_bx, v_lru_lambda, v_conv_w, v_conv_b, v_conv_ln_g, v_conv_ln_b, v_g_out_lru, v_g_out_conv, v_w_out, v_g_post_mix, v_g_pre_ffn, v_w_up, v_ffn_conv_w, v_ffn_conv_b, v_w_down, v_g_post_ffn):
    given = dict(locals())
    W = {n: given[n] for n in W_NAMES}
    M = {n: given['m_' + n] for n in W_NAMES}
    V = {n: given['v_' + n] for n in W_NAMES}
    S, D = x.shape[1], x.shape[2]
    L = g_pre_mix.shape[0]
    Dc = D // 2
    step = math.lcm(_TM_MM, _TM_SEQ, _TM_DW)
    T = -(-(N_META + S) // step) * step

    first, rest = ['w_in'], ['w_out', 'w_up', 'w_down']

    def layer_pack(l, names):
        return _pack_rows([W[n][l] for n in names], _MXU, 16)

    def layer_weights(gathered, names):
        segs = _unpack_cols(gathered.reshape(_N_DEV, -1), [W[n].shape[1:] for n in names])
        return {n: _from_gathered(seg, W[n].shape[1:], 1 if n in ('w_in', 'w_up') else 0)
                for n, seg in zip(names, segs)}

    small_pack = _pack_rows([W[n] for n in SMALL_SHARDED], F32, 8)
    big_g, small_g = _run_plan(_GatherPlan([layer_pack(0, first), small_pack]), "gather_weights_first")
    small_segs = _unpack_cols(small_g.reshape(_N_DEV, -1), [W[n].shape for n in SMALL_SHARDED])
    full = {}
    for n, seg in zip(SMALL_SHARDED, small_segs):
        full[n] = _from_gathered(seg, W[n].shape, W[n].ndim - 1)

    p = {}
    for n in W_NAMES:
        if n in BIG or n == 'meta_tokens':
            continue
        a = full[n] if n in full else W[n]
        p[n] = a.reshape(L, 1, a.shape[1]) if a.ndim == 2 else a

    pad_rows = T - N_META - S
    h = _perm(jnp.concatenate([full['meta_tokens'], x[0], jnp.zeros((pad_rows, D), F32)], axis=0), _TM_SEQ)
    tgt = _perm(jnp.concatenate([jnp.zeros((N_META, D), F32), loss_target[0], jnp.zeros((pad_rows, D), F32)],
                                axis=0), _TM_SEQ)

    saved = []
    wl = layer_weights(big_g, first)
    for l in range(L):
        plan = _GatherPlan([layer_pack(0, rest)]) if l == 0 else None
        (z1, xc, hs, ya, cc, yb), nxt = _mixers_fwd(h, p, wl['w_in'], l, f"mixers_fwd_l{l}", plan)
        if plan:
            wl = {**wl, **layer_weights(nxt[0], rest)}
        plan = _GatherPlan([layer_pack(l + 1, BIG)]) if l + 1 < L else None
        y, o, hmid = _mix_out(ya, yb, h, p, wl['w_out'], l, f"mix_out_l{l}")
        (z2, a2, f, hout), nxt = _ffn_fwd(hmid, p, wl['w_up'], wl['w_down'], l, f"ffn_fwd_l{l}", plan)
        saved.append(dict(h=h, z1=z1, xc=xc, hs=hs, ya=ya, cc=cc, yb=yb, y=y, o=o, hmid=hmid,
                          z2=z2, a2=a2, f=f, w=wl))
        h = hout
        if plan:
            wl = layer_weights(nxt[0], BIG)
    loss_tile, dh = _loss_head(h, tgt, S, "loss_head")
    loss = lax.psum(loss_tile[0, 0], ("x", "y", "c"))

    small_g_names = ['g_pre_mix', 'lru_conv_w', 'lru_conv_b', 'lru_wa', 'lru_ba', 'lru_wx', 'lru_bx', 'lru_lambda',
                     'conv_w', 'conv_b', 'conv_ln_g', 'conv_ln_b', 'g_out_lru', 'g_out_conv', 'g_post_mix',
                     'g_pre_ffn', 'ffn_conv_w', 'ffn_conv_b', 'g_post_ffn']
    per_layer = {n: [None] * L for n in small_g_names}
    recv = {n: [None] * L for n in BIG}
    d_in_pending = None
    for l in reversed(range(L)):
        sv = saved[l]
        wl = sv['w']
        plan = _ExchangePlan([d_in_pending], ['cols']) if d_in_pending is not None else None
        (df, dup0, dwf, dbf, dgpf), got = _ffn_bwd(dh, sv['f'], sv['z2'], p, wl['w_up'], wl['w_down'], l,
                                                   f"ffn_bwd_l{l}", plan)
        if plan:
            recv['w_in'][l + 1] = got[0]
        d_down = _mm_tn(sv['a2'], df, f"dw_down_l{l}")
        d_up = _mm_tn(sv['z2'], dup0, f"dw_up_l{l}")
        (dhm, dgpre), got = _mm_nt_norm_bwd(dup0, wl['w_up'], sv['hmid'], p['g_pre_ffn'], dh, l, f"up_bwd_l{l}",
                                            _ExchangePlan([d_down], ['rows']))
        recv['w_down'][l] = got[0]
        do, dya, dyb, dgpm, dgol, dgoc = _mix_bwd(dhm, sv['o'], sv['ya'], sv['yb'], p, wl['w_out'], l,
                                                  f"mix_bwd_l{l}")
        d_out = _mm_tn(sv['y'], do, f"dw_out_l{l}")
        (dproj, dwl, dbl, dwa, dba, dwx, dbx, dlam, dwc, dbc, dlg, dlb), got = _mixers_bwd(
            dya, dyb, sv['z1'], sv['xc'], sv['hs'], sv['cc'], p, wl['w_in'], l, f"mixers_bwd_l{l}",
            _ExchangePlan([d_up], ['cols']))
        recv['w_up'][l] = got[0]
        d_in = _mm_tn(sv['z1'], dproj, f"dw_in_l{l}")
        plan = _ExchangePlan([d_out, d_in], ['rows', 'cols']) if l == 0 else _ExchangePlan([d_out], ['rows'])
        (dh, dgpmix), got = _mm_nt_norm_bwd(dproj, wl['w_in'], sv['h'], p['g_pre_mix'], dhm, l, f"in_bwd_l{l}", plan)
        recv['w_out'][l] = got[0]
        if l == 0:
            recv['w_in'][0] = got[1]
        d_in_pending = d_in
        for n, val in (('g_pre_mix', dgpmix), ('lru_conv_w', dwl), ('lru_conv_b', dbl), ('lru_wa', dwa),
                       ('lru_ba', dba), ('lru_wx', dwx), ('lru_bx', dbx), ('lru_lambda', dlam), ('conv_w', dwc),
                       ('conv_b', dbc), ('conv_ln_g', dlg), ('conv_ln_b', dlb), ('g_out_lru', dgol),
                       ('g_out_conv', dgoc), ('g_post_mix', dgpm), ('g_pre_ffn', dgpre), ('ffn_conv_w', dwf),
                       ('ffn_conv_b', dbf), ('g_post_ffn', dgpf)):
            per_layer[n][l] = val
    dh = _unperm(dh, _TM_SEQ)
    grad_x = dh[N_META:N_META + S][None]
    partial = {n: jnp.stack(per_layer[n]).reshape((L,) + tuple(
        (full[n] if n in full else W[n]).shape[1:])) for n in small_g_names}
    partial['meta_tokens'] = dh[0:N_META]

    shard_pack = jnp.concatenate([_to_pieces(partial[n]) for n in SMALL_SHARDED], axis=1)
    n_sh = shard_pack.shape[1]
    rs = -(-n_sh // (8 * _LANES)) * 8
    shard_pack = jnp.concatenate([shard_pack, jnp.zeros((_N_DEV, rs * _LANES - n_sh), F32)], axis=1)
    rep_flat = jnp.concatenate([partial[n].reshape(-1) for n in REPLICATED])
    n_rep = rep_flat.shape[0]
    rr = -(-n_rep // (_N_DEV * 8 * _LANES)) * 8
    rep_flat = jnp.concatenate([rep_flat, jnp.zeros((_N_DEV * rr * _LANES - n_rep,), F32)])
    small_send = jnp.concatenate([shard_pack, rep_flat.reshape(_N_DEV, rr * _LANES)], axis=1)
    small_send = small_send.reshape(_N_DEV, rs + rr, _LANES)
    (r_small,) = _run_plan(_ExchangePlan([small_send], ['slots']), "grad_exchange_small")
    small_red = _sum_sources(r_small, "sum_small")
    (rep_g,) = _run_plan(_GatherPlan([small_red[rs:]]), "gather_replicated_grads")
    rep_g = rep_g.reshape(_N_DEV * rr, _LANES)

    out = {}
    for n in BIG:
        out[n] = list(_adamw_layers(recv[n], W[n], M[n], V[n], f"adamw_{n}"))
    sh_shapes = [W[n].shape for n in SMALL_SHARDED]
    res = _adamw(small_red[:rs], _pack_rows([W[n] for n in SMALL_SHARDED], F32, 8),
                 _pack_rows([M[n] for n in SMALL_SHARDED], F32, 8),
                 _pack_rows([V[n] for n in SMALL_SHARDED], F32, 8), "adamw_small_sharded")
    for k in range(4):
        for n, val in zip(SMALL_SHARDED, _unpack(res[k].reshape(-1), sh_shapes)):
            out.setdefault(n, [None] * 4)[k] = val
    rep_shapes = [W[n].shape for n in REPLICATED]
    res = _adamw(rep_g, _pack_rows([W[n] for n in REPLICATED], F32, 8 * _N_DEV),
                 _pack_rows([M[n] for n in REPLICATED], F32, 8 * _N_DEV),
                 _pack_rows([V[n] for n in REPLICATED], F32, 8 * _N_DEV), "adamw_replicated")
    for k in range(4):
        for n, val in zip(REPLICATED, _unpack(res[k].reshape(-1), rep_shapes)):
            out.setdefault(n, [None] * 4)[k] = val

    return (loss, grad_x, *[out[n][0] for n in W_NAMES], *[out[n][1] for n in W_NAMES],
            *[out[n][2] for n in W_NAMES], *[out[n][3] for n in W_NAMES])


def _unpack_cols(gathered, shapes):
    out, off = [], 0
    for s in shapes:
        n = math.prod(s)
        out.append(gathered[:, off:off + n])
        off += n
    return out
```

```python
import math

import jax
import jax.numpy as jnp
from jax import lax
from jax.experimental import pallas as pl
from jax.experimental.pallas import tpu as pltpu

F32 = jnp.float32
_MXU = jnp.bfloat16
_TM_DW = 2816
_TM_SEQ = 256
_NB = 768
_VMEM_LIMIT = 56 * 1024 * 1024
_ADAM_BLOCK_ELEMS = 128 * 1024
_LANES = 128
_N_DEV = 8
_REP_ROWS = 64

EPS = 1e-6
N_META = 16
LRU_C = 8.0
CONV_GROUPS = 4
ADAM_LR, ADAM_B1, ADAM_B2, ADAM_EPS, ADAM_WD, ADAM_STEP = 0.001, 0.9, 0.999, 1e-08, 0.01, 10
_GELU_K0 = math.sqrt(2.0 / math.pi)
_GELU_K1 = 0.044715

W_NAMES = ['meta_tokens', 'g_pre_mix', 'w_in', 'lru_conv_w', 'lru_conv_b', 'lru_wa', 'lru_ba', 'lru_wx', 'lru_bx',
           'lru_lambda', 'conv_w', 'conv_b', 'conv_ln_g', 'conv_ln_b', 'g_out_lru', 'g_out_conv', 'w_out',
           'g_post_mix', 'g_pre_ffn', 'w_up', 'ffn_conv_w', 'ffn_conv_b', 'w_down', 'g_post_ffn']
BIG = ['w_in', 'w_out', 'w_up', 'w_down']
SMALL_SHARDED = ['meta_tokens', 'lru_conv_w', 'conv_w', 'ffn_conv_w']
REPLICATED = [n for n in W_NAMES if n not in BIG and n not in SMALL_SHARDED]


def _params():
    return pltpu.CompilerParams(vmem_limit_bytes=_VMEM_LIMIT)


def _pick(n, pref):
    if n <= pref:
        return n
    best = None
    for b in range(_LANES, pref + 1, _LANES):
        if n % b == 0:
            best = b
    assert best is not None, (n, pref)
    return best


def _pick_rows(n, pref):
    if n <= pref:
        return n
    best = None
    for b in range(8, pref + 1, 8):
        if n % b == 0:
            best = b
    assert best is not None, (n, pref)
    return best


def _lspec(l, *dims):
    zeros = (0,) * len(dims)
    return pl.BlockSpec((None,) + tuple(dims), lambda *_: (l,) + zeros)


def _sds(shape, dtype):
    return jax.ShapeDtypeStruct(tuple(shape), dtype)


def _rms_fwd(x, g):
    r = lax.rsqrt(jnp.mean(x * x, axis=-1, keepdims=True) + EPS)
    return (x * r) * g


def _rms_bwd(x, g, dy):
    r = lax.rsqrt(jnp.mean(x * x, axis=-1, keepdims=True) + EPS)
    xh = x * r
    dg = jnp.sum(dy * xh, axis=0, keepdims=True)
    dxh = dy * g
    dx = r * (dxh - xh * jnp.mean(dxh * xh, axis=-1, keepdims=True))
    return dx, dg


def _gelu(x):
    t = jnp.tanh(_GELU_K0 * (x + _GELU_K1 * (x * x * x)))
    return 0.5 * x * (1.0 + t), t


def _gelu_grad(x, t):
    return 0.5 * (1.0 + t) + 0.5 * x * (1.0 - t * t) * (_GELU_K0 * (1.0 + 3.0 * _GELU_K1 * x * x))


def _log1p(e):
    u = 1.0 + e
    return jnp.where(u == 1.0, e, jnp.log(u) * (e / (u - 1.0)))


def _softplus(z):
    return jnp.maximum(z, 0.0) + _log1p(jnp.exp(-jnp.abs(z)))


def _one_minus_exp(x):
    p = -x * (1.0 + x * (0.5 + x * (1.0 / 6 + x * (1.0 / 24 + x * (1.0 / 120 + x * (1.0 / 720))))))
    return jnp.where(x > -0.125, p, 1.0 - jnp.exp(x))


def _shift_down(x, s, fill, row):
    return jnp.where(row >= s, pltpu.roll(x, s, 0), fill)


def _shift_up(x, s, fill, row):
    n = x.shape[0]
    return jnp.where(row < n - s, pltpu.roll(x, n - s, 0), fill)


def _bd_mm(xb, w_ref, heads, hd):
    return jnp.concatenate(
        [jnp.dot(xb[:, h * hd:(h + 1) * hd], w_ref[h].astype(_MXU), preferred_element_type=F32)
         for h in range(heads)], axis=-1)


def _bd_mm_t(db, w_ref, heads, hd):
    return jnp.concatenate(
        [lax.dot_general(db[:, h * hd:(h + 1) * hd], w_ref[h].astype(_MXU), (((1,), (1,)), ((), ())),
                         preferred_element_type=F32)
         for h in range(heads)], axis=-1)


def _gates(xc, wa_ref, ba, wx_ref, bx, lam, heads, hd):
    xcb = xc.astype(_MXU)
    ra = jax.nn.sigmoid(_bd_mm(xcb, wa_ref, heads, hd) + ba)
    ri = jax.nn.sigmoid(_bd_mm(xcb, wx_ref, heads, hd) + bx)
    sp = _softplus(-lam)
    la = (-LRU_C) * ra * sp
    a = jnp.exp(la)
    mult = jnp.sqrt(_one_minus_exp(2.0 * la))
    return xcb, ra, ri, sp, a, mult


def _group_norm(cc, groups):
    gs = cc.shape[-1] // groups
    outs, rss = [], []
    for g in range(groups):
        seg = cc[:, g * gs:(g + 1) * gs]
        mu = jnp.mean(seg, axis=-1, keepdims=True)
        d = seg - mu
        rs = lax.rsqrt(jnp.mean(d * d, axis=-1, keepdims=True) + EPS)
        outs.append(d * rs)
        rss.append(rs)
    return jnp.concatenate(outs, axis=-1), rss


_MESH = pl.DeviceIdType.MESH
_ANY = pl.BlockSpec(memory_space=pl.ANY)


def _my_index():
    return 4 * lax.axis_index("x") + 2 * lax.axis_index("y") + lax.axis_index("c")


class _GatherPlan:
    def __init__(self, blocks):
        self.blocks = list(blocks)

    def operands(self):
        return self.blocks

    def out_shape(self):
        return [_sds((_N_DEV,) + b.shape, b.dtype) for b in self.blocks]

    def scratch(self):
        n = len(self.blocks)
        return [pltpu.SemaphoreType.DMA((7 * n,)), pltpu.SemaphoreType.DMA((7 * n,)), pltpu.SemaphoreType.DMA((n,))]

    def _copies(self, a, ins, outs, sems):
        send_sems, recv_sems, local_sems = sems
        x, y, c = lax.axis_index("x"), lax.axis_index("y"), lax.axis_index("c")
        me, sibling = (x, y, c), (x, y, 1 - c)
        chips = [(1 - x, y), (x, 1 - y), (1 - x, 1 - y)]

        def slot(dev):
            return outs[a].at[4 * dev[0] + 2 * dev[1] + dev[2]]

        def copy(k, block, to, src=None):
            return pltpu.make_async_remote_copy(
                src_ref=slot(block) if src is None else src, dst_ref=slot(block),
                send_sem=send_sems.at[7 * a + k], recv_sem=recv_sems.at[7 * a + k],
                device_id=to, device_id_type=_MESH)

        mine = pltpu.make_async_copy(ins[a], slot(me), local_sems.at[a])
        first = [copy(0, me, sibling, src=ins[a])]
        first += [copy(1 + j, me, (*chip, c), src=ins[a]) for j, chip in enumerate(chips)]
        passed = [copy(4 + j, (*chip, c), sibling) for j, chip in enumerate(chips)]
        from_chips = [copy(1 + j, (*chip, c), me) for j, chip in enumerate(chips)]
        from_sibling = [copy(0, sibling, me)] + [copy(4 + j, (*chip, 1 - c), me) for j, chip in enumerate(chips)]
        return mine, first, passed, from_chips, from_sibling

    def start(self, ins, outs, sems):
        for a in range(len(self.blocks)):
            mine, first, _, _, _ = self._copies(a, ins, outs, sems)
            mine.start()
            for cp in first:
                cp.start()

    def forward(self, ins, outs, sems):
        for a in range(len(self.blocks)):
            _, _, passed, from_chips, _ = self._copies(a, ins, outs, sems)
            for j in range(3):
                from_chips[j].wait_recv()
                passed[j].start()

    def finish(self, ins, outs, sems):
        for a in range(len(self.blocks)):
            mine, first, passed, _, from_sibling = self._copies(a, ins, outs, sems)
            for cp in from_sibling:
                cp.wait_recv()
            for cp in first + passed:
                cp.wait_send()
            mine.wait()

    def begin(self, i, steps, ins, outs, sems):
        @pl.when(i == 0)
        def _():
            self.start(ins, outs, sems)

        @pl.when(i == (3 * steps) // 4)
        def _():
            self.forward(ins, outs, sems)

    def end(self, i, steps, ins, outs, sems):
        @pl.when(i == steps - 1)
        def _():
            self.finish(ins, outs, sems)


class _ExchangePlan:
    def __init__(self, arrs, kinds):
        self.arrs, self.kinds = list(arrs), list(kinds)

    def _piece_shape(self, a):
        shp = self.arrs[a].shape
        if self.kinds[a] == 'cols':
            return (shp[0], shp[1] // _N_DEV)
        if self.kinds[a] == 'rows':
            return (shp[0] // _N_DEV, shp[1])
        return tuple(shp[1:])

    def operands(self):
        return self.arrs

    def out_shape(self):
        return [_sds((_N_DEV,) + self._piece_shape(a), F32) for a in range(len(self.arrs))]

    def scratch(self):
        n = len(self.arrs)
        return [pltpu.SemaphoreType.DMA((_N_DEV * n,)), pltpu.SemaphoreType.DMA((_N_DEV * n,)),
                pltpu.SemaphoreType.DMA((n,))]

    def _copies(self, ins, outs, sems):
        send_sems, recv_sems, local_sems = sems
        me = _my_index()

        def piece(a, j):
            ps = self._piece_shape(a)
            if self.kinds[a] == 'cols':
                return ins[a].at[:, pl.ds(j * ps[1], ps[1])]
            if self.kinds[a] == 'rows':
                return ins[a].at[pl.ds(j * ps[0], ps[0]), :]
            return ins[a].at[j]

        def remote(a, j):
            return pltpu.make_async_remote_copy(
                src_ref=piece(a, j), dst_ref=outs[a].at[me],
                send_sem=send_sems.at[_N_DEV * a + j], recv_sem=recv_sems.at[_N_DEV * a + me],
                device_id=(j >> 2, (j >> 1) & 1, j & 1), device_id_type=_MESH)

        def arrival(a, s):
            return pltpu.make_async_remote_copy(
                src_ref=piece(a, s), dst_ref=outs[a].at[s],
                send_sem=send_sems.at[_N_DEV * a + s], recv_sem=recv_sems.at[_N_DEV * a + s],
                device_id=(s >> 2, (s >> 1) & 1, s & 1), device_id_type=_MESH)

        def local(a, j):
            return pltpu.make_async_copy(piece(a, j), outs[a].at[j], local_sems.at[a])

        return me, remote, arrival, local

    def start(self, ins, outs, sems):
        me, remote, _, local = self._copies(ins, outs, sems)
        for j in range(_N_DEV):
            for a in range(len(self.arrs)):
                @pl.when(me != j)
                def _(a=a, j=j):
                    remote(a, j).start()

                @pl.when(me == j)
                def _(a=a, j=j):
                    local(a, j).start()

    def finish(self, ins, outs, sems):
        me, remote, arrival, local = self._copies(ins, outs, sems)
        for j in range(_N_DEV):
            for a in range(len(self.arrs)):
                @pl.when(me != j)
                def _(a=a, j=j):
                    arrival(a, j).wait_recv()
                    remote(a, j).wait_send()

                @pl.when(me == j)
                def _(a=a, j=j):
                    local(a, j).wait()

    def forward(self, ins, outs, sems):
        pass

    def begin(self, i, steps, ins, outs, sems):
        @pl.when(i == 0)
        def _():
            self.start(ins, outs, sems)

    def end(self, i, steps, ins, outs, sems):
        @pl.when(i == steps - 1)
        def _():
            self.finish(ins, outs, sems)


def _ride(main, plan, name, grid, in_specs, args, out_specs, out_shape, scratch_shapes):
    grid = (grid,) if isinstance(grid, int) else tuple(grid)
    steps = math.prod(grid)
    n_in, n_out, n_sc = len(in_specs), len(out_specs), len(scratch_shapes)
    p_args = plan.operands() if plan else []
    p_out = plan.out_shape() if plan else []
    p_sc = plan.scratch() if plan else []

    def body(*refs):
        k = 0
        ins = refs[k:k + n_in]; k += n_in
        p_ins = refs[k:k + len(p_args)]; k += len(p_args)
        outs = refs[k:k + n_out]; k += n_out
        p_outs = refs[k:k + len(p_out)]; k += len(p_out)
        scr = refs[k:k + n_sc]; k += n_sc
        sems = refs[k:]
        i = pl.program_id(0)
        for axis in range(1, len(grid)):
            i = i * grid[axis] + pl.program_id(axis)
        if plan:
            plan.begin(i, steps, p_ins, p_outs, sems)
        main(*ins, *outs, *scr)
        if plan:
            plan.end(i, steps, p_ins, p_outs, sems)

    res = pl.pallas_call(
        body, name=name, grid=grid,
        in_specs=list(in_specs) + [_ANY] * len(p_args),
        out_specs=list(out_specs) + [_ANY] * len(p_out),
        out_shape=list(out_shape) + p_out,
        scratch_shapes=list(scratch_shapes) + p_sc,
        compiler_params=_params())(*args, *p_args)
    return res[:n_out], res[n_out:]


def _run_plan(plan, name):
    n_args, n_out = len(plan.operands()), len(plan.out_shape())

    def body(*refs):
        ins, outs, sems = refs[:n_args], refs[n_args:n_args + n_out], refs[n_args + n_out:]
        plan.start(ins, outs, sems)
        plan.forward(ins, outs, sems)
        plan.finish(ins, outs, sems)

    return pl.pallas_call(
        body, name=name, in_specs=[_ANY] * n_args, out_specs=[_ANY] * n_out,
        out_shape=plan.out_shape(), scratch_shapes=plan.scratch())(*plan.operands())


def _perm(a, tm):
    T, C = a.shape
    return a.reshape(T // tm, 8, tm // 8, C).transpose(0, 2, 1, 3).reshape(T, C)


def _unperm(a, tm):
    T, C = a.shape
    return a.reshape(T // tm, tm // 8, 8, C).transpose(0, 2, 1, 3).reshape(T, C)


def _wrap_prev(prev_z, z):
    n = z.shape[0]
    sub = lax.broadcasted_iota(jnp.int32, z.shape, 0) & 7
    return jnp.where(sub == 0, pltpu.roll(prev_z, n - 7, 0), pltpu.roll(z, 1, 0))


def _wrap_next(next_z, z):
    n = z.shape[0]
    sub = lax.broadcasted_iota(jnp.int32, z.shape, 0) & 7
    return jnp.where(sub == 7, pltpu.roll(next_z, 7, 0), pltpu.roll(z, n - 1, 0))


def _fill_causal(buf, halo, x, tm):
    nh = halo.shape[0]
    buf[nh:nh + tm, :] = x
    z = buf[tm:tm + nh, :]
    buf[0:nh, :] = _wrap_prev(halo[...], z)
    halo[...] = z


def _fill_anticausal(buf, halo, dy, tm):
    nh = halo.shape[0]
    buf[0:tm, :] = dy
    z = buf[0:nh, :]
    buf[tm:tm + nh, :] = _wrap_next(halo[...], z)
    halo[...] = z


def _scan_fwd(a, u, abuf, ubuf, hcar, tm):
    D = a.shape[1]
    G = tm // 8
    abuf[...] = a
    ubuf[...] = u

    def step(j, c):
        h, pr = c
        r = pl.multiple_of(j * 8, 8)
        aj = abuf[pl.ds(r, 8), :]
        h = aj * h + ubuf[pl.ds(r, 8), :]
        pr = aj * pr
        ubuf[pl.ds(r, 8), :] = h
        abuf[pl.ds(r, 8), :] = pr
        return h, pr

    sub8 = lax.broadcasted_iota(jnp.int32, (8, D), 0)
    e, q = lax.fori_loop(1, G, step, (ubuf[0:8, :], abuf[0:8, :]))
    for s in (1, 2, 4):
        e = e + q * _shift_down(e, s, 0.0, sub8)
        q = q * _shift_down(q, s, 1.0, sub8)
    e = e + q * hcar[...]
    cin = jnp.where(sub8 == 0, hcar[...], pltpu.roll(e, 1, 0))
    return ubuf[...] + abuf[...] * jnp.tile(cin, (G, 1))


def _scan_bwd(a, d, abuf, gbuf, gcar, tmp8, tm):
    D = a.shape[1]
    G = tm // 8
    abuf[...] = a
    gbuf[...] = d

    def step(k, c):
        g_next, a_next, r_j = c
        r = pl.multiple_of((G - 1 - k) * 8, 8)
        aj = abuf[pl.ds(r, 8), :]
        g = gbuf[pl.ds(r, 8), :] + a_next * g_next
        gbuf[pl.ds(r, 8), :] = g
        abuf[pl.ds(r, 8), :] = r_j
        return g, aj, aj * r_j

    last = 8 * (G - 1)
    a_last = abuf[last:last + 8, :]
    abuf[last:last + 8, :] = jnp.ones((8, D), F32)
    g0, a0, _ = lax.fori_loop(1, G, step, (gbuf[last:last + 8, :], a_last, a_last))
    r0 = abuf[0:8, :]
    sub8 = lax.broadcasted_iota(jnp.int32, (8, D), 0)
    x, q = a0 * g0, a0 * r0
    for s in (1, 2, 4):
        x = x + q * _shift_up(x, s, 0.0, sub8)
        q = q * _shift_up(q, s, 1.0, sub8)
    x = x + q * gcar[...]
    cin = jnp.where(sub8 == 7, gcar[...], pltpu.roll(x, 7, 0))
    g = gbuf[...] + abuf[...] * jnp.tile(cin, (G, 1))
    tmp8[...] = x
    gcar[...] = tmp8[0:1, :]
    return g


def _mixers_fwd(h, p, win, l, name, plan=None):
    T, D = h.shape
    D3 = win.shape[1]
    Dc = D // 2
    heads, hd = p['lru_wa'].shape[1], p['lru_wa'].shape[2]
    tm = _TM_SEQ
    kl, kc = p['lru_conv_w'].shape[1], p['conv_w'].shape[1]
    nhl, nhc = 8 * (kl - 1), 8 * (kc - 1)
    assert nhc <= tm

    def body(h_ref, gz_ref, wl_ref, bl_ref, wa_ref, ba_ref, wx_ref, bx_ref, lam_ref, wc_ref, bc_ref, lg_ref, lb_ref,
             win_hbm, z_ref, xc_ref, hs_ref, ya_ref, cc_ref, yb_ref,
             xbuf, xhalo, hcar, cbuf, chalo, abuf, ubuf, proj_ref, win_ref, wsem):
        _load_weights(pl.program_id(0), [(win_hbm, win_ref)], wsem)

        @pl.when(pl.program_id(0) == 0)
        def _():
            xhalo[...] = jnp.zeros_like(xhalo)
            chalo[...] = jnp.zeros_like(chalo)
            hcar[...] = jnp.zeros_like(hcar)

        z = _rms_fwd(h_ref[...], gz_ref[...]).astype(_MXU)
        z_ref[...] = z
        _up_into(proj_ref, z, win_ref, 0, tm, _pick(D3, _NB))

        _fill_causal(xbuf, xhalo, proj_ref[:, 0:D], tm)
        xc = bl_ref[...] + wl_ref[0:1, :] * xbuf[0:tm, :]
        for k in range(1, kl):
            xc = xc + wl_ref[k:k + 1, :] * xbuf[8 * k:8 * k + tm, :]
        xc_ref[...] = xc
        _, ra, ri, sp, a, mult = _gates(xc, wa_ref, ba_ref[...], wx_ref, bx_ref[...], lam_ref[...], heads, hd)
        hs = _scan_fwd(a, mult * (ri * xc), abuf, ubuf, hcar, tm)
        hs_ref[...] = hs
        hcar[...] = hs_ref[pl.ds(tm - 1, 1), :]
        gg, _ = _gelu(proj_ref[:, D:2 * D])
        ya_ref[...] = hs * gg

        _fill_causal(cbuf, chalo,
                     proj_ref[:, 2 * D:2 * D + Dc] * jax.nn.sigmoid(proj_ref[:, 2 * D + Dc:3 * D]), tm)
        cc = bc_ref[...] + wc_ref[0:1, :] * cbuf[0:tm, :]
        for k in range(1, kc):
            cc = cc + wc_ref[k:k + 1, :] * cbuf[8 * k:8 * k + tm, :]
        cc_ref[...] = cc
        nrm, _ = _group_norm(cc, CONV_GROUPS)
        cl = nrm * lg_ref[...] + lb_ref[...]
        yb_ref[...] = cl * jax.nn.sigmoid(cl)

    row_d = pl.BlockSpec((tm, D), lambda i: (i, 0))
    row_c = pl.BlockSpec((tm, Dc), lambda i: (i, 0))
    return _ride(
        body, plan, name, T // tm,
        in_specs=[row_d, _lspec(l, 1, D),
                  _lspec(l, kl, D), _lspec(l, 1, D), _lspec(l, heads, hd, hd), _lspec(l, 1, D),
                  _lspec(l, heads, hd, hd), _lspec(l, 1, D), _lspec(l, 1, D),
                  _lspec(l, kc, Dc), _lspec(l, 1, Dc), _lspec(l, 1, Dc), _lspec(l, 1, Dc), _ANY],
        args=(h, p['g_pre_mix'], p['lru_conv_w'], p['lru_conv_b'], p['lru_wa'], p['lru_ba'], p['lru_wx'], p['lru_bx'],
              p['lru_lambda'], p['conv_w'], p['conv_b'], p['conv_ln_g'], p['conv_ln_b'], win),
        out_specs=[row_d, row_d, row_d, row_d, row_c, row_c],
        out_shape=[_sds((T, D), _MXU), _sds((T, D), F32), _sds((T, D), F32), _sds((T, D), F32),
                   _sds((T, Dc), F32), _sds((T, Dc), F32)],
        scratch_shapes=[pltpu.VMEM((nhl + tm, D), F32), pltpu.VMEM((nhl, D), F32), pltpu.VMEM((1, D), F32),
                        pltpu.VMEM((nhc + tm, Dc), F32), pltpu.VMEM((nhc, Dc), F32),
                        pltpu.VMEM((tm, D), F32), pltpu.VMEM((tm, D), F32),
                        pltpu.VMEM((tm, D3), F32), pltpu.VMEM((D, D3), _MXU), pltpu.SemaphoreType.DMA((1,))])


def _mix_out(ya, yb, h, p, wout, l, name):
    T, D = ya.shape
    Dc = yb.shape[1]
    tm = _TM_SEQ

    def body(ya_ref, yb_ref, h_ref, gl_ref, gc_ref, gp_ref, w_ref, y_ref, o_ref, hm_ref):
        y = jnp.concatenate([_rms_fwd(ya_ref[...], gl_ref[...]), _rms_fwd(yb_ref[...], gc_ref[...])],
                            axis=-1).astype(_MXU)
        y_ref[...] = y
        o = jnp.dot(y, w_ref[...], preferred_element_type=F32)
        o_ref[...] = o
        hm_ref[...] = h_ref[...] + _rms_fwd(o, gp_ref[...])

    row_d = pl.BlockSpec((tm, D), lambda i: (i, 0))
    return pl.pallas_call(
        body, name=name, grid=(T // tm,),
        in_specs=[row_d, pl.BlockSpec((tm, Dc), lambda i: (i, 0)), row_d,
                  _lspec(l, 1, D), _lspec(l, 1, Dc), _lspec(l, 1, D), pl.BlockSpec((D + Dc, D), lambda i: (0, 0))],
        out_specs=[pl.BlockSpec((tm, D + Dc), lambda i: (i, 0)), row_d, row_d],
        out_shape=[_sds((T, D + Dc), _MXU), _sds((T, D), F32), _sds((T, D), F32)],
        compiler_params=_params(),
    )(ya, yb, h, p['g_out_lru'], p['g_out_conv'], p['g_post_mix'], wout)


def _load_weights(i, pairs, sems):
    @pl.when(i == 0)
    def _():
        copies = [pltpu.make_async_copy(src, dst, sems.at[k]) for k, (src, dst) in enumerate(pairs)]
        for cp in copies:
            cp.start()
        for cp in copies:
            cp.wait()


def _up_into(ubuf, z, wup_v, nh, tm, cw):
    for c in range(wup_v.shape[1] // cw):
        cs = slice(c * cw, (c + 1) * cw)
        ubuf[nh:nh + tm, cs] = jnp.dot(z, wup_v[:, cs], preferred_element_type=F32)


def _ffn_fwd(hmid, p, wup, wdown, l, name, plan=None):
    T, D = hmid.shape
    F2 = wup.shape[1]
    Fh = F2 // 2
    tm = _TM_SEQ
    cw = _pick(Fh, _NB)
    kf = p['ffn_conv_w'].shape[1]
    nh = 8 * (kf - 1)

    def body(hm_ref, gz_ref, wf_ref, bf_ref, g_ref, wu_hbm, wd_hbm, z_ref, a2_ref, f_ref, ho_ref,
             ubuf, uhalo, wu_ref, wd_ref, wsem):
        i = pl.program_id(0)
        _load_weights(i, [(wu_hbm, wu_ref), (wd_hbm, wd_ref)], wsem)

        @pl.when(i == 0)
        def _():
            uhalo[...] = jnp.zeros_like(uhalo)

        z = _rms_fwd(hm_ref[...], gz_ref[...]).astype(_MXU)
        z_ref[...] = z
        _up_into(ubuf, z, wu_ref, nh, tm, cw)
        tail = ubuf[tm:tm + nh, :]
        ubuf[0:nh, :] = _wrap_prev(uhalo[...], tail)
        uhalo[...] = tail

        def conv(cs):
            acc = bf_ref[:, cs]
            for k in range(kf):
                acc = acc + wf_ref[k:k + 1, cs] * ubuf[8 * k:8 * k + tm, cs]
            return acc

        f = None
        for c in range(Fh // cw):
            gs = slice(c * cw, (c + 1) * cw)
            gg, _ = _gelu(conv(gs))
            a2 = (gg * conv(slice(Fh + c * cw, Fh + (c + 1) * cw))).astype(_MXU)
            a2_ref[:, gs] = a2
            part = jnp.dot(a2, wd_ref[gs, :], preferred_element_type=F32)
            f = part if f is None else f + part
        f_ref[...] = f
        ho_ref[...] = hm_ref[...] + _rms_fwd(f, g_ref[...])

    row_d = pl.BlockSpec((tm, D), lambda i: (i, 0))
    return _ride(
        body, plan, name, T // tm,
        in_specs=[row_d, _lspec(l, 1, D), _lspec(l, kf, F2), _lspec(l, 1, F2), _lspec(l, 1, D), _ANY, _ANY],
        args=(hmid, p['g_pre_ffn'], p['ffn_conv_w'], p['ffn_conv_b'], p['g_post_ffn'], wup, wdown),
        out_specs=[row_d, pl.BlockSpec((tm, Fh), lambda i: (i, 0)), row_d, row_d],
        out_shape=[_sds((T, D), _MXU), _sds((T, Fh), _MXU), _sds((T, D), F32), _sds((T, D), F32)],
        scratch_shapes=[pltpu.VMEM((nh + tm, F2), F32), pltpu.VMEM((nh, F2), F32),
                        pltpu.VMEM((D, F2), _MXU), pltpu.VMEM((Fh, D), _MXU), pltpu.SemaphoreType.DMA((2,))])


def _loss_head(h, tgt, n_real, name):
    T, D = h.shape
    tm = _TM_SEQ

    def body(h_ref, t_ref, loss_ref, dh_ref):
        i = pl.program_id(0)

        @pl.when(i == 0)
        def _():
            loss_ref[...] = jnp.zeros_like(loss_ref)

        pos = lax.broadcasted_iota(jnp.int32, (tm, D), 0)
        row = i * tm + (pos & 7) * (tm // 8) + (pos >> 3)
        e = jnp.where((row >= N_META) & (row < N_META + n_real), h_ref[...] - t_ref[...], 0.0)
        dh_ref[...] = e * (1.0 / D)
        loss_ref[...] += 0.5 * jnp.sum(jnp.mean(e * e, axis=-1, keepdims=True), axis=0, keepdims=True)

    row_d = pl.BlockSpec((tm, D), lambda i: (i, 0))
    return pl.pallas_call(
        body, name=name, grid=(T // tm,),
        in_specs=[row_d, row_d],
        out_specs=[pl.BlockSpec((8, _LANES), lambda i: (0, 0)), row_d],
        out_shape=[_sds((8, _LANES), F32), _sds((T, D), F32)],
        compiler_params=_params())(h, tgt)


def _ffn_bwd(dh, f, z2, hmid, p, wup, wdown, l, name, plan=None):
    T, D = dh.shape
    F2 = wup.shape[1]
    Fh = F2 // 2
    tm = _TM_SEQ
    nt = T // tm
    cw = _pick(Fh, _NB)
    kf = p['ffn_conv_w'].shape[1]
    nh = 8 * (kf - 1)
    assert tm % nh == 0

    def body(dh_ref, f_ref, z_ref, zp_ref, hm_ref, wf_ref, bf_ref, g_ref, gz_ref, wu_hbm, wd_hbm,
             df_ref, dup_ref, dwf_ref, dbf_ref, dg_ref, dhm_ref, dgz_ref,
             ubuf, dbuf, dhalo, wu_ref, wd_ref, wsem):
        i = pl.program_id(0)
        r = nt - 1 - i
        _load_weights(i, [(wu_hbm, wu_ref), (wd_hbm, wd_ref)], wsem)

        @pl.when(i == 0)
        def _():
            for ref in (dhalo, dwf_ref, dbf_ref, dg_ref, dgz_ref):
                ref[...] = jnp.zeros_like(ref)

        df, dg = _rms_bwd(f_ref[...], g_ref[...], dh_ref[...])
        dg_ref[...] += dg
        dfb = df.astype(_MXU)
        df_ref[...] = dfb
        zb, zpb = z_ref[...], zp_ref[...]

        def conv(cs):
            ubuf[nh:nh + tm, cs] = jnp.dot(zb, wu_ref[:, cs], preferred_element_type=F32)
            up_prev = jnp.dot(zpb, wu_ref[:, cs], preferred_element_type=F32)
            ubuf[0:nh, cs] = _wrap_prev(jnp.where(r == 0, 0.0, up_prev), ubuf[tm:tm + nh, cs])
            acc = bf_ref[:, cs]
            for k in range(kf):
                acc = acc + wf_ref[k:k + 1, cs] * ubuf[8 * k:8 * k + tm, cs]
            return acc

        for c in range(Fh // cw):
            gs = slice(c * cw, (c + 1) * cw)
            us = slice(Fh + c * cw, Fh + (c + 1) * cw)
            ug = conv(gs)
            gg, t = _gelu(ug)
            da2 = lax.dot_general(dfb, wd_ref[gs, :], (((1,), (1,)), ((), ())), preferred_element_type=F32)
            dbuf[0:tm, gs] = da2 * conv(us) * _gelu_grad(ug, t)
            dbuf[0:tm, us] = da2 * gg
        z = dbuf[0:nh, :]
        dbuf[tm:tm + nh, :] = _wrap_next(dhalo[...], z)
        dhalo[...] = z
        dz = None
        for c in range(F2 // cw):
            cs = slice(c * cw, (c + 1) * cw)
            upc = ubuf[nh:nh + tm, cs]
            dup = None
            for k in range(kf):
                dsh = dbuf[8 * (kf - 1 - k):8 * (kf - 1 - k) + tm, cs]
                term = wf_ref[k:k + 1, cs] * dsh
                dup = term if dup is None else dup + term
                dwf_ref[k:k + 1, cs] += jnp.sum(dsh * upc, axis=0, keepdims=True)
            dbf_ref[:, cs] += jnp.sum(dbuf[0:tm, cs], axis=0, keepdims=True)
            dupb = dup.astype(_MXU)
            dup_ref[:, cs] = dupb
            part = lax.dot_general(dupb, wu_ref[:, cs], (((1,), (1,)), ((), ())), preferred_element_type=F32)
            dz = part if dz is None else dz + part
        dx, dgz = _rms_bwd(hm_ref[...], gz_ref[...], dz)
        dhm_ref[...] = dh_ref[...] + dx
        dgz_ref[...] += dgz

    rev_d = pl.BlockSpec((tm, D), lambda i: (nt - 1 - i, 0))
    rev_f = pl.BlockSpec((tm, F2), lambda i: (nt - 1 - i, 0))
    prev = pl.BlockSpec((nh, D), lambda i: (jnp.maximum((nt - 1 - i) * (tm // nh) - 1, 0), 0))
    full = lambda *s: pl.BlockSpec(s, lambda i: (0,) * len(s))
    return _ride(
        body, plan, name, nt,
        in_specs=[rev_d, rev_d, rev_d, prev, rev_d, _lspec(l, kf, F2), _lspec(l, 1, F2), _lspec(l, 1, D),
                  _lspec(l, 1, D), _ANY, _ANY],
        args=(dh, f, z2, z2, hmid, p['ffn_conv_w'], p['ffn_conv_b'], p['g_post_ffn'], p['g_pre_ffn'], wup, wdown),
        out_specs=[rev_d, rev_f, full(kf, F2), full(1, F2), full(1, D), rev_d, full(1, D)],
        out_shape=[_sds((T, D), _MXU), _sds((T, F2), _MXU), _sds((kf, F2), F32), _sds((1, F2), F32), _sds((1, D), F32),
                   _sds((T, D), F32), _sds((1, D), F32)],
        scratch_shapes=[pltpu.VMEM((nh + tm, F2), F32), pltpu.VMEM((tm + nh, F2), F32), pltpu.VMEM((nh, F2), F32),
                        pltpu.VMEM((D, F2), _MXU), pltpu.VMEM((Fh, D), _MXU), pltpu.SemaphoreType.DMA((2,))])


def _mm_tn(xs, dy, name):
    T, K = xs.shape
    N = dy.shape[1]
    tm = _TM_DW
    kb = K if K <= 1024 else _pick(K, _NB)
    nb = _pick(N, _NB)

    def body(x_ref, dy_ref, o_ref):
        @pl.when(pl.program_id(2) == 0)
        def _():
            o_ref[...] = jnp.zeros_like(o_ref)

        o_ref[...] += lax.dot_general(x_ref[...], dy_ref[...], (((0,), (0,)), ((), ())), preferred_element_type=F32)

    return pl.pallas_call(
        body, name=name, grid=(K // kb, N // nb, T // tm),
        in_specs=[pl.BlockSpec((tm, kb), lambda a, b, t: (t, a)), pl.BlockSpec((tm, nb), lambda a, b, t: (t, b))],
        out_specs=pl.BlockSpec((kb, nb), lambda a, b, t: (a, b)),
        out_shape=_sds((K, N), F32),
        compiler_params=_params())(xs, dy)


def _mix_bwd(dhm, o, ya, yb, p, wout, l, name):
    T, D = ya.shape
    Dc = yb.shape[1]
    tm = _TM_SEQ

    def body(dh_ref, o_ref, ya_ref, yb_ref, gp_ref, gl_ref, gc_ref, w_ref,
             do_ref, dya_ref, dyb_ref, dgp_ref, dgl_ref, dgc_ref):
        @pl.when(pl.program_id(0) == 0)
        def _():
            dgp_ref[...] = jnp.zeros_like(dgp_ref)
            dgl_ref[...] = jnp.zeros_like(dgl_ref)
            dgc_ref[...] = jnp.zeros_like(dgc_ref)

        do, dgp = _rms_bwd(o_ref[...], gp_ref[...], dh_ref[...])
        dgp_ref[...] += dgp
        dob = do.astype(_MXU)
        do_ref[...] = dob
        dy = lax.dot_general(dob, w_ref[...], (((1,), (1,)), ((), ())), preferred_element_type=F32)
        dya, dgl = _rms_bwd(ya_ref[...], gl_ref[...], dy[:, 0:D])
        dyb, dgc = _rms_bwd(yb_ref[...], gc_ref[...], dy[:, D:D + Dc])
        dya_ref[...] = dya
        dyb_ref[...] = dyb
        dgl_ref[...] += dgl
        dgc_ref[...] += dgc

    row_d = pl.BlockSpec((tm, D), lambda i: (i, 0))
    row_c = pl.BlockSpec((tm, Dc), lambda i: (i, 0))
    full = lambda *s: pl.BlockSpec(s, lambda i: (0,) * len(s))
    return pl.pallas_call(
        body, name=name, grid=(T // tm,),
        in_specs=[row_d, row_d, row_d, row_c, _lspec(l, 1, D), _lspec(l, 1, D), _lspec(l, 1, Dc),
                  pl.BlockSpec((D + Dc, D), lambda i: (0, 0))],
        out_specs=[row_d, row_d, row_c, full(1, D), full(1, D), full(1, Dc)],
        out_shape=[_sds((T, D), _MXU), _sds((T, D), F32), _sds((T, Dc), F32),
                   _sds((1, D), F32), _sds((1, D), F32), _sds((1, Dc), F32)],
        compiler_params=_params(),
    )(dhm, o, ya, yb, p['g_post_mix'], p['g_out_lru'], p['g_out_conv'], wout)


def _mixers_bwd(dya, dyb, z1, xc, hs, cc, h, dres, p, win, l, name, plan=None):
    T, D = z1.shape
    D3 = win.shape[1]
    Dc = D // 2
    heads, hd = p['lru_wa'].shape[1], p['lru_wa'].shape[2]
    tm = _TM_SEQ
    nt = T // tm
    per8 = tm // 8
    kl, kc = p['lru_conv_w'].shape[1], p['conv_w'].shape[1]
    nhl, nhc = 8 * (kl - 1), 8 * (kc - 1)
    assert nhc <= tm

    def body(dya_ref, dyb_ref, z_ref, xc_ref, hs_ref, hsp_ref, cc_ref, h_ref, dres_ref,
             wl_ref, wa_ref, ba_ref, wx_ref, bx_ref, lam_ref, wc_ref, lg_ref, lb_ref, gz_ref, win_hbm,
             dproj_ref, dwl_ref, dbl_ref, dwa_ref, dba_ref, dwx_ref, dbx_ref, dlam_ref,
             dwc_ref, dbc_ref, dlg_ref, dlb_ref, dh_ref, dgz_ref,
             gcar, dxbuf, dxhalo, dcbuf, dchalo, abuf, gbuf, hbuf, tmp8, proj_ref, win_ref, wsem):
        i = pl.program_id(0)
        r = nt - 1 - i
        _load_weights(i, [(win_hbm, win_ref)], wsem)

        @pl.when(i == 0)
        def _():
            for ref in (gcar, dxhalo, dchalo, dwl_ref, dbl_ref, dwa_ref, dba_ref, dwx_ref, dbx_ref, dlam_ref,
                        dwc_ref, dbc_ref, dlg_ref, dlb_ref, dgz_ref):
                ref[...] = jnp.zeros_like(ref)

        _up_into(proj_ref, z_ref[...], win_ref, 0, tm, _pick(D3, _NB))

        dya_v = dya_ref[...]
        hs = hs_ref[...]
        gl = proj_ref[:, D:2 * D]
        gg, tg = _gelu(gl)
        dproj_ref[:, D:2 * D] = (dya_v * hs * _gelu_grad(gl, tg)).astype(_MXU)
        dhs = dya_v * gg
        xc = xc_ref[...]
        lam = lam_ref[...]
        xcb, ra, ri, sp, a, mult = _gates(xc, wa_ref, ba_ref[...], wx_ref, bx_ref[...], lam, heads, hd)
        g = _scan_bwd(a, dhs, abuf, gbuf, gcar, tmp8, tm)
        sub8 = lax.broadcasted_iota(jnp.int32, (8, D), 0)
        hbuf[8:8 + tm, :] = hs
        hbuf[0:8, :] = jnp.where(sub8 == 0, jnp.where(r == 0, 0.0, hsp_ref[7:8, :]),
                                 pltpu.roll(hs_ref[tm - 8:tm, :], 1, 0))
        da = g * hbuf[0:tm, :]
        gx = g * xc
        dxc = g * mult * ri
        dla = da * a - (gx * ri) * (a * a) / mult
        dlam_ref[...] += jnp.sum(dla * ra, axis=0, keepdims=True) * (LRU_C * jax.nn.sigmoid(-lam))
        dpa = (dla * ((-LRU_C) * sp)) * ra * (1.0 - ra)
        dpx = (gx * mult) * ri * (1.0 - ri)
        dba_ref[...] += jnp.sum(dpa, axis=0, keepdims=True)
        dbx_ref[...] += jnp.sum(dpx, axis=0, keepdims=True)
        dpab, dpxb = dpa.astype(_MXU), dpx.astype(_MXU)
        for h in range(heads):
            hsl = slice(h * hd, (h + 1) * hd)
            dwa_ref[h] += lax.dot_general(xcb[:, hsl], dpab[:, hsl], (((0,), (0,)), ((), ())),
                                          preferred_element_type=F32)
            dwx_ref[h] += lax.dot_general(xcb[:, hsl], dpxb[:, hsl], (((0,), (0,)), ((), ())),
                                          preferred_element_type=F32)
        dxc = dxc + _bd_mm_t(dpab, wa_ref, heads, hd) + _bd_mm_t(dpxb, wx_ref, heads, hd)
        dbl_ref[...] += jnp.sum(dxc, axis=0, keepdims=True)
        _fill_anticausal(dxbuf, dxhalo, dxc, tm)
        xl = proj_ref[:, 0:D]
        dxl = None
        for k in range(kl):
            dsh = dxbuf[8 * (kl - 1 - k):8 * (kl - 1 - k) + tm, :]
            term = wl_ref[k:k + 1, :] * dsh
            dxl = term if dxl is None else dxl + term
            dwl_ref[k:k + 1, :] += jnp.sum(dsh * xl, axis=0, keepdims=True)
        dproj_ref[:, 0:D] = dxl.astype(_MXU)

        ca = proj_ref[:, 2 * D:2 * D + Dc]
        sg = jax.nn.sigmoid(proj_ref[:, 2 * D + Dc:3 * D])
        cg = ca * sg
        nrm, rss = _group_norm(cc_ref[...], CONV_GROUPS)
        lg = lg_ref[...]
        cl = nrm * lg + lb_ref[...]
        sc = jax.nn.sigmoid(cl)
        dcl = dyb_ref[...] * (sc * (1.0 + cl * (1.0 - sc)))
        dlg_ref[...] += jnp.sum(dcl * nrm, axis=0, keepdims=True)
        dlb_ref[...] += jnp.sum(dcl, axis=0, keepdims=True)
        dnrm = dcl * lg
        gsz = Dc // CONV_GROUPS
        parts = []
        for gi in range(CONV_GROUPS):
            sl = slice(gi * gsz, (gi + 1) * gsz)
            dn, nn = dnrm[:, sl], nrm[:, sl]
            parts.append(rss[gi] * (dn - jnp.mean(dn, axis=-1, keepdims=True)
                                    - nn * jnp.mean(dn * nn, axis=-1, keepdims=True)))
        dcc = jnp.concatenate(parts, axis=-1)
        dbc_ref[...] += jnp.sum(dcc, axis=0, keepdims=True)
        _fill_anticausal(dcbuf, dchalo, dcc, tm)
        dcg = None
        for k in range(kc):
            dsh = dcbuf[8 * (kc - 1 - k):8 * (kc - 1 - k) + tm, :]
            term = wc_ref[k:k + 1, :] * dsh
            dcg = term if dcg is None else dcg + term
            dwc_ref[k:k + 1, :] += jnp.sum(dsh * cg, axis=0, keepdims=True)
        dproj_ref[:, 2 * D:2 * D + Dc] = (dcg * sg).astype(_MXU)
        dproj_ref[:, 2 * D + Dc:3 * D] = (dcg * ca * sg * (1.0 - sg)).astype(_MXU)

        cwp = _pick(D3, _NB)
        dz = None
        for c in range(D3 // cwp):
            cs = slice(c * cwp, (c + 1) * cwp)
            part = lax.dot_general(dproj_ref[:, cs], win_ref[:, cs], (((1,), (1,)), ((), ())),
                                   preferred_element_type=F32)
            dz = part if dz is None else dz + part
        dx, dgz = _rms_bwd(h_ref[...], gz_ref[...], dz)
        dh_ref[...] = dres_ref[...] + dx
        dgz_ref[...] += dgz

    rev_d = pl.BlockSpec((tm, D), lambda i: (nt - 1 - i, 0))
    rev_c = pl.BlockSpec((tm, Dc), lambda i: (nt - 1 - i, 0))
    rev_p = pl.BlockSpec((tm, D3), lambda i: (nt - 1 - i, 0))
    prev8 = pl.BlockSpec((8, D), lambda i: (jnp.maximum((nt - 1 - i) * per8 - 1, 0), 0))
    full = lambda *s: pl.BlockSpec(s, lambda i: (0,) * len(s))
    return _ride(
        body, plan, name, nt,
        in_specs=[rev_d, rev_c, rev_d, rev_d, rev_d, prev8, rev_c, rev_d, rev_d,
                  _lspec(l, kl, D), _lspec(l, heads, hd, hd), _lspec(l, 1, D), _lspec(l, heads, hd, hd),
                  _lspec(l, 1, D), _lspec(l, 1, D), _lspec(l, kc, Dc), _lspec(l, 1, Dc), _lspec(l, 1, Dc),
                  _lspec(l, 1, D), _ANY],
        args=(dya, dyb, z1, xc, hs, hs, cc, h, dres, p['lru_conv_w'], p['lru_wa'], p['lru_ba'], p['lru_wx'],
              p['lru_bx'], p['lru_lambda'], p['conv_w'], p['conv_ln_g'], p['conv_ln_b'], p['g_pre_mix'], win),
        out_specs=[rev_p, full(kl, D), full(1, D), full(heads, hd, hd), full(1, D), full(heads, hd, hd), full(1, D),
                   full(1, D), full(kc, Dc), full(1, Dc), full(1, Dc), full(1, Dc), rev_d, full(1, D)],
        out_shape=[_sds((T, D3), _MXU), _sds((kl, D), F32), _sds((1, D), F32), _sds((heads, hd, hd), F32),
                   _sds((1, D), F32), _sds((heads, hd, hd), F32), _sds((1, D), F32), _sds((1, D), F32),
                   _sds((kc, Dc), F32), _sds((1, Dc), F32), _sds((1, Dc), F32), _sds((1, Dc), F32),
                   _sds((T, D), F32), _sds((1, D), F32)],
        scratch_shapes=[pltpu.VMEM((1, D), F32), pltpu.VMEM((tm + nhl, D), F32), pltpu.VMEM((nhl, D), F32),
                        pltpu.VMEM((tm + nhc, Dc), F32), pltpu.VMEM((nhc, Dc), F32),
                        pltpu.VMEM((tm, D), F32), pltpu.VMEM((tm, D), F32), pltpu.VMEM((8 + tm, D), F32),
                        pltpu.VMEM((8, D), F32),
                        pltpu.VMEM((tm, D3), F32), pltpu.VMEM((D, D3), _MXU), pltpu.SemaphoreType.DMA((1,))])


def _sum_sources(recv, name):
    _, R, C = recv.shape
    rb = _pick_rows(R, 1024)

    def body(r_ref, o_ref):
        acc = r_ref[0]
        for s in range(1, _N_DEV):
            acc = acc + r_ref[s]
        o_ref[...] = acc

    return pl.pallas_call(
        body, name=name, grid=(R // rb,),
        in_specs=[pl.BlockSpec((_N_DEV, rb, C), lambda i: (0, i, 0))],
        out_specs=pl.BlockSpec((rb, C), lambda i: (i, 0)),
        out_shape=_sds((R, C), F32), compiler_params=_params())(recv)


def _adamw(g, w, m, v, name):
    R, C = w.shape
    summed = g.ndim == 3
    rb = _pick_rows(R, max(8, min(512, _ADAM_BLOCK_ELEMS // C)))
    c1 = 1.0 - ADAM_B1 ** ADAM_STEP
    c2 = 1.0 - ADAM_B2 ** ADAM_STEP

    def body(g_ref, w_ref, m_ref, v_ref, go_ref, d_ref, mo_ref, vo_ref):
        if summed:
            gv = g_ref[0]
            for s in range(1, _N_DEV):
                gv = gv + g_ref[s]
        else:
            gv = g_ref[...]
        go_ref[...] = gv
        mn = ADAM_B1 * m_ref[...] + (1.0 - ADAM_B1) * gv
        vn = ADAM_B2 * v_ref[...] + (1.0 - ADAM_B2) * (gv * gv)
        mo_ref[...] = mn
        vo_ref[...] = vn
        d_ref[...] = (-ADAM_LR) * ((mn / c1) / (jnp.sqrt(vn / c2) + ADAM_EPS) + ADAM_WD * w_ref[...])

    blk = pl.BlockSpec((rb, C), lambda i: (i, 0))
    gspec = pl.BlockSpec((_N_DEV, rb, C), lambda i: (0, i, 0)) if summed else blk
    return pl.pallas_call(
        body, name=name, grid=(R // rb,),
        in_specs=[gspec, blk, blk, blk], out_specs=[blk, blk, blk, blk],
        out_shape=[_sds((R, C), F32)] * 4, compiler_params=_params())(g, w, m, v)


def _adamw_layers(recvs, w, m, v, name, plan=None):
    L, R, C = w.shape
    rb = _pick_rows(R, max(8, _ADAM_BLOCK_ELEMS // (4 * C)))
    c1 = 1.0 - ADAM_B1 ** ADAM_STEP
    c2 = 1.0 - ADAM_B2 ** ADAM_STEP

    def body(*refs):
        r_refs = refs[:L]
        w_ref, m_ref, v_ref, go_ref, d_ref, mo_ref, vo_ref = refs[L:]
        for l in range(L):
            gv = r_refs[l][0]
            for s in range(1, _N_DEV):
                gv = gv + r_refs[l][s]
            go_ref[l] = gv
            mn = ADAM_B1 * m_ref[l] + (1.0 - ADAM_B1) * gv
            vn = ADAM_B2 * v_ref[l] + (1.0 - ADAM_B2) * (gv * gv)
            mo_ref[l] = mn
            vo_ref[l] = vn
            d_ref[l] = (-ADAM_LR) * ((mn / c1) / (jnp.sqrt(vn / c2) + ADAM_EPS) + ADAM_WD * w_ref[l])

    blk = pl.BlockSpec((L, rb, C), lambda i: (0, i, 0))
    return _ride(
        body, plan, name, R // rb,
        in_specs=[pl.BlockSpec((_N_DEV, rb, C), lambda i: (0, i, 0))] * L + [blk, blk, blk],
        args=(*recvs, w, m, v),
        out_specs=[blk, blk, blk, blk],
        out_shape=[_sds((L, R, C), F32)] * 4, scratch_shapes=[])


def _pack_rows(flat_parts, dtype, row_mult):
    flat = jnp.concatenate([f.reshape(-1).astype(dtype) for f in flat_parts])
    n = flat.shape[0]
    per = _LANES * row_mult
    padded = -(-n // per) * per
    if padded != n:
        flat = jnp.concatenate([flat, jnp.zeros((padded - n,), dtype)])
    return flat.reshape(-1, _LANES)


def _unpack(flat, shapes):
    out, off = [], 0
    for s in shapes:
        n = math.prod(s)
        out.append(flat[off:off + n].reshape(s))
        off += n
    return out


def _to_pieces(full):
    n = full.shape[-1] // _N_DEV
    t = full.reshape(full.shape[:-1] + (_N_DEV, n))
    return jnp.moveaxis(t, -2, 0).reshape(_N_DEV, -1)


def _from_gathered(seg, shard_shape, axis):
    t = seg.reshape((_N_DEV,) + tuple(shard_shape))
    t = jnp.moveaxis(t, 0, axis)
    shape = list(shard_shape)
    shape[axis] *= _N_DEV
    return t.reshape(shape)


def kernel(x, meta_tokens, g_pre_mix, w_in, lru_conv_w, lru_conv_b, lru_wa, lru_ba, lru_wx, lru_bx, lru_lambda, conv_w, conv_b, conv_ln_g, conv_ln_b, g_out_lru, g_out_conv, w_out, g_post_mix, g_pre_ffn, w_up, ffn_conv_w, ffn_conv_b, w_down, g_post_ffn, loss_target, m_meta_tokens, m_g_pre_mix, m_w_in, m_lru_conv_w, m_lru_conv_b, m_lru_wa, m_lru_ba, m_lru_wx, m_lru_bx, m_lru_lambda, m_conv_w, m_conv_b, m_conv_ln_g, m_conv_ln_b, m_g_out_lru, m_g_out_conv, m_w_out, m_g_post_mix, m_g_pre_ffn, m_w_up, m_ffn_conv_w, m_ffn_conv_b, m_w_down, m_g_post_ffn, v_meta_tokens, v_g_pre_mix, v_w_in, v_lru_conv_w, v_lru_conv_b, v_lru_wa, v_lru_ba, v_lru_wx, v_lru_bx, v_lru_lambda, v_conv_w, v_conv_b, v_conv_ln_g, v_conv_ln_b, v_g_out_lru, v_g_out_conv, v_w_out, v_g_post_mix, v_g_pre_ffn, v_w_up, v_ffn_conv_w, v_ffn_conv_b, v_w_down, v_g_post_ffn):
    given = dict(locals())
    W = {n: given[n] for n in W_NAMES}
    M = {n: given['m_' + n] for n in W_NAMES}
    V = {n: given['v_' + n] for n in W_NAMES}
    S, D = x.shape[1], x.shape[2]
    L = g_pre_mix.shape[0]
    Dc = D // 2
    step = math.lcm(_TM_SEQ, _TM_DW)
    T = -(-(N_META + S) // step) * step

    first, rest = ['w_in'], ['w_out', 'w_up', 'w_down']

    def layer_pack(l, names):
        return _pack_rows([W[n][l] for n in names], _MXU, 16)

    def layer_weights(gathered, names):
        segs = _unpack_cols(gathered.reshape(_N_DEV, -1), [W[n].shape[1:] for n in names])
        return {n: _from_gathered(seg, W[n].shape[1:], 1 if n in ('w_in', 'w_up') else 0)
                for n, seg in zip(names, segs)}

    small_pack = _pack_rows([W[n] for n in SMALL_SHARDED], F32, 8)
    big_g, small_g = _run_plan(_GatherPlan([layer_pack(0, first), small_pack]), "gather_weights_first")
    small_segs = _unpack_cols(small_g.reshape(_N_DEV, -1), [W[n].shape for n in SMALL_SHARDED])
    full = {}
    for n, seg in zip(SMALL_SHARDED, small_segs):
        full[n] = _from_gathered(seg, W[n].shape, W[n].ndim - 1)

    p = {}
    for n in W_NAMES:
        if n in BIG or n == 'meta_tokens':
            continue
        a = full[n] if n in full else W[n]
        p[n] = a.reshape(L, 1, a.shape[1]) if a.ndim == 2 else a

    pad_rows = T - N_META - S
    h = _perm(jnp.concatenate([full['meta_tokens'], x[0], jnp.zeros((pad_rows, D), F32)], axis=0), _TM_SEQ)
    tgt = _perm(jnp.concatenate([jnp.zeros((N_META, D), F32), loss_target[0], jnp.zeros((pad_rows, D), F32)],
                                axis=0), _TM_SEQ)

    saved = []
    wl = layer_weights(big_g, first)
    for l in range(L):
        plan = _GatherPlan([layer_pack(0, rest)]) if l == 0 else None
        (z1, xc, hs, ya, cc, yb), nxt = _mixers_fwd(h, p, wl['w_in'], l, f"mixers_fwd_l{l}", plan)
        if plan:
            wl = {**wl, **layer_weights(nxt[0], rest)}
        plan = _GatherPlan([layer_pack(l + 1, BIG)]) if l + 1 < L else None
        y, o, hmid = _mix_out(ya, yb, h, p, wl['w_out'], l, f"mix_out_l{l}")
        (z2, a2, f, hout), nxt = _ffn_fwd(hmid, p, wl['w_up'], wl['w_down'], l, f"ffn_fwd_l{l}", plan)
        saved.append(dict(h=h, z1=z1, xc=xc, hs=hs, ya=ya, cc=cc, yb=yb, y=y, o=o, hmid=hmid,
                          z2=z2, a2=a2, f=f, w=wl))
        h = hout
        if plan:
            wl = layer_weights(nxt[0], BIG)
    loss_tile, dh = _loss_head(h, tgt, S, "loss_head")
    loss = lax.psum(loss_tile[0, 0], ("x", "y", "c"))

    small_g_names = ['g_pre_mix', 'lru_conv_w', 'lru_conv_b', 'lru_wa', 'lru_ba', 'lru_wx', 'lru_bx', 'lru_lambda',
                     'conv_w', 'conv_b', 'conv_ln_g', 'conv_ln_b', 'g_out_lru', 'g_out_conv', 'g_post_mix',
                     'g_pre_ffn', 'ffn_conv_w', 'ffn_conv_b', 'g_post_ffn']
    per_layer = {n: [None] * L for n in small_g_names}
    recv = {n: [None] * L for n in BIG}
    pending = None
    for l in reversed(range(L)):
        sv = saved[l]
        wl = sv['w']
        plan = _ExchangePlan(list(pending), ['cols', 'cols']) if pending else None
        (df, dup0, dwf, dbf, dgpf, dhm, dgpre), got = _ffn_bwd(
            dh, sv['f'], sv['z2'], sv['hmid'], p, wl['w_up'], wl['w_down'], l, f"ffn_bwd_l{l}", plan)
        if plan:
            recv['w_up'][l + 1], recv['w_in'][l + 1] = got
        d_down = _mm_tn(sv['a2'], df, f"dw_down_l{l}")
        d_up = _mm_tn(sv['z2'], dup0, f"dw_up_l{l}")
        do, dya, dyb, dgpm, dgol, dgoc = _mix_bwd(dhm, sv['o'], sv['ya'], sv['yb'], p, wl['w_out'], l,
                                                  f"mix_bwd_l{l}")
        d_out = _mm_tn(sv['y'], do, f"dw_out_l{l}")
        plan = (_ExchangePlan([d_down, d_out, d_up], ['rows', 'rows', 'cols']) if l == 0
                else _ExchangePlan([d_down, d_out], ['rows', 'rows']))
        (dproj, dwl, dbl, dwa, dba, dwx, dbx, dlam, dwc, dbc, dlg, dlb, dh, dgpmix), got = _mixers_bwd(
            dya, dyb, sv['z1'], sv['xc'], sv['hs'], sv['cc'], sv['h'], dhm, p, wl['w_in'], l,
            f"mixers_bwd_l{l}", plan)
        recv['w_down'][l], recv['w_out'][l] = got[0], got[1]
        if l == 0:
            recv['w_up'][0] = got[2]
        d_in = _mm_tn(sv['z1'], dproj, f"dw_in_l{l}")
        pending = (d_up, d_in)
        for n, val in (('g_pre_mix', dgpmix), ('lru_conv_w', dwl), ('lru_conv_b', dbl), ('lru_wa', dwa),
                       ('lru_ba', dba), ('lru_wx', dwx), ('lru_bx', dbx), ('lru_lambda', dlam), ('conv_w', dwc),
                       ('conv_b', dbc), ('conv_ln_g', dlg), ('conv_ln_b', dlb), ('g_out_lru', dgol),
                       ('g_out_conv', dgoc), ('g_post_mix', dgpm), ('g_pre_ffn', dgpre), ('ffn_conv_w', dwf),
                       ('ffn_conv_b', dbf), ('g_post_ffn', dgpf)):
            per_layer[n][l] = val
    dh = _unperm(dh, _TM_SEQ)
    grad_x = dh[N_META:N_META + S][None]
    partial = {n: jnp.stack(per_layer[n]).reshape((L,) + tuple(
        (full[n] if n in full else W[n]).shape[1:])) for n in small_g_names}
    partial['meta_tokens'] = dh[0:N_META]

    shard_pack = jnp.concatenate([_to_pieces(partial[n]) for n in SMALL_SHARDED], axis=1)
    n_sh = shard_pack.shape[1]
    rs = -(-n_sh // (8 * _LANES)) * 8
    shard_pack = jnp.concatenate([shard_pack, jnp.zeros((_N_DEV, rs * _LANES - n_sh), F32)], axis=1)
    rep_flat = jnp.concatenate([partial[n].reshape(-1) for n in REPLICATED])
    n_rep = rep_flat.shape[0]
    rr = -(-n_rep // (_N_DEV * _REP_ROWS * _LANES)) * _REP_ROWS
    rep_flat = jnp.concatenate([rep_flat, jnp.zeros((_N_DEV * rr * _LANES - n_rep,), F32)])
    small_send = jnp.concatenate([shard_pack, rep_flat.reshape(_N_DEV, rr * _LANES)], axis=1)
    small_send = small_send.reshape(_N_DEV, rs + rr, _LANES)
    out = {}
    res, got = _adamw_layers(recv['w_up'], W['w_up'], M['w_up'], V['w_up'], "adamw_w_up",
                             _ExchangePlan([pending[1], small_send], ['cols', 'slots']))
    out['w_up'] = list(res)
    recv['w_in'][0], r_small = got
    small_red = _sum_sources(r_small, "sum_small")
    (rep_g,) = _run_plan(_GatherPlan([small_red[rs:]]), "gather_replicated_grads")
    rep_g = rep_g.reshape(_N_DEV * rr, _LANES)
    for n in ('w_in', 'w_out', 'w_down'):
        out[n] = list(_adamw_layers(recv[n], W[n], M[n], V[n], f"adamw_{n}")[0])
    sh_shapes = [W[n].shape for n in SMALL_SHARDED]
    res = _adamw(small_red[:rs], _pack_rows([W[n] for n in SMALL_SHARDED], F32, 8),
                 _pack_rows([M[n] for n in SMALL_SHARDED], F32, 8),
                 _pack_rows([V[n] for n in SMALL_SHARDED], F32, 8), "adamw_small_sharded")
    for k in range(4):
        for n, val in zip(SMALL_SHARDED, _unpack(res[k].reshape(-1), sh_shapes)):
            out.setdefault(n, [None] * 4)[k] = val
    rep_shapes = [W[n].shape for n in REPLICATED]
    res = _adamw(rep_g, _pack_rows([W[n] for n in REPLICATED], F32, _REP_ROWS * _N_DEV),
                 _pack_rows([M[n] for n in REPLICATED], F32, _REP_ROWS * _N_DEV),
                 _pack_rows([V[n] for n in REPLICATED], F32, _REP_ROWS * _N_DEV), "adamw_replicated")
    for k in range(4):
        for n, val in zip(REPLICATED, _unpack(res[k].reshape(-1), rep_shapes)):
            out.setdefault(n, [None] * 4)[k] = val

    return (loss, grad_x, *[out[n][0] for n in W_NAMES], *[out[n][1] for n in W_NAMES],
            *[out[n][2] for n in W_NAMES], *[out[n][3] for n in W_NAMES])


def _unpack_cols(gathered, shapes):
    out, off = [], 0
    for s in shapes:
        n = math.prod(s)
        out.append(gathered[:, off:off + n])
        off += n
    return out
```

```python
import math

import jax
import jax.numpy as jnp
from jax import lax
from jax.experimental import pallas as pl
from jax.experimental.pallas import tpu as pltpu

F32 = jnp.float32
_MXU = jnp.bfloat16
_TM_DW = 2816
_TM_SEQ = 256
_NB = 768
_VMEM_LIMIT = 56 * 1024 * 1024
_ADAM_BLOCK_ELEMS = 128 * 1024
_LANES = 128
_N_DEV = 8
_REP_ROWS = 64

EPS = 1e-6
N_META = 16
LRU_C = 8.0
CONV_GROUPS = 4
ADAM_LR, ADAM_B1, ADAM_B2, ADAM_EPS, ADAM_WD, ADAM_STEP = 0.001, 0.9, 0.999, 1e-08, 0.01, 10
_GELU_K0 = math.sqrt(2.0 / math.pi)
_GELU_K1 = 0.044715

W_NAMES = ['meta_tokens', 'g_pre_mix', 'w_in', 'lru_conv_w', 'lru_conv_b', 'lru_wa', 'lru_ba', 'lru_wx', 'lru_bx',
           'lru_lambda', 'conv_w', 'conv_b', 'conv_ln_g', 'conv_ln_b', 'g_out_lru', 'g_out_conv', 'w_out',
           'g_post_mix', 'g_pre_ffn', 'w_up', 'ffn_conv_w', 'ffn_conv_b', 'w_down', 'g_post_ffn']
BIG = ['w_in', 'w_out', 'w_up', 'w_down']
SMALL_SHARDED = ['meta_tokens', 'lru_conv_w', 'conv_w', 'ffn_conv_w']
REPLICATED = [n for n in W_NAMES if n not in BIG and n not in SMALL_SHARDED]


def _params():
    return pltpu.CompilerParams(vmem_limit_bytes=_VMEM_LIMIT)


def _pick(n, pref):
    if n <= pref:
        return n
    best = None
    for b in range(_LANES, pref + 1, _LANES):
        if n % b == 0:
            best = b
    assert best is not None, (n, pref)
    return best


def _pick_rows(n, pref):
    if n <= pref:
        return n
    best = None
    for b in range(8, pref + 1, 8):
        if n % b == 0:
            best = b
    assert best is not None, (n, pref)
    return best


def _lspec(l, *dims):
    zeros = (0,) * len(dims)
    return pl.BlockSpec((None,) + tuple(dims), lambda *_: (l,) + zeros)


def _sds(shape, dtype):
    return jax.ShapeDtypeStruct(tuple(shape), dtype)


def _rms_fwd(x, g):
    r = lax.rsqrt(jnp.mean(x * x, axis=-1, keepdims=True) + EPS)
    return (x * r) * g


def _rms_bwd(x, g, dy):
    r = lax.rsqrt(jnp.mean(x * x, axis=-1, keepdims=True) + EPS)
    xh = x * r
    dg = jnp.sum(dy * xh, axis=0, keepdims=True)
    dxh = dy * g
    dx = r * (dxh - xh * jnp.mean(dxh * xh, axis=-1, keepdims=True))
    return dx, dg


def _gelu(x):
    t = jnp.tanh(_GELU_K0 * (x + _GELU_K1 * (x * x * x)))
    return 0.5 * x * (1.0 + t), t


def _gelu_grad(x, t):
    return 0.5 * (1.0 + t) + 0.5 * x * (1.0 - t * t) * (_GELU_K0 * (1.0 + 3.0 * _GELU_K1 * x * x))


def _log1p(e):
    u = 1.0 + e
    return jnp.where(u == 1.0, e, jnp.log(u) * (e / (u - 1.0)))


def _softplus(z):
    return jnp.maximum(z, 0.0) + _log1p(jnp.exp(-jnp.abs(z)))


def _one_minus_exp(x):
    p = -x * (1.0 + x * (0.5 + x * (1.0 / 6 + x * (1.0 / 24 + x * (1.0 / 120 + x * (1.0 / 720))))))
    return jnp.where(x > -0.125, p, 1.0 - jnp.exp(x))


def _shift_down(x, s, fill, row):
    return jnp.where(row >= s, pltpu.roll(x, s, 0), fill)


def _shift_up(x, s, fill, row):
    n = x.shape[0]
    return jnp.where(row < n - s, pltpu.roll(x, n - s, 0), fill)


def _bd_mm(xb, w_ref, heads, hd):
    return jnp.concatenate(
        [jnp.dot(xb[:, h * hd:(h + 1) * hd], w_ref[h].astype(_MXU), preferred_element_type=F32)
         for h in range(heads)], axis=-1)


def _bd_mm_t(db, w_ref, heads, hd):
    return jnp.concatenate(
        [lax.dot_general(db[:, h * hd:(h + 1) * hd], w_ref[h].astype(_MXU), (((1,), (1,)), ((), ())),
                         preferred_element_type=F32)
         for h in range(heads)], axis=-1)


def _gates(xc, wa_ref, ba, wx_ref, bx, lam, heads, hd):
    xcb = xc.astype(_MXU)
    ra = jax.nn.sigmoid(_bd_mm(xcb, wa_ref, heads, hd) + ba)
    ri = jax.nn.sigmoid(_bd_mm(xcb, wx_ref, heads, hd) + bx)
    sp = _softplus(-lam)
    la = (-LRU_C) * ra * sp
    a = jnp.exp(la)
    mult = jnp.sqrt(_one_minus_exp(2.0 * la))
    return xcb, ra, ri, sp, a, mult


def _group_norm(cc, groups):
    gs = cc.shape[-1] // groups
    outs, rss = [], []
    for g in range(groups):
        seg = cc[:, g * gs:(g + 1) * gs]
        mu = jnp.mean(seg, axis=-1, keepdims=True)
        d = seg - mu
        rs = lax.rsqrt(jnp.mean(d * d, axis=-1, keepdims=True) + EPS)
        outs.append(d * rs)
        rss.append(rs)
    return jnp.concatenate(outs, axis=-1), rss


_MESH = pl.DeviceIdType.MESH
_ANY = pl.BlockSpec(memory_space=pl.ANY)


def _my_index():
    return 4 * lax.axis_index("x") + 2 * lax.axis_index("y") + lax.axis_index("c")


class _GatherPlan:
    def __init__(self, blocks):
        self.blocks = list(blocks)

    def operands(self):
        return self.blocks

    def out_shape(self):
        return [_sds((_N_DEV,) + b.shape, b.dtype) for b in self.blocks]

    def scratch(self):
        n = len(self.blocks)
        return [pltpu.SemaphoreType.DMA((7 * n,)), pltpu.SemaphoreType.DMA((7 * n,)), pltpu.SemaphoreType.DMA((n,))]

    def _copies(self, a, ins, outs, sems):
        send_sems, recv_sems, local_sems = sems
        x, y, c = lax.axis_index("x"), lax.axis_index("y"), lax.axis_index("c")
        me, sibling = (x, y, c), (x, y, 1 - c)
        chips = [(1 - x, y), (x, 1 - y), (1 - x, 1 - y)]

        def slot(dev):
            return outs[a].at[4 * dev[0] + 2 * dev[1] + dev[2]]

        def copy(k, block, to, src=None):
            return pltpu.make_async_remote_copy(
                src_ref=slot(block) if src is None else src, dst_ref=slot(block),
                send_sem=send_sems.at[7 * a + k], recv_sem=recv_sems.at[7 * a + k],
                device_id=to, device_id_type=_MESH)

        mine = pltpu.make_async_copy(ins[a], slot(me), local_sems.at[a])
        first = [copy(0, me, sibling, src=ins[a])]
        first += [copy(1 + j, me, (*chip, c), src=ins[a]) for j, chip in enumerate(chips)]
        passed = [copy(4 + j, (*chip, c), sibling) for j, chip in enumerate(chips)]
        from_chips = [copy(1 + j, (*chip, c), me) for j, chip in enumerate(chips)]
        from_sibling = [copy(0, sibling, me)] + [copy(4 + j, (*chip, 1 - c), me) for j, chip in enumerate(chips)]
        return mine, first, passed, from_chips, from_sibling

    def start(self, ins, outs, sems):
        for a in range(len(self.blocks)):
            mine, first, _, _, _ = self._copies(a, ins, outs, sems)
            mine.start()
            for cp in first:
                cp.start()

    def forward(self, ins, outs, sems):
        for a in range(len(self.blocks)):
            _, _, passed, from_chips, _ = self._copies(a, ins, outs, sems)
            for j in range(3):
                from_chips[j].wait_recv()
                passed[j].start()

    def finish(self, ins, outs, sems):
        for a in range(len(self.blocks)):
            mine, first, passed, _, from_sibling = self._copies(a, ins, outs, sems)
            for cp in from_sibling:
                cp.wait_recv()
            for cp in first + passed:
                cp.wait_send()
            mine.wait()

    def begin(self, i, steps, ins, outs, sems):
        @pl.when(i == 0)
        def _():
            self.start(ins, outs, sems)

        @pl.when(i == (3 * steps) // 4)
        def _():
            self.forward(ins, outs, sems)

    def end(self, i, steps, ins, outs, sems):
        @pl.when(i == steps - 1)
        def _():
            self.finish(ins, outs, sems)


class _ExchangePlan:
    def __init__(self, arrs, kinds):
        self.arrs, self.kinds = list(arrs), list(kinds)

    def _piece_shape(self, a):
        shp = self.arrs[a].shape
        if self.kinds[a] == 'cols':
            return (shp[0], shp[1] // _N_DEV)
        if self.kinds[a] == 'rows':
            return (shp[0] // _N_DEV, shp[1])
        return tuple(shp[1:])

    def operands(self):
        return self.arrs

    def out_shape(self):
        return [_sds((_N_DEV,) + self._piece_shape(a), F32) for a in range(len(self.arrs))]

    def scratch(self):
        n = len(self.arrs)
        return [pltpu.SemaphoreType.DMA((_N_DEV * n,)), pltpu.SemaphoreType.DMA((_N_DEV * n,)),
                pltpu.SemaphoreType.DMA((n,))]

    def _copies(self, ins, outs, sems):
        send_sems, recv_sems, local_sems = sems
        me = _my_index()

        def piece(a, j):
            ps = self._piece_shape(a)
            if self.kinds[a] == 'cols':
                return ins[a].at[:, pl.ds(j * ps[1], ps[1])]
            if self.kinds[a] == 'rows':
                return ins[a].at[pl.ds(j * ps[0], ps[0]), :]
            return ins[a].at[j]

        def remote(a, j):
            return pltpu.make_async_remote_copy(
                src_ref=piece(a, j), dst_ref=outs[a].at[me],
                send_sem=send_sems.at[_N_DEV * a + j], recv_sem=recv_sems.at[_N_DEV * a + me],
                device_id=(j >> 2, (j >> 1) & 1, j & 1), device_id_type=_MESH)

        def arrival(a, s):
            return pltpu.make_async_remote_copy(
                src_ref=piece(a, s), dst_ref=outs[a].at[s],
                send_sem=send_sems.at[_N_DEV * a + s], recv_sem=recv_sems.at[_N_DEV * a + s],
                device_id=(s >> 2, (s >> 1) & 1, s & 1), device_id_type=_MESH)

        def local(a, j):
            return pltpu.make_async_copy(piece(a, j), outs[a].at[j], local_sems.at[a])

        return me, remote, arrival, local

    def start(self, ins, outs, sems):
        me, remote, _, local = self._copies(ins, outs, sems)
        for j in range(_N_DEV):
            for a in range(len(self.arrs)):
                @pl.when(me != j)
                def _(a=a, j=j):
                    remote(a, j).start()

                @pl.when(me == j)
                def _(a=a, j=j):
                    local(a, j).start()

    def finish(self, ins, outs, sems):
        me, remote, arrival, local = self._copies(ins, outs, sems)
        for j in range(_N_DEV):
            for a in range(len(self.arrs)):
                @pl.when(me != j)
                def _(a=a, j=j):
                    arrival(a, j).wait_recv()
                    remote(a, j).wait_send()

                @pl.when(me == j)
                def _(a=a, j=j):
                    local(a, j).wait()

    def forward(self, ins, outs, sems):
        pass

    def begin(self, i, steps, ins, outs, sems):
        @pl.when(i == 0)
        def _():
            self.start(ins, outs, sems)

    def end(self, i, steps, ins, outs, sems):
        @pl.when(i == steps - 1)
        def _():
            self.finish(ins, outs, sems)


def _ride(main, plan, name, grid, in_specs, args, out_specs, out_shape, scratch_shapes):
    grid = (grid,) if isinstance(grid, int) else tuple(grid)
    steps = math.prod(grid)
    n_in, n_out, n_sc = len(in_specs), len(out_specs), len(scratch_shapes)
    p_args = plan.operands() if plan else []
    p_out = plan.out_shape() if plan else []
    p_sc = plan.scratch() if plan else []

    def body(*refs):
        k = 0
        ins = refs[k:k + n_in]; k += n_in
        p_ins = refs[k:k + len(p_args)]; k += len(p_args)
        outs = refs[k:k + n_out]; k += n_out
        p_outs = refs[k:k + len(p_out)]; k += len(p_out)
        scr = refs[k:k + n_sc]; k += n_sc
        sems = refs[k:]
        i = pl.program_id(0)
        for axis in range(1, len(grid)):
            i = i * grid[axis] + pl.program_id(axis)
        if plan:
            plan.begin(i, steps, p_ins, p_outs, sems)
        main(*ins, *outs, *scr)
        if plan:
            plan.end(i, steps, p_ins, p_outs, sems)

    res = pl.pallas_call(
        body, name=name, grid=grid,
        in_specs=list(in_specs) + [_ANY] * len(p_args),
        out_specs=list(out_specs) + [_ANY] * len(p_out),
        out_shape=list(out_shape) + p_out,
        scratch_shapes=list(scratch_shapes) + p_sc,
        compiler_params=_params())(*args, *p_args)
    return res[:n_out], res[n_out:]


def _run_plan(plan, name):
    n_args, n_out = len(plan.operands()), len(plan.out_shape())

    def body(*refs):
        ins, outs, sems = refs[:n_args], refs[n_args:n_args + n_out], refs[n_args + n_out:]
        plan.start(ins, outs, sems)
        plan.forward(ins, outs, sems)
        plan.finish(ins, outs, sems)

    return pl.pallas_call(
        body, name=name, in_specs=[_ANY] * n_args, out_specs=[_ANY] * n_out,
        out_shape=plan.out_shape(), scratch_shapes=plan.scratch())(*plan.operands())


def _perm(a, tm):
    T, C = a.shape
    return a.reshape(T // tm, 8, tm // 8, C).transpose(0, 2, 1, 3).reshape(T, C)


def _unperm(a, tm):
    T, C = a.shape
    return a.reshape(T // tm, tm // 8, 8, C).transpose(0, 2, 1, 3).reshape(T, C)


def _wrap_prev(prev_z, z):
    n = z.shape[0]
    sub = lax.broadcasted_iota(jnp.int32, z.shape, 0) & 7
    return jnp.where(sub == 0, pltpu.roll(prev_z, n - 7, 0), pltpu.roll(z, 1, 0))


def _wrap_next(next_z, z):
    n = z.shape[0]
    sub = lax.broadcasted_iota(jnp.int32, z.shape, 0) & 7
    return jnp.where(sub == 7, pltpu.roll(next_z, 7, 0), pltpu.roll(z, n - 1, 0))


def _fill_causal(buf, halo, x, tm):
    nh = halo.shape[0]
    buf[nh:nh + tm, :] = x
    z = buf[tm:tm + nh, :]
    buf[0:nh, :] = _wrap_prev(halo[...], z)
    halo[...] = z


def _fill_anticausal(buf, halo, dy, tm):
    nh = halo.shape[0]
    buf[0:tm, :] = dy
    z = buf[0:nh, :]
    buf[tm:tm + nh, :] = _wrap_next(halo[...], z)
    halo[...] = z


def _scan_fwd(a, u, abuf, ubuf, hcar, tm):
    D = a.shape[1]
    G = tm // 8
    abuf[...] = a
    ubuf[...] = u

    def step(j, c):
        h, pr = c
        r = pl.multiple_of(j * 8, 8)
        aj = abuf[pl.ds(r, 8), :]
        h = aj * h + ubuf[pl.ds(r, 8), :]
        pr = aj * pr
        ubuf[pl.ds(r, 8), :] = h
        abuf[pl.ds(r, 8), :] = pr
        return h, pr

    sub8 = lax.broadcasted_iota(jnp.int32, (8, D), 0)
    e, q = lax.fori_loop(1, G, step, (ubuf[0:8, :], abuf[0:8, :]))
    for s in (1, 2, 4):
        e = e + q * _shift_down(e, s, 0.0, sub8)
        q = q * _shift_down(q, s, 1.0, sub8)
    e = e + q * hcar[...]
    cin = jnp.where(sub8 == 0, hcar[...], pltpu.roll(e, 1, 0))
    return ubuf[...] + abuf[...] * jnp.tile(cin, (G, 1))


def _scan_bwd(a, d, abuf, gbuf, gcar, tmp8, tm):
    D = a.shape[1]
    G = tm // 8
    abuf[...] = a
    gbuf[...] = d

    def step(k, c):
        g_next, a_next, r_j = c
        r = pl.multiple_of((G - 1 - k) * 8, 8)
        aj = abuf[pl.ds(r, 8), :]
        g = gbuf[pl.ds(r, 8), :] + a_next * g_next
        gbuf[pl.ds(r, 8), :] = g
        abuf[pl.ds(r, 8), :] = r_j
        return g, aj, aj * r_j

    last = 8 * (G - 1)
    a_last = abuf[last:last + 8, :]
    abuf[last:last + 8, :] = jnp.ones((8, D), F32)
    g0, a0, _ = lax.fori_loop(1, G, step, (gbuf[last:last + 8, :], a_last, a_last))
    r0 = abuf[0:8, :]
    sub8 = lax.broadcasted_iota(jnp.int32, (8, D), 0)
    x, q = a0 * g0, a0 * r0
    for s in (1, 2, 4):
        x = x + q * _shift_up(x, s, 0.0, sub8)
        q = q * _shift_up(q, s, 1.0, sub8)
    x = x + q * gcar[...]
    cin = jnp.where(sub8 == 7, gcar[...], pltpu.roll(x, 7, 0))
    g = gbuf[...] + abuf[...] * jnp.tile(cin, (G, 1))
    tmp8[...] = x
    gcar[...] = tmp8[0:1, :]
    return g


def _mixers_fwd(h, p, win, l, name, plan=None):
    T, D = h.shape
    D3 = win.shape[0] * win.shape[2]
    Dc = D // 2
    heads, hd = p['lru_wa'].shape[1], p['lru_wa'].shape[2]
    tm = _TM_SEQ
    kl, kc = p['lru_conv_w'].shape[1], p['conv_w'].shape[1]
    nhl, nhc = 8 * (kl - 1), 8 * (kc - 1)
    assert nhc <= tm

    def body(h_ref, gz_ref, wl_ref, bl_ref, wa_ref, ba_ref, wx_ref, bx_ref, lam_ref, wc_ref, bc_ref, lg_ref, lb_ref,
             win_hbm, z_ref, xc_ref, hs_ref, ya_ref, cc_ref, yb_ref,
             xbuf, xhalo, hcar, cbuf, chalo, abuf, ubuf, win_ref, wsem):
        _load_weights(pl.program_id(0), _shard_pairs(win_hbm, win_ref), wsem)

        @pl.when(pl.program_id(0) == 0)
        def _():
            xhalo[...] = jnp.zeros_like(xhalo)
            chalo[...] = jnp.zeros_like(chalo)
            hcar[...] = jnp.zeros_like(hcar)

        z = _rms_fwd(h_ref[...], gz_ref[...]).astype(_MXU)
        z_ref[...] = z

        def proj(lo, hi):
            return jnp.dot(z, win_ref[:, lo:hi], preferred_element_type=F32)

        _fill_causal(xbuf, xhalo, proj(0, D), tm)
        xc = bl_ref[...] + wl_ref[0:1, :] * xbuf[0:tm, :]
        for k in range(1, kl):
            xc = xc + wl_ref[k:k + 1, :] * xbuf[8 * k:8 * k + tm, :]
        xc_ref[...] = xc
        _, ra, ri, sp, a, mult = _gates(xc, wa_ref, ba_ref[...], wx_ref, bx_ref[...], lam_ref[...], heads, hd)
        hs = _scan_fwd(a, mult * (ri * xc), abuf, ubuf, hcar, tm)
        hs_ref[...] = hs
        hcar[...] = hs_ref[pl.ds(tm - 1, 1), :]
        gg, _ = _gelu(proj(D, 2 * D))
        ya_ref[...] = hs * gg

        _fill_causal(cbuf, chalo, proj(2 * D, 2 * D + Dc) * jax.nn.sigmoid(proj(2 * D + Dc, 3 * D)), tm)
        cc = bc_ref[...] + wc_ref[0:1, :] * cbuf[0:tm, :]
        for k in range(1, kc):
            cc = cc + wc_ref[k:k + 1, :] * cbuf[8 * k:8 * k + tm, :]
        cc_ref[...] = cc
        nrm, _ = _group_norm(cc, CONV_GROUPS)
        cl = nrm * lg_ref[...] + lb_ref[...]
        yb_ref[...] = cl * jax.nn.sigmoid(cl)

    row_d = pl.BlockSpec((tm, D), lambda i: (i, 0))
    row_c = pl.BlockSpec((tm, Dc), lambda i: (i, 0))
    return _ride(
        body, plan, name, T // tm,
        in_specs=[row_d, _lspec(l, 1, D),
                  _lspec(l, kl, D), _lspec(l, 1, D), _lspec(l, heads, hd, hd), _lspec(l, 1, D),
                  _lspec(l, heads, hd, hd), _lspec(l, 1, D), _lspec(l, 1, D),
                  _lspec(l, kc, Dc), _lspec(l, 1, Dc), _lspec(l, 1, Dc), _lspec(l, 1, Dc), _ANY],
        args=(h, p['g_pre_mix'], p['lru_conv_w'], p['lru_conv_b'], p['lru_wa'], p['lru_ba'], p['lru_wx'], p['lru_bx'],
              p['lru_lambda'], p['conv_w'], p['conv_b'], p['conv_ln_g'], p['conv_ln_b'], win),
        out_specs=[row_d, row_d, row_d, row_d, row_c, row_c],
        out_shape=[_sds((T, D), _MXU), _sds((T, D), F32), _sds((T, D), F32), _sds((T, D), F32),
                   _sds((T, Dc), F32), _sds((T, Dc), F32)],
        scratch_shapes=[pltpu.VMEM((nhl + tm, D), F32), pltpu.VMEM((nhl, D), F32), pltpu.VMEM((1, D), F32),
                        pltpu.VMEM((nhc + tm, Dc), F32), pltpu.VMEM((nhc, Dc), F32),
                        pltpu.VMEM((tm, D), F32), pltpu.VMEM((tm, D), F32),
                        pltpu.VMEM((D, D3), _MXU), pltpu.SemaphoreType.DMA((_N_DEV,))])


def _mix_out(ya, yb, h, p, wout, l, name):
    T, D = ya.shape
    Dc = yb.shape[1]
    tm = _TM_SEQ

    def body(ya_ref, yb_ref, h_ref, gl_ref, gc_ref, gp_ref, w_ref, y_ref, o_ref, hm_ref):
        y = jnp.concatenate([_rms_fwd(ya_ref[...], gl_ref[...]), _rms_fwd(yb_ref[...], gc_ref[...])],
                            axis=-1).astype(_MXU)
        y_ref[...] = y
        o = jnp.dot(y, w_ref[...], preferred_element_type=F32)
        o_ref[...] = o
        hm_ref[...] = h_ref[...] + _rms_fwd(o, gp_ref[...])

    row_d = pl.BlockSpec((tm, D), lambda i: (i, 0))
    return pl.pallas_call(
        body, name=name, grid=(T // tm,),
        in_specs=[row_d, pl.BlockSpec((tm, Dc), lambda i: (i, 0)), row_d,
                  _lspec(l, 1, D), _lspec(l, 1, Dc), _lspec(l, 1, D), pl.BlockSpec((D + Dc, D), lambda i: (0, 0))],
        out_specs=[pl.BlockSpec((tm, D + Dc), lambda i: (i, 0)), row_d, row_d],
        out_shape=[_sds((T, D + Dc), _MXU), _sds((T, D), F32), _sds((T, D), F32)],
        compiler_params=_params(),
    )(ya, yb, h, p['g_out_lru'], p['g_out_conv'], p['g_post_mix'], wout)


def _shard_pairs(w_hbm, w_ref):
    n = w_hbm.shape[2]
    return [(w_hbm.at[j], w_ref.at[:, pl.ds(j * n, n)]) for j in range(w_hbm.shape[0])]


def _load_weights(i, pairs, sems):
    @pl.when(i == 0)
    def _():
        copies = [pltpu.make_async_copy(src, dst, sems.at[k]) for k, (src, dst) in enumerate(pairs)]
        for cp in copies:
            cp.start()
        for cp in copies:
            cp.wait()


def _up_into(ubuf, z, wup_v, nh, tm, cw):
    for c in range(wup_v.shape[1] // cw):
        cs = slice(c * cw, (c + 1) * cw)
        ubuf[nh:nh + tm, cs] = jnp.dot(z, wup_v[:, cs], preferred_element_type=F32)


def _ffn_fwd(hmid, p, wup, wdown, l, name, plan=None):
    T, D = hmid.shape
    F2 = wup.shape[0] * wup.shape[2]
    Fh = F2 // 2
    tm = _TM_SEQ
    cw = _pick(Fh, _NB)
    kf = p['ffn_conv_w'].shape[1]
    nh = 8 * (kf - 1)

    def body(hm_ref, gz_ref, wf_ref, bf_ref, g_ref, wu_hbm, wd_hbm, z_ref, a2_ref, f_ref, ho_ref,
             ubuf, uhalo, wu_ref, wd_ref, wsem):
        i = pl.program_id(0)
        _load_weights(i, _shard_pairs(wu_hbm, wu_ref) + [(wd_hbm, wd_ref)], wsem)

        @pl.when(i == 0)
        def _():
            uhalo[...] = jnp.zeros_like(uhalo)

        z = _rms_fwd(hm_ref[...], gz_ref[...]).astype(_MXU)
        z_ref[...] = z
        _up_into(ubuf, z, wu_ref, nh, tm, cw)
        tail = ubuf[tm:tm + nh, :]
        ubuf[0:nh, :] = _wrap_prev(uhalo[...], tail)
        uhalo[...] = tail

        def conv(cs):
            acc = bf_ref[:, cs]
            for k in range(kf):
                acc = acc + wf_ref[k:k + 1, cs] * ubuf[8 * k:8 * k + tm, cs]
            return acc

        f = None
        for c in range(Fh // cw):
            gs = slice(c * cw, (c + 1) * cw)
            gg, _ = _gelu(conv(gs))
            a2 = (gg * conv(slice(Fh + c * cw, Fh + (c + 1) * cw))).astype(_MXU)
            a2_ref[:, gs] = a2
            part = jnp.dot(a2, wd_ref[gs, :], preferred_element_type=F32)
            f = part if f is None else f + part
        f_ref[...] = f
        ho_ref[...] = hm_ref[...] + _rms_fwd(f, g_ref[...])

    row_d = pl.BlockSpec((tm, D), lambda i: (i, 0))
    return _ride(
        body, plan, name, T // tm,
        in_specs=[row_d, _lspec(l, 1, D), _lspec(l, kf, F2), _lspec(l, 1, F2), _lspec(l, 1, D), _ANY, _ANY],
        args=(hmid, p['g_pre_ffn'], p['ffn_conv_w'], p['ffn_conv_b'], p['g_post_ffn'], wup, wdown),
        out_specs=[row_d, pl.BlockSpec((tm, Fh), lambda i: (i, 0)), row_d, row_d],
        out_shape=[_sds((T, D), _MXU), _sds((T, Fh), _MXU), _sds((T, D), F32), _sds((T, D), F32)],
        scratch_shapes=[pltpu.VMEM((nh + tm, F2), F32), pltpu.VMEM((nh, F2), F32),
                        pltpu.VMEM((D, F2), _MXU), pltpu.VMEM((Fh, D), _MXU), pltpu.SemaphoreType.DMA((_N_DEV + 1,))])


def _loss_head(h, tgt, n_real, name):
    T, D = h.shape
    tm = _TM_SEQ

    def body(h_ref, t_ref, loss_ref, dh_ref):
        i = pl.program_id(0)

        @pl.when(i == 0)
        def _():
            loss_ref[...] = jnp.zeros_like(loss_ref)

        pos = lax.broadcasted_iota(jnp.int32, (tm, D), 0)
        row = i * tm + (pos & 7) * (tm // 8) + (pos >> 3)
        e = jnp.where((row >= N_META) & (row < N_META + n_real), h_ref[...] - t_ref[...], 0.0)
        dh_ref[...] = e * (1.0 / D)
        loss_ref[...] += 0.5 * jnp.sum(jnp.mean(e * e, axis=-1, keepdims=True), axis=0, keepdims=True)

    row_d = pl.BlockSpec((tm, D), lambda i: (i, 0))
    return pl.pallas_call(
        body, name=name, grid=(T // tm,),
        in_specs=[row_d, row_d],
        out_specs=[pl.BlockSpec((8, _LANES), lambda i: (0, 0)), row_d],
        out_shape=[_sds((8, _LANES), F32), _sds((T, D), F32)],
        compiler_params=_params())(h, tgt)


def _ffn_bwd(dh, f, z2, hmid, p, wup, wdown, l, name, plan=None):
    T, D = dh.shape
    F2 = wup.shape[0] * wup.shape[2]
    Fh = F2 // 2
    tm = _TM_SEQ
    nt = T // tm
    cw = _pick(Fh, _NB)
    kf = p['ffn_conv_w'].shape[1]
    nh = 8 * (kf - 1)
    assert tm % nh == 0

    def body(dh_ref, f_ref, z_ref, zp_ref, hm_ref, wf_ref, bf_ref, g_ref, gz_ref, wu_hbm, wd_hbm,
             df_ref, dup_ref, dwf_ref, dbf_ref, dg_ref, dhm_ref, dgz_ref,
             ubuf, dbuf, dhalo, wu_ref, wd_ref, wsem):
        i = pl.program_id(0)
        r = nt - 1 - i
        _load_weights(i, _shard_pairs(wu_hbm, wu_ref) + [(wd_hbm, wd_ref)], wsem)

        @pl.when(i == 0)
        def _():
            for ref in (dhalo, dwf_ref, dbf_ref, dg_ref, dgz_ref):
                ref[...] = jnp.zeros_like(ref)

        df, dg = _rms_bwd(f_ref[...], g_ref[...], dh_ref[...])
        dg_ref[...] += dg
        dfb = df.astype(_MXU)
        df_ref[...] = dfb
        zb, zpb = z_ref[...], zp_ref[...]

        def conv(cs):
            ubuf[nh:nh + tm, cs] = jnp.dot(zb, wu_ref[:, cs], preferred_element_type=F32)
            up_prev = jnp.dot(zpb, wu_ref[:, cs], preferred_element_type=F32)
            ubuf[0:nh, cs] = _wrap_prev(jnp.where(r == 0, 0.0, up_prev), ubuf[tm:tm + nh, cs])
            acc = bf_ref[:, cs]
            for k in range(kf):
                acc = acc + wf_ref[k:k + 1, cs] * ubuf[8 * k:8 * k + tm, cs]
            return acc

        for c in range(Fh // cw):
            gs = slice(c * cw, (c + 1) * cw)
            us = slice(Fh + c * cw, Fh + (c + 1) * cw)
            ug = conv(gs)
            gg, t = _gelu(ug)
            da2 = lax.dot_general(dfb, wd_ref[gs, :], (((1,), (1,)), ((), ())), preferred_element_type=F32)
            dbuf[0:tm, gs] = da2 * conv(us) * _gelu_grad(ug, t)
            dbuf[0:tm, us] = da2 * gg
        z = dbuf[0:nh, :]
        dbuf[tm:tm + nh, :] = _wrap_next(dhalo[...], z)
        dhalo[...] = z
        dz = None
        for c in range(F2 // cw):
            cs = slice(c * cw, (c + 1) * cw)
            upc = ubuf[nh:nh + tm, cs]
            dup = None
            for k in range(kf):
                dsh = dbuf[8 * (kf - 1 - k):8 * (kf - 1 - k) + tm, cs]
                term = wf_ref[k:k + 1, cs] * dsh
                dup = term if dup is None else dup + term
                dwf_ref[k:k + 1, cs] += jnp.sum(dsh * upc, axis=0, keepdims=True)
            dbf_ref[:, cs] += jnp.sum(dbuf[0:tm, cs], axis=0, keepdims=True)
            dupb = dup.astype(_MXU)
            dup_ref[:, cs] = dupb
            part = lax.dot_general(dupb, wu_ref[:, cs], (((1,), (1,)), ((), ())), preferred_element_type=F32)
            dz = part if dz is None else dz + part
        dx, dgz = _rms_bwd(hm_ref[...], gz_ref[...], dz)
        dhm_ref[...] = dh_ref[...] + dx
        dgz_ref[...] += dgz

    rev_d = pl.BlockSpec((tm, D), lambda i: (nt - 1 - i, 0))
    rev_f = pl.BlockSpec((tm, F2), lambda i: (nt - 1 - i, 0))
    prev = pl.BlockSpec((nh, D), lambda i: (jnp.maximum((nt - 1 - i) * (tm // nh) - 1, 0), 0))
    full = lambda *s: pl.BlockSpec(s, lambda i: (0,) * len(s))
    return _ride(
        body, plan, name, nt,
        in_specs=[rev_d, rev_d, rev_d, prev, rev_d, _lspec(l, kf, F2), _lspec(l, 1, F2), _lspec(l, 1, D),
                  _lspec(l, 1, D), _ANY, _ANY],
        args=(dh, f, z2, z2, hmid, p['ffn_conv_w'], p['ffn_conv_b'], p['g_post_ffn'], p['g_pre_ffn'], wup, wdown),
        out_specs=[rev_d, rev_f, full(kf, F2), full(1, F2), full(1, D), rev_d, full(1, D)],
        out_shape=[_sds((T, D), _MXU), _sds((T, F2), _MXU), _sds((kf, F2), F32), _sds((1, F2), F32), _sds((1, D), F32),
                   _sds((T, D), F32), _sds((1, D), F32)],
        scratch_shapes=[pltpu.VMEM((nh + tm, F2), F32), pltpu.VMEM((tm + nh, F2), F32), pltpu.VMEM((nh, F2), F32),
                        pltpu.VMEM((D, F2), _MXU), pltpu.VMEM((Fh, D), _MXU), pltpu.SemaphoreType.DMA((_N_DEV + 1,))])


def _mm_tn(xs, dy, name):
    T, K = xs.shape
    N = dy.shape[1]
    tm = _TM_DW
    kb = K if K <= 1024 else _pick(K, _NB)
    nb = _pick(N, _NB)

    def body(x_ref, dy_ref, o_ref):
        @pl.when(pl.program_id(2) == 0)
        def _():
            o_ref[...] = jnp.zeros_like(o_ref)

        o_ref[...] += lax.dot_general(x_ref[...], dy_ref[...], (((0,), (0,)), ((), ())), preferred_element_type=F32)

    return pl.pallas_call(
        body, name=name, grid=(K // kb, N // nb, T // tm),
        in_specs=[pl.BlockSpec((tm, kb), lambda a, b, t: (t, a)), pl.BlockSpec((tm, nb), lambda a, b, t: (t, b))],
        out_specs=pl.BlockSpec((kb, nb), lambda a, b, t: (a, b)),
        out_shape=_sds((K, N), F32),
        compiler_params=_params())(xs, dy)


def _mix_bwd(dhm, o, ya, yb, p, wout, l, name):
    T, D = ya.shape
    Dc = yb.shape[1]
    tm = _TM_SEQ

    def body(dh_ref, o_ref, ya_ref, yb_ref, gp_ref, gl_ref, gc_ref, w_ref,
             do_ref, dya_ref, dyb_ref, dgp_ref, dgl_ref, dgc_ref):
        @pl.when(pl.program_id(0) == 0)
        def _():
            dgp_ref[...] = jnp.zeros_like(dgp_ref)
            dgl_ref[...] = jnp.zeros_like(dgl_ref)
            dgc_ref[...] = jnp.zeros_like(dgc_ref)

        do, dgp = _rms_bwd(o_ref[...], gp_ref[...], dh_ref[...])
        dgp_ref[...] += dgp
        dob = do.astype(_MXU)
        do_ref[...] = dob
        dy = lax.dot_general(dob, w_ref[...], (((1,), (1,)), ((), ())), preferred_element_type=F32)
        dya, dgl = _rms_bwd(ya_ref[...], gl_ref[...], dy[:, 0:D])
        dyb, dgc = _rms_bwd(yb_ref[...], gc_ref[...], dy[:, D:D + Dc])
        dya_ref[...] = dya
        dyb_ref[...] = dyb
        dgl_ref[...] += dgl
        dgc_ref[...] += dgc

    row_d = pl.BlockSpec((tm, D), lambda i: (i, 0))
    row_c = pl.BlockSpec((tm, Dc), lambda i: (i, 0))
    full = lambda *s: pl.BlockSpec(s, lambda i: (0,) * len(s))
    return pl.pallas_call(
        body, name=name, grid=(T // tm,),
        in_specs=[row_d, row_d, row_d, row_c, _lspec(l, 1, D), _lspec(l, 1, D), _lspec(l, 1, Dc),
                  pl.BlockSpec((D + Dc, D), lambda i: (0, 0))],
        out_specs=[row_d, row_d, row_c, full(1, D), full(1, D), full(1, Dc)],
        out_shape=[_sds((T, D), _MXU), _sds((T, D), F32), _sds((T, Dc), F32),
                   _sds((1, D), F32), _sds((1, D), F32), _sds((1, Dc), F32)],
        compiler_params=_params(),
    )(dhm, o, ya, yb, p['g_post_mix'], p['g_out_lru'], p['g_out_conv'], wout)


def _mixers_bwd(dya, dyb, z1, xc, hs, cc, h, dres, p, win, l, name, plan=None):
    T, D = z1.shape
    D3 = win.shape[0] * win.shape[2]
    Dc = D // 2
    heads, hd = p['lru_wa'].shape[1], p['lru_wa'].shape[2]
    tm = _TM_SEQ
    nt = T // tm
    per8 = tm // 8
    kl, kc = p['lru_conv_w'].shape[1], p['conv_w'].shape[1]
    nhl, nhc = 8 * (kl - 1), 8 * (kc - 1)
    assert nhc <= tm

    def body(dya_ref, dyb_ref, z_ref, xc_ref, hs_ref, hsp_ref, cc_ref, h_ref, dres_ref,
             wl_ref, wa_ref, ba_ref, wx_ref, bx_ref, lam_ref, wc_ref, lg_ref, lb_ref, gz_ref, win_hbm,
             dproj_ref, dwl_ref, dbl_ref, dwa_ref, dba_ref, dwx_ref, dbx_ref, dlam_ref,
             dwc_ref, dbc_ref, dlg_ref, dlb_ref, dh_ref, dgz_ref,
             gcar, dxbuf, dxhalo, dcbuf, dchalo, abuf, gbuf, hbuf, tmp8, win_ref, wsem):
        i = pl.program_id(0)
        r = nt - 1 - i
        _load_weights(i, _shard_pairs(win_hbm, win_ref), wsem)

        @pl.when(i == 0)
        def _():
            for ref in (gcar, dxhalo, dchalo, dwl_ref, dbl_ref, dwa_ref, dba_ref, dwx_ref, dbx_ref, dlam_ref,
                        dwc_ref, dbc_ref, dlg_ref, dlb_ref, dgz_ref):
                ref[...] = jnp.zeros_like(ref)

        zb = z_ref[...]

        def proj(lo, hi):
            return jnp.dot(zb, win_ref[:, lo:hi], preferred_element_type=F32)

        xl, gl, ca, cb = proj(0, D), proj(D, 2 * D), proj(2 * D, 2 * D + Dc), proj(2 * D + Dc, 3 * D)

        dya_v = dya_ref[...]
        hs = hs_ref[...]
        gg, tg = _gelu(gl)
        dproj_ref[:, D:2 * D] = (dya_v * hs * _gelu_grad(gl, tg)).astype(_MXU)
        dhs = dya_v * gg
        xc = xc_ref[...]
        lam = lam_ref[...]
        xcb, ra, ri, sp, a, mult = _gates(xc, wa_ref, ba_ref[...], wx_ref, bx_ref[...], lam, heads, hd)
        g = _scan_bwd(a, dhs, abuf, gbuf, gcar, tmp8, tm)
        sub8 = lax.broadcasted_iota(jnp.int32, (8, D), 0)
        hbuf[8:8 + tm, :] = hs
        hbuf[0:8, :] = jnp.where(sub8 == 0, jnp.where(r == 0, 0.0, hsp_ref[7:8, :]),
                                 pltpu.roll(hs_ref[tm - 8:tm, :], 1, 0))
        da = g * hbuf[0:tm, :]
        gx = g * xc
        dxc = g * mult * ri
        dla = da * a - (gx * ri) * (a * a) / mult
        dlam_ref[...] += jnp.sum(dla * ra, axis=0, keepdims=True) * (LRU_C * jax.nn.sigmoid(-lam))
        dpa = (dla * ((-LRU_C) * sp)) * ra * (1.0 - ra)
        dpx = (gx * mult) * ri * (1.0 - ri)
        dba_ref[...] += jnp.sum(dpa, axis=0, keepdims=True)
        dbx_ref[...] += jnp.sum(dpx, axis=0, keepdims=True)
        dpab, dpxb = dpa.astype(_MXU), dpx.astype(_MXU)
        for h in range(heads):
            hsl = slice(h * hd, (h + 1) * hd)
            dwa_ref[h] += lax.dot_general(xcb[:, hsl], dpab[:, hsl], (((0,), (0,)), ((), ())),
                                          preferred_element_type=F32)
            dwx_ref[h] += lax.dot_general(xcb[:, hsl], dpxb[:, hsl], (((0,), (0,)), ((), ())),
                                          preferred_element_type=F32)
        dxc = dxc + _bd_mm_t(dpab, wa_ref, heads, hd) + _bd_mm_t(dpxb, wx_ref, heads, hd)
        dbl_ref[...] += jnp.sum(dxc, axis=0, keepdims=True)
        _fill_anticausal(dxbuf, dxhalo, dxc, tm)
        dxl = None
        for k in range(kl):
            dsh = dxbuf[8 * (kl - 1 - k):8 * (kl - 1 - k) + tm, :]
            term = wl_ref[k:k + 1, :] * dsh
            dxl = term if dxl is None else dxl + term
            dwl_ref[k:k + 1, :] += jnp.sum(dsh * xl, axis=0, keepdims=True)
        dproj_ref[:, 0:D] = dxl.astype(_MXU)

        sg = jax.nn.sigmoid(cb)
        cg = ca * sg
        nrm, rss = _group_norm(cc_ref[...], CONV_GROUPS)
        lg = lg_ref[...]
        cl = nrm * lg + lb_ref[...]
        sc = jax.nn.sigmoid(cl)
        dcl = dyb_ref[...] * (sc * (1.0 + cl * (1.0 - sc)))
        dlg_ref[...] += jnp.sum(dcl * nrm, axis=0, keepdims=True)
        dlb_ref[...] += jnp.sum(dcl, axis=0, keepdims=True)
        dnrm = dcl * lg
        gsz = Dc // CONV_GROUPS
        parts = []
        for gi in range(CONV_GROUPS):
            sl = slice(gi * gsz, (gi + 1) * gsz)
            dn, nn = dnrm[:, sl], nrm[:, sl]
            parts.append(rss[gi] * (dn - jnp.mean(dn, axis=-1, keepdims=True)
                                    - nn * jnp.mean(dn * nn, axis=-1, keepdims=True)))
        dcc = jnp.concatenate(parts, axis=-1)
        dbc_ref[...] += jnp.sum(dcc, axis=0, keepdims=True)
        _fill_anticausal(dcbuf, dchalo, dcc, tm)
        dcg = None
        for k in range(kc):
            dsh = dcbuf[8 * (kc - 1 - k):8 * (kc - 1 - k) + tm, :]
            term = wc_ref[k:k + 1, :] * dsh
            dcg = term if dcg is None else dcg + term
            dwc_ref[k:k + 1, :] += jnp.sum(dsh * cg, axis=0, keepdims=True)
        dproj_ref[:, 2 * D:2 * D + Dc] = (dcg * sg).astype(_MXU)
        dproj_ref[:, 2 * D + Dc:3 * D] = (dcg * ca * sg * (1.0 - sg)).astype(_MXU)

        cwp = _pick(D3, _NB)
        dz = None
        for c in range(D3 // cwp):
            cs = slice(c * cwp, (c + 1) * cwp)
            part = lax.dot_general(dproj_ref[:, cs], win_ref[:, cs], (((1,), (1,)), ((), ())),
                                   preferred_element_type=F32)
            dz = part if dz is None else dz + part
        dx, dgz = _rms_bwd(h_ref[...], gz_ref[...], dz)
        dh_ref[...] = dres_ref[...] + dx
        dgz_ref[...] += dgz

    rev_d = pl.BlockSpec((tm, D), lambda i: (nt - 1 - i, 0))
    rev_c = pl.BlockSpec((tm, Dc), lambda i: (nt - 1 - i, 0))
    rev_p = pl.BlockSpec((tm, D3), lambda i: (nt - 1 - i, 0))
    prev8 = pl.BlockSpec((8, D), lambda i: (jnp.maximum((nt - 1 - i) * per8 - 1, 0), 0))
    full = lambda *s: pl.BlockSpec(s, lambda i: (0,) * len(s))
    return _ride(
        body, plan, name, nt,
        in_specs=[rev_d, rev_c, rev_d, rev_d, rev_d, prev8, rev_c, rev_d, rev_d,
                  _lspec(l, kl, D), _lspec(l, heads, hd, hd), _lspec(l, 1, D), _lspec(l, heads, hd, hd),
                  _lspec(l, 1, D), _lspec(l, 1, D), _lspec(l, kc, Dc), _lspec(l, 1, Dc), _lspec(l, 1, Dc),
                  _lspec(l, 1, D), _ANY],
        args=(dya, dyb, z1, xc, hs, hs, cc, h, dres, p['lru_conv_w'], p['lru_wa'], p['lru_ba'], p['lru_wx'],
              p['lru_bx'], p['lru_lambda'], p['conv_w'], p['conv_ln_g'], p['conv_ln_b'], p['g_pre_mix'], win),
        out_specs=[rev_p, full(kl, D), full(1, D), full(heads, hd, hd), full(1, D), full(heads, hd, hd), full(1, D),
                   full(1, D), full(kc, Dc), full(1, Dc), full(1, Dc), full(1, Dc), rev_d, full(1, D)],
        out_shape=[_sds((T, D3), _MXU), _sds((kl, D), F32), _sds((1, D), F32), _sds((heads, hd, hd), F32),
                   _sds((1, D), F32), _sds((heads, hd, hd), F32), _sds((1, D), F32), _sds((1, D), F32),
                   _sds((kc, Dc), F32), _sds((1, Dc), F32), _sds((1, Dc), F32), _sds((1, Dc), F32),
                   _sds((T, D), F32), _sds((1, D), F32)],
        scratch_shapes=[pltpu.VMEM((1, D), F32), pltpu.VMEM((tm + nhl, D), F32), pltpu.VMEM((nhl, D), F32),
                        pltpu.VMEM((tm + nhc, Dc), F32), pltpu.VMEM((nhc, Dc), F32),
                        pltpu.VMEM((tm, D), F32), pltpu.VMEM((tm, D), F32), pltpu.VMEM((8 + tm, D), F32),
                        pltpu.VMEM((8, D), F32), pltpu.VMEM((D, D3), _MXU), pltpu.SemaphoreType.DMA((_N_DEV,))])


def _sum_sources(recv, name):
    _, R, C = recv.shape
    rb = _pick_rows(R, 1024)

    def body(r_ref, o_ref):
        acc = r_ref[0]
        for s in range(1, _N_DEV):
            acc = acc + r_ref[s]
        o_ref[...] = acc

    return pl.pallas_call(
        body, name=name, grid=(R // rb,),
        in_specs=[pl.BlockSpec((_N_DEV, rb, C), lambda i: (0, i, 0))],
        out_specs=pl.BlockSpec((rb, C), lambda i: (i, 0)),
        out_shape=_sds((R, C), F32), compiler_params=_params())(recv)


def _adamw(g, w, m, v, name):
    R, C = w.shape
    summed = g.ndim == 3
    rb = _pick_rows(R, max(8, min(512, _ADAM_BLOCK_ELEMS // C)))
    c1 = 1.0 - ADAM_B1 ** ADAM_STEP
    c2 = 1.0 - ADAM_B2 ** ADAM_STEP

    def body(g_ref, w_ref, m_ref, v_ref, go_ref, d_ref, mo_ref, vo_ref):
        if summed:
            gv = g_ref[0]
            for s in range(1, _N_DEV):
                gv = gv + g_ref[s]
        else:
            gv = g_ref[...]
        go_ref[...] = gv
        mn = ADAM_B1 * m_ref[...] + (1.0 - ADAM_B1) * gv
        vn = ADAM_B2 * v_ref[...] + (1.0 - ADAM_B2) * (gv * gv)
        mo_ref[...] = mn
        vo_ref[...] = vn
        d_ref[...] = (-ADAM_LR) * ((mn / c1) / (jnp.sqrt(vn / c2) + ADAM_EPS) + ADAM_WD * w_ref[...])

    blk = pl.BlockSpec((rb, C), lambda i: (i, 0))
    gspec = pl.BlockSpec((_N_DEV, rb, C), lambda i: (0, i, 0)) if summed else blk
    return pl.pallas_call(
        body, name=name, grid=(R // rb,),
        in_specs=[gspec, blk, blk, blk], out_specs=[blk, blk, blk, blk],
        out_shape=[_sds((R, C), F32)] * 4, compiler_params=_params())(g, w, m, v)


def _adamw_layers(recvs, w, m, v, name, plan=None):
    L, R, C = w.shape
    rb = _pick_rows(R, max(8, _ADAM_BLOCK_ELEMS // (4 * C)))
    c1 = 1.0 - ADAM_B1 ** ADAM_STEP
    c2 = 1.0 - ADAM_B2 ** ADAM_STEP

    def body(*refs):
        r_refs = refs[:L]
        w_ref, m_ref, v_ref, go_ref, d_ref, mo_ref, vo_ref = refs[L:]
        for l in range(L):
            gv = r_refs[l][0]
            for s in range(1, _N_DEV):
                gv = gv + r_refs[l][s]
            go_ref[l] = gv
            mn = ADAM_B1 * m_ref[l] + (1.0 - ADAM_B1) * gv
            vn = ADAM_B2 * v_ref[l] + (1.0 - ADAM_B2) * (gv * gv)
            mo_ref[l] = mn
            vo_ref[l] = vn
            d_ref[l] = (-ADAM_LR) * ((mn / c1) / (jnp.sqrt(vn / c2) + ADAM_EPS) + ADAM_WD * w_ref[l])

    blk = pl.BlockSpec((L, rb, C), lambda i: (0, i, 0))
    return _ride(
        body, plan, name, R // rb,
        in_specs=[pl.BlockSpec((_N_DEV, rb, C), lambda i: (0, i, 0))] * L + [blk, blk, blk],
        args=(*recvs, w, m, v),
        out_specs=[blk, blk, blk, blk],
        out_shape=[_sds((L, R, C), F32)] * 4, scratch_shapes=[])


def _pack_rows(flat_parts, dtype, row_mult):
    flat = jnp.concatenate([f.reshape(-1).astype(dtype) for f in flat_parts])
    n = flat.shape[0]
    per = _LANES * row_mult
    padded = -(-n // per) * per
    if padded != n:
        flat = jnp.concatenate([flat, jnp.zeros((padded - n,), dtype)])
    return flat.reshape(-1, _LANES)


def _unpack(flat, shapes):
    out, off = [], 0
    for s in shapes:
        n = math.prod(s)
        out.append(flat[off:off + n].reshape(s))
        off += n
    return out


def _to_pieces(full):
    n = full.shape[-1] // _N_DEV
    t = full.reshape(full.shape[:-1] + (_N_DEV, n))
    return jnp.moveaxis(t, -2, 0).reshape(_N_DEV, -1)


def _from_gathered(seg, shard_shape, axis):
    t = seg.reshape((_N_DEV,) + tuple(shard_shape))
    t = jnp.moveaxis(t, 0, axis)
    shape = list(shard_shape)
    shape[axis] *= _N_DEV
    return t.reshape(shape)


def kernel(x, meta_tokens, g_pre_mix, w_in, lru_conv_w, lru_conv_b, lru_wa, lru_ba, lru_wx, lru_bx, lru_lambda, conv_w, conv_b, conv_ln_g, conv_ln_b, g_out_lru, g_out_conv, w_out, g_post_mix, g_pre_ffn, w_up, ffn_conv_w, ffn_conv_b, w_down, g_post_ffn, loss_target, m_meta_tokens, m_g_pre_mix, m_w_in, m_lru_conv_w, m_lru_conv_b, m_lru_wa, m_lru_ba, m_lru_wx, m_lru_bx, m_lru_lambda, m_conv_w, m_conv_b, m_conv_ln_g, m_conv_ln_b, m_g_out_lru, m_g_out_conv, m_w_out, m_g_post_mix, m_g_pre_ffn, m_w_up, m_ffn_conv_w, m_ffn_conv_b, m_w_down, m_g_post_ffn, v_meta_tokens, v_g_pre_mix, v_w_in, v_lru_conv_w, v_lru_conv_b, v_lru_wa, v_lru_ba, v_lru_wx, v_lru_bx, v_lru_lambda, v_conv_w, v_conv_b, v_conv_ln_g, v_conv_ln_b, v_g_out_lru, v_g_out_conv, v_w_out, v_g_post_mix, v_g_pre_ffn, v_w_up, v_ffn_conv_w, v_ffn_conv_b, v_w_down, v_g_post_ffn):
    given = dict(locals())
    W = {n: given[n] for n in W_NAMES}
    M = {n: given['m_' + n] for n in W_NAMES}
    V = {n: given['v_' + n] for n in W_NAMES}
    S, D = x.shape[1], x.shape[2]
    L = g_pre_mix.shape[0]
    Dc = D // 2
    step = math.lcm(_TM_SEQ, _TM_DW)
    T = -(-(N_META + S) // step) * step

    first, rest = ['w_in'], ['w_out', 'w_up', 'w_down']

    def layer_pack(l, names):
        return [W[n][l].astype(_MXU) for n in names]

    def layer_weights(gathered, names):
        return {n: g if n in ('w_in', 'w_up') else g.reshape(-1, g.shape[2]) for n, g in zip(names, gathered)}

    small_pack = _pack_rows([W[n] for n in SMALL_SHARDED], F32, 8)
    *big_g, small_g = _run_plan(_GatherPlan(layer_pack(0, first) + [small_pack]), "gather_weights_first")
    small_segs = _unpack_cols(small_g.reshape(_N_DEV, -1), [W[n].shape for n in SMALL_SHARDED])
    full = {}
    for n, seg in zip(SMALL_SHARDED, small_segs):
        full[n] = _from_gathered(seg, W[n].shape, W[n].ndim - 1)

    p = {}
    for n in W_NAMES:
        if n in BIG or n == 'meta_tokens':
            continue
        a = full[n] if n in full else W[n]
        p[n] = a.reshape(L, 1, a.shape[1]) if a.ndim == 2 else a

    pad_rows = T - N_META - S
    h = _perm(jnp.concatenate([full['meta_tokens'], x[0], jnp.zeros((pad_rows, D), F32)], axis=0), _TM_SEQ)
    tgt = _perm(jnp.concatenate([jnp.zeros((N_META, D), F32), loss_target[0], jnp.zeros((pad_rows, D), F32)],
                                axis=0), _TM_SEQ)

    saved = []
    wl = layer_weights(big_g, first)
    for l in range(L):
        plan = _GatherPlan(layer_pack(0, rest)) if l == 0 else None
        (z1, xc, hs, ya, cc, yb), nxt = _mixers_fwd(h, p, wl['w_in'], l, f"mixers_fwd_l{l}", plan)
        if plan:
            wl = {**wl, **layer_weights(nxt, rest)}
        plan = _GatherPlan(layer_pack(l + 1, BIG)) if l + 1 < L else None
        y, o, hmid = _mix_out(ya, yb, h, p, wl['w_out'], l, f"mix_out_l{l}")
        (z2, a2, f, hout), nxt = _ffn_fwd(hmid, p, wl['w_up'], wl['w_down'], l, f"ffn_fwd_l{l}", plan)
        saved.append(dict(h=h, z1=z1, xc=xc, hs=hs, ya=ya, cc=cc, yb=yb, y=y, o=o, hmid=hmid,
                          z2=z2, a2=a2, f=f, w=wl))
        h = hout
        if plan:
            wl = layer_weights(nxt, BIG)
    loss_tile, dh = _loss_head(h, tgt, S, "loss_head")
    loss = lax.psum(loss_tile[0, 0], ("x", "y", "c"))

    small_g_names = ['g_pre_mix', 'lru_conv_w', 'lru_conv_b', 'lru_wa', 'lru_ba', 'lru_wx', 'lru_bx', 'lru_lambda',
                     'conv_w', 'conv_b', 'conv_ln_g', 'conv_ln_b', 'g_out_lru', 'g_out_conv', 'g_post_mix',
                     'g_pre_ffn', 'ffn_conv_w', 'ffn_conv_b', 'g_post_ffn']
    per_layer = {n: [None] * L for n in small_g_names}
    recv = {n: [None] * L for n in BIG}
    pending = None
    for l in reversed(range(L)):
        sv = saved[l]
        wl = sv['w']
        plan = _ExchangePlan(list(pending), ['cols', 'cols']) if pending else None
        (df, dup0, dwf, dbf, dgpf, dhm, dgpre), got = _ffn_bwd(
            dh, sv['f'], sv['z2'], sv['hmid'], p, wl['w_up'], wl['w_down'], l, f"ffn_bwd_l{l}", plan)
        if plan:
            recv['w_up'][l + 1], recv['w_in'][l + 1] = got
        d_down = _mm_tn(sv['a2'], df, f"dw_down_l{l}")
        d_up = _mm_tn(sv['z2'], dup0, f"dw_up_l{l}")
        do, dya, dyb, dgpm, dgol, dgoc = _mix_bwd(dhm, sv['o'], sv['ya'], sv['yb'], p, wl['w_out'], l,
                                                  f"mix_bwd_l{l}")
        d_out = _mm_tn(sv['y'], do, f"dw_out_l{l}")
        plan = (_ExchangePlan([d_down, d_out, d_up], ['rows', 'rows', 'cols']) if l == 0
                else _ExchangePlan([d_down, d_out], ['rows', 'rows']))
        (dproj, dwl, dbl, dwa, dba, dwx, dbx, dlam, dwc, dbc, dlg, dlb, dh, dgpmix), got = _mixers_bwd(
            dya, dyb, sv['z1'], sv['xc'], sv['hs'], sv['cc'], sv['h'], dhm, p, wl['w_in'], l,
            f"mixers_bwd_l{l}", plan)
        recv['w_down'][l], recv['w_out'][l] = got[0], got[1]
        if l == 0:
            recv['w_up'][0] = got[2]
        d_in = _mm_tn(sv['z1'], dproj, f"dw_in_l{l}")
        pending = (d_up, d_in)
        for n, val in (('g_pre_mix', dgpmix), ('lru_conv_w', dwl), ('lru_conv_b', dbl), ('lru_wa', dwa),
                       ('lru_ba', dba), ('lru_wx', dwx), ('lru_bx', dbx), ('lru_lambda', dlam), ('conv_w', dwc),
                       ('conv_b', dbc), ('conv_ln_g', dlg), ('conv_ln_b', dlb), ('g_out_lru', dgol),
                       ('g_out_conv', dgoc), ('g_post_mix', dgpm), ('g_pre_ffn', dgpre), ('ffn_conv_w', dwf),
                       ('ffn_conv_b', dbf), ('g_post_ffn', dgpf)):
            per_layer[n][l] = val
    dh = _unperm(dh, _TM_SEQ)
    grad_x = dh[N_META:N_META + S][None]
    partial = {n: jnp.stack(per_layer[n]).reshape((L,) + tuple(
        (full[n] if n in full else W[n]).shape[1:])) for n in small_g_names}
    partial['meta_tokens'] = dh[0:N_META]

    shard_pack = jnp.concatenate([_to_pieces(partial[n]) for n in SMALL_SHARDED], axis=1)
    n_sh = shard_pack.shape[1]
    rs = -(-n_sh // (8 * _LANES)) * 8
    shard_pack = jnp.concatenate([shard_pack, jnp.zeros((_N_DEV, rs * _LANES - n_sh), F32)], axis=1)
    rep_flat = jnp.concatenate([partial[n].reshape(-1) for n in REPLICATED])
    n_rep = rep_flat.shape[0]
    rr = -(-n_rep // (_N_DEV * _REP_ROWS * _LANES)) * _REP_ROWS
    rep_flat = jnp.concatenate([rep_flat, jnp.zeros((_N_DEV * rr * _LANES - n_rep,), F32)])
    small_send = jnp.concatenate([shard_pack, rep_flat.reshape(_N_DEV, rr * _LANES)], axis=1)
    small_send = small_send.reshape(_N_DEV, rs + rr, _LANES)
    out = {}
    res, got = _adamw_layers(recv['w_up'], W['w_up'], M['w_up'], V['w_up'], "adamw_w_up",
                             _ExchangePlan([pending[1], small_send], ['cols', 'slots']))
    out['w_up'] = list(res)
    recv['w_in'][0], r_small = got
    small_red = _sum_sources(r_small, "sum_small")
    (rep_g,) = _run_plan(_GatherPlan([small_red[rs:]]), "gather_replicated_grads")
    rep_g = rep_g.reshape(_N_DEV * rr, _LANES)
    for n in ('w_in', 'w_out', 'w_down'):
        out[n] = list(_adamw_layers(recv[n], W[n], M[n], V[n], f"adamw_{n}")[0])
    sh_shapes = [W[n].shape for n in SMALL_SHARDED]
    res = _adamw(small_red[:rs], _pack_rows([W[n] for n in SMALL_SHARDED], F32, 8),
                 _pack_rows([M[n] for n in SMALL_SHARDED], F32, 8),
                 _pack_rows([V[n] for n in SMALL_SHARDED], F32, 8), "adamw_small_sharded")
    for k in range(4):
        for n, val in zip(SMALL_SHARDED, _unpack(res[k].reshape(-1), sh_shapes)):
            out.setdefault(n, [None] * 4)[k] = val
    rep_shapes = [W[n].shape for n in REPLICATED]
    res = _adamw(rep_g, _pack_rows([W[n] for n in REPLICATED], F32, _REP_ROWS * _N_DEV),
                 _pack_rows([M[n] for n in REPLICATED], F32, _REP_ROWS * _N_DEV),
                 _pack_rows([V[n] for n in REPLICATED], F32, _REP_ROWS * _N_DEV), "adamw_replicated")
    for k in range(4):
        for n, val in zip(REPLICATED, _unpack(res[k].reshape(-1), rep_shapes)):
            out.setdefault(n, [None] * 4)[k] = val

    return (loss, grad_x, *[out[n][0] for n in W_NAMES], *[out[n][1] for n in W_NAMES],
            *[out[n][2] for n in W_NAMES], *[out[n][3] for n in W_NAMES])


def _unpack_cols(gathered, shapes):
    out, off = [], 0
    for s in shapes:
        n = math.prod(s)
        out.append(gathered[:, off:off + n])
        off += n
    return out
```

```python
import math

import jax
import jax.numpy as jnp
from jax import lax
from jax.experimental import pallas as pl
from jax.experimental.pallas import tpu as pltpu

F32 = jnp.float32
_MXU = jnp.bfloat16
_TM_DW = 2816
_TM_SEQ = 256
_NB = 768
_VMEM_LIMIT = 56 * 1024 * 1024
_ADAM_BLOCK_ELEMS = 128 * 1024
_LANES = 128
_N_DEV = 8
_REP_ROWS = 64

EPS = 1e-6
N_META = 16
LRU_C = 8.0
CONV_GROUPS = 4
ADAM_LR, ADAM_B1, ADAM_B2, ADAM_EPS, ADAM_WD, ADAM_STEP = 0.001, 0.9, 0.999, 1e-08, 0.01, 10
_GELU_K0 = math.sqrt(2.0 / math.pi)
_GELU_K1 = 0.044715

W_NAMES = ['meta_tokens', 'g_pre_mix', 'w_in', 'lru_conv_w', 'lru_conv_b', 'lru_wa', 'lru_ba', 'lru_wx', 'lru_bx',
           'lru_lambda', 'conv_w', 'conv_b', 'conv_ln_g', 'conv_ln_b', 'g_out_lru', 'g_out_conv', 'w_out',
           'g_post_mix', 'g_pre_ffn', 'w_up', 'ffn_conv_w', 'ffn_conv_b', 'w_down', 'g_post_ffn']
BIG = ['w_in', 'w_out', 'w_up', 'w_down']
SMALL_SHARDED = ['meta_tokens', 'lru_conv_w', 'conv_w', 'ffn_conv_w']
REPLICATED = [n for n in W_NAMES if n not in BIG and n not in SMALL_SHARDED]


def _params():
    return pltpu.CompilerParams(vmem_limit_bytes=_VMEM_LIMIT)


def _pick(n, pref):
    if n <= pref:
        return n
    best = None
    for b in range(_LANES, pref + 1, _LANES):
        if n % b == 0:
            best = b
    assert best is not None, (n, pref)
    return best


def _pick_rows(n, pref):
    if n <= pref:
        return n
    best = None
    for b in range(8, pref + 1, 8):
        if n % b == 0:
            best = b
    assert best is not None, (n, pref)
    return best


def _lspec(l, *dims):
    zeros = (0,) * len(dims)
    return pl.BlockSpec((None,) + tuple(dims), lambda *_: (l,) + zeros)


def _sds(shape, dtype):
    return jax.ShapeDtypeStruct(tuple(shape), dtype)


def _rms_fwd(x, g):
    r = lax.rsqrt(jnp.mean(x * x, axis=-1, keepdims=True) + EPS)
    return (x * r) * g


def _rms_bwd(x, g, dy):
    r = lax.rsqrt(jnp.mean(x * x, axis=-1, keepdims=True) + EPS)
    xh = x * r
    dg = jnp.sum(dy * xh, axis=0, keepdims=True)
    dxh = dy * g
    dx = r * (dxh - xh * jnp.mean(dxh * xh, axis=-1, keepdims=True))
    return dx, dg


def _gelu(x):
    t = jnp.tanh(_GELU_K0 * (x + _GELU_K1 * (x * x * x)))
    return 0.5 * x * (1.0 + t), t


def _gelu_grad(x, t):
    return 0.5 * (1.0 + t) + 0.5 * x * (1.0 - t * t) * (_GELU_K0 * (1.0 + 3.0 * _GELU_K1 * x * x))


def _log1p(e):
    u = 1.0 + e
    return jnp.where(u == 1.0, e, jnp.log(u) * (e / (u - 1.0)))


def _softplus(z):
    return jnp.maximum(z, 0.0) + _log1p(jnp.exp(-jnp.abs(z)))


def _one_minus_exp(x):
    p = -x * (1.0 + x * (0.5 + x * (1.0 / 6 + x * (1.0 / 24 + x * (1.0 / 120 + x * (1.0 / 720))))))
    return jnp.where(x > -0.125, p, 1.0 - jnp.exp(x))


def _shift_down(x, s, fill, row):
    return jnp.where(row >= s, pltpu.roll(x, s, 0), fill)


def _shift_up(x, s, fill, row):
    n = x.shape[0]
    return jnp.where(row < n - s, pltpu.roll(x, n - s, 0), fill)


def _bd_mm(xb, w_ref, heads, hd):
    return jnp.concatenate(
        [jnp.dot(xb[:, h * hd:(h + 1) * hd], w_ref[h].astype(_MXU), preferred_element_type=F32)
         for h in range(heads)], axis=-1)


def _bd_mm_t(db, w_ref, heads, hd):
    return jnp.concatenate(
        [lax.dot_general(db[:, h * hd:(h + 1) * hd], w_ref[h].astype(_MXU), (((1,), (1,)), ((), ())),
                         preferred_element_type=F32)
         for h in range(heads)], axis=-1)


def _gates(xc, wa_ref, ba, wx_ref, bx, lam, heads, hd):
    xcb = xc.astype(_MXU)
    ra = jax.nn.sigmoid(_bd_mm(xcb, wa_ref, heads, hd) + ba)
    ri = jax.nn.sigmoid(_bd_mm(xcb, wx_ref, heads, hd) + bx)
    sp = _softplus(-lam)
    la = (-LRU_C) * ra * sp
    a = jnp.exp(la)
    mult = jnp.sqrt(_one_minus_exp(2.0 * la))
    return xcb, ra, ri, sp, a, mult


def _group_norm(cc, groups):
    gs = cc.shape[-1] // groups
    outs, rss = [], []
    for g in range(groups):
        seg = cc[:, g * gs:(g + 1) * gs]
        mu = jnp.mean(seg, axis=-1, keepdims=True)
        d = seg - mu
        rs = lax.rsqrt(jnp.mean(d * d, axis=-1, keepdims=True) + EPS)
        outs.append(d * rs)
        rss.append(rs)
    return jnp.concatenate(outs, axis=-1), rss


_MESH = pl.DeviceIdType.MESH
_ANY = pl.BlockSpec(memory_space=pl.ANY)


def _my_index():
    return 4 * lax.axis_index("x") + 2 * lax.axis_index("y") + lax.axis_index("c")


class _GatherPlan:
    def __init__(self, blocks):
        self.blocks = list(blocks)

    def operands(self):
        return self.blocks

    def out_shape(self):
        return [_sds((_N_DEV,) + b.shape, b.dtype) for b in self.blocks]

    def scratch(self):
        n = len(self.blocks)
        return [pltpu.SemaphoreType.DMA((7 * n,)), pltpu.SemaphoreType.DMA((7 * n,)), pltpu.SemaphoreType.DMA((n,))]

    def _copies(self, a, ins, outs, sems):
        send_sems, recv_sems, local_sems = sems
        x, y, c = lax.axis_index("x"), lax.axis_index("y"), lax.axis_index("c")
        me, sibling = (x, y, c), (x, y, 1 - c)
        chips = [(1 - x, y), (x, 1 - y), (1 - x, 1 - y)]

        def slot(dev):
            return outs[a].at[4 * dev[0] + 2 * dev[1] + dev[2]]

        def copy(k, block, to, src=None):
            return pltpu.make_async_remote_copy(
                src_ref=slot(block) if src is None else src, dst_ref=slot(block),
                send_sem=send_sems.at[7 * a + k], recv_sem=recv_sems.at[7 * a + k],
                device_id=to, device_id_type=_MESH)

        mine = pltpu.make_async_copy(ins[a], slot(me), local_sems.at[a])
        first = [copy(0, me, sibling, src=ins[a])]
        first += [copy(1 + j, me, (*chip, c), src=ins[a]) for j, chip in enumerate(chips)]
        passed = [copy(4 + j, (*chip, c), sibling) for j, chip in enumerate(chips)]
        from_chips = [copy(1 + j, (*chip, c), me) for j, chip in enumerate(chips)]
        from_sibling = [copy(0, sibling, me)] + [copy(4 + j, (*chip, 1 - c), me) for j, chip in enumerate(chips)]
        return mine, first, passed, from_chips, from_sibling

    def start(self, ins, outs, sems):
        for a in range(len(self.blocks)):
            mine, first, _, _, _ = self._copies(a, ins, outs, sems)
            mine.start()
            for cp in first:
                cp.start()

    def forward(self, ins, outs, sems):
        for a in range(len(self.blocks)):
            _, _, passed, from_chips, _ = self._copies(a, ins, outs, sems)
            for j in range(3):
                from_chips[j].wait_recv()
                passed[j].start()

    def finish(self, ins, outs, sems):
        for a in range(len(self.blocks)):
            mine, first, passed, _, from_sibling = self._copies(a, ins, outs, sems)
            for cp in from_sibling:
                cp.wait_recv()
            for cp in first + passed:
                cp.wait_send()
            mine.wait()

    def begin(self, i, steps, ins, outs, sems):
        @pl.when(i == 0)
        def _():
            self.start(ins, outs, sems)

        @pl.when(i == (3 * steps) // 4)
        def _():
            self.forward(ins, outs, sems)

    def end(self, i, steps, ins, outs, sems):
        @pl.when(i == steps - 1)
        def _():
            self.finish(ins, outs, sems)


class _ExchangePlan:
    def __init__(self, arrs, kinds):
        self.arrs, self.kinds = list(arrs), list(kinds)

    def _piece_shape(self, a):
        shp = self.arrs[a].shape
        if self.kinds[a] == 'cols':
            return (shp[0], shp[1] // _N_DEV)
        if self.kinds[a] == 'rows':
            return (shp[0] // _N_DEV, shp[1])
        return tuple(shp[1:])

    def operands(self):
        return self.arrs

    def out_shape(self):
        return [_sds((_N_DEV,) + self._piece_shape(a), F32) for a in range(len(self.arrs))]

    def scratch(self):
        n = len(self.arrs)
        return [pltpu.SemaphoreType.DMA((_N_DEV * n,)), pltpu.SemaphoreType.DMA((_N_DEV * n,)),
                pltpu.SemaphoreType.DMA((n,))]

    def _copies(self, ins, outs, sems):
        send_sems, recv_sems, local_sems = sems
        me = _my_index()

        def piece(a, j):
            ps = self._piece_shape(a)
            if self.kinds[a] == 'cols':
                return ins[a].at[:, pl.ds(j * ps[1], ps[1])]
            if self.kinds[a] == 'rows':
                return ins[a].at[pl.ds(j * ps[0], ps[0]), :]
            return ins[a].at[j]

        def remote(a, j):
            return pltpu.make_async_remote_copy(
                src_ref=piece(a, j), dst_ref=outs[a].at[me],
                send_sem=send_sems.at[_N_DEV * a + j], recv_sem=recv_sems.at[_N_DEV * a + me],
                device_id=(j >> 2, (j >> 1) & 1, j & 1), device_id_type=_MESH)

        def arrival(a, s):
            return pltpu.make_async_remote_copy(
                src_ref=piece(a, s), dst_ref=outs[a].at[s],
                send_sem=send_sems.at[_N_DEV * a + s], recv_sem=recv_sems.at[_N_DEV * a + s],
                device_id=(s >> 2, (s >> 1) & 1, s & 1), device_id_type=_MESH)

        def local(a, j):
            return pltpu.make_async_copy(piece(a, j), outs[a].at[j], local_sems.at[a])

        return me, remote, arrival, local

    def start(self, ins, outs, sems):
        me, remote, _, local = self._copies(ins, outs, sems)
        for j in range(_N_DEV):
            for a in range(len(self.arrs)):
                @pl.when(me != j)
                def _(a=a, j=j):
                    remote(a, j).start()

                @pl.when(me == j)
                def _(a=a, j=j):
                    local(a, j).start()

    def finish(self, ins, outs, sems):
        me, remote, arrival, local = self._copies(ins, outs, sems)
        for j in range(_N_DEV):
            for a in range(len(self.arrs)):
                @pl.when(me != j)
                def _(a=a, j=j):
                    arrival(a, j).wait_recv()
                    remote(a, j).wait_send()

                @pl.when(me == j)
                def _(a=a, j=j):
                    local(a, j).wait()

    def forward(self, ins, outs, sems):
        pass

    def begin(self, i, steps, ins, outs, sems):
        @pl.when(i == 0)
        def _():
            self.start(ins, outs, sems)

    def end(self, i, steps, ins, outs, sems):
        @pl.when(i == steps - 1)
        def _():
            self.finish(ins, outs, sems)


def _ride(main, plan, name, grid, in_specs, args, out_specs, out_shape, scratch_shapes):
    grid = (grid,) if isinstance(grid, int) else tuple(grid)
    steps = math.prod(grid)
    n_in, n_out, n_sc = len(in_specs), len(out_specs), len(scratch_shapes)
    p_args = plan.operands() if plan else []
    p_out = plan.out_shape() if plan else []
    p_sc = plan.scratch() if plan else []

    def body(*refs):
        k = 0
        ins = refs[k:k + n_in]; k += n_in
        p_ins = refs[k:k + len(p_args)]; k += len(p_args)
        outs = refs[k:k + n_out]; k += n_out
        p_outs = refs[k:k + len(p_out)]; k += len(p_out)
        scr = refs[k:k + n_sc]; k += n_sc
        sems = refs[k:]
        i = pl.program_id(0)
        for axis in range(1, len(grid)):
            i = i * grid[axis] + pl.program_id(axis)
        if plan:
            plan.begin(i, steps, p_ins, p_outs, sems)
        main(*ins, *outs, *scr)
        if plan:
            plan.end(i, steps, p_ins, p_outs, sems)

    res = pl.pallas_call(
        body, name=name, grid=grid,
        in_specs=list(in_specs) + [_ANY] * len(p_args),
        out_specs=list(out_specs) + [_ANY] * len(p_out),
        out_shape=list(out_shape) + p_out,
        scratch_shapes=list(scratch_shapes) + p_sc,
        compiler_params=_params())(*args, *p_args)
    return res[:n_out], res[n_out:]


def _run_plan(plan, name):
    n_args, n_out = len(plan.operands()), len(plan.out_shape())

    def body(*refs):
        ins, outs, sems = refs[:n_args], refs[n_args:n_args + n_out], refs[n_args + n_out:]
        plan.start(ins, outs, sems)
        plan.forward(ins, outs, sems)
        plan.finish(ins, outs, sems)

    return pl.pallas_call(
        body, name=name, in_specs=[_ANY] * n_args, out_specs=[_ANY] * n_out,
        out_shape=plan.out_shape(), scratch_shapes=plan.scratch())(*plan.operands())


def _perm(a, tm):
    T, C = a.shape
    return a.reshape(T // tm, 8, tm // 8, C).transpose(0, 2, 1, 3).reshape(T, C)


def _unperm(a, tm):
    T, C = a.shape
    return a.reshape(T // tm, tm // 8, 8, C).transpose(0, 2, 1, 3).reshape(T, C)


def _wrap_prev(prev_z, z):
    n = z.shape[0]
    sub = lax.broadcasted_iota(jnp.int32, z.shape, 0) & 7
    return jnp.where(sub == 0, pltpu.roll(prev_z, n - 7, 0), pltpu.roll(z, 1, 0))


def _wrap_next(next_z, z):
    n = z.shape[0]
    sub = lax.broadcasted_iota(jnp.int32, z.shape, 0) & 7
    return jnp.where(sub == 7, pltpu.roll(next_z, 7, 0), pltpu.roll(z, n - 1, 0))


def _fill_causal(buf, halo, x, tm):
    nh = halo.shape[0]
    buf[nh:nh + tm, :] = x
    z = buf[tm:tm + nh, :]
    buf[0:nh, :] = _wrap_prev(halo[...], z)
    halo[...] = z


def _fill_anticausal(buf, halo, dy, tm):
    nh = halo.shape[0]
    buf[0:tm, :] = dy
    z = buf[0:nh, :]
    buf[tm:tm + nh, :] = _wrap_next(halo[...], z)
    halo[...] = z


def _scan_fwd(a, u, abuf, ubuf, hcar, tm):
    D = a.shape[1]
    G = tm // 8
    abuf[...] = a
    ubuf[...] = u

    def step(j, c):
        h, pr = c
        r = pl.multiple_of(j * 8, 8)
        aj = abuf[pl.ds(r, 8), :]
        h = aj * h + ubuf[pl.ds(r, 8), :]
        pr = aj * pr
        ubuf[pl.ds(r, 8), :] = h
        abuf[pl.ds(r, 8), :] = pr
        return h, pr

    sub8 = lax.broadcasted_iota(jnp.int32, (8, D), 0)
    e, q = lax.fori_loop(1, G, step, (ubuf[0:8, :], abuf[0:8, :]))
    for s in (1, 2, 4):
        e = e + q * _shift_down(e, s, 0.0, sub8)
        q = q * _shift_down(q, s, 1.0, sub8)
    e = e + q * hcar[...]
    cin = jnp.where(sub8 == 0, hcar[...], pltpu.roll(e, 1, 0))
    return ubuf[...] + abuf[...] * jnp.tile(cin, (G, 1))


def _scan_bwd(a, d, abuf, gbuf, gcar, tmp8, tm):
    D = a.shape[1]
    G = tm // 8
    abuf[...] = a
    gbuf[...] = d

    def step(k, c):
        g_next, a_next, r_j = c
        r = pl.multiple_of((G - 1 - k) * 8, 8)
        aj = abuf[pl.ds(r, 8), :]
        g = gbuf[pl.ds(r, 8), :] + a_next * g_next
        gbuf[pl.ds(r, 8), :] = g
        abuf[pl.ds(r, 8), :] = r_j
        return g, aj, aj * r_j

    last = 8 * (G - 1)
    a_last = abuf[last:last + 8, :]
    abuf[last:last + 8, :] = jnp.ones((8, D), F32)
    g0, a0, _ = lax.fori_loop(1, G, step, (gbuf[last:last + 8, :], a_last, a_last))
    r0 = abuf[0:8, :]
    sub8 = lax.broadcasted_iota(jnp.int32, (8, D), 0)
    x, q = a0 * g0, a0 * r0
    for s in (1, 2, 4):
        x = x + q * _shift_up(x, s, 0.0, sub8)
        q = q * _shift_up(q, s, 1.0, sub8)
    x = x + q * gcar[...]
    cin = jnp.where(sub8 == 7, gcar[...], pltpu.roll(x, 7, 0))
    g = gbuf[...] + abuf[...] * jnp.tile(cin, (G, 1))
    tmp8[...] = x
    gcar[...] = tmp8[0:1, :]
    return g


def _mixers_fwd(h, p, win, l, name, plan=None):
    T, D = h.shape
    D3 = win.shape[0] * win.shape[2]
    Dc = D // 2
    heads, hd = p['lru_wa'].shape[1], p['lru_wa'].shape[2]
    tm = _TM_SEQ
    kl, kc = p['lru_conv_w'].shape[1], p['conv_w'].shape[1]
    nhl, nhc = 8 * (kl - 1), 8 * (kc - 1)
    assert nhc <= tm

    def body(h_ref, gz_ref, wl_ref, bl_ref, wa_ref, ba_ref, wx_ref, bx_ref, lam_ref, wc_ref, bc_ref, lg_ref, lb_ref,
             win_hbm, z_ref, xc_ref, hs_ref, ya_ref, cc_ref, yb_ref,
             xbuf, xhalo, hcar, cbuf, chalo, abuf, ubuf, win_ref, wsem):
        _load_weights(pl.program_id(0), _shard_pairs(win_hbm, win_ref), wsem)

        @pl.when(pl.program_id(0) == 0)
        def _():
            xhalo[...] = jnp.zeros_like(xhalo)
            chalo[...] = jnp.zeros_like(chalo)
            hcar[...] = jnp.zeros_like(hcar)

        z = _rms_fwd(h_ref[...], gz_ref[...]).astype(_MXU)
        z_ref[...] = z

        def proj(lo, hi):
            return jnp.dot(z, win_ref[:, lo:hi], preferred_element_type=F32)

        _fill_causal(xbuf, xhalo, proj(0, D), tm)
        xc = bl_ref[...] + wl_ref[0:1, :] * xbuf[0:tm, :]
        for k in range(1, kl):
            xc = xc + wl_ref[k:k + 1, :] * xbuf[8 * k:8 * k + tm, :]
        xc_ref[...] = xc
        _, ra, ri, sp, a, mult = _gates(xc, wa_ref, ba_ref[...], wx_ref, bx_ref[...], lam_ref[...], heads, hd)
        hs = _scan_fwd(a, mult * (ri * xc), abuf, ubuf, hcar, tm)
        hs_ref[...] = hs
        hcar[...] = hs_ref[pl.ds(tm - 1, 1), :]
        gg, _ = _gelu(proj(D, 2 * D))
        ya_ref[...] = hs * gg

        _fill_causal(cbuf, chalo, proj(2 * D, 2 * D + Dc) * jax.nn.sigmoid(proj(2 * D + Dc, 3 * D)), tm)
        cc = bc_ref[...] + wc_ref[0:1, :] * cbuf[0:tm, :]
        for k in range(1, kc):
            cc = cc + wc_ref[k:k + 1, :] * cbuf[8 * k:8 * k + tm, :]
        cc_ref[...] = cc
        nrm, _ = _group_norm(cc, CONV_GROUPS)
        cl = nrm * lg_ref[...] + lb_ref[...]
        yb_ref[...] = cl * jax.nn.sigmoid(cl)

    row_d = pl.BlockSpec((tm, D), lambda i: (i, 0))
    row_c = pl.BlockSpec((tm, Dc), lambda i: (i, 0))
    return _ride(
        body, plan, name, T // tm,
        in_specs=[row_d, _lspec(l, 1, D),
                  _lspec(l, kl, D), _lspec(l, 1, D), _lspec(l, heads, hd, hd), _lspec(l, 1, D),
                  _lspec(l, heads, hd, hd), _lspec(l, 1, D), _lspec(l, 1, D),
                  _lspec(l, kc, Dc), _lspec(l, 1, Dc), _lspec(l, 1, Dc), _lspec(l, 1, Dc), _ANY],
        args=(h, p['g_pre_mix'], p['lru_conv_w'], p['lru_conv_b'], p['lru_wa'], p['lru_ba'], p['lru_wx'], p['lru_bx'],
              p['lru_lambda'], p['conv_w'], p['conv_b'], p['conv_ln_g'], p['conv_ln_b'], win),
        out_specs=[row_d, row_d, row_d, row_d, row_c, row_c],
        out_shape=[_sds((T, D), _MXU), _sds((T, D), F32), _sds((T, D), F32), _sds((T, D), F32),
                   _sds((T, Dc), F32), _sds((T, Dc), F32)],
        scratch_shapes=[pltpu.VMEM((nhl + tm, D), F32), pltpu.VMEM((nhl, D), F32), pltpu.VMEM((1, D), F32),
                        pltpu.VMEM((nhc + tm, Dc), F32), pltpu.VMEM((nhc, Dc), F32),
                        pltpu.VMEM((tm, D), F32), pltpu.VMEM((tm, D), F32),
                        pltpu.VMEM((D, D3), _MXU), pltpu.SemaphoreType.DMA((_N_DEV,))])


def _mix_out(ya, yb, h, p, wout, l, name):
    T, D = ya.shape
    Dc = yb.shape[1]
    tm = _TM_SEQ

    def body(ya_ref, yb_ref, h_ref, gl_ref, gc_ref, gp_ref, w_ref, y_ref, o_ref, hm_ref):
        y = jnp.concatenate([_rms_fwd(ya_ref[...], gl_ref[...]), _rms_fwd(yb_ref[...], gc_ref[...])],
                            axis=-1).astype(_MXU)
        y_ref[...] = y
        o = jnp.dot(y, w_ref[...], preferred_element_type=F32)
        o_ref[...] = o
        hm_ref[...] = h_ref[...] + _rms_fwd(o, gp_ref[...])

    row_d = pl.BlockSpec((tm, D), lambda i: (i, 0))
    return pl.pallas_call(
        body, name=name, grid=(T // tm,),
        in_specs=[row_d, pl.BlockSpec((tm, Dc), lambda i: (i, 0)), row_d,
                  _lspec(l, 1, D), _lspec(l, 1, Dc), _lspec(l, 1, D), pl.BlockSpec((D + Dc, D), lambda i: (0, 0))],
        out_specs=[pl.BlockSpec((tm, D + Dc), lambda i: (i, 0)), row_d, row_d],
        out_shape=[_sds((T, D + Dc), _MXU), _sds((T, D), F32), _sds((T, D), F32)],
        compiler_params=_params(),
    )(ya, yb, h, p['g_out_lru'], p['g_out_conv'], p['g_post_mix'], wout)


def _shard_pairs(w_hbm, w_ref):
    n = w_hbm.shape[2]
    return [(w_hbm.at[j], w_ref.at[:, pl.ds(j * n, n)]) for j in range(w_hbm.shape[0])]


def _load_weights(i, pairs, sems):
    @pl.when(i == 0)
    def _():
        copies = [pltpu.make_async_copy(src, dst, sems.at[k]) for k, (src, dst) in enumerate(pairs)]
        for cp in copies:
            cp.start()
        for cp in copies:
            cp.wait()


def _up_into(ubuf, z, wup_v, nh, tm, cw):
    for c in range(wup_v.shape[1] // cw):
        cs = slice(c * cw, (c + 1) * cw)
        ubuf[nh:nh + tm, cs] = jnp.dot(z, wup_v[:, cs], preferred_element_type=F32)


def _ffn_fwd(hmid, p, wup, wdown, l, name, plan=None):
    T, D = hmid.shape
    F2 = wup.shape[0] * wup.shape[2]
    Fh = F2 // 2
    tm = _TM_SEQ
    cw = _pick(Fh, _NB)
    kf = p['ffn_conv_w'].shape[1]
    nh = 8 * (kf - 1)

    def body(hm_ref, gz_ref, wf_ref, bf_ref, g_ref, wu_hbm, wd_hbm, z_ref, a2_ref, f_ref, ho_ref,
             ubuf, uhalo, wu_ref, wd_ref, wsem):
        i = pl.program_id(0)
        _load_weights(i, _shard_pairs(wu_hbm, wu_ref) + [(wd_hbm, wd_ref)], wsem)

        @pl.when(i == 0)
        def _():
            uhalo[...] = jnp.zeros_like(uhalo)

        z = _rms_fwd(hm_ref[...], gz_ref[...]).astype(_MXU)
        z_ref[...] = z
        _up_into(ubuf, z, wu_ref, nh, tm, cw)
        tail = ubuf[tm:tm + nh, :]
        ubuf[0:nh, :] = _wrap_prev(uhalo[...], tail)
        uhalo[...] = tail

        def conv(cs):
            acc = bf_ref[:, cs]
            for k in range(kf):
                acc = acc + wf_ref[k:k + 1, cs] * ubuf[8 * k:8 * k + tm, cs]
            return acc

        f = None
        for c in range(Fh // cw):
            gs = slice(c * cw, (c + 1) * cw)
            gg, _ = _gelu(conv(gs))
            a2 = (gg * conv(slice(Fh + c * cw, Fh + (c + 1) * cw))).astype(_MXU)
            a2_ref[:, gs] = a2
            part = jnp.dot(a2, wd_ref[gs, :], preferred_element_type=F32)
            f = part if f is None else f + part
        f_ref[...] = f
        ho_ref[...] = hm_ref[...] + _rms_fwd(f, g_ref[...])

    row_d = pl.BlockSpec((tm, D), lambda i: (i, 0))
    return _ride(
        body, plan, name, T // tm,
        in_specs=[row_d, _lspec(l, 1, D), _lspec(l, kf, F2), _lspec(l, 1, F2), _lspec(l, 1, D), _ANY, _ANY],
        args=(hmid, p['g_pre_ffn'], p['ffn_conv_w'], p['ffn_conv_b'], p['g_post_ffn'], wup, wdown),
        out_specs=[row_d, pl.BlockSpec((tm, Fh), lambda i: (i, 0)), row_d, row_d],
        out_shape=[_sds((T, D), _MXU), _sds((T, Fh), _MXU), _sds((T, D), F32), _sds((T, D), F32)],
        scratch_shapes=[pltpu.VMEM((nh + tm, F2), F32), pltpu.VMEM((nh, F2), F32),
                        pltpu.VMEM((D, F2), _MXU), pltpu.VMEM((Fh, D), _MXU), pltpu.SemaphoreType.DMA((_N_DEV + 1,))])


def _loss_head(h, tgt, n_real, name):
    T, D = h.shape
    tm = _TM_SEQ

    def body(h_ref, t_ref, loss_ref, dh_ref):
        i = pl.program_id(0)

        @pl.when(i == 0)
        def _():
            loss_ref[...] = jnp.zeros_like(loss_ref)

        pos = lax.broadcasted_iota(jnp.int32, (tm, D), 0)
        row = i * tm + (pos & 7) * (tm // 8) + (pos >> 3)
        e = jnp.where((row >= N_META) & (row < N_META + n_real), h_ref[...] - t_ref[...], 0.0)
        dh_ref[...] = e * (1.0 / D)
        loss_ref[...] += 0.5 * jnp.sum(jnp.mean(e * e, axis=-1, keepdims=True), axis=0, keepdims=True)

    row_d = pl.BlockSpec((tm, D), lambda i: (i, 0))
    return pl.pallas_call(
        body, name=name, grid=(T // tm,),
        in_specs=[row_d, row_d],
        out_specs=[pl.BlockSpec((8, _LANES), lambda i: (0, 0)), row_d],
        out_shape=[_sds((8, _LANES), F32), _sds((T, D), F32)],
        compiler_params=_params())(h, tgt)


def _ffn_bwd(dh, f, z2, hmid, p, wup, wdown, l, name, plan=None):
    T, D = dh.shape
    F2 = wup.shape[0] * wup.shape[2]
    Fh = F2 // 2
    tm = _TM_SEQ
    nt = T // tm
    cw = _pick(Fh, _NB)
    kf = p['ffn_conv_w'].shape[1]
    nh = 8 * (kf - 1)
    assert tm % nh == 0

    def body(dh_ref, f_ref, z_ref, zp_ref, hm_ref, wf_ref, bf_ref, g_ref, gz_ref, wu_hbm, wd_hbm,
             df_ref, dup_ref, dwf_ref, dbf_ref, dg_ref, dhm_ref, dgz_ref,
             ubuf, dbuf, dhalo, wu_ref, wd_ref, wsem):
        i = pl.program_id(0)
        r = nt - 1 - i
        _load_weights(i, _shard_pairs(wu_hbm, wu_ref) + [(wd_hbm, wd_ref)], wsem)

        @pl.when(i == 0)
        def _():
            for ref in (dhalo, dwf_ref, dbf_ref, dg_ref, dgz_ref):
                ref[...] = jnp.zeros_like(ref)

        df, dg = _rms_bwd(f_ref[...], g_ref[...], dh_ref[...])
        dg_ref[...] += dg
        dfb = df.astype(_MXU)
        df_ref[...] = dfb
        zcat = jnp.concatenate([zp_ref[...], z_ref[...]], axis=0)

        def conv(cs):
            ubuf[:, cs] = jnp.dot(zcat, wu_ref[:, cs], preferred_element_type=F32)
            ubuf[0:nh, cs] = _wrap_prev(jnp.where(r == 0, 0.0, ubuf[0:nh, cs]), ubuf[tm:tm + nh, cs])
            acc = bf_ref[:, cs]
            for k in range(kf):
                acc = acc + wf_ref[k:k + 1, cs] * ubuf[8 * k:8 * k + tm, cs]
            return acc

        for c in range(Fh // cw):
            gs = slice(c * cw, (c + 1) * cw)
            us = slice(Fh + c * cw, Fh + (c + 1) * cw)
            ug = conv(gs)
            gg, t = _gelu(ug)
            da2 = lax.dot_general(dfb, wd_ref[gs, :], (((1,), (1,)), ((), ())), preferred_element_type=F32)
            dbuf[0:tm, gs] = da2 * conv(us) * _gelu_grad(ug, t)
            dbuf[0:tm, us] = da2 * gg
        z = dbuf[0:nh, :]
        dbuf[tm:tm + nh, :] = _wrap_next(dhalo[...], z)
        dhalo[...] = z
        dz = None
        for c in range(F2 // cw):
            cs = slice(c * cw, (c + 1) * cw)
            upc = ubuf[nh:nh + tm, cs]
            dup = None
            for k in range(kf):
                dsh = dbuf[8 * (kf - 1 - k):8 * (kf - 1 - k) + tm, cs]
                term = wf_ref[k:k + 1, cs] * dsh
                dup = term if dup is None else dup + term
                dwf_ref[k:k + 1, cs] += jnp.sum(dsh * upc, axis=0, keepdims=True)
            dbf_ref[:, cs] += jnp.sum(dbuf[0:tm, cs], axis=0, keepdims=True)
            dupb = dup.astype(_MXU)
            dup_ref[:, cs] = dupb
            part = lax.dot_general(dupb, wu_ref[:, cs], (((1,), (1,)), ((), ())), preferred_element_type=F32)
            dz = part if dz is None else dz + part
        dx, dgz = _rms_bwd(hm_ref[...], gz_ref[...], dz)
        dhm_ref[...] = dh_ref[...] + dx
        dgz_ref[...] += dgz

    rev_d = pl.BlockSpec((tm, D), lambda i: (nt - 1 - i, 0))
    rev_f = pl.BlockSpec((tm, F2), lambda i: (nt - 1 - i, 0))
    prev = pl.BlockSpec((nh, D), lambda i: (jnp.maximum((nt - 1 - i) * (tm // nh) - 1, 0), 0))
    full = lambda *s: pl.BlockSpec(s, lambda i: (0,) * len(s))
    return _ride(
        body, plan, name, nt,
        in_specs=[rev_d, rev_d, rev_d, prev, rev_d, _lspec(l, kf, F2), _lspec(l, 1, F2), _lspec(l, 1, D),
                  _lspec(l, 1, D), _ANY, _ANY],
        args=(dh, f, z2, z2, hmid, p['ffn_conv_w'], p['ffn_conv_b'], p['g_post_ffn'], p['g_pre_ffn'], wup, wdown),
        out_specs=[rev_d, rev_f, full(kf, F2), full(1, F2), full(1, D), rev_d, full(1, D)],
        out_shape=[_sds((T, D), _MXU), _sds((T, F2), _MXU), _sds((kf, F2), F32), _sds((1, F2), F32), _sds((1, D), F32),
                   _sds((T, D), F32), _sds((1, D), F32)],
        scratch_shapes=[pltpu.VMEM((nh + tm, F2), F32), pltpu.VMEM((tm + nh, F2), F32), pltpu.VMEM((nh, F2), F32),
                        pltpu.VMEM((D, F2), _MXU), pltpu.VMEM((Fh, D), _MXU), pltpu.SemaphoreType.DMA((_N_DEV + 1,))])


def _mm_tn(xs, dy, name):
    T, K = xs.shape
    N = dy.shape[1]
    tm = _TM_DW
    kb = K if K <= 1024 else _pick(K, _NB)
    nb = _pick(N, _NB)

    def body(x_ref, dy_ref, o_ref):
        @pl.when(pl.program_id(2) == 0)
        def _():
            o_ref[...] = jnp.zeros_like(o_ref)

        o_ref[...] += lax.dot_general(x_ref[...], dy_ref[...], (((0,), (0,)), ((), ())), preferred_element_type=F32)

    return pl.pallas_call(
        body, name=name, grid=(K // kb, N // nb, T // tm),
        in_specs=[pl.BlockSpec((tm, kb), lambda a, b, t: (t, a)), pl.BlockSpec((tm, nb), lambda a, b, t: (t, b))],
        out_specs=pl.BlockSpec((kb, nb), lambda a, b, t: (a, b)),
        out_shape=_sds((K, N), F32),
        compiler_params=_params())(xs, dy)


def _mix_bwd(dhm, o, ya, yb, p, wout, l, name):
    T, D = ya.shape
    Dc = yb.shape[1]
    tm = _TM_SEQ

    def body(dh_ref, o_ref, ya_ref, yb_ref, gp_ref, gl_ref, gc_ref, w_ref,
             do_ref, dya_ref, dyb_ref, dgp_ref, dgl_ref, dgc_ref):
        @pl.when(pl.program_id(0) == 0)
        def _():
            dgp_ref[...] = jnp.zeros_like(dgp_ref)
            dgl_ref[...] = jnp.zeros_like(dgl_ref)
            dgc_ref[...] = jnp.zeros_like(dgc_ref)

        do, dgp = _rms_bwd(o_ref[...], gp_ref[...], dh_ref[...])
        dgp_ref[...] += dgp
        dob = do.astype(_MXU)
        do_ref[...] = dob
        dy = lax.dot_general(dob, w_ref[...], (((1,), (1,)), ((), ())), preferred_element_type=F32)
        dya, dgl = _rms_bwd(ya_ref[...], gl_ref[...], dy[:, 0:D])
        dyb, dgc = _rms_bwd(yb_ref[...], gc_ref[...], dy[:, D:D + Dc])
        dya_ref[...] = dya
        dyb_ref[...] = dyb
        dgl_ref[...] += dgl
        dgc_ref[...] += dgc

    row_d = pl.BlockSpec((tm, D), lambda i: (i, 0))
    row_c = pl.BlockSpec((tm, Dc), lambda i: (i, 0))
    full = lambda *s: pl.BlockSpec(s, lambda i: (0,) * len(s))
    return pl.pallas_call(
        body, name=name, grid=(T // tm,),
        in_specs=[row_d, row_d, row_d, row_c, _lspec(l, 1, D), _lspec(l, 1, D), _lspec(l, 1, Dc),
                  pl.BlockSpec((D + Dc, D), lambda i: (0, 0))],
        out_specs=[row_d, row_d, row_c, full(1, D), full(1, D), full(1, Dc)],
        out_shape=[_sds((T, D), _MXU), _sds((T, D), F32), _sds((T, Dc), F32),
                   _sds((1, D), F32), _sds((1, D), F32), _sds((1, Dc), F32)],
        compiler_params=_params(),
    )(dhm, o, ya, yb, p['g_post_mix'], p['g_out_lru'], p['g_out_conv'], wout)


def _mixers_bwd(dya, dyb, z1, xc, hs, cc, h, dres, p, win, l, name, plan=None):
    T, D = z1.shape
    D3 = win.shape[0] * win.shape[2]
    Dc = D // 2
    heads, hd = p['lru_wa'].shape[1], p['lru_wa'].shape[2]
    tm = _TM_SEQ
    nt = T // tm
    per8 = tm // 8
    kl, kc = p['lru_conv_w'].shape[1], p['conv_w'].shape[1]
    nhl, nhc = 8 * (kl - 1), 8 * (kc - 1)
    assert nhc <= tm

    def body(dya_ref, dyb_ref, z_ref, xc_ref, hs_ref, hsp_ref, cc_ref, h_ref, dres_ref,
             wl_ref, wa_ref, ba_ref, wx_ref, bx_ref, lam_ref, wc_ref, lg_ref, lb_ref, gz_ref, win_hbm,
             dproj_ref, dwl_ref, dbl_ref, dwa_ref, dba_ref, dwx_ref, dbx_ref, dlam_ref,
             dwc_ref, dbc_ref, dlg_ref, dlb_ref, dh_ref, dgz_ref,
             gcar, dxbuf, dxhalo, dcbuf, dchalo, abuf, gbuf, hbuf, tmp8, win_ref, wsem):
        i = pl.program_id(0)
        r = nt - 1 - i
        _load_weights(i, _shard_pairs(win_hbm, win_ref), wsem)

        @pl.when(i == 0)
        def _():
            for ref in (gcar, dxhalo, dchalo, dwl_ref, dbl_ref, dwa_ref, dba_ref, dwx_ref, dbx_ref, dlam_ref,
                        dwc_ref, dbc_ref, dlg_ref, dlb_ref, dgz_ref):
                ref[...] = jnp.zeros_like(ref)

        zb = z_ref[...]

        def proj(lo, hi):
            return jnp.dot(zb, win_ref[:, lo:hi], preferred_element_type=F32)

        xl, gl, ca, cb = proj(0, D), proj(D, 2 * D), proj(2 * D, 2 * D + Dc), proj(2 * D + Dc, 3 * D)

        dya_v = dya_ref[...]
        hs = hs_ref[...]
        gg, tg = _gelu(gl)
        dproj_ref[:, D:2 * D] = (dya_v * hs * _gelu_grad(gl, tg)).astype(_MXU)
        dhs = dya_v * gg
        xc = xc_ref[...]
        lam = lam_ref[...]
        xcb, ra, ri, sp, a, mult = _gates(xc, wa_ref, ba_ref[...], wx_ref, bx_ref[...], lam, heads, hd)
        g = _scan_bwd(a, dhs, abuf, gbuf, gcar, tmp8, tm)
        sub8 = lax.broadcasted_iota(jnp.int32, (8, D), 0)
        hbuf[8:8 + tm, :] = hs
        hbuf[0:8, :] = jnp.where(sub8 == 0, jnp.where(r == 0, 0.0, hsp_ref[7:8, :]),
                                 pltpu.roll(hs_ref[tm - 8:tm, :], 1, 0))
        da = g * hbuf[0:tm, :]
        gx = g * xc
        dxc = g * mult * ri
        dla = da * a - (gx * ri) * (a * a) / mult
        dlam_ref[...] += jnp.sum(dla * ra, axis=0, keepdims=True) * (LRU_C * jax.nn.sigmoid(-lam))
        dpa = (dla * ((-LRU_C) * sp)) * ra * (1.0 - ra)
        dpx = (gx * mult) * ri * (1.0 - ri)
        dba_ref[...] += jnp.sum(dpa, axis=0, keepdims=True)
        dbx_ref[...] += jnp.sum(dpx, axis=0, keepdims=True)
        dpab, dpxb = dpa.astype(_MXU), dpx.astype(_MXU)
        for h in range(heads):
            hsl = slice(h * hd, (h + 1) * hd)
            dwa_ref[h] += lax.dot_general(xcb[:, hsl], dpab[:, hsl], (((0,), (0,)), ((), ())),
                                          preferred_element_type=F32)
            dwx_ref[h] += lax.dot_general(xcb[:, hsl], dpxb[:, hsl], (((0,), (0,)), ((), ())),
                                          preferred_element_type=F32)
        dxc = dxc + _bd_mm_t(dpab, wa_ref, heads, hd) + _bd_mm_t(dpxb, wx_ref, heads, hd)
        dbl_ref[...] += jnp.sum(dxc, axis=0, keepdims=True)
        _fill_anticausal(dxbuf, dxhalo, dxc, tm)
        dxl = None
        for k in range(kl):
            dsh = dxbuf[8 * (kl - 1 - k):8 * (kl - 1 - k) + tm, :]
            term = wl_ref[k:k + 1, :] * dsh
            dxl = term if dxl is None else dxl + term
            dwl_ref[k:k + 1, :] += jnp.sum(dsh * xl, axis=0, keepdims=True)
        dproj_ref[:, 0:D] = dxl.astype(_MXU)

        sg = jax.nn.sigmoid(cb)
        cg = ca * sg
        nrm, rss = _group_norm(cc_ref[...], CONV_GROUPS)
        lg = lg_ref[...]
        cl = nrm * lg + lb_ref[...]
        sc = jax.nn.sigmoid(cl)
        dcl = dyb_ref[...] * (sc * (1.0 + cl * (1.0 - sc)))
        dlg_ref[...] += jnp.sum(dcl * nrm, axis=0, keepdims=True)
        dlb_ref[...] += jnp.sum(dcl, axis=0, keepdims=True)
        dnrm = dcl * lg
        gsz = Dc // CONV_GROUPS
        parts = []
        for gi in range(CONV_GROUPS):
            sl = slice(gi * gsz, (gi + 1) * gsz)
            dn, nn = dnrm[:, sl], nrm[:, sl]
            parts.append(rss[gi] * (dn - jnp.mean(dn, axis=-1, keepdims=True)
                                    - nn * jnp.mean(dn * nn, axis=-1, keepdims=True)))
        dcc = jnp.concatenate(parts, axis=-1)
        dbc_ref[...] += jnp.sum(dcc, axis=0, keepdims=True)
        _fill_anticausal(dcbuf, dchalo, dcc, tm)
        dcg = None
        for k in range(kc):
            dsh = dcbuf[8 * (kc - 1 - k):8 * (kc - 1 - k) + tm, :]
            term = wc_ref[k:k + 1, :] * dsh
            dcg = term if dcg is None else dcg + term
            dwc_ref[k:k + 1, :] += jnp.sum(dsh * cg, axis=0, keepdims=True)
        dproj_ref[:, 2 * D:2 * D + Dc] = (dcg * sg).astype(_MXU)
        dproj_ref[:, 2 * D + Dc:3 * D] = (dcg * ca * sg * (1.0 - sg)).astype(_MXU)

        cwp = _pick(D3, _NB)
        dz = None
        for c in range(D3 // cwp):
            cs = slice(c * cwp, (c + 1) * cwp)
            part = lax.dot_general(dproj_ref[:, cs], win_ref[:, cs], (((1,), (1,)), ((), ())),
                                   preferred_element_type=F32)
            dz = part if dz is None else dz + part
        dx, dgz = _rms_bwd(h_ref[...], gz_ref[...], dz)
        dh_ref[...] = dres_ref[...] + dx
        dgz_ref[...] += dgz

    rev_d = pl.BlockSpec((tm, D), lambda i: (nt - 1 - i, 0))
    rev_c = pl.BlockSpec((tm, Dc), lambda i: (nt - 1 - i, 0))
    rev_p = pl.BlockSpec((tm, D3), lambda i: (nt - 1 - i, 0))
    prev8 = pl.BlockSpec((8, D), lambda i: (jnp.maximum((nt - 1 - i) * per8 - 1, 0), 0))
    full = lambda *s: pl.BlockSpec(s, lambda i: (0,) * len(s))
    return _ride(
        body, plan, name, nt,
        in_specs=[rev_d, rev_c, rev_d, rev_d, rev_d, prev8, rev_c, rev_d, rev_d,
                  _lspec(l, kl, D), _lspec(l, heads, hd, hd), _lspec(l, 1, D), _lspec(l, heads, hd, hd),
                  _lspec(l, 1, D), _lspec(l, 1, D), _lspec(l, kc, Dc), _lspec(l, 1, Dc), _lspec(l, 1, Dc),
                  _lspec(l, 1, D), _ANY],
        args=(dya, dyb, z1, xc, hs, hs, cc, h, dres, p['lru_conv_w'], p['lru_wa'], p['lru_ba'], p['lru_wx'],
              p['lru_bx'], p['lru_lambda'], p['conv_w'], p['conv_ln_g'], p['conv_ln_b'], p['g_pre_mix'], win),
        out_specs=[rev_p, full(kl, D), full(1, D), full(heads, hd, hd), full(1, D), full(heads, hd, hd), full(1, D),
                   full(1, D), full(kc, Dc), full(1, Dc), full(1, Dc), full(1, Dc), rev_d, full(1, D)],
        out_shape=[_sds((T, D3), _MXU), _sds((kl, D), F32), _sds((1, D), F32), _sds((heads, hd, hd), F32),
                   _sds((1, D), F32), _sds((heads, hd, hd), F32), _sds((1, D), F32), _sds((1, D), F32),
                   _sds((kc, Dc), F32), _sds((1, Dc), F32), _sds((1, Dc), F32), _sds((1, Dc), F32),
                   _sds((T, D), F32), _sds((1, D), F32)],
        scratch_shapes=[pltpu.VMEM((1, D), F32), pltpu.VMEM((tm + nhl, D), F32), pltpu.VMEM((nhl, D), F32),
                        pltpu.VMEM((tm + nhc, Dc), F32), pltpu.VMEM((nhc, Dc), F32),
                        pltpu.VMEM((tm, D), F32), pltpu.VMEM((tm, D), F32), pltpu.VMEM((8 + tm, D), F32),
                        pltpu.VMEM((8, D), F32), pltpu.VMEM((D, D3), _MXU), pltpu.SemaphoreType.DMA((_N_DEV,))])


def _sum_sources(recv, name):
    _, R, C = recv.shape
    rb = _pick_rows(R, 1024)

    def body(r_ref, o_ref):
        acc = r_ref[0]
        for s in range(1, _N_DEV):
            acc = acc + r_ref[s]
        o_ref[...] = acc

    return pl.pallas_call(
        body, name=name, grid=(R // rb,),
        in_specs=[pl.BlockSpec((_N_DEV, rb, C), lambda i: (0, i, 0))],
        out_specs=pl.BlockSpec((rb, C), lambda i: (i, 0)),
        out_shape=_sds((R, C), F32), compiler_params=_params())(recv)


def _adamw(g, w, m, v, name):
    R, C = w.shape
    summed = g.ndim == 3
    rb = _pick_rows(R, max(8, min(512, _ADAM_BLOCK_ELEMS // C)))
    c1 = 1.0 - ADAM_B1 ** ADAM_STEP
    c2 = 1.0 - ADAM_B2 ** ADAM_STEP

    def body(g_ref, w_ref, m_ref, v_ref, go_ref, d_ref, mo_ref, vo_ref):
        if summed:
            gv = g_ref[0]
            for s in range(1, _N_DEV):
                gv = gv + g_ref[s]
        else:
            gv = g_ref[...]
        go_ref[...] = gv
        mn = ADAM_B1 * m_ref[...] + (1.0 - ADAM_B1) * gv
        vn = ADAM_B2 * v_ref[...] + (1.0 - ADAM_B2) * (gv * gv)
        mo_ref[...] = mn
        vo_ref[...] = vn
        d_ref[...] = (-ADAM_LR) * ((mn / c1) / (jnp.sqrt(vn / c2) + ADAM_EPS) + ADAM_WD * w_ref[...])

    blk = pl.BlockSpec((rb, C), lambda i: (i, 0))
    gspec = pl.BlockSpec((_N_DEV, rb, C), lambda i: (0, i, 0)) if summed else blk
    return pl.pallas_call(
        body, name=name, grid=(R // rb,),
        in_specs=[gspec, blk, blk, blk], out_specs=[blk, blk, blk, blk],
        out_shape=[_sds((R, C), F32)] * 4, compiler_params=_params())(g, w, m, v)


def _adamw_layers(recvs, w, m, v, name, plan=None):
    L, R, C = w.shape
    rb = _pick_rows(R, max(8, _ADAM_BLOCK_ELEMS // (4 * C)))
    c1 = 1.0 - ADAM_B1 ** ADAM_STEP
    c2 = 1.0 - ADAM_B2 ** ADAM_STEP

    def body(*refs):
        r_refs = refs[:L]
        w_ref, m_ref, v_ref, go_ref, d_ref, mo_ref, vo_ref = refs[L:]
        for l in range(L):
            gv = r_refs[l][0]
            for s in range(1, _N_DEV):
                gv = gv + r_refs[l][s]
            go_ref[l] = gv
            mn = ADAM_B1 * m_ref[l] + (1.0 - ADAM_B1) * gv
            vn = ADAM_B2 * v_ref[l] + (1.0 - ADAM_B2) * (gv * gv)
            mo_ref[l] = mn
            vo_ref[l] = vn
            d_ref[l] = (-ADAM_LR) * ((mn / c1) / (jnp.sqrt(vn / c2) + ADAM_EPS) + ADAM_WD * w_ref[l])

    blk = pl.BlockSpec((L, rb, C), lambda i: (0, i, 0))
    return _ride(
        body, plan, name, R // rb,
        in_specs=[pl.BlockSpec((_N_DEV, rb, C), lambda i: (0, i, 0))] * L + [blk, blk, blk],
        args=(*recvs, w, m, v),
        out_specs=[blk, blk, blk, blk],
        out_shape=[_sds((L, R, C), F32)] * 4, scratch_shapes=[])


def _pack_rows(flat_parts, dtype, row_mult):
    flat = jnp.concatenate([f.reshape(-1).astype(dtype) for f in flat_parts])
    n = flat.shape[0]
    per = _LANES * row_mult
    padded = -(-n // per) * per
    if padded != n:
        flat = jnp.concatenate([flat, jnp.zeros((padded - n,), dtype)])
    return flat.reshape(-1, _LANES)


def _unpack(flat, shapes):
    out, off = [], 0
    for s in shapes:
        n = math.prod(s)
        out.append(flat[off:off + n].reshape(s))
        off += n
    return out


def _to_pieces(full):
    n = full.shape[-1] // _N_DEV
    t = full.reshape(full.shape[:-1] + (_N_DEV, n))
    return jnp.moveaxis(t, -2, 0).reshape(_N_DEV, -1)


def _from_gathered(seg, shard_shape, axis):
    t = seg.reshape((_N_DEV,) + tuple(shard_shape))
    t = jnp.moveaxis(t, 0, axis)
    shape = list(shard_shape)
    shape[axis] *= _N_DEV
    return t.reshape(shape)


def kernel(x, meta_tokens, g_pre_mix, w_in, lru_conv_w, lru_conv_b, lru_wa, lru_ba, lru_wx, lru_bx, lru_lambda, conv_w, conv_b, conv_ln_g, conv_ln_b, g_out_lru, g_out_conv, w_out, g_post_mix, g_pre_ffn, w_up, ffn_conv_w, ffn_conv_b, w_down, g_post_ffn, loss_target, m_meta_tokens, m_g_pre_mix, m_w_in, m_lru_conv_w, m_lru_conv_b, m_lru_wa, m_lru_ba, m_lru_wx, m_lru_bx, m_lru_lambda, m_conv_w, m_conv_b, m_conv_ln_g, m_conv_ln_b, m_g_out_lru, m_g_out_conv, m_w_out, m_g_post_mix, m_g_pre_ffn, m_w_up, m_ffn_conv_w, m_ffn_conv_b, m_w_down, m_g_post_ffn, v_meta_tokens, v_g_pre_mix, v_w_in, v_lru_conv_w, v_lru_conv_b, v_lru_wa, v_lru_ba, v_lru_wx, v_lru_bx, v_lru_lambda, v_conv_w, v_conv_b, v_conv_ln_g, v_conv_ln_b, v_g_out_lru, v_g_out_conv, v_w_out, v_g_post_mix, v_g_pre_ffn, v_w_up, v_ffn_conv_w, v_ffn_conv_b, v_w_down, v_g_post_ffn):
    given = dict(locals())
    W = {n: given[n] for n in W_NAMES}
    M = {n: given['m_' + n] for n in W_NAMES}
    V = {n: given['v_' + n] for n in W_NAMES}
    S, D = x.shape[1], x.shape[2]
    L = g_pre_mix.shape[0]
    Dc = D // 2
    step = math.lcm(_TM_SEQ, _TM_DW)
    T = -(-(N_META + S) // step) * step

    first, rest = ['w_in'], ['w_out', 'w_up', 'w_down']

    def layer_pack(l, names):
        return [W[n][l].astype(_MXU) for n in names]

    def layer_weights(gathered, names):
        return {n: g if n in ('w_in', 'w_up') else g.reshape(-1, g.shape[2]) for n, g in zip(names, gathered)}

    small_pack = _pack_rows([W[n] for n in SMALL_SHARDED], F32, 8)
    *big_g, small_g = _run_plan(_GatherPlan(layer_pack(0, first) + [small_pack]), "gather_weights_first")
    small_segs = _unpack_cols(small_g.reshape(_N_DEV, -1), [W[n].shape for n in SMALL_SHARDED])
    full = {}
    for n, seg in zip(SMALL_SHARDED, small_segs):
        full[n] = _from_gathered(seg, W[n].shape, W[n].ndim - 1)

    p = {}
    for n in W_NAMES:
        if n in BIG or n == 'meta_tokens':
            continue
        a = full[n] if n in full else W[n]
        p[n] = a.reshape(L, 1, a.shape[1]) if a.ndim == 2 else a

    pad_rows = T - N_META - S
    h = _perm(jnp.concatenate([full['meta_tokens'], x[0], jnp.zeros((pad_rows, D), F32)], axis=0), _TM_SEQ)
    tgt = _perm(jnp.concatenate([jnp.zeros((N_META, D), F32), loss_target[0], jnp.zeros((pad_rows, D), F32)],
                                axis=0), _TM_SEQ)

    saved = []
    wl = layer_weights(big_g, first)
    for l in range(L):
        plan = _GatherPlan(layer_pack(0, rest)) if l == 0 else None
        (z1, xc, hs, ya, cc, yb), nxt = _mixers_fwd(h, p, wl['w_in'], l, f"mixers_fwd_l{l}", plan)
        if plan:
            wl = {**wl, **layer_weights(nxt, rest)}
        plan = _GatherPlan(layer_pack(l + 1, BIG)) if l + 1 < L else None
        y, o, hmid = _mix_out(ya, yb, h, p, wl['w_out'], l, f"mix_out_l{l}")
        (z2, a2, f, hout), nxt = _ffn_fwd(hmid, p, wl['w_up'], wl['w_down'], l, f"ffn_fwd_l{l}", plan)
        saved.append(dict(h=h, z1=z1, xc=xc, hs=hs, ya=ya, cc=cc, yb=yb, y=y, o=o, hmid=hmid,
                          z2=z2, a2=a2, f=f, w=wl))
        h = hout
        if plan:
            wl = layer_weights(nxt, BIG)
    loss_tile, dh = _loss_head(h, tgt, S, "loss_head")
    loss = lax.psum(loss_tile[0, 0], ("x", "y", "c"))

    small_g_names = ['g_pre_mix', 'lru_conv_w', 'lru_conv_b', 'lru_wa', 'lru_ba', 'lru_wx', 'lru_bx', 'lru_lambda',
                     'conv_w', 'conv_b', 'conv_ln_g', 'conv_ln_b', 'g_out_lru', 'g_out_conv', 'g_post_mix',
                     'g_pre_ffn', 'ffn_conv_w', 'ffn_conv_b', 'g_post_ffn']
    per_layer = {n: [None] * L for n in small_g_names}
    recv = {n: [None] * L for n in BIG}
    pending = None
    for l in reversed(range(L)):
        sv = saved[l]
        wl = sv['w']
        plan = _ExchangePlan(list(pending), ['cols', 'cols']) if pending else None
        (df, dup0, dwf, dbf, dgpf, dhm, dgpre), got = _ffn_bwd(
            dh, sv['f'], sv['z2'], sv['hmid'], p, wl['w_up'], wl['w_down'], l, f"ffn_bwd_l{l}", plan)
        if plan:
            recv['w_up'][l + 1], recv['w_in'][l + 1] = got
        d_down = _mm_tn(sv['a2'], df, f"dw_down_l{l}")
        d_up = _mm_tn(sv['z2'], dup0, f"dw_up_l{l}")
        do, dya, dyb, dgpm, dgol, dgoc = _mix_bwd(dhm, sv['o'], sv['ya'], sv['yb'], p, wl['w_out'], l,
                                                  f"mix_bwd_l{l}")
        d_out = _mm_tn(sv['y'], do, f"dw_out_l{l}")
        plan = (_ExchangePlan([d_down, d_out, d_up], ['rows', 'rows', 'cols']) if l == 0
                else _ExchangePlan([d_down, d_out], ['rows', 'rows']))
        (dproj, dwl, dbl, dwa, dba, dwx, dbx, dlam, dwc, dbc, dlg, dlb, dh, dgpmix), got = _mixers_bwd(
            dya, dyb, sv['z1'], sv['xc'], sv['hs'], sv['cc'], sv['h'], dhm, p, wl['w_in'], l,
            f"mixers_bwd_l{l}", plan)
        recv['w_down'][l], recv['w_out'][l] = got[0], got[1]
        if l == 0:
            recv['w_up'][0] = got[2]
        d_in = _mm_tn(sv['z1'], dproj, f"dw_in_l{l}")
        pending = (d_up, d_in)
        for n, val in (('g_pre_mix', dgpmix), ('lru_conv_w', dwl), ('lru_conv_b', dbl), ('lru_wa', dwa),
                       ('lru_ba', dba), ('lru_wx', dwx), ('lru_bx', dbx), ('lru_lambda', dlam), ('conv_w', dwc),
                       ('conv_b', dbc), ('conv_ln_g', dlg), ('conv_ln_b', dlb), ('g_out_lru', dgol),
                       ('g_out_conv', dgoc), ('g_post_mix', dgpm), ('g_pre_ffn', dgpre), ('ffn_conv_w', dwf),
                       ('ffn_conv_b', dbf), ('g_post_ffn', dgpf)):
            per_layer[n][l] = val
    dh = _unperm(dh, _TM_SEQ)
    grad_x = dh[N_META:N_META + S][None]
    partial = {n: jnp.stack(per_layer[n]).reshape((L,) + tuple(
        (full[n] if n in full else W[n]).shape[1:])) for n in small_g_names}
    partial['meta_tokens'] = dh[0:N_META]

    shard_pack = jnp.concatenate([_to_pieces(partial[n]) for n in SMALL_SHARDED], axis=1)
    n_sh = shard_pack.shape[1]
    rs = -(-n_sh // (8 * _LANES)) * 8
    shard_pack = jnp.concatenate([shard_pack, jnp.zeros((_N_DEV, rs * _LANES - n_sh), F32)], axis=1)
    rep_flat = jnp.concatenate([partial[n].reshape(-1) for n in REPLICATED])
    n_rep = rep_flat.shape[0]
    rr = -(-n_rep // (_N_DEV * _REP_ROWS * _LANES)) * _REP_ROWS
    rep_flat = jnp.concatenate([rep_flat, jnp.zeros((_N_DEV * rr * _LANES - n_rep,), F32)])
    small_send = jnp.concatenate([shard_pack, rep_flat.reshape(_N_DEV, rr * _LANES)], axis=1)
    small_send = small_send.reshape(_N_DEV, rs + rr, _LANES)
    out = {}
    res, got = _adamw_layers(recv['w_up'], W['w_up'], M['w_up'], V['w_up'], "adamw_w_up",
                             _ExchangePlan([pending[1], small_send], ['cols', 'slots']))
    out['w_up'] = list(res)
    recv['w_in'][0], r_small = got
    small_red = _sum_sources(r_small, "sum_small")
    (rep_g,) = _run_plan(_GatherPlan([small_red[rs:]]), "gather_replicated_grads")
    rep_g = rep_g.reshape(_N_DEV * rr, _LANES)
    for n in ('w_in', 'w_out', 'w_down'):
        out[n] = list(_adamw_layers(recv[n], W[n], M[n], V[n], f"adamw_{n}")[0])
    sh_shapes = [W[n].shape for n in SMALL_SHARDED]
    res = _adamw(small_red[:rs], _pack_rows([W[n] for n in SMALL_SHARDED], F32, 8),
                 _pack_rows([M[n] for n in SMALL_SHARDED], F32, 8),
                 _pack_rows([V[n] for n in SMALL_SHARDED], F32, 8), "adamw_small_sharded")
    for k in range(4):
        for n, val in zip(SMALL_SHARDED, _unpack(res[k].reshape(-1), sh_shapes)):
            out.setdefault(n, [None] * 4)[k] = val
    rep_shapes = [W[n].shape for n in REPLICATED]
    res = _adamw(rep_g, _pack_rows([W[n] for n in REPLICATED], F32, _REP_ROWS * _N_DEV),
                 _pack_rows([M[n] for n in REPLICATED], F32, _REP_ROWS * _N_DEV),
                 _pack_rows([V[n] for n in REPLICATED], F32, _REP_ROWS * _N_DEV), "adamw_replicated")
    for k in range(4):
        for n, val in zip(REPLICATED, _unpack(res[k].reshape(-1), rep_shapes)):
            out.setdefault(n, [None] * 4)[k] = val

    return (loss, grad_x, *[out[n][0] for n in W_NAMES], *[out[n][1] for n in W_NAMES],
            *[out[n][2] for n in W_NAMES], *[out[n][3] for n in W_NAMES])


def _unpack_cols(gathered, shapes):
    out, off = [], 0
    for s in shapes:
        n = math.prod(s)
        out.append(gathered[:, off:off + n])
        off += n
    return out
```

```python
import math

import jax
import jax.numpy as jnp
from jax import lax
from jax.experimental import pallas as pl
from jax.experimental.pallas import tpu as pltpu

F32 = jnp.float32
_MXU = jnp.bfloat16
_TM_DW = 2816
_TM_SEQ = 256
_NB = 768
_VMEM_LIMIT = 56 * 1024 * 1024
_ADAM_BLOCK_ELEMS = 128 * 1024
_LANES = 128
_N_DEV = 8
_REP_ROWS = 64

EPS = 1e-6
N_META = 16
LRU_C = 8.0
CONV_GROUPS = 4
ADAM_LR, ADAM_B1, ADAM_B2, ADAM_EPS, ADAM_WD, ADAM_STEP = 0.001, 0.9, 0.999, 1e-08, 0.01, 10
_GELU_K0 = math.sqrt(2.0 / math.pi)
_GELU_K1 = 0.044715

W_NAMES = ['meta_tokens', 'g_pre_mix', 'w_in', 'lru_conv_w', 'lru_conv_b', 'lru_wa', 'lru_ba', 'lru_wx', 'lru_bx',
           'lru_lambda', 'conv_w', 'conv_b', 'conv_ln_g', 'conv_ln_b', 'g_out_lru', 'g_out_conv', 'w_out',
           'g_post_mix', 'g_pre_ffn', 'w_up', 'ffn_conv_w', 'ffn_conv_b', 'w_down', 'g_post_ffn']
BIG = ['w_in', 'w_out', 'w_up', 'w_down']
SMALL_SHARDED = ['meta_tokens', 'lru_conv_w', 'conv_w', 'ffn_conv_w']
REPLICATED = [n for n in W_NAMES if n not in BIG and n not in SMALL_SHARDED]


def _params():
    return pltpu.CompilerParams(vmem_limit_bytes=_VMEM_LIMIT)


def _pick(n, pref):
    if n <= pref:
        return n
    best = None
    for b in range(_LANES, pref + 1, _LANES):
        if n % b == 0:
            best = b
    assert best is not None, (n, pref)
    return best


def _pick_rows(n, pref):
    if n <= pref:
        return n
    best = None
    for b in range(8, pref + 1, 8):
        if n % b == 0:
            best = b
    assert best is not None, (n, pref)
    return best


def _lspec(l, *dims):
    zeros = (0,) * len(dims)
    return pl.BlockSpec((None,) + tuple(dims), lambda *_: (l,) + zeros)


def _sds(shape, dtype):
    return jax.ShapeDtypeStruct(tuple(shape), dtype)


def _rms_fwd(x, g):
    r = lax.rsqrt(jnp.mean(x * x, axis=-1, keepdims=True) + EPS)
    return (x * r) * g


def _rms_bwd(x, g, dy):
    r = lax.rsqrt(jnp.mean(x * x, axis=-1, keepdims=True) + EPS)
    xh = x * r
    dg = jnp.sum(dy * xh, axis=0, keepdims=True)
    dxh = dy * g
    dx = r * (dxh - xh * jnp.mean(dxh * xh, axis=-1, keepdims=True))
    return dx, dg


def _gelu(x):
    t = jnp.tanh(_GELU_K0 * (x + _GELU_K1 * (x * x * x)))
    return 0.5 * x * (1.0 + t), t


def _gelu_grad(x, t):
    return 0.5 * (1.0 + t) + 0.5 * x * (1.0 - t * t) * (_GELU_K0 * (1.0 + 3.0 * _GELU_K1 * x * x))


def _log1p(e):
    u = 1.0 + e
    return jnp.where(u == 1.0, e, jnp.log(u) * (e / (u - 1.0)))


def _softplus(z):
    return jnp.maximum(z, 0.0) + _log1p(jnp.exp(-jnp.abs(z)))


def _one_minus_exp(x):
    p = -x * (1.0 + x * (0.5 + x * (1.0 / 6 + x * (1.0 / 24 + x * (1.0 / 120 + x * (1.0 / 720))))))
    return jnp.where(x > -0.125, p, 1.0 - jnp.exp(x))


def _shift_down(x, s, fill, row):
    return jnp.where(row >= s, pltpu.roll(x, s, 0), fill)


def _shift_up(x, s, fill, row):
    n = x.shape[0]
    return jnp.where(row < n - s, pltpu.roll(x, n - s, 0), fill)


def _bd_mm(xb, w_ref, heads, hd):
    return jnp.concatenate(
        [jnp.dot(xb[:, h * hd:(h + 1) * hd], w_ref[h].astype(_MXU), preferred_element_type=F32)
         for h in range(heads)], axis=-1)


def _bd_mm_t(db, w_ref, heads, hd):
    return jnp.concatenate(
        [lax.dot_general(db[:, h * hd:(h + 1) * hd], w_ref[h].astype(_MXU), (((1,), (1,)), ((), ())),
                         preferred_element_type=F32)
         for h in range(heads)], axis=-1)


def _gates(xc, wa_ref, ba, wx_ref, bx, lam, heads, hd):
    xcb = xc.astype(_MXU)
    ra = jax.nn.sigmoid(_bd_mm(xcb, wa_ref, heads, hd) + ba)
    ri = jax.nn.sigmoid(_bd_mm(xcb, wx_ref, heads, hd) + bx)
    sp = _softplus(-lam)
    la = (-LRU_C) * ra * sp
    a = jnp.exp(la)
    mult = jnp.sqrt(_one_minus_exp(2.0 * la))
    return xcb, ra, ri, sp, a, mult


def _group_norm(cc, groups):
    gs = cc.shape[-1] // groups
    outs, rss = [], []
    for g in range(groups):
        seg = cc[:, g * gs:(g + 1) * gs]
        mu = jnp.mean(seg, axis=-1, keepdims=True)
        d = seg - mu
        rs = lax.rsqrt(jnp.mean(d * d, axis=-1, keepdims=True) + EPS)
        outs.append(d * rs)
        rss.append(rs)
    return jnp.concatenate(outs, axis=-1), rss


_MESH = pl.DeviceIdType.MESH
_ANY = pl.BlockSpec(memory_space=pl.ANY)


def _my_index():
    return 4 * lax.axis_index("x") + 2 * lax.axis_index("y") + lax.axis_index("c")


class _GatherPlan:
    def __init__(self, blocks):
        self.blocks = list(blocks)

    def operands(self):
        return self.blocks

    def out_shape(self):
        return [_sds((_N_DEV,) + b.shape, b.dtype) for b in self.blocks]

    def scratch(self):
        n = len(self.blocks)
        return [pltpu.SemaphoreType.DMA((7 * n,)), pltpu.SemaphoreType.DMA((7 * n,)), pltpu.SemaphoreType.DMA((n,))]

    def _copies(self, a, ins, outs, sems):
        send_sems, recv_sems, local_sems = sems
        x, y, c = lax.axis_index("x"), lax.axis_index("y"), lax.axis_index("c")
        me, sibling = (x, y, c), (x, y, 1 - c)
        chips = [(1 - x, y), (x, 1 - y), (1 - x, 1 - y)]

        def slot(dev):
            return outs[a].at[4 * dev[0] + 2 * dev[1] + dev[2]]

        def copy(k, block, to, src=None):
            return pltpu.make_async_remote_copy(
                src_ref=slot(block) if src is None else src, dst_ref=slot(block),
                send_sem=send_sems.at[7 * a + k], recv_sem=recv_sems.at[7 * a + k],
                device_id=to, device_id_type=_MESH)

        mine = pltpu.make_async_copy(ins[a], slot(me), local_sems.at[a])
        first = [copy(0, me, sibling, src=ins[a])]
        first += [copy(1 + j, me, (*chip, c), src=ins[a]) for j, chip in enumerate(chips)]
        passed = [copy(4 + j, (*chip, c), sibling) for j, chip in enumerate(chips)]
        from_chips = [copy(1 + j, (*chip, c), me) for j, chip in enumerate(chips)]
        from_sibling = [copy(0, sibling, me)] + [copy(4 + j, (*chip, 1 - c), me) for j, chip in enumerate(chips)]
        return mine, first, passed, from_chips, from_sibling

    def start(self, ins, outs, sems):
        for a in range(len(self.blocks)):
            mine, first, _, _, _ = self._copies(a, ins, outs, sems)
            mine.start()
            for cp in first:
                cp.start()

    def forward(self, ins, outs, sems):
        for a in range(len(self.blocks)):
            _, _, passed, from_chips, _ = self._copies(a, ins, outs, sems)
            for j in range(3):
                from_chips[j].wait_recv()
                passed[j].start()

    def finish(self, ins, outs, sems):
        for a in range(len(self.blocks)):
            mine, first, passed, _, from_sibling = self._copies(a, ins, outs, sems)
            for cp in from_sibling:
                cp.wait_recv()
            for cp in first + passed:
                cp.wait_send()
            mine.wait()

    def begin(self, i, steps, ins, outs, sems):
        @pl.when(i == 0)
        def _():
            self.start(ins, outs, sems)

        @pl.when(i == (3 * steps) // 4)
        def _():
            self.forward(ins, outs, sems)

    def end(self, i, steps, ins, outs, sems):
        @pl.when(i == steps - 1)
        def _():
            self.finish(ins, outs, sems)


class _ExchangePlan:
    def __init__(self, arrs, kinds):
        self.arrs, self.kinds = list(arrs), list(kinds)

    def _piece_shape(self, a):
        shp = self.arrs[a].shape
        if self.kinds[a] == 'cols':
            return (shp[0], shp[1] // _N_DEV)
        if self.kinds[a] == 'rows':
            return (shp[0] // _N_DEV, shp[1])
        return tuple(shp[1:])

    def operands(self):
        return self.arrs

    def out_shape(self):
        return [_sds((_N_DEV,) + self._piece_shape(a), F32) for a in range(len(self.arrs))]

    def scratch(self):
        n = len(self.arrs)
        return [pltpu.SemaphoreType.DMA((_N_DEV * n,)), pltpu.SemaphoreType.DMA((_N_DEV * n,)),
                pltpu.SemaphoreType.DMA((n,))]

    def _copies(self, ins, outs, sems):
        send_sems, recv_sems, local_sems = sems
        me = _my_index()

        def piece(a, j):
            ps = self._piece_shape(a)
            if self.kinds[a] == 'cols':
                return ins[a].at[:, pl.ds(j * ps[1], ps[1])]
            if self.kinds[a] == 'rows':
                return ins[a].at[pl.ds(j * ps[0], ps[0]), :]
            return ins[a].at[j]

        def remote(a, j):
            return pltpu.make_async_remote_copy(
                src_ref=piece(a, j), dst_ref=outs[a].at[me],
                send_sem=send_sems.at[_N_DEV * a + j], recv_sem=recv_sems.at[_N_DEV * a + me],
                device_id=(j >> 2, (j >> 1) & 1, j & 1), device_id_type=_MESH)

        def arrival(a, s):
            return pltpu.make_async_remote_copy(
                src_ref=piece(a, s), dst_ref=outs[a].at[s],
                send_sem=send_sems.at[_N_DEV * a + s], recv_sem=recv_sems.at[_N_DEV * a + s],
                device_id=(s >> 2, (s >> 1) & 1, s & 1), device_id_type=_MESH)

        def local(a, j):
            return pltpu.make_async_copy(piece(a, j), outs[a].at[j], local_sems.at[a])

        return me, remote, arrival, local

    def start(self, ins, outs, sems):
        me, remote, _, local = self._copies(ins, outs, sems)
        for j in range(_N_DEV):
            for a in range(len(self.arrs)):
                @pl.when(me != j)
                def _(a=a, j=j):
                    remote(a, j).start()

                @pl.when(me == j)
                def _(a=a, j=j):
                    local(a, j).start()

    def finish(self, ins, outs, sems):
        me, remote, arrival, local = self._copies(ins, outs, sems)
        for j in range(_N_DEV):
            for a in range(len(self.arrs)):
                @pl.when(me != j)
                def _(a=a, j=j):
                    arrival(a, j).wait_recv()
                    remote(a, j).wait_send()

                @pl.when(me == j)
                def _(a=a, j=j):
                    local(a, j).wait()

    def forward(self, ins, outs, sems):
        pass

    def begin(self, i, steps, ins, outs, sems):
        @pl.when(i == 0)
        def _():
            self.start(ins, outs, sems)

    def end(self, i, steps, ins, outs, sems):
        @pl.when(i == steps - 1)
        def _():
            self.finish(ins, outs, sems)


def _ride(main, plan, name, grid, in_specs, args, out_specs, out_shape, scratch_shapes):
    grid = (grid,) if isinstance(grid, int) else tuple(grid)
    steps = math.prod(grid)
    n_in, n_out, n_sc = len(in_specs), len(out_specs), len(scratch_shapes)
    p_args = plan.operands() if plan else []
    p_out = plan.out_shape() if plan else []
    p_sc = plan.scratch() if plan else []

    def body(*refs):
        k = 0
        ins = refs[k:k + n_in]; k += n_in
        p_ins = refs[k:k + len(p_args)]; k += len(p_args)
        outs = refs[k:k + n_out]; k += n_out
        p_outs = refs[k:k + len(p_out)]; k += len(p_out)
        scr = refs[k:k + n_sc]; k += n_sc
        sems = refs[k:]
        i = pl.program_id(0)
        for axis in range(1, len(grid)):
            i = i * grid[axis] + pl.program_id(axis)
        if plan:
            plan.begin(i, steps, p_ins, p_outs, sems)
        main(*ins, *outs, *scr)
        if plan:
            plan.end(i, steps, p_ins, p_outs, sems)

    res = pl.pallas_call(
        body, name=name, grid=grid,
        in_specs=list(in_specs) + [_ANY] * len(p_args),
        out_specs=list(out_specs) + [_ANY] * len(p_out),
        out_shape=list(out_shape) + p_out,
        scratch_shapes=list(scratch_shapes) + p_sc,
        compiler_params=_params())(*args, *p_args)
    return res[:n_out], res[n_out:]


def _run_plan(plan, name):
    n_args, n_out = len(plan.operands()), len(plan.out_shape())

    def body(*refs):
        ins, outs, sems = refs[:n_args], refs[n_args:n_args + n_out], refs[n_args + n_out:]
        plan.start(ins, outs, sems)
        plan.forward(ins, outs, sems)
        plan.finish(ins, outs, sems)

    return pl.pallas_call(
        body, name=name, in_specs=[_ANY] * n_args, out_specs=[_ANY] * n_out,
        out_shape=plan.out_shape(), scratch_shapes=plan.scratch())(*plan.operands())


def _perm(a, tm):
    T, C = a.shape
    return a.reshape(T // tm, 8, tm // 8, C).transpose(0, 2, 1, 3).reshape(T, C)


def _unperm(a, tm):
    T, C = a.shape
    return a.reshape(T // tm, tm // 8, 8, C).transpose(0, 2, 1, 3).reshape(T, C)


def _wrap_prev(prev_z, z):
    n = z.shape[0]
    sub = lax.broadcasted_iota(jnp.int32, z.shape, 0) & 7
    return jnp.where(sub == 0, pltpu.roll(prev_z, n - 7, 0), pltpu.roll(z, 1, 0))


def _wrap_next(next_z, z):
    n = z.shape[0]
    sub = lax.broadcasted_iota(jnp.int32, z.shape, 0) & 7
    return jnp.where(sub == 7, pltpu.roll(next_z, 7, 0), pltpu.roll(z, n - 1, 0))


def _fill_causal(buf, halo, x, tm):
    nh = halo.shape[0]
    buf[nh:nh + tm, :] = x
    z = buf[tm:tm + nh, :]
    buf[0:nh, :] = _wrap_prev(halo[...], z)
    halo[...] = z


def _fill_anticausal(buf, halo, dy, tm):
    nh = halo.shape[0]
    buf[0:tm, :] = dy
    z = buf[0:nh, :]
    buf[tm:tm + nh, :] = _wrap_next(halo[...], z)
    halo[...] = z


def _scan_fwd(a, u, abuf, ubuf, hcar, tm):
    D = a.shape[1]
    G = tm // 8
    abuf[...] = a
    ubuf[...] = u

    def step(j, c):
        h, pr = c
        r = pl.multiple_of(j * 8, 8)
        aj = abuf[pl.ds(r, 8), :]
        h = aj * h + ubuf[pl.ds(r, 8), :]
        pr = aj * pr
        ubuf[pl.ds(r, 8), :] = h
        abuf[pl.ds(r, 8), :] = pr
        return h, pr

    sub8 = lax.broadcasted_iota(jnp.int32, (8, D), 0)
    e, q = lax.fori_loop(1, G, step, (ubuf[0:8, :], abuf[0:8, :]))
    for s in (1, 2, 4):
        e = e + q * _shift_down(e, s, 0.0, sub8)
        q = q * _shift_down(q, s, 1.0, sub8)
    e = e + q * hcar[...]
    cin = jnp.where(sub8 == 0, hcar[...], pltpu.roll(e, 1, 0))
    return ubuf[...] + abuf[...] * jnp.tile(cin, (G, 1))


def _scan_bwd(a, d, abuf, gbuf, gcar, tmp8, tm):
    D = a.shape[1]
    G = tm // 8
    abuf[...] = a
    gbuf[...] = d

    def step(k, c):
        g_next, a_next, r_j = c
        r = pl.multiple_of((G - 1 - k) * 8, 8)
        aj = abuf[pl.ds(r, 8), :]
        g = gbuf[pl.ds(r, 8), :] + a_next * g_next
        gbuf[pl.ds(r, 8), :] = g
        abuf[pl.ds(r, 8), :] = r_j
        return g, aj, aj * r_j

    last = 8 * (G - 1)
    a_last = abuf[last:last + 8, :]
    abuf[last:last + 8, :] = jnp.ones((8, D), F32)
    g0, a0, _ = lax.fori_loop(1, G, step, (gbuf[last:last + 8, :], a_last, a_last))
    r0 = abuf[0:8, :]
    sub8 = lax.broadcasted_iota(jnp.int32, (8, D), 0)
    x, q = a0 * g0, a0 * r0
    for s in (1, 2, 4):
        x = x + q * _shift_up(x, s, 0.0, sub8)
        q = q * _shift_up(q, s, 1.0, sub8)
    x = x + q * gcar[...]
    cin = jnp.where(sub8 == 7, gcar[...], pltpu.roll(x, 7, 0))
    g = gbuf[...] + abuf[...] * jnp.tile(cin, (G, 1))
    tmp8[...] = x
    gcar[...] = tmp8[0:1, :]
    return g


def _mixers_fwd(h, p, win, wout, l, name, plan=None):
    T, D = h.shape
    D3 = win.shape[0] * win.shape[2]
    Dc = D // 2
    heads, hd = p['lru_wa'].shape[1], p['lru_wa'].shape[2]
    tm = _TM_SEQ
    kl, kc = p['lru_conv_w'].shape[1], p['conv_w'].shape[1]
    nhl, nhc = 8 * (kl - 1), 8 * (kc - 1)
    assert nhc <= tm

    def body(h_ref, gz_ref, wl_ref, bl_ref, wa_ref, ba_ref, wx_ref, bx_ref, lam_ref, wc_ref, bc_ref, lg_ref, lb_ref,
             gol_ref, goc_ref, gpm_ref, win_hbm, wout_hbm,
             z_ref, xc_ref, hs_ref, ya_ref, cc_ref, yb_ref, y_ref, o_ref, hm_ref,
             xbuf, xhalo, hcar, cbuf, chalo, abuf, ubuf, win_ref, wout_ref, wsem):
        _load_weights(pl.program_id(0), _shard_pairs(win_hbm, win_ref) + [(wout_hbm, wout_ref)], wsem)

        @pl.when(pl.program_id(0) == 0)
        def _():
            xhalo[...] = jnp.zeros_like(xhalo)
            chalo[...] = jnp.zeros_like(chalo)
            hcar[...] = jnp.zeros_like(hcar)

        z = _rms_fwd(h_ref[...], gz_ref[...]).astype(_MXU)
        z_ref[...] = z

        def proj(lo, hi):
            return jnp.dot(z, win_ref[:, lo:hi], preferred_element_type=F32)

        _fill_causal(xbuf, xhalo, proj(0, D), tm)
        xc = bl_ref[...] + wl_ref[0:1, :] * xbuf[0:tm, :]
        for k in range(1, kl):
            xc = xc + wl_ref[k:k + 1, :] * xbuf[8 * k:8 * k + tm, :]
        xc_ref[...] = xc
        _, ra, ri, sp, a, mult = _gates(xc, wa_ref, ba_ref[...], wx_ref, bx_ref[...], lam_ref[...], heads, hd)
        hs = _scan_fwd(a, mult * (ri * xc), abuf, ubuf, hcar, tm)
        hs_ref[...] = hs
        hcar[...] = hs_ref[pl.ds(tm - 1, 1), :]
        gg, _ = _gelu(proj(D, 2 * D))
        ya = hs * gg
        ya_ref[...] = ya

        _fill_causal(cbuf, chalo, proj(2 * D, 2 * D + Dc) * jax.nn.sigmoid(proj(2 * D + Dc, 3 * D)), tm)
        cc = bc_ref[...] + wc_ref[0:1, :] * cbuf[0:tm, :]
        for k in range(1, kc):
            cc = cc + wc_ref[k:k + 1, :] * cbuf[8 * k:8 * k + tm, :]
        cc_ref[...] = cc
        nrm, _ = _group_norm(cc, CONV_GROUPS)
        cl = nrm * lg_ref[...] + lb_ref[...]
        yb = cl * jax.nn.sigmoid(cl)
        yb_ref[...] = yb

        y = jnp.concatenate([_rms_fwd(ya, gol_ref[...]), _rms_fwd(yb, goc_ref[...])], axis=-1).astype(_MXU)
        y_ref[...] = y
        o = jnp.dot(y, wout_ref[...], preferred_element_type=F32)
        o_ref[...] = o
        hm_ref[...] = h_ref[...] + _rms_fwd(o, gpm_ref[...])

    row_d = pl.BlockSpec((tm, D), lambda i: (i, 0))
    row_c = pl.BlockSpec((tm, Dc), lambda i: (i, 0))
    return _ride(
        body, plan, name, T // tm,
        in_specs=[row_d, _lspec(l, 1, D),
                  _lspec(l, kl, D), _lspec(l, 1, D), _lspec(l, heads, hd, hd), _lspec(l, 1, D),
                  _lspec(l, heads, hd, hd), _lspec(l, 1, D), _lspec(l, 1, D),
                  _lspec(l, kc, Dc), _lspec(l, 1, Dc), _lspec(l, 1, Dc), _lspec(l, 1, Dc),
                  _lspec(l, 1, D), _lspec(l, 1, Dc), _lspec(l, 1, D), _ANY, _ANY],
        args=(h, p['g_pre_mix'], p['lru_conv_w'], p['lru_conv_b'], p['lru_wa'], p['lru_ba'], p['lru_wx'], p['lru_bx'],
              p['lru_lambda'], p['conv_w'], p['conv_b'], p['conv_ln_g'], p['conv_ln_b'],
              p['g_out_lru'], p['g_out_conv'], p['g_post_mix'], win, wout),
        out_specs=[row_d, row_d, row_d, row_d, row_c, row_c, pl.BlockSpec((tm, D + Dc), lambda i: (i, 0)), row_d, row_d],
        out_shape=[_sds((T, D), _MXU), _sds((T, D), F32), _sds((T, D), F32), _sds((T, D), F32),
                   _sds((T, Dc), F32), _sds((T, Dc), F32), _sds((T, D + Dc), _MXU), _sds((T, D), F32),
                   _sds((T, D), F32)],
        scratch_shapes=[pltpu.VMEM((nhl + tm, D), F32), pltpu.VMEM((nhl, D), F32), pltpu.VMEM((1, D), F32),
                        pltpu.VMEM((nhc + tm, Dc), F32), pltpu.VMEM((nhc, Dc), F32),
                        pltpu.VMEM((tm, D), F32), pltpu.VMEM((tm, D), F32),
                        pltpu.VMEM((D, D3), _MXU), pltpu.VMEM((D + Dc, D), _MXU),
                        pltpu.SemaphoreType.DMA((_N_DEV + 1,))])


def _shard_pairs(w_hbm, w_ref):
    n = w_hbm.shape[2]
    return [(w_hbm.at[j], w_ref.at[:, pl.ds(j * n, n)]) for j in range(w_hbm.shape[0])]


def _load_weights(i, pairs, sems):
    @pl.when(i == 0)
    def _():
        copies = [pltpu.make_async_copy(src, dst, sems.at[k]) for k, (src, dst) in enumerate(pairs)]
        for cp in copies:
            cp.start()
        for cp in copies:
            cp.wait()


def _up_into(ubuf, z, wup_v, nh, tm, cw):
    for c in range(wup_v.shape[1] // cw):
        cs = slice(c * cw, (c + 1) * cw)
        ubuf[nh:nh + tm, cs] = jnp.dot(z, wup_v[:, cs], preferred_element_type=F32)


def _ffn_fwd(hmid, p, wup, wdown, l, name, plan=None):
    T, D = hmid.shape
    F2 = wup.shape[0] * wup.shape[2]
    Fh = F2 // 2
    tm = _TM_SEQ
    cw = _pick(Fh, _NB)
    kf = p['ffn_conv_w'].shape[1]
    nh = 8 * (kf - 1)

    def body(hm_ref, gz_ref, wf_ref, bf_ref, g_ref, wu_hbm, wd_hbm, z_ref, a2_ref, f_ref, ho_ref,
             ubuf, uhalo, wu_ref, wd_ref, wsem):
        i = pl.program_id(0)
        _load_weights(i, _shard_pairs(wu_hbm, wu_ref) + [(wd_hbm, wd_ref)], wsem)

        @pl.when(i == 0)
        def _():
            uhalo[...] = jnp.zeros_like(uhalo)

        z = _rms_fwd(hm_ref[...], gz_ref[...]).astype(_MXU)
        z_ref[...] = z
        _up_into(ubuf, z, wu_ref, nh, tm, cw)
        tail = ubuf[tm:tm + nh, :]
        ubuf[0:nh, :] = _wrap_prev(uhalo[...], tail)
        uhalo[...] = tail

        def conv(cs):
            acc = bf_ref[:, cs]
            for k in range(kf):
                acc = acc + wf_ref[k:k + 1, cs] * ubuf[8 * k:8 * k + tm, cs]
            return acc

        f = None
        for c in range(Fh // cw):
            gs = slice(c * cw, (c + 1) * cw)
            gg, _ = _gelu(conv(gs))
            a2 = (gg * conv(slice(Fh + c * cw, Fh + (c + 1) * cw))).astype(_MXU)
            a2_ref[:, gs] = a2
            part = jnp.dot(a2, wd_ref[gs, :], preferred_element_type=F32)
            f = part if f is None else f + part
        f_ref[...] = f
        ho_ref[...] = hm_ref[...] + _rms_fwd(f, g_ref[...])

    row_d = pl.BlockSpec((tm, D), lambda i: (i, 0))
    return _ride(
        body, plan, name, T // tm,
        in_specs=[row_d, _lspec(l, 1, D), _lspec(l, kf, F2), _lspec(l, 1, F2), _lspec(l, 1, D), _ANY, _ANY],
        args=(hmid, p['g_pre_ffn'], p['ffn_conv_w'], p['ffn_conv_b'], p['g_post_ffn'], wup, wdown),
        out_specs=[row_d, pl.BlockSpec((tm, Fh), lambda i: (i, 0)), row_d, row_d],
        out_shape=[_sds((T, D), _MXU), _sds((T, Fh), _MXU), _sds((T, D), F32), _sds((T, D), F32)],
        scratch_shapes=[pltpu.VMEM((nh + tm, F2), F32), pltpu.VMEM((nh, F2), F32),
                        pltpu.VMEM((D, F2), _MXU), pltpu.VMEM((Fh, D), _MXU), pltpu.SemaphoreType.DMA((_N_DEV + 1,))])


def _loss_head(h, tgt, n_real, name):
    T, D = h.shape
    tm = _TM_SEQ

    def body(h_ref, t_ref, loss_ref, dh_ref):
        i = pl.program_id(0)

        @pl.when(i == 0)
        def _():
            loss_ref[...] = jnp.zeros_like(loss_ref)

        pos = lax.broadcasted_iota(jnp.int32, (tm, D), 0)
        row = i * tm + (pos & 7) * (tm // 8) + (pos >> 3)
        e = jnp.where((row >= N_META) & (row < N_META + n_real), h_ref[...] - t_ref[...], 0.0)
        dh_ref[...] = e * (1.0 / D)
        loss_ref[...] += 0.5 * jnp.sum(jnp.mean(e * e, axis=-1, keepdims=True), axis=0, keepdims=True)

    row_d = pl.BlockSpec((tm, D), lambda i: (i, 0))
    return pl.pallas_call(
        body, name=name, grid=(T // tm,),
        in_specs=[row_d, row_d],
        out_specs=[pl.BlockSpec((8, _LANES), lambda i: (0, 0)), row_d],
        out_shape=[_sds((8, _LANES), F32), _sds((T, D), F32)],
        compiler_params=_params())(h, tgt)


def _ffn_bwd(dh, f, z2, hmid, p, wup, wdown, l, name, plan=None):
    T, D = dh.shape
    F2 = wup.shape[0] * wup.shape[2]
    Fh = F2 // 2
    tm = _TM_SEQ
    nt = T // tm
    cw = _pick(Fh, _NB)
    kf = p['ffn_conv_w'].shape[1]
    nh = 8 * (kf - 1)
    assert tm % nh == 0

    def body(dh_ref, f_ref, z_ref, zp_ref, hm_ref, wf_ref, bf_ref, g_ref, gz_ref, wu_hbm, wd_hbm,
             df_ref, dup_ref, dwf_ref, dbf_ref, dg_ref, dhm_ref, dgz_ref,
             ubuf, dbuf, dhalo, wu_ref, wd_ref, wsem):
        i = pl.program_id(0)
        r = nt - 1 - i
        _load_weights(i, _shard_pairs(wu_hbm, wu_ref) + [(wd_hbm, wd_ref)], wsem)

        @pl.when(i == 0)
        def _():
            for ref in (dhalo, dwf_ref, dbf_ref, dg_ref, dgz_ref):
                ref[...] = jnp.zeros_like(ref)

        df, dg = _rms_bwd(f_ref[...], g_ref[...], dh_ref[...])
        dg_ref[...] += dg
        dfb = df.astype(_MXU)
        df_ref[...] = dfb
        zcat = jnp.concatenate([zp_ref[...], z_ref[...]], axis=0)

        def conv(cs):
            ubuf[:, cs] = jnp.dot(zcat, wu_ref[:, cs], preferred_element_type=F32)
            ubuf[0:nh, cs] = _wrap_prev(jnp.where(r == 0, 0.0, ubuf[0:nh, cs]), ubuf[tm:tm + nh, cs])
            acc = bf_ref[:, cs]
            for k in range(kf):
                acc = acc + wf_ref[k:k + 1, cs] * ubuf[8 * k:8 * k + tm, cs]
            return acc

        for c in range(Fh // cw):
            gs = slice(c * cw, (c + 1) * cw)
            us = slice(Fh + c * cw, Fh + (c + 1) * cw)
            ug = conv(gs)
            gg, t = _gelu(ug)
            da2 = lax.dot_general(dfb, wd_ref[gs, :], (((1,), (1,)), ((), ())), preferred_element_type=F32)
            dbuf[0:tm, gs] = da2 * conv(us) * _gelu_grad(ug, t)
            dbuf[0:tm, us] = da2 * gg
        z = dbuf[0:nh, :]
        dbuf[tm:tm + nh, :] = _wrap_next(dhalo[...], z)
        dhalo[...] = z
        dz = None
        for c in range(F2 // cw):
            cs = slice(c * cw, (c + 1) * cw)
            upc = ubuf[nh:nh + tm, cs]
            dup = None
            for k in range(kf):
                dsh = dbuf[8 * (kf - 1 - k):8 * (kf - 1 - k) + tm, cs]
                term = wf_ref[k:k + 1, cs] * dsh
                dup = term if dup is None else dup + term
                dwf_ref[k:k + 1, cs] += jnp.sum(dsh * upc, axis=0, keepdims=True)
            dbf_ref[:, cs] += jnp.sum(dbuf[0:tm, cs], axis=0, keepdims=True)
            dupb = dup.astype(_MXU)
            dup_ref[:, cs] = dupb
            part = lax.dot_general(dupb, wu_ref[:, cs], (((1,), (1,)), ((), ())), preferred_element_type=F32)
            dz = part if dz is None else dz + part
        dx, dgz = _rms_bwd(hm_ref[...], gz_ref[...], dz)
        dhm_ref[...] = dh_ref[...] + dx
        dgz_ref[...] += dgz

    rev_d = pl.BlockSpec((tm, D), lambda i: (nt - 1 - i, 0))
    rev_f = pl.BlockSpec((tm, F2), lambda i: (nt - 1 - i, 0))
    prev = pl.BlockSpec((nh, D), lambda i: (jnp.maximum((nt - 1 - i) * (tm // nh) - 1, 0), 0))
    full = lambda *s: pl.BlockSpec(s, lambda i: (0,) * len(s))
    return _ride(
        body, plan, name, nt,
        in_specs=[rev_d, rev_d, rev_d, prev, rev_d, _lspec(l, kf, F2), _lspec(l, 1, F2), _lspec(l, 1, D),
                  _lspec(l, 1, D), _ANY, _ANY],
        args=(dh, f, z2, z2, hmid, p['ffn_conv_w'], p['ffn_conv_b'], p['g_post_ffn'], p['g_pre_ffn'], wup, wdown),
        out_specs=[rev_d, rev_f, full(kf, F2), full(1, F2), full(1, D), rev_d, full(1, D)],
        out_shape=[_sds((T, D), _MXU), _sds((T, F2), _MXU), _sds((kf, F2), F32), _sds((1, F2), F32), _sds((1, D), F32),
                   _sds((T, D), F32), _sds((1, D), F32)],
        scratch_shapes=[pltpu.VMEM((nh + tm, F2), F32), pltpu.VMEM((tm + nh, F2), F32), pltpu.VMEM((nh, F2), F32),
                        pltpu.VMEM((D, F2), _MXU), pltpu.VMEM((Fh, D), _MXU), pltpu.SemaphoreType.DMA((_N_DEV + 1,))])


def _mm_tn(xs, dy, name):
    T, K = xs.shape
    N = dy.shape[1]
    tm = _TM_DW
    kb = K if K <= 1024 else _pick(K, _NB)
    nb = _pick(N, _NB)

    def body(x_ref, dy_ref, o_ref):
        @pl.when(pl.program_id(2) == 0)
        def _():
            o_ref[...] = jnp.zeros_like(o_ref)

        o_ref[...] += lax.dot_general(x_ref[...], dy_ref[...], (((0,), (0,)), ((), ())), preferred_element_type=F32)

    return pl.pallas_call(
        body, name=name, grid=(K // kb, N // nb, T // tm),
        in_specs=[pl.BlockSpec((tm, kb), lambda a, b, t: (t, a)), pl.BlockSpec((tm, nb), lambda a, b, t: (t, b))],
        out_specs=pl.BlockSpec((kb, nb), lambda a, b, t: (a, b)),
        out_shape=_sds((K, N), F32),
        compiler_params=_params())(xs, dy)


def _mix_bwd(dhm, o, ya, yb, p, wout, l, name):
    T, D = ya.shape
    Dc = yb.shape[1]
    tm = _TM_SEQ

    def body(dh_ref, o_ref, ya_ref, yb_ref, gp_ref, gl_ref, gc_ref, w_ref,
             do_ref, dya_ref, dyb_ref, dgp_ref, dgl_ref, dgc_ref):
        @pl.when(pl.program_id(0) == 0)
        def _():
            dgp_ref[...] = jnp.zeros_like(dgp_ref)
            dgl_ref[...] = jnp.zeros_like(dgl_ref)
            dgc_ref[...] = jnp.zeros_like(dgc_ref)

        do, dgp = _rms_bwd(o_ref[...], gp_ref[...], dh_ref[...])
        dgp_ref[...] += dgp
        dob = do.astype(_MXU)
        do_ref[...] = dob
        dy = lax.dot_general(dob, w_ref[...], (((1,), (1,)), ((), ())), preferred_element_type=F32)
        dya, dgl = _rms_bwd(ya_ref[...], gl_ref[...], dy[:, 0:D])
        dyb, dgc = _rms_bwd(yb_ref[...], gc_ref[...], dy[:, D:D + Dc])
        dya_ref[...] = dya
        dyb_ref[...] = dyb
        dgl_ref[...] += dgl
        dgc_ref[...] += dgc

    row_d = pl.BlockSpec((tm, D), lambda i: (i, 0))
    row_c = pl.BlockSpec((tm, Dc), lambda i: (i, 0))
    full = lambda *s: pl.BlockSpec(s, lambda i: (0,) * len(s))
    return pl.pallas_call(
        body, name=name, grid=(T // tm,),
        in_specs=[row_d, row_d, row_d, row_c, _lspec(l, 1, D), _lspec(l, 1, D), _lspec(l, 1, Dc),
                  pl.BlockSpec((D + Dc, D), lambda i: (0, 0))],
        out_specs=[row_d, row_d, row_c, full(1, D), full(1, D), full(1, Dc)],
        out_shape=[_sds((T, D), _MXU), _sds((T, D), F32), _sds((T, Dc), F32),
                   _sds((1, D), F32), _sds((1, D), F32), _sds((1, Dc), F32)],
        compiler_params=_params(),
    )(dhm, o, ya, yb, p['g_post_mix'], p['g_out_lru'], p['g_out_conv'], wout)


def _mixers_bwd(dya, dyb, z1, xc, hs, cc, h, dres, p, win, l, name, plan=None):
    T, D = z1.shape
    D3 = win.shape[0] * win.shape[2]
    Dc = D // 2
    heads, hd = p['lru_wa'].shape[1], p['lru_wa'].shape[2]
    tm = _TM_SEQ
    nt = T // tm
    per8 = tm // 8
    kl, kc = p['lru_conv_w'].shape[1], p['conv_w'].shape[1]
    nhl, nhc = 8 * (kl - 1), 8 * (kc - 1)
    assert nhc <= tm

    def body(dya_ref, dyb_ref, z_ref, xc_ref, hs_ref, hsp_ref, cc_ref, h_ref, dres_ref,
             wl_ref, wa_ref, ba_ref, wx_ref, bx_ref, lam_ref, wc_ref, lg_ref, lb_ref, gz_ref, win_hbm,
             dproj_ref, dwl_ref, dbl_ref, dwa_ref, dba_ref, dwx_ref, dbx_ref, dlam_ref,
             dwc_ref, dbc_ref, dlg_ref, dlb_ref, dh_ref, dgz_ref,
             gcar, dxbuf, dxhalo, dcbuf, dchalo, abuf, gbuf, hbuf, tmp8, win_ref, wsem):
        i = pl.program_id(0)
        r = nt - 1 - i
        _load_weights(i, _shard_pairs(win_hbm, win_ref), wsem)

        @pl.when(i == 0)
        def _():
            for ref in (gcar, dxhalo, dchalo, dwl_ref, dbl_ref, dwa_ref, dba_ref, dwx_ref, dbx_ref, dlam_ref,
                        dwc_ref, dbc_ref, dlg_ref, dlb_ref, dgz_ref):
                ref[...] = jnp.zeros_like(ref)

        zb = z_ref[...]

        def proj(lo, hi):
            return jnp.dot(zb, win_ref[:, lo:hi], preferred_element_type=F32)

        xl, gl, ca, cb = proj(0, D), proj(D, 2 * D), proj(2 * D, 2 * D + Dc), proj(2 * D + Dc, 3 * D)

        dya_v = dya_ref[...]
        hs = hs_ref[...]
        gg, tg = _gelu(gl)
        dproj_ref[:, D:2 * D] = (dya_v * hs * _gelu_grad(gl, tg)).astype(_MXU)
        dhs = dya_v * gg
        xc = xc_ref[...]
        lam = lam_ref[...]
        xcb, ra, ri, sp, a, mult = _gates(xc, wa_ref, ba_ref[...], wx_ref, bx_ref[...], lam, heads, hd)
        g = _scan_bwd(a, dhs, abuf, gbuf, gcar, tmp8, tm)
        sub8 = lax.broadcasted_iota(jnp.int32, (8, D), 0)
        hbuf[8:8 + tm, :] = hs
        hbuf[0:8, :] = jnp.where(sub8 == 0, jnp.where(r == 0, 0.0, hsp_ref[7:8, :]),
                                 pltpu.roll(hs_ref[tm - 8:tm, :], 1, 0))
        da = g * hbuf[0:tm, :]
        gx = g * xc
        dxc = g * mult * ri
        dla = da * a - (gx * ri) * (a * a) / mult
        dlam_ref[...] += jnp.sum(dla * ra, axis=0, keepdims=True) * (LRU_C * jax.nn.sigmoid(-lam))
        dpa = (dla * ((-LRU_C) * sp)) * ra * (1.0 - ra)
        dpx = (gx * mult) * ri * (1.0 - ri)
        dba_ref[...] += jnp.sum(dpa, axis=0, keepdims=True)
        dbx_ref[...] += jnp.sum(dpx, axis=0, keepdims=True)
        dpab, dpxb = dpa.astype(_MXU), dpx.astype(_MXU)
        for h in range(heads):
            hsl = slice(h * hd, (h + 1) * hd)
            dwa_ref[h] += lax.dot_general(xcb[:, hsl], dpab[:, hsl], (((0,), (0,)), ((), ())),
                                          preferred_element_type=F32)
            dwx_ref[h] += lax.dot_general(xcb[:, hsl], dpxb[:, hsl], (((0,), (0,)), ((), ())),
                                          preferred_element_type=F32)
        dxc = dxc + _bd_mm_t(dpab, wa_ref, heads, hd) + _bd_mm_t(dpxb, wx_ref, heads, hd)
        dbl_ref[...] += jnp.sum(dxc, axis=0, keepdims=True)
        _fill_anticausal(dxbuf, dxhalo, dxc, tm)
        dxl = None
        for k in range(kl):
            dsh = dxbuf[8 * (kl - 1 - k):8 * (kl - 1 - k) + tm, :]
            term = wl_ref[k:k + 1, :] * dsh
            dxl = term if dxl is None else dxl + term
            dwl_ref[k:k + 1, :] += jnp.sum(dsh * xl, axis=0, keepdims=True)
        dproj_ref[:, 0:D] = dxl.astype(_MXU)

        sg = jax.nn.sigmoid(cb)
        cg = ca * sg
        nrm, rss = _group_norm(cc_ref[...], CONV_GROUPS)
        lg = lg_ref[...]
        cl = nrm * lg + lb_ref[...]
        sc = jax.nn.sigmoid(cl)
        dcl = dyb_ref[...] * (sc * (1.0 + cl * (1.0 - sc)))
        dlg_ref[...] += jnp.sum(dcl * nrm, axis=0, keepdims=True)
        dlb_ref[...] += jnp.sum(dcl, axis=0, keepdims=True)
        dnrm = dcl * lg
        gsz = Dc // CONV_GROUPS
        parts = []
        for gi in range(CONV_GROUPS):
            sl = slice(gi * gsz, (gi + 1) * gsz)
            dn, nn = dnrm[:, sl], nrm[:, sl]
            parts.append(rss[gi] * (dn - jnp.mean(dn, axis=-1, keepdims=True)
                                    - nn * jnp.mean(dn * nn, axis=-1, keepdims=True)))
        dcc = jnp.concatenate(parts, axis=-1)
        dbc_ref[...] += jnp.sum(dcc, axis=0, keepdims=True)
        _fill_anticausal(dcbuf, dchalo, dcc, tm)
        dcg = None
        for k in range(kc):
            dsh = dcbuf[8 * (kc - 1 - k):8 * (kc - 1 - k) + tm, :]
            term = wc_ref[k:k + 1, :] * dsh
            dcg = term if dcg is None else dcg + term
            dwc_ref[k:k + 1, :] += jnp.sum(dsh * cg, axis=0, keepdims=True)
        dproj_ref[:, 2 * D:2 * D + Dc] = (dcg * sg).astype(_MXU)
        dproj_ref[:, 2 * D + Dc:3 * D] = (dcg * ca * sg * (1.0 - sg)).astype(_MXU)

        cwp = _pick(D3, _NB)
        dz = None
        for c in range(D3 // cwp):
            cs = slice(c * cwp, (c + 1) * cwp)
            part = lax.dot_general(dproj_ref[:, cs], win_ref[:, cs], (((1,), (1,)), ((), ())),
                                   preferred_element_type=F32)
            dz = part if dz is None else dz + part
        dx, dgz = _rms_bwd(h_ref[...], gz_ref[...], dz)
        dh_ref[...] = dres_ref[...] + dx
        dgz_ref[...] += dgz

    rev_d = pl.BlockSpec((tm, D), lambda i: (nt - 1 - i, 0))
    rev_c = pl.BlockSpec((tm, Dc), lambda i: (nt - 1 - i, 0))
    rev_p = pl.BlockSpec((tm, D3), lambda i: (nt - 1 - i, 0))
    prev8 = pl.BlockSpec((8, D), lambda i: (jnp.maximum((nt - 1 - i) * per8 - 1, 0), 0))
    full = lambda *s: pl.BlockSpec(s, lambda i: (0,) * len(s))
    return _ride(
        body, plan, name, nt,
        in_specs=[rev_d, rev_c, rev_d, rev_d, rev_d, prev8, rev_c, rev_d, rev_d,
                  _lspec(l, kl, D), _lspec(l, heads, hd, hd), _lspec(l, 1, D), _lspec(l, heads, hd, hd),
                  _lspec(l, 1, D), _lspec(l, 1, D), _lspec(l, kc, Dc), _lspec(l, 1, Dc), _lspec(l, 1, Dc),
                  _lspec(l, 1, D), _ANY],
        args=(dya, dyb, z1, xc, hs, hs, cc, h, dres, p['lru_conv_w'], p['lru_wa'], p['lru_ba'], p['lru_wx'],
              p['lru_bx'], p['lru_lambda'], p['conv_w'], p['conv_ln_g'], p['conv_ln_b'], p['g_pre_mix'], win),
        out_specs=[rev_p, full(kl, D), full(1, D), full(heads, hd, hd), full(1, D), full(heads, hd, hd), full(1, D),
                   full(1, D), full(kc, Dc), full(1, Dc), full(1, Dc), full(1, Dc), rev_d, full(1, D)],
        out_shape=[_sds((T, D3), _MXU), _sds((kl, D), F32), _sds((1, D), F32), _sds((heads, hd, hd), F32),
                   _sds((1, D), F32), _sds((heads, hd, hd), F32), _sds((1, D), F32), _sds((1, D), F32),
                   _sds((kc, Dc), F32), _sds((1, Dc), F32), _sds((1, Dc), F32), _sds((1, Dc), F32),
                   _sds((T, D), F32), _sds((1, D), F32)],
        scratch_shapes=[pltpu.VMEM((1, D), F32), pltpu.VMEM((tm + nhl, D), F32), pltpu.VMEM((nhl, D), F32),
                        pltpu.VMEM((tm + nhc, Dc), F32), pltpu.VMEM((nhc, Dc), F32),
                        pltpu.VMEM((tm, D), F32), pltpu.VMEM((tm, D), F32), pltpu.VMEM((8 + tm, D), F32),
                        pltpu.VMEM((8, D), F32), pltpu.VMEM((D, D3), _MXU), pltpu.SemaphoreType.DMA((_N_DEV,))])


def _sum_sources(recv, name):
    _, R, C = recv.shape
    rb = _pick_rows(R, 1024)

    def body(r_ref, o_ref):
        acc = r_ref[0]
        for s in range(1, _N_DEV):
            acc = acc + r_ref[s]
        o_ref[...] = acc

    return pl.pallas_call(
        body, name=name, grid=(R // rb,),
        in_specs=[pl.BlockSpec((_N_DEV, rb, C), lambda i: (0, i, 0))],
        out_specs=pl.BlockSpec((rb, C), lambda i: (i, 0)),
        out_shape=_sds((R, C), F32), compiler_params=_params())(recv)


def _adamw(g, w, m, v, name):
    R, C = w.shape
    summed = g.ndim == 3
    rb = _pick_rows(R, max(8, min(512, _ADAM_BLOCK_ELEMS // C)))
    c1 = 1.0 - ADAM_B1 ** ADAM_STEP
    c2 = 1.0 - ADAM_B2 ** ADAM_STEP

    def body(g_ref, w_ref, m_ref, v_ref, go_ref, d_ref, mo_ref, vo_ref):
        if summed:
            gv = g_ref[0]
            for s in range(1, _N_DEV):
                gv = gv + g_ref[s]
        else:
            gv = g_ref[...]
        go_ref[...] = gv
        mn = ADAM_B1 * m_ref[...] + (1.0 - ADAM_B1) * gv
        vn = ADAM_B2 * v_ref[...] + (1.0 - ADAM_B2) * (gv * gv)
        mo_ref[...] = mn
        vo_ref[...] = vn
        d_ref[...] = (-ADAM_LR) * ((mn / c1) / (jnp.sqrt(vn / c2) + ADAM_EPS) + ADAM_WD * w_ref[...])

    blk = pl.BlockSpec((rb, C), lambda i: (i, 0))
    gspec = pl.BlockSpec((_N_DEV, rb, C), lambda i: (0, i, 0)) if summed else blk
    return pl.pallas_call(
        body, name=name, grid=(R // rb,),
        in_specs=[gspec, blk, blk, blk], out_specs=[blk, blk, blk, blk],
        out_shape=[_sds((R, C), F32)] * 4, compiler_params=_params())(g, w, m, v)


def _adamw_layers(recvs, w, m, v, name, plan=None):
    L, R, C = w.shape
    rb = _pick_rows(R, max(8, _ADAM_BLOCK_ELEMS // (4 * C)))
    c1 = 1.0 - ADAM_B1 ** ADAM_STEP
    c2 = 1.0 - ADAM_B2 ** ADAM_STEP

    def body(*refs):
        r_refs = refs[:L]
        w_ref, m_ref, v_ref, go_ref, d_ref, mo_ref, vo_ref = refs[L:]
        for l in range(L):
            gv = r_refs[l][0]
            for s in range(1, _N_DEV):
                gv = gv + r_refs[l][s]
            go_ref[l] = gv
            mn = ADAM_B1 * m_ref[l] + (1.0 - ADAM_B1) * gv
            vn = ADAM_B2 * v_ref[l] + (1.0 - ADAM_B2) * (gv * gv)
            mo_ref[l] = mn
            vo_ref[l] = vn
            d_ref[l] = (-ADAM_LR) * ((mn / c1) / (jnp.sqrt(vn / c2) + ADAM_EPS) + ADAM_WD * w_ref[l])

    blk = pl.BlockSpec((L, rb, C), lambda i: (0, i, 0))
    return _ride(
        body, plan, name, R // rb,
        in_specs=[pl.BlockSpec((_N_DEV, rb, C), lambda i: (0, i, 0))] * L + [blk, blk, blk],
        args=(*recvs, w, m, v),
        out_specs=[blk, blk, blk, blk],
        out_shape=[_sds((L, R, C), F32)] * 4, scratch_shapes=[])


def _pack_rows(flat_parts, dtype, row_mult):
    flat = jnp.concatenate([f.reshape(-1).astype(dtype) for f in flat_parts])
    n = flat.shape[0]
    per = _LANES * row_mult
    padded = -(-n // per) * per
    if padded != n:
        flat = jnp.concatenate([flat, jnp.zeros((padded - n,), dtype)])
    return flat.reshape(-1, _LANES)


def _unpack(flat, shapes):
    out, off = [], 0
    for s in shapes:
        n = math.prod(s)
        out.append(flat[off:off + n].reshape(s))
        off += n
    return out


def _to_pieces(full):
    n = full.shape[-1] // _N_DEV
    t = full.reshape(full.shape[:-1] + (_N_DEV, n))
    return jnp.moveaxis(t, -2, 0).reshape(_N_DEV, -1)


def _from_gathered(seg, shard_shape, axis):
    t = seg.reshape((_N_DEV,) + tuple(shard_shape))
    t = jnp.moveaxis(t, 0, axis)
    shape = list(shard_shape)
    shape[axis] *= _N_DEV
    return t.reshape(shape)


def kernel(x, meta_tokens, g_pre_mix, w_in, lru_conv_w, lru_conv_b, lru_wa, lru_ba, lru_wx, lru_bx, lru_lambda, conv_w, conv_b, conv_ln_g, conv_ln_b, g_out_lru, g_out_conv, w_out, g_post_mix, g_pre_ffn, w_up, ffn_conv_w, ffn_conv_b, w_down, g_post_ffn, loss_target, m_meta_tokens, m_g_pre_mix, m_w_in, m_lru_conv_w, m_lru_conv_b, m_lru_wa, m_lru_ba, m_lru_wx, m_lru_bx, m_lru_lambda, m_conv_w, m_conv_b, m_conv_ln_g, m_conv_ln_b, m_g_out_lru, m_g_out_conv, m_w_out, m_g_post_mix, m_g_pre_ffn, m_w_up, m_ffn_conv_w, m_ffn_conv_b, m_w_down, m_g_post_ffn, v_meta_tokens, v_g_pre_mix, v_w_in, v_lru_conv_w, v_lru_conv_b, v_lru_wa, v_lru_ba, v_lru_wx, v_lru_bx, v_lru_lambda, v_conv_w, v_conv_b, v_conv_ln_g, v_conv_ln_b, v_g_out_lru, v_g_out_conv, v_w_out, v_g_post_mix, v_g_pre_ffn, v_w_up, v_ffn_conv_w, v_ffn_conv_b, v_w_down, v_g_post_ffn):
    given = dict(locals())
    W = {n: given[n] for n in W_NAMES}
    M = {n: given['m_' + n] for n in W_NAMES}
    V = {n: given['v_' + n] for n in W_NAMES}
    S, D = x.shape[1], x.shape[2]
    L = g_pre_mix.shape[0]
    Dc = D // 2
    step = math.lcm(_TM_SEQ, _TM_DW)
    T = -(-(N_META + S) // step) * step

    first, rest = ['w_in', 'w_out'], ['w_up', 'w_down']

    def layer_pack(l, names):
        return [W[n][l].astype(_MXU) for n in names]

    def layer_weights(gathered, names):
        return {n: g if n in ('w_in', 'w_up') else g.reshape(-1, g.shape[2]) for n, g in zip(names, gathered)}

    small_pack = _pack_rows([W[n] for n in SMALL_SHARDED], F32, 8)
    *big_g, small_g = _run_plan(_GatherPlan(layer_pack(0, first) + [small_pack]), "gather_weights_first")
    small_segs = _unpack_cols(small_g.reshape(_N_DEV, -1), [W[n].shape for n in SMALL_SHARDED])
    full = {}
    for n, seg in zip(SMALL_SHARDED, small_segs):
        full[n] = _from_gathered(seg, W[n].shape, W[n].ndim - 1)

    p = {}
    for n in W_NAMES:
        if n in BIG or n == 'meta_tokens':
            continue
        a = full[n] if n in full else W[n]
        p[n] = a.reshape(L, 1, a.shape[1]) if a.ndim == 2 else a

    pad_rows = T - N_META - S
    h = _perm(jnp.concatenate([full['meta_tokens'], x[0], jnp.zeros((pad_rows, D), F32)], axis=0), _TM_SEQ)
    tgt = _perm(jnp.concatenate([jnp.zeros((N_META, D), F32), loss_target[0], jnp.zeros((pad_rows, D), F32)],
                                axis=0), _TM_SEQ)

    saved = []
    wl = layer_weights(big_g, first)
    for l in range(L):
        plan = _GatherPlan(layer_pack(0, rest)) if l == 0 else None
        (z1, xc, hs, ya, cc, yb, y, o, hmid), nxt = _mixers_fwd(h, p, wl['w_in'], wl['w_out'], l,
                                                                f"mixers_fwd_l{l}", plan)
        if plan:
            wl = {**wl, **layer_weights(nxt, rest)}
        plan = _GatherPlan(layer_pack(l + 1, BIG)) if l + 1 < L else None
        (z2, a2, f, hout), nxt = _ffn_fwd(hmid, p, wl['w_up'], wl['w_down'], l, f"ffn_fwd_l{l}", plan)
        saved.append(dict(h=h, z1=z1, xc=xc, hs=hs, ya=ya, cc=cc, yb=yb, y=y, o=o, hmid=hmid,
                          z2=z2, a2=a2, f=f, w=wl))
        h = hout
        if plan:
            wl = layer_weights(nxt, BIG)
    loss_tile, dh = _loss_head(h, tgt, S, "loss_head")
    loss = lax.psum(loss_tile[0, 0], ("x", "y", "c"))

    small_g_names = ['g_pre_mix', 'lru_conv_w', 'lru_conv_b', 'lru_wa', 'lru_ba', 'lru_wx', 'lru_bx', 'lru_lambda',
                     'conv_w', 'conv_b', 'conv_ln_g', 'conv_ln_b', 'g_out_lru', 'g_out_conv', 'g_post_mix',
                     'g_pre_ffn', 'ffn_conv_w', 'ffn_conv_b', 'g_post_ffn']
    per_layer = {n: [None] * L for n in small_g_names}
    recv = {n: [None] * L for n in BIG}
    pending = None
    for l in reversed(range(L)):
        sv = saved[l]
        wl = sv['w']
        plan = _ExchangePlan(list(pending), ['cols', 'cols']) if pending else None
        (df, dup0, dwf, dbf, dgpf, dhm, dgpre), got = _ffn_bwd(
            dh, sv['f'], sv['z2'], sv['hmid'], p, wl['w_up'], wl['w_down'], l, f"ffn_bwd_l{l}", plan)
        if plan:
            recv['w_up'][l + 1], recv['w_in'][l + 1] = got
        d_down = _mm_tn(sv['a2'], df, f"dw_down_l{l}")
        d_up = _mm_tn(sv['z2'], dup0, f"dw_up_l{l}")
        do, dya, dyb, dgpm, dgol, dgoc = _mix_bwd(dhm, sv['o'], sv['ya'], sv['yb'], p, wl['w_out'], l,
                                                  f"mix_bwd_l{l}")
        d_out = _mm_tn(sv['y'], do, f"dw_out_l{l}")
        plan = (_ExchangePlan([d_down, d_out, d_up], ['rows', 'rows', 'cols']) if l == 0
                else _ExchangePlan([d_down, d_out], ['rows', 'rows']))
        (dproj, dwl, dbl, dwa, dba, dwx, dbx, dlam, dwc, dbc, dlg, dlb, dh, dgpmix), got = _mixers_bwd(
            dya, dyb, sv['z1'], sv['xc'], sv['hs'], sv['cc'], sv['h'], dhm, p, wl['w_in'], l,
            f"mixers_bwd_l{l}", plan)
        recv['w_down'][l], recv['w_out'][l] = got[0], got[1]
        if l == 0:
            recv['w_up'][0] = got[2]
        d_in = _mm_tn(sv['z1'], dproj, f"dw_in_l{l}")
        pending = (d_up, d_in)
        for n, val in (('g_pre_mix', dgpmix), ('lru_conv_w', dwl), ('lru_conv_b', dbl), ('lru_wa', dwa),
                       ('lru_ba', dba), ('lru_wx', dwx), ('lru_bx', dbx), ('lru_lambda', dlam), ('conv_w', dwc),
                       ('conv_b', dbc), ('conv_ln_g', dlg), ('conv_ln_b', dlb), ('g_out_lru', dgol),
                       ('g_out_conv', dgoc), ('g_post_mix', dgpm), ('g_pre_ffn', dgpre), ('ffn_conv_w', dwf),
                       ('ffn_conv_b', dbf), ('g_post_ffn', dgpf)):
            per_layer[n][l] = val
    dh = _unperm(dh, _TM_SEQ)
    grad_x = dh[N_META:N_META + S][None]
    partial = {n: jnp.stack(per_layer[n]).reshape((L,) + tuple(
        (full[n] if n in full else W[n]).shape[1:])) for n in small_g_names}
    partial['meta_tokens'] = dh[0:N_META]

    shard_pack = jnp.concatenate([_to_pieces(partial[n]) for n in SMALL_SHARDED], axis=1)
    n_sh = shard_pack.shape[1]
    rs = -(-n_sh // (8 * _LANES)) * 8
    shard_pack = jnp.concatenate([shard_pack, jnp.zeros((_N_DEV, rs * _LANES - n_sh), F32)], axis=1)
    rep_flat = jnp.concatenate([partial[n].reshape(-1) for n in REPLICATED])
    n_rep = rep_flat.shape[0]
    rr = -(-n_rep // (_N_DEV * _REP_ROWS * _LANES)) * _REP_ROWS
    rep_flat = jnp.concatenate([rep_flat, jnp.zeros((_N_DEV * rr * _LANES - n_rep,), F32)])
    small_send = jnp.concatenate([shard_pack, rep_flat.reshape(_N_DEV, rr * _LANES)], axis=1)
    small_send = small_send.reshape(_N_DEV, rs + rr, _LANES)
    out = {}
    res, got = _adamw_layers(recv['w_up'], W['w_up'], M['w_up'], V['w_up'], "adamw_w_up",
                             _ExchangePlan([pending[1], small_send], ['cols', 'slots']))
    out['w_up'] = list(res)
    recv['w_in'][0], r_small = got
    small_red = _sum_sources(r_small, "sum_small")
    (rep_g,) = _run_plan(_GatherPlan([small_red[rs:]]), "gather_replicated_grads")
    rep_g = rep_g.reshape(_N_DEV * rr, _LANES)
    for n in ('w_in', 'w_out', 'w_down'):
        out[n] = list(_adamw_layers(recv[n], W[n], M[n], V[n], f"adamw_{n}")[0])
    sh_shapes = [W[n].shape for n in SMALL_SHARDED]
    res = _adamw(small_red[:rs], _pack_rows([W[n] for n in SMALL_SHARDED], F32, 8),
                 _pack_rows([M[n] for n in SMALL_SHARDED], F32, 8),
                 _pack_rows([V[n] for n in SMALL_SHARDED], F32, 8), "adamw_small_sharded")
    for k in range(4):
        for n, val in zip(SMALL_SHARDED, _unpack(res[k].reshape(-1), sh_shapes)):
            out.setdefault(n, [None] * 4)[k] = val
    rep_shapes = [W[n].shape for n in REPLICATED]
    res = _adamw(rep_g, _pack_rows([W[n] for n in REPLICATED], F32, _REP_ROWS * _N_DEV),
                 _pack_rows([M[n] for n in REPLICATED], F32, _REP_ROWS * _N_DEV),
                 _pack_rows([V[n] for n in REPLICATED], F32, _REP_ROWS * _N_DEV), "adamw_replicated")
    for k in range(4):
        for n, val in zip(REPLICATED, _unpack(res[k].reshape(-1), rep_shapes)):
            out.setdefault(n, [None] * 4)[k] = val

    return (loss, grad_x, *[out[n][0] for n in W_NAMES], *[out[n][1] for n in W_NAMES],
            *[out[n][2] for n in W_NAMES], *[out[n][3] for n in W_NAMES])


def _unpack_cols(gathered, shapes):
    out, off = [], 0
    for s in shapes:
        n = math.prod(s)
        out.append(gathered[:, off:off + n])
        off += n
    return out
```

```python
import math

import jax
import jax.numpy as jnp
from jax import lax
from jax.experimental import pallas as pl
from jax.experimental.pallas import tpu as pltpu

F32 = jnp.float32
_MXU = jnp.bfloat16
_WIRE = jnp.bfloat16
_TM_DW = 2816
_TM_SEQ = 256
_NB = 768
_VMEM_LIMIT = 56 * 1024 * 1024
_ADAM_BLOCK_ELEMS = 128 * 1024
_LANES = 128
_N_DEV = 8
_REP_ROWS = 64

EPS = 1e-6
N_META = 16
LRU_C = 8.0
CONV_GROUPS = 4
ADAM_LR, ADAM_B1, ADAM_B2, ADAM_EPS, ADAM_WD, ADAM_STEP = 0.001, 0.9, 0.999, 1e-08, 0.01, 10
_GELU_K0 = math.sqrt(2.0 / math.pi)
_GELU_K1 = 0.044715

W_NAMES = ['meta_tokens', 'g_pre_mix', 'w_in', 'lru_conv_w', 'lru_conv_b', 'lru_wa', 'lru_ba', 'lru_wx', 'lru_bx',
           'lru_lambda', 'conv_w', 'conv_b', 'conv_ln_g', 'conv_ln_b', 'g_out_lru', 'g_out_conv', 'w_out',
           'g_post_mix', 'g_pre_ffn', 'w_up', 'ffn_conv_w', 'ffn_conv_b', 'w_down', 'g_post_ffn']
BIG = ['w_in', 'w_out', 'w_up', 'w_down']
SMALL_SHARDED = ['meta_tokens', 'lru_conv_w', 'conv_w', 'ffn_conv_w']
REPLICATED = [n for n in W_NAMES if n not in BIG and n not in SMALL_SHARDED]


def _params():
    return pltpu.CompilerParams(vmem_limit_bytes=_VMEM_LIMIT)


def _pick(n, pref):
    if n <= pref:
        return n
    best = None
    for b in range(_LANES, pref + 1, _LANES):
        if n % b == 0:
            best = b
    assert best is not None, (n, pref)
    return best


def _pick_rows(n, pref):
    if n <= pref:
        return n
    best = None
    for b in range(8, pref + 1, 8):
        if n % b == 0:
            best = b
    assert best is not None, (n, pref)
    return best


def _lspec(l, *dims):
    zeros = (0,) * len(dims)
    return pl.BlockSpec((None,) + tuple(dims), lambda *_: (l,) + zeros)


def _sds(shape, dtype):
    return jax.ShapeDtypeStruct(tuple(shape), dtype)


def _rms_fwd(x, g):
    r = lax.rsqrt(jnp.mean(x * x, axis=-1, keepdims=True) + EPS)
    return (x * r) * g


def _rms_bwd(x, g, dy):
    r = lax.rsqrt(jnp.mean(x * x, axis=-1, keepdims=True) + EPS)
    xh = x * r
    dg = jnp.sum(dy * xh, axis=0, keepdims=True)
    dxh = dy * g
    dx = r * (dxh - xh * jnp.mean(dxh * xh, axis=-1, keepdims=True))
    return dx, dg


def _gelu(x):
    t = jnp.tanh(_GELU_K0 * (x + _GELU_K1 * (x * x * x)))
    return 0.5 * x * (1.0 + t), t


def _gelu_grad(x, t):
    return 0.5 * (1.0 + t) + 0.5 * x * (1.0 - t * t) * (_GELU_K0 * (1.0 + 3.0 * _GELU_K1 * x * x))


def _log1p(e):
    u = 1.0 + e
    return jnp.where(u == 1.0, e, jnp.log(u) * (e / (u - 1.0)))


def _softplus(z):
    return jnp.maximum(z, 0.0) + _log1p(jnp.exp(-jnp.abs(z)))


def _one_minus_exp(x):
    p = -x * (1.0 + x * (0.5 + x * (1.0 / 6 + x * (1.0 / 24 + x * (1.0 / 120 + x * (1.0 / 720))))))
    return jnp.where(x > -0.125, p, 1.0 - jnp.exp(x))


def _shift_down(x, s, fill, row):
    return jnp.where(row >= s, pltpu.roll(x, s, 0), fill)


def _shift_up(x, s, fill, row):
    n = x.shape[0]
    return jnp.where(row < n - s, pltpu.roll(x, n - s, 0), fill)


def _bd_mm(xb, w_ref, heads, hd):
    return jnp.concatenate(
        [jnp.dot(xb[:, h * hd:(h + 1) * hd], w_ref[h].astype(_MXU), preferred_element_type=F32)
         for h in range(heads)], axis=-1)


def _bd_mm_t(db, w_ref, heads, hd):
    return jnp.concatenate(
        [lax.dot_general(db[:, h * hd:(h + 1) * hd], w_ref[h].astype(_MXU), (((1,), (1,)), ((), ())),
                         preferred_element_type=F32)
         for h in range(heads)], axis=-1)


def _gates(xc, wa_ref, ba, wx_ref, bx, lam, heads, hd):
    xcb = xc.astype(_MXU)
    ra = jax.nn.sigmoid(_bd_mm(xcb, wa_ref, heads, hd) + ba)
    ri = jax.nn.sigmoid(_bd_mm(xcb, wx_ref, heads, hd) + bx)
    sp = _softplus(-lam)
    la = (-LRU_C) * ra * sp
    a = jnp.exp(la)
    mult = jnp.sqrt(_one_minus_exp(2.0 * la))
    return xcb, ra, ri, sp, a, mult


def _group_norm(cc, groups):
    gs = cc.shape[-1] // groups
    outs, rss = [], []
    for g in range(groups):
        seg = cc[:, g * gs:(g + 1) * gs]
        mu = jnp.mean(seg, axis=-1, keepdims=True)
        d = seg - mu
        rs = lax.rsqrt(jnp.mean(d * d, axis=-1, keepdims=True) + EPS)
        outs.append(d * rs)
        rss.append(rs)
    return jnp.concatenate(outs, axis=-1), rss


_MESH = pl.DeviceIdType.MESH
_ANY = pl.BlockSpec(memory_space=pl.ANY)


def _my_index():
    return 4 * lax.axis_index("x") + 2 * lax.axis_index("y") + lax.axis_index("c")


class _GatherPlan:
    def __init__(self, blocks):
        self.blocks = list(blocks)

    def operands(self):
        return self.blocks

    def out_shape(self):
        return [_sds((_N_DEV,) + b.shape, b.dtype) for b in self.blocks]

    def scratch(self):
        n = len(self.blocks)
        return [pltpu.SemaphoreType.DMA((7 * n,)), pltpu.SemaphoreType.DMA((7 * n,)), pltpu.SemaphoreType.DMA((n,))]

    def _copies(self, a, ins, outs, sems):
        send_sems, recv_sems, local_sems = sems
        x, y, c = lax.axis_index("x"), lax.axis_index("y"), lax.axis_index("c")
        me, sibling = (x, y, c), (x, y, 1 - c)
        chips = [(1 - x, y), (x, 1 - y), (1 - x, 1 - y)]

        def slot(dev):
            return outs[a].at[4 * dev[0] + 2 * dev[1] + dev[2]]

        def copy(k, block, to, src=None):
            return pltpu.make_async_remote_copy(
                src_ref=slot(block) if src is None else src, dst_ref=slot(block),
                send_sem=send_sems.at[7 * a + k], recv_sem=recv_sems.at[7 * a + k],
                device_id=to, device_id_type=_MESH)

        mine = pltpu.make_async_copy(ins[a], slot(me), local_sems.at[a])
        first = [copy(0, me, sibling, src=ins[a])]
        first += [copy(1 + j, me, (*chip, c), src=ins[a]) for j, chip in enumerate(chips)]
        passed = [copy(4 + j, (*chip, c), sibling) for j, chip in enumerate(chips)]
        from_chips = [copy(1 + j, (*chip, c), me) for j, chip in enumerate(chips)]
        from_sibling = [copy(0, sibling, me)] + [copy(4 + j, (*chip, 1 - c), me) for j, chip in enumerate(chips)]
        return mine, first, passed, from_chips, from_sibling

    def start(self, ins, outs, sems):
        for a in range(len(self.blocks)):
            mine, first, _, _, _ = self._copies(a, ins, outs, sems)
            mine.start()
            for cp in first:
                cp.start()

    def forward(self, ins, outs, sems):
        for a in range(len(self.blocks)):
            _, _, passed, from_chips, _ = self._copies(a, ins, outs, sems)
            for j in range(3):
                from_chips[j].wait_recv()
                passed[j].start()

    def finish(self, ins, outs, sems):
        for a in range(len(self.blocks)):
            mine, first, passed, _, from_sibling = self._copies(a, ins, outs, sems)
            for cp in from_sibling:
                cp.wait_recv()
            for cp in first + passed:
                cp.wait_send()
            mine.wait()

    def begin(self, i, steps, ins, outs, sems):
        @pl.when(i == 0)
        def _():
            self.start(ins, outs, sems)

        @pl.when(i == (3 * steps) // 4)
        def _():
            self.forward(ins, outs, sems)

    def end(self, i, steps, ins, outs, sems):
        @pl.when(i == steps - 1)
        def _():
            self.finish(ins, outs, sems)


class _ExchangePlan:
    def __init__(self, arrs, kinds):
        self.arrs, self.kinds = list(arrs), list(kinds)

    def _piece_shape(self, a):
        shp = self.arrs[a].shape
        if self.kinds[a] == 'cols':
            return (shp[0], shp[1] // _N_DEV)
        if self.kinds[a] == 'rows':
            return (shp[0] // _N_DEV, shp[1])
        return tuple(shp[1:])

    def operands(self):
        return self.arrs

    def out_shape(self):
        return [_sds((_N_DEV,) + self._piece_shape(a), self.arrs[a].dtype) for a in range(len(self.arrs))]

    def scratch(self):
        n = len(self.arrs)
        return [pltpu.SemaphoreType.DMA((_N_DEV * n,)), pltpu.SemaphoreType.DMA((_N_DEV * n,)),
                pltpu.SemaphoreType.DMA((n,))]

    def _copies(self, ins, outs, sems):
        send_sems, recv_sems, local_sems = sems
        me = _my_index()

        def piece(a, j):
            ps = self._piece_shape(a)
            if self.kinds[a] == 'cols':
                return ins[a].at[:, pl.ds(j * ps[1], ps[1])]
            if self.kinds[a] == 'rows':
                return ins[a].at[pl.ds(j * ps[0], ps[0]), :]
            return ins[a].at[j]

        def remote(a, j):
            return pltpu.make_async_remote_copy(
                src_ref=piece(a, j), dst_ref=outs[a].at[me],
                send_sem=send_sems.at[_N_DEV * a + j], recv_sem=recv_sems.at[_N_DEV * a + me],
                device_id=(j >> 2, (j >> 1) & 1, j & 1), device_id_type=_MESH)

        def arrival(a, s):
            return pltpu.make_async_remote_copy(
                src_ref=piece(a, s), dst_ref=outs[a].at[s],
                send_sem=send_sems.at[_N_DEV * a + s], recv_sem=recv_sems.at[_N_DEV * a + s],
                device_id=(s >> 2, (s >> 1) & 1, s & 1), device_id_type=_MESH)

        def local(a, j):
            return pltpu.make_async_copy(piece(a, j), outs[a].at[j], local_sems.at[a])

        return me, remote, arrival, local

    def start(self, ins, outs, sems):
        me, remote, _, local = self._copies(ins, outs, sems)
        for j in range(_N_DEV):
            for a in range(len(self.arrs)):
                @pl.when(me != j)
                def _(a=a, j=j):
                    remote(a, j).start()

                @pl.when(me == j)
                def _(a=a, j=j):
                    local(a, j).start()

    def finish(self, ins, outs, sems):
        me, remote, arrival, local = self._copies(ins, outs, sems)
        for j in range(_N_DEV):
            for a in range(len(self.arrs)):
                @pl.when(me != j)
                def _(a=a, j=j):
                    arrival(a, j).wait_recv()
                    remote(a, j).wait_send()

                @pl.when(me == j)
                def _(a=a, j=j):
                    local(a, j).wait()

    def forward(self, ins, outs, sems):
        pass

    def begin(self, i, steps, ins, outs, sems):
        @pl.when(i == 0)
        def _():
            self.start(ins, outs, sems)

    def end(self, i, steps, ins, outs, sems):
        @pl.when(i == steps - 1)
        def _():
            self.finish(ins, outs, sems)


def _ride(main, plan, name, grid, in_specs, args, out_specs, out_shape, scratch_shapes):
    grid = (grid,) if isinstance(grid, int) else tuple(grid)
    steps = math.prod(grid)
    n_in, n_out, n_sc = len(in_specs), len(out_specs), len(scratch_shapes)
    p_args = plan.operands() if plan else []
    p_out = plan.out_shape() if plan else []
    p_sc = plan.scratch() if plan else []

    def body(*refs):
        k = 0
        ins = refs[k:k + n_in]; k += n_in
        p_ins = refs[k:k + len(p_args)]; k += len(p_args)
        outs = refs[k:k + n_out]; k += n_out
        p_outs = refs[k:k + len(p_out)]; k += len(p_out)
        scr = refs[k:k + n_sc]; k += n_sc
        sems = refs[k:]
        i = pl.program_id(0)
        for axis in range(1, len(grid)):
            i = i * grid[axis] + pl.program_id(axis)
        if plan:
            plan.begin(i, steps, p_ins, p_outs, sems)
        main(*ins, *outs, *scr)
        if plan:
            plan.end(i, steps, p_ins, p_outs, sems)

    res = pl.pallas_call(
        body, name=name, grid=grid,
        in_specs=list(in_specs) + [_ANY] * len(p_args),
        out_specs=list(out_specs) + [_ANY] * len(p_out),
        out_shape=list(out_shape) + p_out,
        scratch_shapes=list(scratch_shapes) + p_sc,
        compiler_params=_params())(*args, *p_args)
    return res[:n_out], res[n_out:]


def _run_plan(plan, name):
    n_args, n_out = len(plan.operands()), len(plan.out_shape())

    def body(*refs):
        ins, outs, sems = refs[:n_args], refs[n_args:n_args + n_out], refs[n_args + n_out:]
        plan.start(ins, outs, sems)
        plan.forward(ins, outs, sems)
        plan.finish(ins, outs, sems)

    return pl.pallas_call(
        body, name=name, in_specs=[_ANY] * n_args, out_specs=[_ANY] * n_out,
        out_shape=plan.out_shape(), scratch_shapes=plan.scratch())(*plan.operands())


def _perm(a, tm):
    T, C = a.shape
    return a.reshape(T // tm, 8, tm // 8, C).transpose(0, 2, 1, 3).reshape(T, C)


def _unperm(a, tm):
    T, C = a.shape
    return a.reshape(T // tm, tm // 8, 8, C).transpose(0, 2, 1, 3).reshape(T, C)


def _wrap_prev(prev_z, z):
    n = z.shape[0]
    sub = lax.broadcasted_iota(jnp.int32, z.shape, 0) & 7
    return jnp.where(sub == 0, pltpu.roll(prev_z, n - 7, 0), pltpu.roll(z, 1, 0))


def _wrap_next(next_z, z):
    n = z.shape[0]
    sub = lax.broadcasted_iota(jnp.int32, z.shape, 0) & 7
    return jnp.where(sub == 7, pltpu.roll(next_z, 7, 0), pltpu.roll(z, n - 1, 0))


def _fill_causal(buf, halo, x, tm):
    nh = halo.shape[0]
    buf[nh:nh + tm, :] = x
    z = buf[tm:tm + nh, :]
    buf[0:nh, :] = _wrap_prev(halo[...], z)
    halo[...] = z


def _fill_anticausal(buf, halo, dy, tm):
    nh = halo.shape[0]
    buf[0:tm, :] = dy
    z = buf[0:nh, :]
    buf[tm:tm + nh, :] = _wrap_next(halo[...], z)
    halo[...] = z


def _scan_fwd(a, u, abuf, ubuf, hcar, tm):
    D = a.shape[1]
    G = tm // 8
    abuf[...] = a
    ubuf[...] = u

    def step(j, c):
        h, pr = c
        r = pl.multiple_of(j * 8, 8)
        aj = abuf[pl.ds(r, 8), :]
        h = aj * h + ubuf[pl.ds(r, 8), :]
        pr = aj * pr
        ubuf[pl.ds(r, 8), :] = h
        abuf[pl.ds(r, 8), :] = pr
        return h, pr

    sub8 = lax.broadcasted_iota(jnp.int32, (8, D), 0)
    e, q = lax.fori_loop(1, G, step, (ubuf[0:8, :], abuf[0:8, :]))
    for s in (1, 2, 4):
        e = e + q * _shift_down(e, s, 0.0, sub8)
        q = q * _shift_down(q, s, 1.0, sub8)
    e = e + q * hcar[...]
    cin = jnp.where(sub8 == 0, hcar[...], pltpu.roll(e, 1, 0))
    return ubuf[...] + abuf[...] * jnp.tile(cin, (G, 1))


def _scan_bwd(a, d, abuf, gbuf, gcar, tmp8, tm):
    D = a.shape[1]
    G = tm // 8
    abuf[...] = a
    gbuf[...] = d

    def step(k, c):
        g_next, a_next, r_j = c
        r = pl.multiple_of((G - 1 - k) * 8, 8)
        aj = abuf[pl.ds(r, 8), :]
        g = gbuf[pl.ds(r, 8), :] + a_next * g_next
        gbuf[pl.ds(r, 8), :] = g
        abuf[pl.ds(r, 8), :] = r_j
        return g, aj, aj * r_j

    last = 8 * (G - 1)
    a_last = abuf[last:last + 8, :]
    abuf[last:last + 8, :] = jnp.ones((8, D), F32)
    g0, a0, _ = lax.fori_loop(1, G, step, (gbuf[last:last + 8, :], a_last, a_last))
    r0 = abuf[0:8, :]
    sub8 = lax.broadcasted_iota(jnp.int32, (8, D), 0)
    x, q = a0 * g0, a0 * r0
    for s in (1, 2, 4):
        x = x + q * _shift_up(x, s, 0.0, sub8)
        q = q * _shift_up(q, s, 1.0, sub8)
    x = x + q * gcar[...]
    cin = jnp.where(sub8 == 7, gcar[...], pltpu.roll(x, 7, 0))
    g = gbuf[...] + abuf[...] * jnp.tile(cin, (G, 1))
    tmp8[...] = x
    gcar[...] = tmp8[0:1, :]
    return g


def _mixers_fwd(h, p, win, wout, l, name, plan=None):
    T, D = h.shape
    D3 = win.shape[0] * win.shape[2]
    Dc = D // 2
    heads, hd = p['lru_wa'].shape[1], p['lru_wa'].shape[2]
    tm = _TM_SEQ
    kl, kc = p['lru_conv_w'].shape[1], p['conv_w'].shape[1]
    nhl, nhc = 8 * (kl - 1), 8 * (kc - 1)
    assert nhc <= tm

    def body(h_ref, gz_ref, wl_ref, bl_ref, wa_ref, ba_ref, wx_ref, bx_ref, lam_ref, wc_ref, bc_ref, lg_ref, lb_ref,
             gol_ref, goc_ref, gpm_ref, win_hbm, wout_hbm,
             z_ref, xc_ref, hs_ref, ya_ref, cc_ref, yb_ref, y_ref, o_ref, hm_ref,
             xbuf, xhalo, hcar, cbuf, chalo, abuf, ubuf, win_ref, wout_ref, wsem):
        _load_weights(pl.program_id(0), _shard_pairs(win_hbm, win_ref) + [(wout_hbm, wout_ref)], wsem)

        @pl.when(pl.program_id(0) == 0)
        def _():
            xhalo[...] = jnp.zeros_like(xhalo)
            chalo[...] = jnp.zeros_like(chalo)
            hcar[...] = jnp.zeros_like(hcar)

        z = _rms_fwd(h_ref[...], gz_ref[...]).astype(_MXU)
        z_ref[...] = z

        def proj(lo, hi):
            return jnp.dot(z, win_ref[:, lo:hi], preferred_element_type=F32)

        _fill_causal(xbuf, xhalo, proj(0, D), tm)
        xc = bl_ref[...] + wl_ref[0:1, :] * xbuf[0:tm, :]
        for k in range(1, kl):
            xc = xc + wl_ref[k:k + 1, :] * xbuf[8 * k:8 * k + tm, :]
        xc_ref[...] = xc
        _, ra, ri, sp, a, mult = _gates(xc, wa_ref, ba_ref[...], wx_ref, bx_ref[...], lam_ref[...], heads, hd)
        hs = _scan_fwd(a, mult * (ri * xc), abuf, ubuf, hcar, tm)
        hs_ref[...] = hs
        hcar[...] = hs_ref[pl.ds(tm - 1, 1), :]
        gg, _ = _gelu(proj(D, 2 * D))
        ya = hs * gg
        ya_ref[...] = ya

        _fill_causal(cbuf, chalo, proj(2 * D, 2 * D + Dc) * jax.nn.sigmoid(proj(2 * D + Dc, 3 * D)), tm)
        cc = bc_ref[...] + wc_ref[0:1, :] * cbuf[0:tm, :]
        for k in range(1, kc):
            cc = cc + wc_ref[k:k + 1, :] * cbuf[8 * k:8 * k + tm, :]
        cc_ref[...] = cc
        nrm, _ = _group_norm(cc, CONV_GROUPS)
        cl = nrm * lg_ref[...] + lb_ref[...]
        yb = cl * jax.nn.sigmoid(cl)
        yb_ref[...] = yb

        y = jnp.concatenate([_rms_fwd(ya, gol_ref[...]), _rms_fwd(yb, goc_ref[...])], axis=-1).astype(_MXU)
        y_ref[...] = y
        o = jnp.dot(y, wout_ref[...], preferred_element_type=F32)
        o_ref[...] = o
        hm_ref[...] = h_ref[...] + _rms_fwd(o, gpm_ref[...])

    row_d = pl.BlockSpec((tm, D), lambda i: (i, 0))
    row_c = pl.BlockSpec((tm, Dc), lambda i: (i, 0))
    return _ride(
        body, plan, name, T // tm,
        in_specs=[row_d, _lspec(l, 1, D),
                  _lspec(l, kl, D), _lspec(l, 1, D), _lspec(l, heads, hd, hd), _lspec(l, 1, D),
                  _lspec(l, heads, hd, hd), _lspec(l, 1, D), _lspec(l, 1, D),
                  _lspec(l, kc, Dc), _lspec(l, 1, Dc), _lspec(l, 1, Dc), _lspec(l, 1, Dc),
                  _lspec(l, 1, D), _lspec(l, 1, Dc), _lspec(l, 1, D), _ANY, _ANY],
        args=(h, p['g_pre_mix'], p['lru_conv_w'], p['lru_conv_b'], p['lru_wa'], p['lru_ba'], p['lru_wx'], p['lru_bx'],
              p['lru_lambda'], p['conv_w'], p['conv_b'], p['conv_ln_g'], p['conv_ln_b'],
              p['g_out_lru'], p['g_out_conv'], p['g_post_mix'], win, wout),
        out_specs=[row_d, row_d, row_d, row_d, row_c, row_c, pl.BlockSpec((tm, D + Dc), lambda i: (i, 0)), row_d, row_d],
        out_shape=[_sds((T, D), _MXU), _sds((T, D), F32), _sds((T, D), F32), _sds((T, D), F32),
                   _sds((T, Dc), F32), _sds((T, Dc), F32), _sds((T, D + Dc), _MXU), _sds((T, D), F32),
                   _sds((T, D), F32)],
        scratch_shapes=[pltpu.VMEM((nhl + tm, D), F32), pltpu.VMEM((nhl, D), F32), pltpu.VMEM((1, D), F32),
                        pltpu.VMEM((nhc + tm, Dc), F32), pltpu.VMEM((nhc, Dc), F32),
                        pltpu.VMEM((tm, D), F32), pltpu.VMEM((tm, D), F32),
                        pltpu.VMEM((D, D3), _MXU), pltpu.VMEM((D + Dc, D), _MXU),
                        pltpu.SemaphoreType.DMA((_N_DEV + 1,))])


def _shard_pairs(w_hbm, w_ref):
    n = w_hbm.shape[2]
    return [(w_hbm.at[j], w_ref.at[:, pl.ds(j * n, n)]) for j in range(w_hbm.shape[0])]


def _load_weights(i, pairs, sems):
    @pl.when(i == 0)
    def _():
        copies = [pltpu.make_async_copy(src, dst, sems.at[k]) for k, (src, dst) in enumerate(pairs)]
        for cp in copies:
            cp.start()
        for cp in copies:
            cp.wait()


def _up_into(ubuf, z, wup_v, nh, tm, cw):
    for c in range(wup_v.shape[1] // cw):
        cs = slice(c * cw, (c + 1) * cw)
        ubuf[nh:nh + tm, cs] = jnp.dot(z, wup_v[:, cs], preferred_element_type=F32)


def _ffn_fwd(hmid, p, wup, wdown, l, name, plan=None):
    T, D = hmid.shape
    F2 = wup.shape[0] * wup.shape[2]
    Fh = F2 // 2
    tm = _TM_SEQ
    cw = _pick(Fh, _NB)
    kf = p['ffn_conv_w'].shape[1]
    nh = 8 * (kf - 1)

    def body(hm_ref, gz_ref, wf_ref, bf_ref, g_ref, wu_hbm, wd_hbm, z_ref, a2_ref, f_ref, ho_ref,
             ubuf, uhalo, wu_ref, wd_ref, wsem):
        i = pl.program_id(0)
        _load_weights(i, _shard_pairs(wu_hbm, wu_ref) + [(wd_hbm, wd_ref)], wsem)

        @pl.when(i == 0)
        def _():
            uhalo[...] = jnp.zeros_like(uhalo)

        z = _rms_fwd(hm_ref[...], gz_ref[...]).astype(_MXU)
        z_ref[...] = z
        _up_into(ubuf, z, wu_ref, nh, tm, cw)
        tail = ubuf[tm:tm + nh, :]
        ubuf[0:nh, :] = _wrap_prev(uhalo[...], tail)
        uhalo[...] = tail

        def conv(cs):
            acc = bf_ref[:, cs]
            for k in range(kf):
                acc = acc + wf_ref[k:k + 1, cs] * ubuf[8 * k:8 * k + tm, cs]
            return acc

        f = None
        for c in range(Fh // cw):
            gs = slice(c * cw, (c + 1) * cw)
            gg, _ = _gelu(conv(gs))
            a2 = (gg * conv(slice(Fh + c * cw, Fh + (c + 1) * cw))).astype(_MXU)
            a2_ref[:, gs] = a2
            part = jnp.dot(a2, wd_ref[gs, :], preferred_element_type=F32)
            f = part if f is None else f + part
        f_ref[...] = f
        ho_ref[...] = hm_ref[...] + _rms_fwd(f, g_ref[...])

    row_d = pl.BlockSpec((tm, D), lambda i: (i, 0))
    return _ride(
        body, plan, name, T // tm,
        in_specs=[row_d, _lspec(l, 1, D), _lspec(l, kf, F2), _lspec(l, 1, F2), _lspec(l, 1, D), _ANY, _ANY],
        args=(hmid, p['g_pre_ffn'], p['ffn_conv_w'], p['ffn_conv_b'], p['g_post_ffn'], wup, wdown),
        out_specs=[row_d, pl.BlockSpec((tm, Fh), lambda i: (i, 0)), row_d, row_d],
        out_shape=[_sds((T, D), _MXU), _sds((T, Fh), _MXU), _sds((T, D), F32), _sds((T, D), F32)],
        scratch_shapes=[pltpu.VMEM((nh + tm, F2), F32), pltpu.VMEM((nh, F2), F32),
                        pltpu.VMEM((D, F2), _MXU), pltpu.VMEM((Fh, D), _MXU), pltpu.SemaphoreType.DMA((_N_DEV + 1,))])


def _loss_head(h, tgt, n_real, name):
    T, D = h.shape
    tm = _TM_SEQ

    def body(h_ref, t_ref, loss_ref, dh_ref):
        i = pl.program_id(0)

        @pl.when(i == 0)
        def _():
            loss_ref[...] = jnp.zeros_like(loss_ref)

        pos = lax.broadcasted_iota(jnp.int32, (tm, D), 0)
        row = i * tm + (pos & 7) * (tm // 8) + (pos >> 3)
        e = jnp.where((row >= N_META) & (row < N_META + n_real), h_ref[...] - t_ref[...], 0.0)
        dh_ref[...] = e * (1.0 / D)
        loss_ref[...] += 0.5 * jnp.sum(jnp.mean(e * e, axis=-1, keepdims=True), axis=0, keepdims=True)

    row_d = pl.BlockSpec((tm, D), lambda i: (i, 0))
    return pl.pallas_call(
        body, name=name, grid=(T // tm,),
        in_specs=[row_d, row_d],
        out_specs=[pl.BlockSpec((8, _LANES), lambda i: (0, 0)), row_d],
        out_shape=[_sds((8, _LANES), F32), _sds((T, D), F32)],
        compiler_params=_params())(h, tgt)


def _ffn_bwd(dh, f, z2, hmid, p, wup, wdown, l, name, plan=None):
    T, D = dh.shape
    F2 = wup.shape[0] * wup.shape[2]
    Fh = F2 // 2
    tm = _TM_SEQ
    nt = T // tm
    cw = _pick(Fh, _NB)
    kf = p['ffn_conv_w'].shape[1]
    nh = 8 * (kf - 1)
    assert tm % nh == 0

    def body(dh_ref, f_ref, z_ref, zp_ref, hm_ref, wf_ref, bf_ref, g_ref, gz_ref, wu_hbm, wd_hbm,
             df_ref, dup_ref, dwf_ref, dbf_ref, dg_ref, dhm_ref, dgz_ref,
             ubuf, dbuf, dhalo, wu_ref, wd_ref, wsem):
        i = pl.program_id(0)
        r = nt - 1 - i
        _load_weights(i, _shard_pairs(wu_hbm, wu_ref) + [(wd_hbm, wd_ref)], wsem)

        @pl.when(i == 0)
        def _():
            for ref in (dhalo, dwf_ref, dbf_ref, dg_ref, dgz_ref):
                ref[...] = jnp.zeros_like(ref)

        df, dg = _rms_bwd(f_ref[...], g_ref[...], dh_ref[...])
        dg_ref[...] += dg
        dfb = df.astype(_MXU)
        df_ref[...] = dfb
        zcat = jnp.concatenate([zp_ref[...], z_ref[...]], axis=0)

        def conv(cs):
            ubuf[:, cs] = jnp.dot(zcat, wu_ref[:, cs], preferred_element_type=F32)
            ubuf[0:nh, cs] = _wrap_prev(jnp.where(r == 0, 0.0, ubuf[0:nh, cs]), ubuf[tm:tm + nh, cs])
            acc = bf_ref[:, cs]
            for k in range(kf):
                acc = acc + wf_ref[k:k + 1, cs] * ubuf[8 * k:8 * k + tm, cs]
            return acc

        for c in range(Fh // cw):
            gs = slice(c * cw, (c + 1) * cw)
            us = slice(Fh + c * cw, Fh + (c + 1) * cw)
            ug = conv(gs)
            gg, t = _gelu(ug)
            da2 = lax.dot_general(dfb, wd_ref[gs, :], (((1,), (1,)), ((), ())), preferred_element_type=F32)
            dbuf[0:tm, gs] = da2 * conv(us) * _gelu_grad(ug, t)
            dbuf[0:tm, us] = da2 * gg
        z = dbuf[0:nh, :]
        dbuf[tm:tm + nh, :] = _wrap_next(dhalo[...], z)
        dhalo[...] = z
        dz = None
        for c in range(F2 // cw):
            cs = slice(c * cw, (c + 1) * cw)
            upc = ubuf[nh:nh + tm, cs]
            dup = None
            for k in range(kf):
                dsh = dbuf[8 * (kf - 1 - k):8 * (kf - 1 - k) + tm, cs]
                term = wf_ref[k:k + 1, cs] * dsh
                dup = term if dup is None else dup + term
                dwf_ref[k:k + 1, cs] += jnp.sum(dsh * upc, axis=0, keepdims=True)
            dbf_ref[:, cs] += jnp.sum(dbuf[0:tm, cs], axis=0, keepdims=True)
            dupb = dup.astype(_MXU)
            dup_ref[:, cs] = dupb
            part = lax.dot_general(dupb, wu_ref[:, cs], (((1,), (1,)), ((), ())), preferred_element_type=F32)
            dz = part if dz is None else dz + part
        dx, dgz = _rms_bwd(hm_ref[...], gz_ref[...], dz)
        dhm_ref[...] = dh_ref[...] + dx
        dgz_ref[...] += dgz

    rev_d = pl.BlockSpec((tm, D), lambda i: (nt - 1 - i, 0))
    rev_f = pl.BlockSpec((tm, F2), lambda i: (nt - 1 - i, 0))
    prev = pl.BlockSpec((nh, D), lambda i: (jnp.maximum((nt - 1 - i) * (tm // nh) - 1, 0), 0))
    full = lambda *s: pl.BlockSpec(s, lambda i: (0,) * len(s))
    return _ride(
        body, plan, name, nt,
        in_specs=[rev_d, rev_d, rev_d, prev, rev_d, _lspec(l, kf, F2), _lspec(l, 1, F2), _lspec(l, 1, D),
                  _lspec(l, 1, D), _ANY, _ANY],
        args=(dh, f, z2, z2, hmid, p['ffn_conv_w'], p['ffn_conv_b'], p['g_post_ffn'], p['g_pre_ffn'], wup, wdown),
        out_specs=[rev_d, rev_f, full(kf, F2), full(1, F2), full(1, D), rev_d, full(1, D)],
        out_shape=[_sds((T, D), _MXU), _sds((T, F2), _MXU), _sds((kf, F2), F32), _sds((1, F2), F32), _sds((1, D), F32),
                   _sds((T, D), F32), _sds((1, D), F32)],
        scratch_shapes=[pltpu.VMEM((nh + tm, F2), F32), pltpu.VMEM((tm + nh, F2), F32), pltpu.VMEM((nh, F2), F32),
                        pltpu.VMEM((D, F2), _MXU), pltpu.VMEM((Fh, D), _MXU), pltpu.SemaphoreType.DMA((_N_DEV + 1,))])


def _mm_tn(xs, dy, name):
    T, K = xs.shape
    N = dy.shape[1]
    tm = _TM_DW
    nt = T // tm
    kb = K if K <= 1024 else _pick(K, _NB)
    nb = _pick(N, _NB)

    def body(x_ref, dy_ref, o_ref, acc):
        t = pl.program_id(2)
        part = lax.dot_general(x_ref[...], dy_ref[...], (((0,), (0,)), ((), ())), preferred_element_type=F32)

        @pl.when(t == 0)
        def _():
            acc[...] = part

        @pl.when(t > 0)
        def _():
            acc[...] += part

        @pl.when(t == nt - 1)
        def _():
            o_ref[...] = acc[...].astype(o_ref.dtype)

    return pl.pallas_call(
        body, name=name, grid=(K // kb, N // nb, nt),
        in_specs=[pl.BlockSpec((tm, kb), lambda a, b, t: (t, a)), pl.BlockSpec((tm, nb), lambda a, b, t: (t, b))],
        out_specs=pl.BlockSpec((kb, nb), lambda a, b, t: (a, b)),
        out_shape=_sds((K, N), _WIRE),
        scratch_shapes=[pltpu.VMEM((kb, nb), F32)],
        compiler_params=_params())(xs, dy)


def _mix_bwd(dhm, o, ya, yb, p, wout, l, name):
    T, D = ya.shape
    Dc = yb.shape[1]
    tm = _TM_SEQ

    def body(dh_ref, o_ref, ya_ref, yb_ref, gp_ref, gl_ref, gc_ref, w_ref,
             do_ref, dya_ref, dyb_ref, dgp_ref, dgl_ref, dgc_ref):
        @pl.when(pl.program_id(0) == 0)
        def _():
            dgp_ref[...] = jnp.zeros_like(dgp_ref)
            dgl_ref[...] = jnp.zeros_like(dgl_ref)
            dgc_ref[...] = jnp.zeros_like(dgc_ref)

        do, dgp = _rms_bwd(o_ref[...], gp_ref[...], dh_ref[...])
        dgp_ref[...] += dgp
        dob = do.astype(_MXU)
        do_ref[...] = dob
        dy = lax.dot_general(dob, w_ref[...], (((1,), (1,)), ((), ())), preferred_element_type=F32)
        dya, dgl = _rms_bwd(ya_ref[...], gl_ref[...], dy[:, 0:D])
        dyb, dgc = _rms_bwd(yb_ref[...], gc_ref[...], dy[:, D:D + Dc])
        dya_ref[...] = dya
        dyb_ref[...] = dyb
        dgl_ref[...] += dgl
        dgc_ref[...] += dgc

    row_d = pl.BlockSpec((tm, D), lambda i: (i, 0))
    row_c = pl.BlockSpec((tm, Dc), lambda i: (i, 0))
    full = lambda *s: pl.BlockSpec(s, lambda i: (0,) * len(s))
    return pl.pallas_call(
        body, name=name, grid=(T // tm,),
        in_specs=[row_d, row_d, row_d, row_c, _lspec(l, 1, D), _lspec(l, 1, D), _lspec(l, 1, Dc),
                  pl.BlockSpec((D + Dc, D), lambda i: (0, 0))],
        out_specs=[row_d, row_d, row_c, full(1, D), full(1, D), full(1, Dc)],
        out_shape=[_sds((T, D), _MXU), _sds((T, D), F32), _sds((T, Dc), F32),
                   _sds((1, D), F32), _sds((1, D), F32), _sds((1, Dc), F32)],
        compiler_params=_params(),
    )(dhm, o, ya, yb, p['g_post_mix'], p['g_out_lru'], p['g_out_conv'], wout)


def _mixers_bwd(dya, dyb, z1, xc, hs, cc, h, dres, p, win, l, name, plan=None):
    T, D = z1.shape
    D3 = win.shape[0] * win.shape[2]
    Dc = D // 2
    heads, hd = p['lru_wa'].shape[1], p['lru_wa'].shape[2]
    tm = _TM_SEQ
    nt = T // tm
    per8 = tm // 8
    kl, kc = p['lru_conv_w'].shape[1], p['conv_w'].shape[1]
    nhl, nhc = 8 * (kl - 1), 8 * (kc - 1)
    assert nhc <= tm

    def body(dya_ref, dyb_ref, z_ref, xc_ref, hs_ref, hsp_ref, cc_ref, h_ref, dres_ref,
             wl_ref, wa_ref, ba_ref, wx_ref, bx_ref, lam_ref, wc_ref, lg_ref, lb_ref, gz_ref, win_hbm,
             dproj_ref, dwl_ref, dbl_ref, dwa_ref, dba_ref, dwx_ref, dbx_ref, dlam_ref,
             dwc_ref, dbc_ref, dlg_ref, dlb_ref, dh_ref, dgz_ref,
             gcar, dxbuf, dxhalo, dcbuf, dchalo, abuf, gbuf, hbuf, tmp8, win_ref, wsem):
        i = pl.program_id(0)
        r = nt - 1 - i
        _load_weights(i, _shard_pairs(win_hbm, win_ref), wsem)

        @pl.when(i == 0)
        def _():
            for ref in (gcar, dxhalo, dchalo, dwl_ref, dbl_ref, dwa_ref, dba_ref, dwx_ref, dbx_ref, dlam_ref,
                        dwc_ref, dbc_ref, dlg_ref, dlb_ref, dgz_ref):
                ref[...] = jnp.zeros_like(ref)

        zb = z_ref[...]

        def proj(lo, hi):
            return jnp.dot(zb, win_ref[:, lo:hi], preferred_element_type=F32)

        xl, gl, ca, cb = proj(0, D), proj(D, 2 * D), proj(2 * D, 2 * D + Dc), proj(2 * D + Dc, 3 * D)

        dya_v = dya_ref[...]
        hs = hs_ref[...]
        gg, tg = _gelu(gl)
        dproj_ref[:, D:2 * D] = (dya_v * hs * _gelu_grad(gl, tg)).astype(_MXU)
        dhs = dya_v * gg
        xc = xc_ref[...]
        lam = lam_ref[...]
        xcb, ra, ri, sp, a, mult = _gates(xc, wa_ref, ba_ref[...], wx_ref, bx_ref[...], lam, heads, hd)
        g = _scan_bwd(a, dhs, abuf, gbuf, gcar, tmp8, tm)
        sub8 = lax.broadcasted_iota(jnp.int32, (8, D), 0)
        hbuf[8:8 + tm, :] = hs
        hbuf[0:8, :] = jnp.where(sub8 == 0, jnp.where(r == 0, 0.0, hsp_ref[7:8, :]),
                                 pltpu.roll(hs_ref[tm - 8:tm, :], 1, 0))
        da = g * hbuf[0:tm, :]
        gx = g * xc
        dxc = g * mult * ri
        dla = da * a - (gx * ri) * (a * a) / mult
        dlam_ref[...] += jnp.sum(dla * ra, axis=0, keepdims=True) * (LRU_C * jax.nn.sigmoid(-lam))
        dpa = (dla * ((-LRU_C) * sp)) * ra * (1.0 - ra)
        dpx = (gx * mult) * ri * (1.0 - ri)
        dba_ref[...] += jnp.sum(dpa, axis=0, keepdims=True)
        dbx_ref[...] += jnp.sum(dpx, axis=0, keepdims=True)
        dpab, dpxb = dpa.astype(_MXU), dpx.astype(_MXU)
        for h in range(heads):
            hsl = slice(h * hd, (h + 1) * hd)
            dwa_ref[h] += lax.dot_general(xcb[:, hsl], dpab[:, hsl], (((0,), (0,)), ((), ())),
                                          preferred_element_type=F32)
            dwx_ref[h] += lax.dot_general(xcb[:, hsl], dpxb[:, hsl], (((0,), (0,)), ((), ())),
                                          preferred_element_type=F32)
        dxc = dxc + _bd_mm_t(dpab, wa_ref, heads, hd) + _bd_mm_t(dpxb, wx_ref, heads, hd)
        dbl_ref[...] += jnp.sum(dxc, axis=0, keepdims=True)
        _fill_anticausal(dxbuf, dxhalo, dxc, tm)
        dxl = None
        for k in range(kl):
            dsh = dxbuf[8 * (kl - 1 - k):8 * (kl - 1 - k) + tm, :]
            term = wl_ref[k:k + 1, :] * dsh
            dxl = term if dxl is None else dxl + term
            dwl_ref[k:k + 1, :] += jnp.sum(dsh * xl, axis=0, keepdims=True)
        dproj_ref[:, 0:D] = dxl.astype(_MXU)

        sg = jax.nn.sigmoid(cb)
        cg = ca * sg
        nrm, rss = _group_norm(cc_ref[...], CONV_GROUPS)
        lg = lg_ref[...]
        cl = nrm * lg + lb_ref[...]
        sc = jax.nn.sigmoid(cl)
        dcl = dyb_ref[...] * (sc * (1.0 + cl * (1.0 - sc)))
        dlg_ref[...] += jnp.sum(dcl * nrm, axis=0, keepdims=True)
        dlb_ref[...] += jnp.sum(dcl, axis=0, keepdims=True)
        dnrm = dcl * lg
        gsz = Dc // CONV_GROUPS
        parts = []
        for gi in range(CONV_GROUPS):
            sl = slice(gi * gsz, (gi + 1) * gsz)
            dn, nn = dnrm[:, sl], nrm[:, sl]
            parts.append(rss[gi] * (dn - jnp.mean(dn, axis=-1, keepdims=True)
                                    - nn * jnp.mean(dn * nn, axis=-1, keepdims=True)))
        dcc = jnp.concatenate(parts, axis=-1)
        dbc_ref[...] += jnp.sum(dcc, axis=0, keepdims=True)
        _fill_anticausal(dcbuf, dchalo, dcc, tm)
        dcg = None
        for k in range(kc):
            dsh = dcbuf[8 * (kc - 1 - k):8 * (kc - 1 - k) + tm, :]
            term = wc_ref[k:k + 1, :] * dsh
            dcg = term if dcg is None else dcg + term
            dwc_ref[k:k + 1, :] += jnp.sum(dsh * cg, axis=0, keepdims=True)
        dproj_ref[:, 2 * D:2 * D + Dc] = (dcg * sg).astype(_MXU)
        dproj_ref[:, 2 * D + Dc:3 * D] = (dcg * ca * sg * (1.0 - sg)).astype(_MXU)

        cwp = _pick(D3, _NB)
        dz = None
        for c in range(D3 // cwp):
            cs = slice(c * cwp, (c + 1) * cwp)
            part = lax.dot_general(dproj_ref[:, cs], win_ref[:, cs], (((1,), (1,)), ((), ())),
                                   preferred_element_type=F32)
            dz = part if dz is None else dz + part
        dx, dgz = _rms_bwd(h_ref[...], gz_ref[...], dz)
        dh_ref[...] = dres_ref[...] + dx
        dgz_ref[...] += dgz

    rev_d = pl.BlockSpec((tm, D), lambda i: (nt - 1 - i, 0))
    rev_c = pl.BlockSpec((tm, Dc), lambda i: (nt - 1 - i, 0))
    rev_p = pl.BlockSpec((tm, D3), lambda i: (nt - 1 - i, 0))
    prev8 = pl.BlockSpec((8, D), lambda i: (jnp.maximum((nt - 1 - i) * per8 - 1, 0), 0))
    full = lambda *s: pl.BlockSpec(s, lambda i: (0,) * len(s))
    return _ride(
        body, plan, name, nt,
        in_specs=[rev_d, rev_c, rev_d, rev_d, rev_d, prev8, rev_c, rev_d, rev_d,
                  _lspec(l, kl, D), _lspec(l, heads, hd, hd), _lspec(l, 1, D), _lspec(l, heads, hd, hd),
                  _lspec(l, 1, D), _lspec(l, 1, D), _lspec(l, kc, Dc), _lspec(l, 1, Dc), _lspec(l, 1, Dc),
                  _lspec(l, 1, D), _ANY],
        args=(dya, dyb, z1, xc, hs, hs, cc, h, dres, p['lru_conv_w'], p['lru_wa'], p['lru_ba'], p['lru_wx'],
              p['lru_bx'], p['lru_lambda'], p['conv_w'], p['conv_ln_g'], p['conv_ln_b'], p['g_pre_mix'], win),
        out_specs=[rev_p, full(kl, D), full(1, D), full(heads, hd, hd), full(1, D), full(heads, hd, hd), full(1, D),
                   full(1, D), full(kc, Dc), full(1, Dc), full(1, Dc), full(1, Dc), rev_d, full(1, D)],
        out_shape=[_sds((T, D3), _MXU), _sds((kl, D), F32), _sds((1, D), F32), _sds((heads, hd, hd), F32),
                   _sds((1, D), F32), _sds((heads, hd, hd), F32), _sds((1, D), F32), _sds((1, D), F32),
                   _sds((kc, Dc), F32), _sds((1, Dc), F32), _sds((1, Dc), F32), _sds((1, Dc), F32),
                   _sds((T, D), F32), _sds((1, D), F32)],
        scratch_shapes=[pltpu.VMEM((1, D), F32), pltpu.VMEM((tm + nhl, D), F32), pltpu.VMEM((nhl, D), F32),
                        pltpu.VMEM((tm + nhc, Dc), F32), pltpu.VMEM((nhc, Dc), F32),
                        pltpu.VMEM((tm, D), F32), pltpu.VMEM((tm, D), F32), pltpu.VMEM((8 + tm, D), F32),
                        pltpu.VMEM((8, D), F32), pltpu.VMEM((D, D3), _MXU), pltpu.SemaphoreType.DMA((_N_DEV,))])


def _sum_sources(recv, name):
    _, R, C = recv.shape
    rb = _pick_rows(R, 1024)

    def body(r_ref, o_ref):
        acc = r_ref[0]
        for s in range(1, _N_DEV):
            acc = acc + r_ref[s]
        o_ref[...] = acc

    return pl.pallas_call(
        body, name=name, grid=(R // rb,),
        in_specs=[pl.BlockSpec((_N_DEV, rb, C), lambda i: (0, i, 0))],
        out_specs=pl.BlockSpec((rb, C), lambda i: (i, 0)),
        out_shape=_sds((R, C), F32), compiler_params=_params())(recv)


def _adamw(g, w, m, v, name):
    R, C = w.shape
    summed = g.ndim == 3
    rb = _pick_rows(R, max(8, min(512, _ADAM_BLOCK_ELEMS // C)))
    c1 = 1.0 - ADAM_B1 ** ADAM_STEP
    c2 = 1.0 - ADAM_B2 ** ADAM_STEP

    def body(g_ref, w_ref, m_ref, v_ref, go_ref, d_ref, mo_ref, vo_ref):
        if summed:
            gv = g_ref[0]
            for s in range(1, _N_DEV):
                gv = gv + g_ref[s]
        else:
            gv = g_ref[...]
        go_ref[...] = gv
        mn = ADAM_B1 * m_ref[...] + (1.0 - ADAM_B1) * gv
        vn = ADAM_B2 * v_ref[...] + (1.0 - ADAM_B2) * (gv * gv)
        mo_ref[...] = mn
        vo_ref[...] = vn
        d_ref[...] = (-ADAM_LR) * ((mn / c1) / (jnp.sqrt(vn / c2) + ADAM_EPS) + ADAM_WD * w_ref[...])

    blk = pl.BlockSpec((rb, C), lambda i: (i, 0))
    gspec = pl.BlockSpec((_N_DEV, rb, C), lambda i: (0, i, 0)) if summed else blk
    return pl.pallas_call(
        body, name=name, grid=(R // rb,),
        in_specs=[gspec, blk, blk, blk], out_specs=[blk, blk, blk, blk],
        out_shape=[_sds((R, C), F32)] * 4, compiler_params=_params())(g, w, m, v)


def _adamw_layers(recvs, w, m, v, name, plan=None):
    L, R, C = w.shape
    rb = _pick_rows(R, max(8, _ADAM_BLOCK_ELEMS // (4 * C)))
    c1 = 1.0 - ADAM_B1 ** ADAM_STEP
    c2 = 1.0 - ADAM_B2 ** ADAM_STEP

    def body(*refs):
        r_refs = refs[:L]
        w_ref, m_ref, v_ref, go_ref, d_ref, mo_ref, vo_ref = refs[L:]
        for l in range(L):
            gv = r_refs[l][0].astype(F32)
            for s in range(1, _N_DEV):
                gv = gv + r_refs[l][s].astype(F32)
            go_ref[l] = gv
            mn = ADAM_B1 * m_ref[l] + (1.0 - ADAM_B1) * gv
            vn = ADAM_B2 * v_ref[l] + (1.0 - ADAM_B2) * (gv * gv)
            mo_ref[l] = mn
            vo_ref[l] = vn
            d_ref[l] = (-ADAM_LR) * ((mn / c1) / (jnp.sqrt(vn / c2) + ADAM_EPS) + ADAM_WD * w_ref[l])

    blk = pl.BlockSpec((L, rb, C), lambda i: (0, i, 0))
    return _ride(
        body, plan, name, R // rb,
        in_specs=[pl.BlockSpec((_N_DEV, rb, C), lambda i: (0, i, 0))] * L + [blk, blk, blk],
        args=(*recvs, w, m, v),
        out_specs=[blk, blk, blk, blk],
        out_shape=[_sds((L, R, C), F32)] * 4, scratch_shapes=[])


def _pack_rows(flat_parts, dtype, row_mult):
    flat = jnp.concatenate([f.reshape(-1).astype(dtype) for f in flat_parts])
    n = flat.shape[0]
    per = _LANES * row_mult
    padded = -(-n // per) * per
    if padded != n:
        flat = jnp.concatenate([flat, jnp.zeros((padded - n,), dtype)])
    return flat.reshape(-1, _LANES)


def _unpack(flat, shapes):
    out, off = [], 0
    for s in shapes:
        n = math.prod(s)
        out.append(flat[off:off + n].reshape(s))
        off += n
    return out


def _to_pieces(full):
    n = full.shape[-1] // _N_DEV
    t = full.reshape(full.shape[:-1] + (_N_DEV, n))
    return jnp.moveaxis(t, -2, 0).reshape(_N_DEV, -1)


def _from_gathered(seg, shard_shape, axis):
    t = seg.reshape((_N_DEV,) + tuple(shard_shape))
    t = jnp.moveaxis(t, 0, axis)
    shape = list(shard_shape)
    shape[axis] *= _N_DEV
    return t.reshape(shape)


def kernel(x, meta_tokens, g_pre_mix, w_in, lru_conv_w, lru_conv_b, lru_wa, lru_ba, lru_wx, lru_bx, lru_lambda, conv_w, conv_b, conv_ln_g, conv_ln_b, g_out_lru, g_out_conv, w_out, g_post_mix, g_pre_ffn, w_up, ffn_conv_w, ffn_conv_b, w_down, g_post_ffn, loss_target, m_meta_tokens, m_g_pre_mix, m_w_in, m_lru_conv_w, m_lru_conv_b, m_lru_wa, m_lru_ba, m_lru_wx, m_lru_bx, m_lru_lambda, m_conv_w, m_conv_b, m_conv_ln_g, m_conv_ln_b, m_g_out_lru, m_g_out_conv, m_w_out, m_g_post_mix, m_g_pre_ffn, m_w_up, m_ffn_conv_w, m_ffn_conv_b, m_w_down, m_g_post_ffn, v_meta_tokens, v_g_pre_mix, v_w_in, v_lru_conv_w, v_lru_conv_b, v_lru_wa, v_lru_ba, v_lru_wx, v_lru_bx, v_lru_lambda, v_conv_w, v_conv_b, v_conv_ln_g, v_conv_ln_b, v_g_out_lru, v_g_out_conv, v_w_out, v_g_post_mix, v_g_pre_ffn, v_w_up, v_ffn_conv_w, v_ffn_conv_b, v_w_down, v_g_post_ffn):
    given = dict(locals())
    W = {n: given[n] for n in W_NAMES}
    M = {n: given['m_' + n] for n in W_NAMES}
    V = {n: given['v_' + n] for n in W_NAMES}
    S, D = x.shape[1], x.shape[2]
    L = g_pre_mix.shape[0]
    Dc = D // 2
    step = math.lcm(_TM_SEQ, _TM_DW)
    T = -(-(N_META + S) // step) * step

    first, rest = ['w_in', 'w_out'], ['w_up', 'w_down']

    def layer_pack(l, names):
        return [W[n][l].astype(_MXU) for n in names]

    def layer_weights(gathered, names):
        return {n: g if n in ('w_in', 'w_up') else g.reshape(-1, g.shape[2]) for n, g in zip(names, gathered)}

    small_pack = _pack_rows([W[n] for n in SMALL_SHARDED], F32, 8)
    *big_g, small_g = _run_plan(_GatherPlan(layer_pack(0, first) + [small_pack]), "gather_weights_first")
    small_segs = _unpack_cols(small_g.reshape(_N_DEV, -1), [W[n].shape for n in SMALL_SHARDED])
    full = {}
    for n, seg in zip(SMALL_SHARDED, small_segs):
        full[n] = _from_gathered(seg, W[n].shape, W[n].ndim - 1)

    p = {}
    for n in W_NAMES:
        if n in BIG or n == 'meta_tokens':
            continue
        a = full[n] if n in full else W[n]
        p[n] = a.reshape(L, 1, a.shape[1]) if a.ndim == 2 else a

    pad_rows = T - N_META - S
    h = _perm(jnp.concatenate([full['meta_tokens'], x[0], jnp.zeros((pad_rows, D), F32)], axis=0), _TM_SEQ)
    tgt = _perm(jnp.concatenate([jnp.zeros((N_META, D), F32), loss_target[0], jnp.zeros((pad_rows, D), F32)],
                                axis=0), _TM_SEQ)

    saved = []
    wl = layer_weights(big_g, first)
    for l in range(L):
        plan = _GatherPlan(layer_pack(0, rest)) if l == 0 else None
        (z1, xc, hs, ya, cc, yb, y, o, hmid), nxt = _mixers_fwd(h, p, wl['w_in'], wl['w_out'], l,
                                                                f"mixers_fwd_l{l}", plan)
        if plan:
            wl = {**wl, **layer_weights(nxt, rest)}
        plan = _GatherPlan(layer_pack(l + 1, BIG)) if l + 1 < L else None
        (z2, a2, f, hout), nxt = _ffn_fwd(hmid, p, wl['w_up'], wl['w_down'], l, f"ffn_fwd_l{l}", plan)
        saved.append(dict(h=h, z1=z1, xc=xc, hs=hs, ya=ya, cc=cc, yb=yb, y=y, o=o, hmid=hmid,
                          z2=z2, a2=a2, f=f, w=wl))
        h = hout
        if plan:
            wl = layer_weights(nxt, BIG)
    loss_tile, dh = _loss_head(h, tgt, S, "loss_head")
    loss = lax.psum(loss_tile[0, 0], ("x", "y", "c"))

    small_g_names = ['g_pre_mix', 'lru_conv_w', 'lru_conv_b', 'lru_wa', 'lru_ba', 'lru_wx', 'lru_bx', 'lru_lambda',
                     'conv_w', 'conv_b', 'conv_ln_g', 'conv_ln_b', 'g_out_lru', 'g_out_conv', 'g_post_mix',
                     'g_pre_ffn', 'ffn_conv_w', 'ffn_conv_b', 'g_post_ffn']
    per_layer = {n: [None] * L for n in small_g_names}
    recv = {n: [None] * L for n in BIG}
    pending = None
    for l in reversed(range(L)):
        sv = saved[l]
        wl = sv['w']
        plan = _ExchangePlan(list(pending), ['cols', 'cols']) if pending else None
        (df, dup0, dwf, dbf, dgpf, dhm, dgpre), got = _ffn_bwd(
            dh, sv['f'], sv['z2'], sv['hmid'], p, wl['w_up'], wl['w_down'], l, f"ffn_bwd_l{l}", plan)
        if plan:
            recv['w_up'][l + 1], recv['w_in'][l + 1] = got
        d_down = _mm_tn(sv['a2'], df, f"dw_down_l{l}")
        d_up = _mm_tn(sv['z2'], dup0, f"dw_up_l{l}")
        do, dya, dyb, dgpm, dgol, dgoc = _mix_bwd(dhm, sv['o'], sv['ya'], sv['yb'], p, wl['w_out'], l,
                                                  f"mix_bwd_l{l}")
        d_out = _mm_tn(sv['y'], do, f"dw_out_l{l}")
        plan = (_ExchangePlan([d_down, d_out, d_up], ['rows', 'rows', 'cols']) if l == 0
                else _ExchangePlan([d_down, d_out], ['rows', 'rows']))
        (dproj, dwl, dbl, dwa, dba, dwx, dbx, dlam, dwc, dbc, dlg, dlb, dh, dgpmix), got = _mixers_bwd(
            dya, dyb, sv['z1'], sv['xc'], sv['hs'], sv['cc'], sv['h'], dhm, p, wl['w_in'], l,
            f"mixers_bwd_l{l}", plan)
        recv['w_down'][l], recv['w_out'][l] = got[0], got[1]
        if l == 0:
            recv['w_up'][0] = got[2]
        d_in = _mm_tn(sv['z1'], dproj, f"dw_in_l{l}")
        pending = (d_up, d_in)
        for n, val in (('g_pre_mix', dgpmix), ('lru_conv_w', dwl), ('lru_conv_b', dbl), ('lru_wa', dwa),
                       ('lru_ba', dba), ('lru_wx', dwx), ('lru_bx', dbx), ('lru_lambda', dlam), ('conv_w', dwc),
                       ('conv_b', dbc), ('conv_ln_g', dlg), ('conv_ln_b', dlb), ('g_out_lru', dgol),
                       ('g_out_conv', dgoc), ('g_post_mix', dgpm), ('g_pre_ffn', dgpre), ('ffn_conv_w', dwf),
                       ('ffn_conv_b', dbf), ('g_post_ffn', dgpf)):
            per_layer[n][l] = val
    dh = _unperm(dh, _TM_SEQ)
    grad_x = dh[N_META:N_META + S][None]
    partial = {n: jnp.stack(per_layer[n]).reshape((L,) + tuple(
        (full[n] if n in full else W[n]).shape[1:])) for n in small_g_names}
    partial['meta_tokens'] = dh[0:N_META]

    shard_pack = jnp.concatenate([_to_pieces(partial[n]) for n in SMALL_SHARDED], axis=1)
    n_sh = shard_pack.shape[1]
    rs = -(-n_sh // (8 * _LANES)) * 8
    shard_pack = jnp.concatenate([shard_pack, jnp.zeros((_N_DEV, rs * _LANES - n_sh), F32)], axis=1)
    rep_flat = jnp.concatenate([partial[n].reshape(-1) for n in REPLICATED])
    n_rep = rep_flat.shape[0]
    rr = -(-n_rep // (_N_DEV * _REP_ROWS * _LANES)) * _REP_ROWS
    rep_flat = jnp.concatenate([rep_flat, jnp.zeros((_N_DEV * rr * _LANES - n_rep,), F32)])
    small_send = jnp.concatenate([shard_pack, rep_flat.reshape(_N_DEV, rr * _LANES)], axis=1)
    small_send = small_send.reshape(_N_DEV, rs + rr, _LANES)
    out = {}
    res, got = _adamw_layers(recv['w_up'], W['w_up'], M['w_up'], V['w_up'], "adamw_w_up",
                             _ExchangePlan([pending[1], small_send], ['cols', 'slots']))
    out['w_up'] = list(res)
    recv['w_in'][0], r_small = got
    small_red = _sum_sources(r_small, "sum_small")
    (rep_g,) = _run_plan(_GatherPlan([small_red[rs:]]), "gather_replicated_grads")
    rep_g = rep_g.reshape(_N_DEV * rr, _LANES)
    for n in ('w_in', 'w_out', 'w_down'):
        out[n] = list(_adamw_layers(recv[n], W[n], M[n], V[n], f"adamw_{n}")[0])
    sh_shapes = [W[n].shape for n in SMALL_SHARDED]
    res = _adamw(small_red[:rs], _pack_rows([W[n] for n in SMALL_SHARDED], F32, 8),
                 _pack_rows([M[n] for n in SMALL_SHARDED], F32, 8),
                 _pack_rows([V[n] for n in SMALL_SHARDED], F32, 8), "adamw_small_sharded")
    for k in range(4):
        for n, val in zip(SMALL_SHARDED, _unpack(res[k].reshape(-1), sh_shapes)):
            out.setdefault(n, [None] * 4)[k] = val
    rep_shapes = [W[n].shape for n in REPLICATED]
    res = _adamw(rep_g, _pack_rows([W[n] for n in REPLICATED], F32, _REP_ROWS * _N_DEV),
                 _pack_rows([M[n] for n in REPLICATED], F32, _REP_ROWS * _N_DEV),
                 _pack_rows([V[n] for n in REPLICATED], F32, _REP_ROWS * _N_DEV), "adamw_replicated")
    for k in range(4):
        for n, val in zip(REPLICATED, _unpack(res[k].reshape(-1), rep_shapes)):
            out.setdefault(n, [None] * 4)[k] = val

    return (loss, grad_x, *[out[n][0] for n in W_NAMES], *[out[n][1] for n in W_NAMES],
            *[out[n][2] for n in W_NAMES], *[out[n][3] for n in W_NAMES])


def _unpack_cols(gathered, shapes):
    out, off = [], 0
    for s in shapes:
        n = math.prod(s)
        out.append(gathered[:, off:off + n])
        off += n
    return out
```
